```python
import math
import jax, jax.numpy as jnp
from jax import lax
import numpy as np


D_MODEL = 1024
BATCH = 4
SEQ = 4096
DEPTH = 1
DEC_BATCH = 2
DEC_SEQ = 8192
PAST_LEN = 128

HEAD_DIM = 64
DIL_PATTERNS = ((128, 1), (512, 4), (2048, 16))
N_DIL_GROUPS = 3
HEADS_PER_DIL_GROUP = 4
H_A = N_DIL_GROUPS * HEADS_PER_DIL_GROUP
H_B = 8
QA = H_A * HEAD_DIM
QB = H_B * HEAD_DIM
D_IN = 3 * QA + 3 * QB + 2 * D_MODEL
BAND_BLK = 64
GRID_W = 64
NA_KH_MAX = 8
NA_KW = 16
NA_QCB = 16
NA_KCB = 32
N_BUCKETS = 32
REL_MAX_DIST = 1024
N_GROUPS = 4
EXPERTS_PER_GROUP = 8
N_EXPERTS = N_GROUPS * EXPERTS_PER_GROUP
TOP_K_IN_GROUP = 2
D_EXPERT = 512
LN_EPS = 1e-5
NEG_INF = -1e30
DN_ALPHA = (2.0 * DEPTH) ** 0.25
DN_BETA = (8.0 * DEPTH) ** -0.25

kernel_name = 'hybrid_dilated_natten_hmoe_encoder'


def _layer_norm(x, g, b):
    xf = x.astype(jnp.float32)
    mu = jnp.mean(xf, axis=-1, keepdims=True)
    var = jnp.mean(jnp.square(xf - mu), axis=-1, keepdims=True)
    return ((xf - mu) * lax.rsqrt(var + LN_EPS) * g + b).astype(x.dtype)


def _t5_bucket(rel):
    half = N_BUCKETS // 2
    max_exact = half // 2
    ret = jnp.where(rel > 0, half, 0)
    n = jnp.abs(rel)
    nf = jnp.maximum(n, 1).astype(jnp.float32)
    large = max_exact + (jnp.log(nf / max_exact) / math.log(REL_MAX_DIST / max_exact)
                         * (half - max_exact)).astype(jnp.int32)
    large = jnp.minimum(large, half - 1)
    return ret + jnp.where(n < max_exact, n, large)


def _dilated_group_attention(q, k, v, bias_table, dilation, half_span):
    b, t, h, dh = q.shape
    L = t // dilation
    n = b * dilation

    def to_sub(a):
        return a.reshape(b, L, dilation, h, dh).transpose(0, 2, 1, 3, 4).reshape(n, L, h, dh)

    qs, ks, vs = to_sub(q), to_sub(k), to_sub(v)
    nb = -(-L // BAND_BLK)
    lp = nb * BAND_BLK
    qs = jnp.pad(qs, ((0, 0), (0, lp - L), (0, 0), (0, 0))).reshape(n, nb, BAND_BLK, h, dh)

    def to_band(a):
        a = jnp.pad(a, ((0, 0), (BAND_BLK, lp - L + BAND_BLK), (0, 0), (0, 0)))
        a = a.reshape(n, nb + 2, BAND_BLK, h, dh)
        return jnp.concatenate([a[:, :-2], a[:, 1:-1], a[:, 2:]], axis=2)

    kb, vb = to_band(ks), to_band(vs)
    i = jnp.arange(BAND_BLK)
    m = jnp.arange(3 * BAND_BLK)
    delta = m[None, :] - BAND_BLK - i[:, None]
    kpos = jnp.arange(nb)[:, None] * BAND_BLK - BAND_BLK + m[None, :]
    valid = (jnp.abs(delta) <= half_span)[None] & ((kpos >= 0) & (kpos < L))[:, None, :]
    bias = jnp.moveaxis(bias_table[_t5_bucket(delta * dilation)].astype(jnp.float32), -1, 0)
    s = jnp.einsum('njihd,njmhd->nhjim', qs, kb, preferred_element_type=jnp.float32) * (dh ** -0.5)
    s = jnp.where(valid, s + bias[:, None], NEG_INF)
    lse = jax.nn.logsumexp(s, axis=-1)
    p = jnp.exp(s - lse[..., None]).astype(v.dtype)
    o = jnp.einsum('nhjim,njmhd->njihd', p, vb).reshape(n, lp, h, dh)[:, :L]
    o = o.reshape(b, dilation, L, h, dh).transpose(0, 2, 1, 3, 4).reshape(b, t, h, dh)
    lse = lse.transpose(0, 2, 3, 1).reshape(n, lp, h)[:, :L]
    lse = lse.reshape(b, dilation, L, h).transpose(0, 2, 1, 3).reshape(b, t, h)
    return o, lse


def _dilated_mixer(q, k, v, rel_bias_t5):
    outs, lses = [], []
    for g, (window, dil) in enumerate(DIL_PATTERNS):
        sl = slice(g * HEADS_PER_DIL_GROUP, (g + 1) * HEADS_PER_DIL_GROUP)
        o, l = _dilated_group_attention(q[:, :, sl], k[:, :, sl], v[:, :, sl],
                                        rel_bias_t5[:, sl], dil, window // (2 * dil))
        outs.append(o)
        lses.append(l)
    w = jax.nn.softmax(jnp.stack(lses, axis=0), axis=0)
    return jnp.einsum('gbth,gbthd->bthd', w.astype(q.dtype), jnp.stack(outs, axis=0))


def _neighborhood_mixer(q, k, v, rpb):
    b, t, h, dh = q.shape
    rows = t // GRID_W
    kh = min(NA_KH_MAX, rows)
    ncb = GRID_W // NA_QCB
    r = jnp.arange(rows)
    row_idx = jnp.clip(r - kh // 2, 0, rows - kh)[:, None] + jnp.arange(kh)[None]
    cb = jnp.arange(ncb)
    col_idx = jnp.clip(cb * NA_QCB - NA_KW // 2, 0, GRID_W - NA_KCB)[:, None] + jnp.arange(NA_KCB)[None]
    qcol = cb[:, None] * NA_QCB + jnp.arange(NA_QCB)[None]
    qstart = jnp.clip(qcol - NA_KW // 2, 0, GRID_W - NA_KW)
    kc = col_idx[:, None, :]
    valid = (kc >= qstart[..., None]) & (kc < qstart[..., None] + NA_KW)
    dr = row_idx - r[:, None]
    dc = jnp.clip(kc - qcol[..., None], -(NA_KW - 1), NA_KW - 1)
    bias = rpb[:, dr[:, None, None, :, None] + NA_KH_MAX - 1,
               dc[None, :, :, None, :] + NA_KW - 1].astype(jnp.float32)
    qg = q.reshape(b, rows, ncb, NA_QCB, h, dh)
    kg = k.reshape(b, rows, GRID_W, h, dh)[:, row_idx][:, :, :, col_idx]
    vg = v.reshape(b, rows, GRID_W, h, dh)[:, row_idx][:, :, :, col_idx]
    s = jnp.einsum('brjihd,brkjmhd->bhrjikm', qg, kg, preferred_element_type=jnp.float32) * (dh ** -0.5)
    s = jnp.where(valid[:, :, None, :], s + bias[None], NEG_INF)
    p = jax.nn.softmax(s, axis=(-2, -1)).astype(v.dtype)
    o = jnp.einsum('bhrjikm,brkjmhd->brjihd', p, vg)
    return o.reshape(b, t, h, dh)


def _hier_moe(x, w_rg, b_rg, w_re, b_re, w_gate, w_up, w_down):
    b, t, d = x.shape
    xf = x.reshape(b * t, d)
    lg = (xf @ w_rg + b_rg).astype(jnp.float32)
    pg = jax.nn.softmax(lg, axis=-1)
    gi = jnp.argmax(lg, axis=-1)
    wg = jnp.take_along_axis(pg, gi[:, None], axis=-1)
    le = (xf @ w_re + b_re).astype(jnp.float32).reshape(-1, N_GROUPS, EXPERTS_PER_GROUP)
    le = jnp.take_along_axis(le, gi[:, None, None], axis=1)[:, 0]
    pe = jax.nn.softmax(le, axis=-1)
    top_p, top_i = lax.top_k(pe, TOP_K_IN_GROUP)
    top_p = top_p / jnp.sum(top_p, axis=-1, keepdims=True)
    expert_id = gi[:, None] * EXPERTS_PER_GROUP + top_i
    comb = jnp.sum(jax.nn.one_hot(expert_id, N_EXPERTS, dtype=jnp.float32)
                   * (wg * top_p)[..., None], axis=1).astype(x.dtype)
    y = jnp.zeros_like(xf)
    for e in range(N_EXPERTS):
        hdn = jax.nn.silu(xf @ w_gate[e]) * (xf @ w_up[e])
        y = y + comb[:, e:e + 1] * (hdn @ w_down[e])
    return y.reshape(b, t, d)


def _trunk(x, ln_in_g, ln_in_b, w_in, b_in, rel_bias_t5, na_rpb, w_branch_a, w_branch_b,
           w_out, b_out, ln1_g, ln1_b, w_router_group, b_router_group, w_router_expert,
           b_router_expert, w_exp_gate, w_exp_up, w_exp_down, ln2_g, ln2_b):
    b, t, _ = x.shape
    x = _layer_norm(x, ln_in_g, ln_in_b)
    for l in range(DEPTH):
        proj = x @ w_in[l] + b_in[l]
        q_a, k_a, v_a, q_b, k_b, v_b, g_a, g_b = jnp.split(
            proj, [QA, 2 * QA, 3 * QA, 3 * QA + QB, 3 * QA + 2 * QB, 3 * QA + 3 * QB,
                   3 * QA + 3 * QB + D_MODEL], axis=-1)
        hs_a = lambda a: a.reshape(b, t, H_A, HEAD_DIM)
        hs_b = lambda a: a.reshape(b, t, H_B, HEAD_DIM)
        o_a = _dilated_mixer(hs_a(q_a), hs_a(k_a), hs_a(v_a), rel_bias_t5)
        o_b = _neighborhood_mixer(hs_b(q_b), hs_b(k_b), hs_b(v_b), na_rpb[l])
        y_a = o_a.reshape(b, t, -1) @ w_branch_a[l]
        y_b = o_b.reshape(b, t, -1) @ w_branch_b[l]
        mix = jax.nn.sigmoid(g_a) * y_a + jax.nn.sigmoid(g_b) * y_b
        x = _layer_norm(DN_ALPHA * x + mix @ w_out[l] + b_out[l], ln1_g[l], ln1_b[l])
        ffn = _hier_moe(x, w_router_group[l], b_router_group[l], w_router_expert[l],
                        b_router_expert[l], w_exp_gate[l], w_exp_up[l], w_exp_down[l])
        x = _layer_norm(DN_ALPHA * x + ffn, ln2_g[l], ln2_b[l])
    return x


def setup_inputs(seed: int = 0) -> dict:
    key = jax.random.key(seed)
    ks = jax.random.split(key, 32)
    nrm = jax.random.normal
    f32 = jnp.float32
    sd = D_MODEL ** -0.5
    w_in = jnp.concatenate([
        nrm(ks[4], (DEPTH, D_MODEL, 2 * QA), f32) * sd,
        nrm(ks[5], (DEPTH, D_MODEL, QA), f32) * (sd * DN_BETA),
        nrm(ks[6], (DEPTH, D_MODEL, 2 * QB), f32) * sd,
        nrm(ks[7], (DEPTH, D_MODEL, QB), f32) * (sd * DN_BETA),
        nrm(ks[8], (DEPTH, D_MODEL, 2 * D_MODEL), f32) * sd,
    ], axis=-1)
    return {
        'x_prompt': nrm(ks[0], (BATCH, SEQ, D_MODEL), f32),
        'x_sample': nrm(ks[1], (DEC_BATCH, DEC_SEQ, D_MODEL), f32),
        'ln_in_g': 1.0 + 0.05 * nrm(ks[2], (D_MODEL,), f32),
        'ln_in_b': 0.02 * nrm(ks[3], (D_MODEL,), f32),
        'w_in': w_in,
        'b_in': 0.02 * nrm(ks[9], (DEPTH, D_IN), f32),
        'rel_bias_t5': 0.2 * nrm(ks[10], (N_BUCKETS, H_A), f32),
        'na_rpb': 0.2 * nrm(ks[11], (DEPTH, H_B, 2 * NA_KH_MAX - 1, 2 * NA_KW - 1), f32),
        'w_branch_a': nrm(ks[12], (DEPTH, HEADS_PER_DIL_GROUP * HEAD_DIM, D_MODEL), f32)
                      * ((HEADS_PER_DIL_GROUP * HEAD_DIM) ** -0.5 * DN_BETA),
        'w_branch_b': nrm(ks[13], (DEPTH, QB, D_MODEL), f32) * (QB ** -0.5 * DN_BETA),
        'w_out': nrm(ks[14], (DEPTH, D_MODEL, D_MODEL), f32) * (sd * DN_BETA),
        'b_out': 0.02 * nrm(ks[15], (DEPTH, D_MODEL), f32),
        'ln1_g': 1.0 + 0.05 * nrm(ks[16], (DEPTH, D_MODEL), f32),
        'ln1_b': 0.02 * nrm(ks[17], (DEPTH, D_MODEL), f32),
        'w_router_group': nrm(ks[18], (DEPTH, D_MODEL, N_GROUPS), f32) * sd,
        'b_router_group': 0.01 * nrm(ks[19], (DEPTH, N_GROUPS), f32),
        'w_router_expert': nrm(ks[20], (DEPTH, D_MODEL, N_EXPERTS), f32) * sd,
        'b_router_expert': 0.01 * nrm(ks[21], (DEPTH, N_EXPERTS), f32),
        'w_exp_gate': nrm(ks[22], (DEPTH, N_EXPERTS, D_MODEL, D_EXPERT), f32) * sd,
        'w_exp_up': nrm(ks[23], (DEPTH, N_EXPERTS, D_MODEL, D_EXPERT), f32) * (sd * DN_BETA),
        'w_exp_down': nrm(ks[24], (DEPTH, N_EXPERTS, D_EXPERT, D_MODEL), f32) * (D_EXPERT ** -0.5 * DN_BETA),
        'ln2_g': 1.0 + 0.05 * nrm(ks[25], (DEPTH, D_MODEL), f32),
        'ln2_b': 0.02 * nrm(ks[26], (DEPTH, D_MODEL), f32),
    }


def reference(x_prompt, x_sample, ln_in_g, ln_in_b, w_in, b_in, rel_bias_t5, na_rpb,
              w_branch_a, w_branch_b, w_out, b_out, ln1_g, ln1_b, w_router_group,
              b_router_group, w_router_expert, b_router_expert, w_exp_gate, w_exp_up,
              w_exp_down, ln2_g, ln2_b):
    y_prompt = _trunk(x_prompt, ln_in_g, ln_in_b, w_in, b_in, rel_bias_t5, na_rpb, w_branch_a,
                      w_branch_b, w_out, b_out, ln1_g, ln1_b, w_router_group, b_router_group,
                      w_router_expert, b_router_expert, w_exp_gate, w_exp_up, w_exp_down,
                      ln2_g, ln2_b)
    y_sample = _trunk(x_sample, ln_in_g, ln_in_b, w_in, b_in, rel_bias_t5, na_rpb, w_branch_a,
                      w_branch_b, w_out, b_out, ln1_g, ln1_b, w_router_group, b_router_group,
                      w_router_expert, b_router_expert, w_exp_gate, w_exp_up, w_exp_down,
                      ln2_g, ln2_b)
    return (y_prompt, y_sample)
```

```python
import functools
import math

import numpy as np
import jax
import jax.numpy as jnp
from jax import lax
from jax.experimental import pallas as pl
from jax.experimental.pallas import tpu as pltpu

HEAD_DIM = 64
DIL_PATTERNS = ((128, 1), (512, 4), (2048, 16))
HEADS_PER_DIL_GROUP = 4
N_DIL_GROUPS = 3
H_A = N_DIL_GROUPS * HEADS_PER_DIL_GROUP
H_B = 8
QA = H_A * HEAD_DIM
QB = H_B * HEAD_DIM
HALF_SPAN = 64
GRID_W = 64
NA_KH = 8
NA_KW = 16
N_BUCKETS = 32
REL_MAX_DIST = 1024
N_GROUPS = 4
EXPERTS_PER_GROUP = 8
N_EXPERTS = N_GROUPS * EXPERTS_PER_GROUP
LN_EPS = 1e-5
NEG_INF = -1e30
DEPTH = 1
DN_ALPHA = (2.0 * DEPTH) ** 0.25

LANES = 128
HEAD_BLOCK = 4 * HEAD_DIM
VMEM_LIMIT_BYTES = 56 * 1024 * 1024

PROJ_TM = 512
DIL_M = 128
DIL_KBLK = 64
NA_ROWS = 8
POST_TM = 512
FFN_TM = 512
DISP_TT = 512
COMB_TT = 256
ROUTE_LANES = LANES
EXPERT_LANE0 = 32


def _cparams(sem):
    return pltpu.CompilerParams(dimension_semantics=sem, vmem_limit_bytes=VMEM_LIMIT_BYTES)


def _layer_norm(x, g, b):
    mu = jnp.mean(x, axis=-1, keepdims=True)
    xc = x - mu
    var = jnp.mean(xc * xc, axis=-1, keepdims=True)
    return xc * lax.rsqrt(var + LN_EPS) * g + b


_PROJ_SPLITS = (("qa", QA), ("ka", QA), ("va", QA), ("qb", QB), ("kb", QB), ("vb", QB))


def _inproj_kernel(xp_ref, xs_ref, g_ref, b_ref, w_ref, bias_ref, *out_refs, n0_tiles, d_model):
    i = pl.program_id(0)
    x = jnp.where(i < n0_tiles, xp_ref[...], xs_ref[...])
    xn = _layer_norm(x, g_ref[...], b_ref[...]).astype(jnp.bfloat16)
    col = 0
    for o_ref in out_refs:
        width = o_ref.shape[1]
        for c in range(0, width, 512):
            cw = min(512, width - c)
            y = jnp.dot(xn, w_ref[:, col + c:col + c + cw], preferred_element_type=jnp.float32)
            y = y + bias_ref[:, col + c:col + c + cw]
            o_ref[:, c:c + cw] = y.astype(o_ref.dtype)
        col += width


def _in_projection(xp, xs, ln_g, ln_b, w, b):
    n0, d_model = xp.shape
    n1 = xs.shape[0]
    n = n0 + n1
    tm = PROJ_TM
    n0_tiles, n1_tiles = n0 // tm, n1 // tm
    widths = [wd for _, wd in _PROJ_SPLITS] + [d_model, d_model]
    out_shape = [jax.ShapeDtypeStruct((n, wd), jnp.bfloat16) for wd in widths]
    out_specs = [pl.BlockSpec((tm, wd), lambda i: (i, 0)) for wd in widths]
    d_in = w.shape[1]
    kern = functools.partial(_inproj_kernel, n0_tiles=n0_tiles, d_model=d_model)
    return pl.pallas_call(
        kern,
        grid=(n0_tiles + n1_tiles,),
        in_specs=[
            pl.BlockSpec((tm, d_model), lambda i: (jnp.minimum(i, n0_tiles - 1), 0)),
            pl.BlockSpec((tm, d_model), lambda i: (jnp.maximum(i - n0_tiles, 0), 0)),
            pl.BlockSpec((1, d_model), lambda i: (0, 0)),
            pl.BlockSpec((1, d_model), lambda i: (0, 0)),
            pl.BlockSpec((d_model, d_in), lambda i: (0, 0)),
            pl.BlockSpec((1, d_in), lambda i: (0, 0)),
        ],
        out_specs=out_specs,
        out_shape=out_shape,
        compiler_params=_cparams(("arbitrary",)),
        name="in_projection",
    )(xp, xs, ln_g, ln_b, w, b)


def _t5_bucket_np(rel):
    half = N_BUCKETS // 2
    max_exact = half // 2
    ret = np.where(rel > 0, half, 0)
    n = np.abs(rel)
    nf = np.maximum(n, 1).astype(np.float32)
    large = max_exact + (np.log(nf / np.float32(max_exact)) / np.float32(math.log(REL_MAX_DIST / max_exact))
                         * np.float32(half - max_exact)).astype(np.int32)
    large = np.minimum(large, half - 1)
    return ret + np.where(n < max_exact, n, large)


def _dilated_bias(rel_bias_t5, group, dil):
    kb = DIL_M + 2 * HALF_SPAN
    i = np.arange(DIL_M)[:, None]
    m = np.arange(kb)[None, :]
    delta = m - HALF_SPAN - i
    band = np.abs(delta) <= HALF_SPAN
    bucket = _t5_bucket_np(delta * dil)
    table = rel_bias_t5[:, group * HEADS_PER_DIL_GROUP:(group + 1) * HEADS_PER_DIL_GROUP].astype(jnp.float32)
    bias = jnp.moveaxis(table[bucket], -1, 0)
    masks = np.stack([band, band & (m >= HALF_SPAN), band & (m < kb - HALF_SPAN)])
    out = jnp.where(masks[:, None], bias[None], NEG_INF)
    return out.reshape(3, HEADS_PER_DIL_GROUP * DIL_M, kb)


def _stack_heads(q):
    head_of_col = lax.broadcasted_iota(jnp.int32, (1, HEAD_BLOCK), 1) // HEAD_DIM
    zero = jnp.zeros_like(q)
    return jnp.concatenate([jnp.where(head_of_col == h, q, zero) for h in range(4)], axis=0)


def _merge_heads(x, m):
    head_of_col = lax.broadcasted_iota(jnp.int32, (1, HEAD_BLOCK), 1) // HEAD_DIM
    out = jnp.zeros((m, HEAD_BLOCK), x.dtype)
    for h in range(4):
        out = jnp.where(head_of_col == h, x[h * m:(h + 1) * m], out)
    return out


def _softmax_pv(s, v, m_rows):
    mx = jnp.max(s, axis=-1, keepdims=True)
    p = jnp.exp(s - mx)
    l = jnp.sum(p, axis=-1, keepdims=True)
    pv = jnp.dot(p.astype(jnp.bfloat16), v, preferred_element_type=jnp.float32)
    o = _merge_heads(pv * (1.0 / l), m_rows)
    return o, mx + jnp.log(l)


def _dilated_kernel(q_ref, k0_ref, k1_ref, k2_ref, k3_ref, v0_ref, v1_ref, v2_ref, v3_ref, bias_ref,
                    o_ref, lse_ref, *, seg0_blocks, blocks_per_seq):
    jb = pl.program_id(1)
    in0 = jb < seg0_blocks
    jl = jnp.where(in0, jb, jb - seg0_blocks)
    nbs = jnp.where(in0, blocks_per_seq[0], blocks_per_seq[1])
    pos = lax.rem(jl, nbs)
    variant = jnp.where(pos == 0, 1, jnp.where(pos == nbs - 1, 2, 0))
    k = jnp.concatenate([k0_ref[...], k1_ref[...], k2_ref[...], k3_ref[...]], axis=0)
    v = jnp.concatenate([v0_ref[...], v1_ref[...], v2_ref[...], v3_ref[...]], axis=0)
    qs = _stack_heads(q_ref[...])
    s = lax.dot_general(qs, k, (((1,), (1,)), ((), ())), preferred_element_type=jnp.float32)
    s = s + bias_ref[variant]
    o, lse = _softmax_pv(s, v, DIL_M)
    o_ref[...] = o.astype(o_ref.dtype)
    lse_ref[...] = _merge_heads(jnp.broadcast_to(lse, (4 * DIL_M, HEAD_BLOCK)), DIL_M)


def _dilated_group(qa, ka, va, bias, group, dil, seg_tokens, seq_lens):
    n = qa.shape[0]
    rows = n // dil
    width = dil * QA
    q2, k2, v2 = (a.reshape(rows, width) for a in (qa, ka, va))
    nb = rows // DIL_M
    nkb = rows // DIL_KBLK
    ratio = DIL_M // DIL_KBLK
    seg0_blocks = seg_tokens[0] // dil // DIL_M
    blocks_per_seq = tuple(t // dil // DIL_M for t in seq_lens)
    assert min(blocks_per_seq) >= 2
    ncol = QA // HEAD_BLOCK

    def kv_spec(t):
        return pl.BlockSpec((DIL_KBLK, HEAD_BLOCK),
                            lambda r, jb: (jnp.clip(ratio * jb - 1 + t, 0, nkb - 1), r * ncol + group))

    kern = functools.partial(_dilated_kernel, seg0_blocks=seg0_blocks, blocks_per_seq=blocks_per_seq)
    o, lse = pl.pallas_call(
        kern,
        grid=(dil, nb),
        in_specs=[pl.BlockSpec((DIL_M, HEAD_BLOCK), lambda r, jb: (jb, r * ncol + group))]
        + [kv_spec(t) for t in range(4)] + [kv_spec(t) for t in range(4)]
        + [pl.BlockSpec(bias.shape, lambda r, jb: (0, 0, 0))],
        out_specs=[pl.BlockSpec((DIL_M, HEAD_BLOCK), lambda r, jb: (jb, r)),
                   pl.BlockSpec((DIL_M, HEAD_BLOCK), lambda r, jb: (jb, r))],
        out_shape=[jax.ShapeDtypeStruct((rows, dil * HEAD_BLOCK), jnp.bfloat16),
                   jax.ShapeDtypeStruct((rows, dil * HEAD_BLOCK), jnp.float32)],
        compiler_params=_cparams(("arbitrary", "arbitrary")),
        name=f"dilated_attention_g{group}",
    )(q2, k2, k2, k2, k2, v2, v2, v2, v2, bias)
    return o.reshape(n, HEAD_BLOCK), lse.reshape(n, HEAD_BLOCK)


def _na_bias(rpb):
    qc = np.arange(GRID_W)[:, None]
    kc = np.arange(GRID_W)[None, :]
    qstart = np.clip(qc - NA_KW // 2, 0, GRID_W - NA_KW)
    valid = (kc >= qstart) & (kc < qstart + NA_KW)
    dc = np.clip(kc - qc, -(NA_KW - 1), NA_KW - 1) + NA_KW - 1
    var = np.arange(NA_KH)[:, None]
    kr = np.arange(NA_KH)[None, :]
    dr = kr - var + NA_KH - 1
    g = rpb.astype(jnp.float32)[:, dr[:, :, None, None], dc[None, None, :, :]]
    g = jnp.where(valid[None, None, None], g, NEG_INF)
    g = g.reshape(2, 4, NA_KH, NA_KH, GRID_W, GRID_W)
    g = jnp.transpose(g, (0, 2, 1, 4, 3, 5))
    return g.reshape(2, NA_KH, 4 * GRID_W, NA_KH * GRID_W)


def _na_kernel(q_ref, kp_ref, kc_ref, kn_ref, vp_ref, vc_ref, vn_ref, bias_ref, o_ref, k_scr, v_scr,
               *, seg0_blocks, blocks_per_seq):
    ib = pl.program_id(1)
    in0 = ib < seg0_blocks
    il = jnp.where(in0, ib, ib - seg0_blocks)
    nbs = jnp.where(in0, blocks_per_seq[0], blocks_per_seq[1])
    pos = lax.rem(il, nbs)
    first = pos == 0
    last = pos == nbs - 1
    blk = NA_ROWS * GRID_W
    k_scr[0:blk] = kp_ref[...]
    k_scr[blk:2 * blk] = kc_ref[...]
    k_scr[2 * blk:3 * blk] = kn_ref[...]
    v_scr[0:blk] = vp_ref[...]
    v_scr[blk:2 * blk] = vc_ref[...]
    v_scr[2 * blk:3 * blk] = vn_ref[...]
    half = NA_KH // 2
    for rr in range(NA_ROWS):
        start = jnp.where(first, max(rr + half, NA_ROWS), jnp.where(last, min(rr + half, NA_ROWS), rr + half))
        var = jnp.where(first, min(rr, half), jnp.where(last, max(rr, half), half))
        off = pl.multiple_of(start * GRID_W, GRID_W)
        kk = k_scr[pl.ds(off, NA_KH * GRID_W), :]
        vv = v_scr[pl.ds(off, NA_KH * GRID_W), :]
        qs = _stack_heads(q_ref[rr * GRID_W:(rr + 1) * GRID_W, :])
        s = lax.dot_general(qs, kk, (((1,), (1,)), ((), ())), preferred_element_type=jnp.float32)
        s = s + bias_ref[0, var]
        o, _ = _softmax_pv(s, vv, GRID_W)
        o_ref[rr * GRID_W:(rr + 1) * GRID_W, :] = o.astype(o_ref.dtype)


def _neighborhood(qb, kb, vb, bias, seg_tokens, seq_lens):
    n = qb.shape[0]
    blk = NA_ROWS * GRID_W
    nblk = n // blk
    seg0_blocks = seg_tokens[0] // blk
    blocks_per_seq = tuple(t // blk for t in seq_lens)
    assert min(blocks_per_seq) >= 2
    nset = QB // HEAD_BLOCK

    def kv_spec(t):
        return pl.BlockSpec((blk, HEAD_BLOCK), lambda c, ib: (jnp.clip(ib - 1 + t, 0, nblk - 1), c))

    kern = functools.partial(_na_kernel, seg0_blocks=seg0_blocks, blocks_per_seq=blocks_per_seq)
    return pl.pallas_call(
        kern,
        grid=(nset, nblk),
        in_specs=[pl.BlockSpec((blk, HEAD_BLOCK), lambda c, ib: (ib, c))]
        + [kv_spec(t) for t in range(3)] + [kv_spec(t) for t in range(3)]
        + [pl.BlockSpec((1,) + bias.shape[1:], lambda c, ib: (c, 0, 0, 0))],
        out_specs=pl.BlockSpec((blk, HEAD_BLOCK), lambda c, ib: (ib, c)),
        out_shape=jax.ShapeDtypeStruct((n, QB), jnp.bfloat16),
        scratch_shapes=[pltpu.VMEM((3 * blk, HEAD_BLOCK), jnp.bfloat16),
                        pltpu.VMEM((3 * blk, HEAD_BLOCK), jnp.bfloat16)],
        compiler_params=_cparams(("arbitrary", "arbitrary")),
        name="neighborhood_attention",
    )(qb, kb, kb, kb, vb, vb, vb, bias)


def _post_kernel(xp_ref, xs_ref, lng_ref, lnb_ref, o0_ref, o1_ref, o2_ref, l0_ref, l1_ref, l2_ref, ob_ref,
                 ga_ref, gb_ref, wa_ref, wb_ref, wo_ref, bo_ref, g1_ref, b1_ref, wr_ref, br_ref, tri_ref,
                 x1_ref, route_ref, counts_ref, carry_ref, *, n0_tiles):
    i = pl.program_id(0)

    @pl.when(i == 0)
    def _():
        carry_ref[...] = jnp.zeros_like(carry_ref)

    l0, l1, l2 = l0_ref[...], l1_ref[...], l2_ref[...]
    lm = jnp.maximum(jnp.maximum(l0, l1), l2)
    e0, e1, e2 = jnp.exp(l0 - lm), jnp.exp(l1 - lm), jnp.exp(l2 - lm)
    inv = 1.0 / (e0 + e1 + e2)
    o_a = (e0 * inv) * o0_ref[...].astype(jnp.float32) + (e1 * inv) * o1_ref[...].astype(jnp.float32) \
        + (e2 * inv) * o2_ref[...].astype(jnp.float32)
    y_a = jnp.dot(o_a.astype(jnp.bfloat16), wa_ref[...], preferred_element_type=jnp.float32)
    y_b = jnp.dot(ob_ref[...], wb_ref[...], preferred_element_type=jnp.float32)
    mix = jax.nn.sigmoid(ga_ref[...].astype(jnp.float32)) * y_a + jax.nn.sigmoid(gb_ref[...].astype(jnp.float32)) * y_b
    out = jnp.dot(mix.astype(jnp.bfloat16), wo_ref[...], preferred_element_type=jnp.float32) + bo_ref[...]
    x = jnp.where(i < n0_tiles, xp_ref[...], xs_ref[...])
    x0 = _layer_norm(x, lng_ref[...], lnb_ref[...])
    x1 = _layer_norm(DN_ALPHA * x0 + out, g1_ref[...], b1_ref[...])
    x1_ref[...] = x1

    r = jnp.dot(x1.astype(jnp.bfloat16), wr_ref[...], preferred_element_type=jnp.float32) + br_ref[...]
    tm = r.shape[0]
    lane = lax.broadcasted_iota(jnp.int32, (tm, ROUTE_LANES), 1)
    neg = jnp.float32(-jnp.inf)
    lg = jnp.where(lane < N_GROUPS, r, neg)
    gmax = jnp.max(lg, axis=-1, keepdims=True)
    gi = jnp.min(jnp.where(lg == gmax, lane, ROUTE_LANES), axis=-1, keepdims=True)
    wg = 1.0 / jnp.sum(jnp.exp(lg - gmax), axis=-1, keepdims=True)
    lo = EXPERT_LANE0 + EXPERTS_PER_GROUP * gi
    le = jnp.where((lane >= lo) & (lane < lo + EXPERTS_PER_GROUP), r, neg)
    m1 = jnp.max(le, axis=-1, keepdims=True)
    i1 = jnp.min(jnp.where(le == m1, lane, ROUTE_LANES), axis=-1, keepdims=True)
    le2 = jnp.where(lane == i1, neg, le)
    m2 = jnp.max(le2, axis=-1, keepdims=True)
    i2 = jnp.min(jnp.where(le2 == m2, lane, ROUTE_LANES), axis=-1, keepdims=True)
    t2 = jnp.exp(m2 - m1)
    w1 = wg / (1.0 + t2)
    w2 = wg * t2 / (1.0 + t2)

    hot1 = lane == i1
    hot2 = lane == i2
    hot = (hot1 | hot2).astype(jnp.float32)
    before = jnp.dot(tri_ref[...], hot.astype(jnp.bfloat16), preferred_element_type=jnp.float32) + carry_ref[0:1, :]
    rank1 = jnp.sum(jnp.where(hot1, before, 0.0), axis=-1, keepdims=True)
    rank2 = jnp.sum(jnp.where(hot2, before, 0.0), axis=-1, keepdims=True)
    total = carry_ref[0:1, :] + jnp.sum(hot, axis=0, keepdims=True)
    carry_ref[...] = jnp.broadcast_to(total, carry_ref.shape)
    counts_ref[...] = jnp.broadcast_to(total, counts_ref.shape)

    e1f = (i1 - EXPERT_LANE0).astype(jnp.float32)
    e2f = (i2 - EXPERT_LANE0).astype(jnp.float32)
    route = jnp.zeros((tm, ROUTE_LANES), jnp.float32)
    for idx, val in enumerate((e1f, e2f, w1, w2, rank1, rank2)):
        route = jnp.where(lane == idx, val, route)
    route_ref[...] = route


def _post_attention(xp, xs, ln_in_g, ln_in_b, o_groups, lse_groups, ob, ga, gb, wa, wb, wo, bo, g1, b1, wr, br):
    n0, d = xp.shape
    n = n0 + xs.shape[0]
    tm = POST_TM
    n0_tiles = n0 // tm
    nt = n // tm
    tri = jnp.asarray(np.tril(np.ones((tm, tm), np.float32), -1), jnp.bfloat16)

    def tok(width):
        return pl.BlockSpec((tm, width), lambda i: (i, 0))

    def full(a):
        return pl.BlockSpec(a.shape, lambda i: (0,) * a.ndim)

    kern = functools.partial(_post_kernel, n0_tiles=n0_tiles)
    return pl.pallas_call(
        kern,
        grid=(nt,),
        in_specs=[
            pl.BlockSpec((tm, d), lambda i: (jnp.minimum(i, n0_tiles - 1), 0)),
            pl.BlockSpec((tm, d), lambda i: (jnp.maximum(i - n0_tiles, 0), 0)),
            full(ln_in_g), full(ln_in_b),
            tok(HEAD_BLOCK), tok(HEAD_BLOCK), tok(HEAD_BLOCK), tok(HEAD_BLOCK), tok(HEAD_BLOCK), tok(HEAD_BLOCK),
            tok(QB), tok(d), tok(d),
            full(wa), full(wb), full(wo), full(bo), full(g1), full(b1), full(wr), full(br), full(tri),
        ],
        out_specs=[tok(d), tok(ROUTE_LANES), pl.BlockSpec((8, ROUTE_LANES), lambda i: (0, 0))],
        out_shape=[jax.ShapeDtypeStruct((n, d), jnp.float32),
                   jax.ShapeDtypeStruct((n, ROUTE_LANES), jnp.float32),
                   jax.ShapeDtypeStruct((8, ROUTE_LANES), jnp.float32)],
        scratch_shapes=[pltpu.VMEM((8, ROUTE_LANES), jnp.float32)],
        compiler_params=_cparams(("arbitrary",)),
        name="post_attention_router",
    )(xp, xs, ln_in_g, ln_in_b, *o_groups, *lse_groups, ob, ga, gb, wa, wb, wo, bo, g1, b1, wr, br, tri)


def _dispatch_kernel(pos_ref, x_ref, xs_ref, sem):
    tt = x_ref.shape[0]

    def issue(t, carry):
        for slot in range(2):
            p = pos_ref[0, slot, t]
            pltpu.make_async_copy(x_ref.at[pl.ds(t, 1)], xs_ref.at[pl.ds(p, 1)], sem).start()
        return carry

    lax.fori_loop(0, tt, issue, 0, unroll=8)
    for _ in range(2):
        pltpu.make_async_copy(x_ref, xs_ref.at[pl.ds(0, tt)], sem).wait()


def _dispatch(x1, pos, n_rows):
    n, d = x1.shape
    tt = DISP_TT
    return pl.pallas_call(
        _dispatch_kernel,
        grid=(n // tt,),
        in_specs=[pl.BlockSpec((1, 2, tt), lambda i: (i, 0, 0), memory_space=pltpu.SMEM),
                  pl.BlockSpec((tt, d), lambda i: (i, 0))],
        out_specs=pl.BlockSpec(memory_space=pl.ANY),
        out_shape=jax.ShapeDtypeStruct((n_rows, d), x1.dtype),
        scratch_shapes=[pltpu.SemaphoreType.DMA(())],
        compiler_params=_cparams(("arbitrary",)),
        name="moe_dispatch",
    )(pos.reshape(n // tt, tt, 2).transpose(0, 2, 1), x1)


def _ffn_kernel(item_expert_ref, item_tile_ref, item_lo_ref, item_hi_ref, n_items_ref,
                x_ref, wg_ref, wu_ref, wd_ref, y_ref):
    j = pl.program_id(0)

    @pl.when(j < n_items_ref[0])
    def _():
        x = x_ref[...].astype(jnp.bfloat16)
        a = jnp.dot(x, wg_ref[0], preferred_element_type=jnp.float32)
        u = jnp.dot(x, wu_ref[0], preferred_element_type=jnp.float32)
        h = (a * jax.nn.sigmoid(a) * u).astype(jnp.bfloat16)
        y = jnp.dot(h, wd_ref[0], preferred_element_type=jnp.float32)
        rows = lax.broadcasted_iota(jnp.int32, (y.shape[0], 1), 0)
        y = jnp.where((rows >= item_lo_ref[j]) & (rows < item_hi_ref[j]), y, 0.0)
        first_of_tile = jnp.logical_or(j == 0, item_tile_ref[j] != item_tile_ref[jnp.maximum(j - 1, 0)])

        @pl.when(first_of_tile)
        def _():
            y_ref[...] = y

        @pl.when(jnp.logical_not(first_of_tile))
        def _():
            y_ref[...] += y


def _ffn_items(cnt, n_rows):
    tm = FFN_TM
    n_tiles = n_rows // tm
    max_items = n_tiles + N_EXPERTS - 1
    ends = jnp.cumsum(cnt)
    starts = ends - cnt
    first_tile = starts // tm
    last_tile = jnp.maximum(ends - 1, 0) // tm
    items_per_expert = jnp.where(cnt > 0, last_tile - first_tile + 1, 0)
    item_ends = jnp.cumsum(items_per_expert)
    n_items = item_ends[-1:]
    j = jnp.minimum(jnp.arange(max_items, dtype=jnp.int32), n_items[0] - 1)
    expert = jnp.minimum(jnp.searchsorted(item_ends, j, side="right"), N_EXPERTS - 1).astype(jnp.int32)
    tile = first_tile[expert] + (j - (item_ends - items_per_expert)[expert])
    lo = jnp.clip(starts[expert] - tile * tm, 0, tm)
    hi = jnp.clip(ends[expert] - tile * tm, 0, tm)
    i32 = lambda a: a.astype(jnp.int32)
    return i32(expert), i32(tile), i32(lo), i32(hi), i32(n_items), starts


def _grouped_ffn(xs, items, w_gate, w_up, w_down):
    n_rows, d = xs.shape
    tm = FFN_TM
    de = w_gate.shape[2]
    item_expert, item_tile, item_lo, item_hi, n_items = items
    max_items = item_expert.shape[0]

    def row_map(j, ie, it, lo, hi, ni):
        return (it[j], 0)

    def w_map(j, ie, it, lo, hi, ni):
        return (ie[j], 0, 0)

    grid_spec = pltpu.PrefetchScalarGridSpec(
        num_scalar_prefetch=5,
        grid=(max_items,),
        in_specs=[pl.BlockSpec((tm, d), row_map),
                  pl.BlockSpec((1, d, de), w_map),
                  pl.BlockSpec((1, d, de), w_map),
                  pl.BlockSpec((1, de, d), w_map)],
        out_specs=pl.BlockSpec((tm, d), row_map),
    )
    return pl.pallas_call(
        _ffn_kernel,
        grid_spec=grid_spec,
        out_shape=jax.ShapeDtypeStruct((n_rows, d), jnp.float32),
        compiler_params=_cparams(("arbitrary",)),
        name="moe_grouped_ffn",
    )(item_expert, item_tile, item_lo, item_hi, n_items, xs, w_gate, w_up, w_down)


def _combine_kernel(pos_ref, x1_ref, route_ref, g2_ref, b2_ref, ys_ref, o_ref, buf, sem):
    tt = x1_ref.shape[0]

    def issue(t, carry):
        for slot in range(2):
            p = pos_ref[0, slot, t]
            pltpu.make_async_copy(ys_ref.at[pl.ds(p, 1)], buf.at[slot, pl.ds(t, 1)], sem).start()
        return carry

    lax.fori_loop(0, tt, issue, 0, unroll=8)
    for slot in range(2):
        pltpu.make_async_copy(ys_ref.at[pl.ds(0, tt)], buf.at[slot], sem).wait()
    route = route_ref[...]
    y = route[:, 2:3] * buf[0] + route[:, 3:4] * buf[1]
    o_ref[...] = _layer_norm(DN_ALPHA * x1_ref[...] + y, g2_ref[...], b2_ref[...])


def _combine(x1, route, pos, ys, g2, b2, tok0, n_tok):
    d = x1.shape[1]
    tt = COMB_TT
    t0 = tok0 // tt
    pos3 = pos.reshape(pos.shape[0] // tt, tt, 2).transpose(0, 2, 1)
    return pl.pallas_call(
        _combine_kernel,
        grid=(n_tok // tt,),
        in_specs=[pl.BlockSpec((1, 2, tt), lambda i: (i + t0, 0, 0), memory_space=pltpu.SMEM),
                  pl.BlockSpec((tt, d), lambda i: (i + t0, 0)),
                  pl.BlockSpec((tt, ROUTE_LANES), lambda i: (i + t0, 0)),
                  pl.BlockSpec((1, d), lambda i: (0, 0)),
                  pl.BlockSpec((1, d), lambda i: (0, 0)),
                  pl.BlockSpec(memory_space=pl.ANY)],
        out_specs=pl.BlockSpec((tt, d), lambda i: (i, 0)),
        out_shape=jax.ShapeDtypeStruct((n_tok, d), jnp.float32),
        scratch_shapes=[pltpu.VMEM((2, tt, d), ys.dtype), pltpu.SemaphoreType.DMA(())],
        compiler_params=_cparams(("arbitrary",)),
        name="moe_combine_ln",
    )(pos3, x1, route, g2, b2, ys)


def _layer(x_prompt, x_sample, ln_in_g, ln_in_b, w_in, b_in, rel_bias_t5, na_rpb, w_branch_a, w_branch_b,
           w_out, b_out, ln1_g, ln1_b, w_router_group, b_router_group, w_router_expert, b_router_expert,
           w_exp_gate, w_exp_up, w_exp_down, ln2_g, ln2_b):
    bp, tp, d = x_prompt.shape
    bs, ts, _ = x_sample.shape
    seg_tokens = (bp * tp, bs * ts)
    seq_lens = (tp, ts)
    n = sum(seg_tokens)
    xp = x_prompt.reshape(seg_tokens[0], d)
    xs = x_sample.reshape(seg_tokens[1], d)
    row = lambda a: a.reshape(1, -1).astype(jnp.float32)
    bf = lambda a: a.astype(jnp.bfloat16)

    scale = HEAD_DIM ** -0.5
    col_scale = np.ones((w_in.shape[2],), np.float32)
    col_scale[0:QA] = scale
    col_scale[3 * QA:3 * QA + QB] = scale
    w_proj = bf(w_in[0] * col_scale)
    b_proj = row(b_in[0] * col_scale)
    qa, ka, va, qb, kb, vb, ga, gb = _in_projection(xp, xs, row(ln_in_g), row(ln_in_b), w_proj, b_proj)

    o_groups, lse_groups = [], []
    for g, (window, dil) in enumerate(DIL_PATTERNS):
        assert window // (2 * dil) == HALF_SPAN
        o, lse = _dilated_group(qa, ka, va, _dilated_bias(rel_bias_t5, g, dil), g, dil, seg_tokens, seq_lens)
        o_groups.append(o)
        lse_groups.append(lse)
    ob = _neighborhood(qb, kb, vb, _na_bias(na_rpb[0]), seg_tokens, seq_lens)

    wr = jnp.zeros((d, ROUTE_LANES), jnp.float32)
    wr = wr.at[:, 0:N_GROUPS].set(w_router_group[0]).at[:, EXPERT_LANE0:EXPERT_LANE0 + N_EXPERTS].set(w_router_expert[0])
    br = jnp.zeros((1, ROUTE_LANES), jnp.float32)
    br = br.at[0, 0:N_GROUPS].set(b_router_group[0]).at[0, EXPERT_LANE0:EXPERT_LANE0 + N_EXPERTS].set(b_router_expert[0])
    x1, route, counts = _post_attention(
        xp, xs, row(ln_in_g), row(ln_in_b), o_groups, lse_groups, ob, ga, gb, bf(w_branch_a[0]), bf(w_branch_b[0]),
        bf(w_out[0]), row(b_out[0]), row(ln1_g[0]), row(ln1_b[0]), bf(wr), br)

    experts = route[:, 0:2].astype(jnp.int32)
    ranks = route[:, 4:6].astype(jnp.int32)
    cnt = counts[0, EXPERT_LANE0:EXPERT_LANE0 + N_EXPERTS].astype(jnp.int32)
    n_rows = 2 * n
    *items, starts = _ffn_items(cnt, n_rows)
    pos = starts[experts] + ranks

    xsorted = _dispatch(x1, pos, n_rows)
    ys = _grouped_ffn(xsorted, items, bf(w_exp_gate[0]), bf(w_exp_up[0]), bf(w_exp_down[0]))
    y_prompt = _combine(x1, route, pos, ys, row(ln2_g[0]), row(ln2_b[0]), 0, seg_tokens[0])
    y_sample = _combine(x1, route, pos, ys, row(ln2_g[0]), row(ln2_b[0]), seg_tokens[0], seg_tokens[1])
    return y_prompt.reshape(bp, tp, d), y_sample.reshape(bs, ts, d)


def kernel(x_prompt, x_sample, ln_in_g, ln_in_b, w_in, b_in, rel_bias_t5, na_rpb, w_branch_a, w_branch_b, w_out, b_out, ln1_g, ln1_b, w_router_group, b_router_group, w_router_expert, b_router_expert, w_exp_gate, w_exp_up, w_exp_down, ln2_g, ln2_b):
    return _layer(x_prompt, x_sample, ln_in_g, ln_in_b, w_in, b_in, rel_bias_t5, na_rpb, w_branch_a, w_branch_b,
                  w_out, b_out, ln1_g, ln1_b, w_router_group, b_router_group, w_router_expert, b_router_expert,
                  w_exp_gate, w_exp_up, w_exp_down, ln2_g, ln2_b)
```

```python
import functools
import math

import numpy as np
import jax
import jax.numpy as jnp
from jax import lax
from jax.experimental import pallas as pl
from jax.experimental.pallas import tpu as pltpu

HEAD_DIM = 64
DIL_PATTERNS = ((128, 1), (512, 4), (2048, 16))
HEADS_PER_DIL_GROUP = 4
N_DIL_GROUPS = 3
H_A = N_DIL_GROUPS * HEADS_PER_DIL_GROUP
H_B = 8
QA = H_A * HEAD_DIM
QB = H_B * HEAD_DIM
HALF_SPAN = 64
GRID_W = 64
NA_KH = 8
NA_KW = 16
N_BUCKETS = 32
REL_MAX_DIST = 1024
N_GROUPS = 4
EXPERTS_PER_GROUP = 8
N_EXPERTS = N_GROUPS * EXPERTS_PER_GROUP
LN_EPS = 1e-5
NEG_INF = -1e30
DEPTH = 1
DN_ALPHA = (2.0 * DEPTH) ** 0.25

LANES = 128
HEAD_BLOCK = 4 * HEAD_DIM
VMEM_LIMIT_BYTES = 56 * 1024 * 1024

PROJ_TM = 512
DIL_M = 128
DIL_KBLK = 64
NA_ROWS = 8
POST_TM = 512
FFN_TM = 512
DISP_TT = 512
COMB_TT = 256
ROUTE_LANES = LANES
EXPERT_LANE0 = 32


def _cparams(sem):
    return pltpu.CompilerParams(dimension_semantics=sem, vmem_limit_bytes=VMEM_LIMIT_BYTES)


def _layer_norm(x, g, b):
    mu = jnp.mean(x, axis=-1, keepdims=True)
    xc = x - mu
    var = jnp.mean(xc * xc, axis=-1, keepdims=True)
    return xc * lax.rsqrt(var + LN_EPS) * g + b


def _strided_rows(scr, slab0, r, count, stride):
    return jnp.concatenate([scr[slab0 + s, pl.ds(r, count, stride=stride), :] for s in range(2)], axis=1)


def _inproj_kernel(xp_ref, xs_ref, g_ref, b_ref, w_ref, bias_ref, *refs, n0_tiles):
    out_refs, scr = refs[:-1], refs[-1]
    i = pl.program_id(0)
    x = jnp.where(i < n0_tiles, xp_ref[...], xs_ref[...])
    xn = _layer_norm(x, g_ref[...], b_ref[...]).astype(jnp.bfloat16)
    tm = xn.shape[0]

    def proj(c0, cw):
        return jnp.dot(xn, w_ref[:, c0:c0 + cw], preferred_element_type=jnp.float32) + bias_ref[:, c0:c0 + cw]

    n_dil = 3 * N_DIL_GROUPS
    for idx in range(n_dil):
        g = idx % N_DIL_GROUPS
        dil = DIL_PATTERNS[g][1]
        o_ref = out_refs[idx]
        y = proj(idx * HEAD_BLOCK, HEAD_BLOCK)
        if dil == 1:
            o_ref[0, 0] = y.astype(o_ref.dtype)
        else:
            slab0 = 2 * (idx % 2)
            scr[slab0] = y[:, :LANES]
            scr[slab0 + 1] = y[:, LANES:]
            for r in range(dil):
                o_ref[0, r] = _strided_rows(scr, slab0, r, tm // dil, dil).astype(o_ref.dtype)
    col = n_dil * HEAD_BLOCK
    for o_ref in out_refs[n_dil:]:
        width = o_ref.shape[1]
        for c in range(0, width, 512):
            cw = min(512, width - c)
            o_ref[:, c:c + cw] = proj(col + c, cw).astype(o_ref.dtype)
        col += width


def _in_projection(xp, xs, ln_g, ln_b, w, b):
    n0, d_model = xp.shape
    n1 = xs.shape[0]
    n = n0 + n1
    tm = PROJ_TM
    n0_tiles, n1_tiles = n0 // tm, n1 // tm
    nt = n0_tiles + n1_tiles
    out_shape, out_specs = [], []
    for _ in range(3):
        for _, dil in DIL_PATTERNS:
            out_shape.append(jax.ShapeDtypeStruct((nt, dil, tm // dil, HEAD_BLOCK), jnp.bfloat16))
            out_specs.append(pl.BlockSpec((1, dil, tm // dil, HEAD_BLOCK), lambda i: (i, 0, 0, 0)))
    for wd in (QB, QB, QB, d_model, d_model):
        out_shape.append(jax.ShapeDtypeStruct((n, wd), jnp.bfloat16))
        out_specs.append(pl.BlockSpec((tm, wd), lambda i: (i, 0)))
    d_in = w.shape[1]
    kern = functools.partial(_inproj_kernel, n0_tiles=n0_tiles)
    return pl.pallas_call(
        kern,
        grid=(nt,),
        in_specs=[
            pl.BlockSpec((tm, d_model), lambda i: (jnp.minimum(i, n0_tiles - 1), 0)),
            pl.BlockSpec((tm, d_model), lambda i: (jnp.maximum(i - n0_tiles, 0), 0)),
            pl.BlockSpec((1, d_model), lambda i: (0, 0)),
            pl.BlockSpec((1, d_model), lambda i: (0, 0)),
            pl.BlockSpec((d_model, d_in), lambda i: (0, 0)),
            pl.BlockSpec((1, d_in), lambda i: (0, 0)),
        ],
        out_specs=out_specs,
        out_shape=out_shape,
        scratch_shapes=[pltpu.VMEM((4, tm, LANES), jnp.float32)],
        compiler_params=_cparams(("arbitrary",)),
        name="in_projection",
    )(xp, xs, ln_g, ln_b, w, b)


def _t5_bucket_np(rel):
    half = N_BUCKETS // 2
    max_exact = half // 2
    ret = np.where(rel > 0, half, 0)
    n = np.abs(rel)
    nf = np.maximum(n, 1).astype(np.float32)
    large = max_exact + (np.log(nf / np.float32(max_exact)) / np.float32(math.log(REL_MAX_DIST / max_exact))
                         * np.float32(half - max_exact)).astype(np.int32)
    large = np.minimum(large, half - 1)
    return ret + np.where(n < max_exact, n, large)


def _t5_bucket_starts():
    half = N_BUCKETS // 2
    dist = np.arange(0, HALF_SPAN * max(d for _, d in DIL_PATTERNS) + 1)
    buckets = _t5_bucket_np(-dist)
    return [int(np.argmax(buckets >= k)) for k in range(half)]


def _build_dilated_bias(table_ref, bias_scr, dil):
    half = N_BUCKETS // 2
    starts = _t5_bucket_starts()
    kb = DIL_M + 2 * HALF_SPAN
    i = lax.broadcasted_iota(jnp.int32, (DIL_M, kb), 0)
    m = lax.broadcasted_iota(jnp.int32, (DIL_M, kb), 1)
    delta = m - HALF_SPAN - i
    dist = jnp.abs(delta) * dil
    band = jnp.abs(delta) <= HALF_SPAN
    for h in range(HEADS_PER_DIL_GROUP):
        sides = []
        for side in range(2):
            val = jnp.full((DIL_M, kb), table_ref[(side * half + half - 1) * HEADS_PER_DIL_GROUP + h], jnp.float32)
            for k in range(half - 1, 0, -1):
                val = jnp.where(dist < starts[k], table_ref[(side * half + k - 1) * HEADS_PER_DIL_GROUP + h], val)
            sides.append(val)
        base = jnp.where(band, jnp.where(delta > 0, sides[1], sides[0]), NEG_INF)
        rows = slice(h * DIL_M, (h + 1) * DIL_M)
        bias_scr[0, rows, :] = base
        bias_scr[1, rows, :] = jnp.where(m >= HALF_SPAN, base, NEG_INF)
        bias_scr[2, rows, :] = jnp.where(m < kb - HALF_SPAN, base, NEG_INF)


def _stack_heads(q):
    head_of_col = lax.broadcasted_iota(jnp.int32, (1, HEAD_BLOCK), 1) // HEAD_DIM
    zero = jnp.zeros_like(q)
    return jnp.concatenate([jnp.where(head_of_col == h, q, zero) for h in range(4)], axis=0)


def _merge_heads(x, m):
    head_of_col = lax.broadcasted_iota(jnp.int32, (1, HEAD_BLOCK), 1) // HEAD_DIM
    out = jnp.zeros((m, HEAD_BLOCK), x.dtype)
    for h in range(4):
        out = jnp.where(head_of_col == h, x[h * m:(h + 1) * m], out)
    return out


def _softmax_pv(s, v, m_rows):
    mx = jnp.max(s, axis=-1, keepdims=True)
    p = jnp.exp(s - mx)
    l = jnp.sum(p, axis=-1, keepdims=True)
    pv = jnp.dot(p.astype(jnp.bfloat16), v, preferred_element_type=jnp.float32)
    o = _merge_heads(pv * (1.0 / l), m_rows)
    return o, mx + jnp.log(l)


def _dilated_kernel(table_ref, q_ref, k0_ref, k1_ref, k2_ref, k3_ref, v0_ref, v1_ref, v2_ref, v3_ref,
                    o_ref, lse_ref, bias_scr, *, dil, seg0_blocks, blocks_per_seq):
    jb = pl.program_id(1)

    @pl.when((pl.program_id(0) == 0) & (jb == 0))
    def _():
        _build_dilated_bias(table_ref, bias_scr, dil)

    in0 = jb < seg0_blocks
    jl = jnp.where(in0, jb, jb - seg0_blocks)
    nbs = jnp.where(in0, blocks_per_seq[0], blocks_per_seq[1])
    pos = lax.rem(jl, nbs)
    variant = jnp.where(pos == 0, 1, jnp.where(pos == nbs - 1, 2, 0))
    rows2d = lambda ref: ref[...].reshape(-1, HEAD_BLOCK)
    k = jnp.concatenate([rows2d(r) for r in (k0_ref, k1_ref, k2_ref, k3_ref)], axis=0)
    v = jnp.concatenate([rows2d(r) for r in (v0_ref, v1_ref, v2_ref, v3_ref)], axis=0)
    qs = _stack_heads(rows2d(q_ref))
    s = lax.dot_general(qs, k, (((1,), (1,)), ((), ())), preferred_element_type=jnp.float32)
    s = s + bias_scr[variant]
    o, lse = _softmax_pv(s, v, DIL_M)
    o_ref[...] = o.astype(o_ref.dtype).reshape(o_ref.shape)
    lse_ref[...] = _merge_heads(jnp.broadcast_to(lse, (4 * DIL_M, HEAD_BLOCK)), DIL_M).reshape(lse_ref.shape)


def _residue_spec(rows_blk, dil, block_of):
    per_tile = PROJ_TM // dil
    if per_tile >= rows_blk:
        per = per_tile // rows_blk
        return pl.BlockSpec((1, None, rows_blk, HEAD_BLOCK),
                            lambda r, jb: (block_of(jb) // per, r, block_of(jb) % per, 0))
    return pl.BlockSpec((rows_blk // per_tile, None, per_tile, HEAD_BLOCK), lambda r, jb: (block_of(jb), r, 0, 0))


def _dilated_group(q, k, v, table, dil, seg_tokens, seq_lens):
    n = q.shape[0] * PROJ_TM
    rows = n // dil
    nb = rows // DIL_M
    nkb = rows // DIL_KBLK
    ratio = DIL_M // DIL_KBLK
    seg0_blocks = seg_tokens[0] // dil // DIL_M
    blocks_per_seq = tuple(t // dil // DIL_M for t in seq_lens)
    assert min(blocks_per_seq) >= 2

    def kv_spec(t):
        return _residue_spec(DIL_KBLK, dil, lambda jb: jnp.clip(ratio * jb - 1 + t, 0, nkb - 1))

    qo_spec = _residue_spec(DIL_M, dil, lambda jb: jb)
    kern = functools.partial(_dilated_kernel, dil=dil, seg0_blocks=seg0_blocks, blocks_per_seq=blocks_per_seq)
    return pl.pallas_call(
        kern,
        grid=(dil, nb),
        in_specs=[pl.BlockSpec(memory_space=pltpu.SMEM), qo_spec]
        + [kv_spec(t) for t in range(4)] + [kv_spec(t) for t in range(4)],
        out_specs=[qo_spec, qo_spec],
        out_shape=[jax.ShapeDtypeStruct(q.shape, jnp.bfloat16), jax.ShapeDtypeStruct(q.shape, jnp.float32)],
        scratch_shapes=[pltpu.VMEM((3, HEADS_PER_DIL_GROUP * DIL_M, DIL_M + 2 * HALF_SPAN), jnp.float32)],
        compiler_params=_cparams(("arbitrary", "arbitrary")),
        name=f"dilated_attention_d{dil}",
    )(table, q, k, k, k, k, v, v, v, v)


NA_DR = 2 * NA_KH - 1
NA_DC = 2 * NA_KW - 1
NA_PAIRS = NA_DR - 1


def _build_na_bias(rpb_ref, pair_scr, head0):
    qc = lax.broadcasted_iota(jnp.int32, (GRID_W, 2 * GRID_W), 0)
    lane = lax.broadcasted_iota(jnp.int32, (GRID_W, 2 * GRID_W), 1)
    kc = lane % GRID_W
    upper = lane >= GRID_W
    qstart = jnp.clip(qc - NA_KW // 2, 0, GRID_W - NA_KW)
    valid = (kc >= qstart) & (kc < qstart + NA_KW)
    dc = jnp.clip(kc - qc, -(NA_KW - 1), NA_KW - 1) + NA_KW - 1

    def one_pair(idx, carry):
        h = idx // NA_PAIRS
        dr = idx % NA_PAIRS
        base = ((head0 + h) * NA_DR + dr) * NA_DC
        val = jnp.zeros((GRID_W, 2 * GRID_W), jnp.float32)
        for c in range(NA_DC):
            val = jnp.where(dc == c, jnp.where(upper, rpb_ref[base + NA_DC + c], rpb_ref[base + c]), val)
        pair_scr[h, dr] = jnp.where(valid, val, NEG_INF)
        return carry

    lax.fori_loop(0, 4 * NA_PAIRS, one_pair, 0)


def _na_kernel(rpb_ref, q_ref, kp_ref, kc_ref, kn_ref, vp_ref, vc_ref, vn_ref, o_ref, k_scr, v_scr, pair_scr,
               *, seg0_blocks, blocks_per_seq):
    ib = pl.program_id(1)

    @pl.when(ib == 0)
    def _():
        _build_na_bias(rpb_ref, pair_scr, pl.program_id(0) * 4)

    in0 = ib < seg0_blocks
    il = jnp.where(in0, ib, ib - seg0_blocks)
    nbs = jnp.where(in0, blocks_per_seq[0], blocks_per_seq[1])
    pos = lax.rem(il, nbs)
    first = pos == 0
    last = pos == nbs - 1
    blk = NA_ROWS * GRID_W
    k_scr[0:blk] = kp_ref[...]
    k_scr[blk:2 * blk] = kc_ref[...]
    k_scr[2 * blk:3 * blk] = kn_ref[...]
    v_scr[0:blk] = vp_ref[...]
    v_scr[blk:2 * blk] = vc_ref[...]
    v_scr[2 * blk:3 * blk] = vn_ref[...]
    half = NA_KH // 2
    for rr in range(NA_ROWS):
        start = jnp.where(first, max(rr + half, NA_ROWS), jnp.where(last, min(rr + half, NA_ROWS), rr + half))
        var = jnp.where(first, min(rr, half), jnp.where(last, max(rr, half), half))
        off = pl.multiple_of(start * GRID_W, GRID_W)
        kk = k_scr[pl.ds(off, NA_KH * GRID_W), :]
        vv = v_scr[pl.ds(off, NA_KH * GRID_W), :]
        qs = _stack_heads(q_ref[rr * GRID_W:(rr + 1) * GRID_W, :])
        s = lax.dot_general(qs, kk, (((1,), (1,)), ((), ())), preferred_element_type=jnp.float32)
        bias = jnp.concatenate(
            [jnp.concatenate([pair_scr[h, 2 * p - var + NA_KH - 1] for p in range(NA_KH // 2)], axis=1)
             for h in range(4)], axis=0)
        s = s + bias
        o, _ = _softmax_pv(s, vv, GRID_W)
        o_ref[rr * GRID_W:(rr + 1) * GRID_W, :] = o.astype(o_ref.dtype)


def _neighborhood(qb, kb, vb, rpb_flat, seg_tokens, seq_lens):
    n = qb.shape[0]
    blk = NA_ROWS * GRID_W
    nblk = n // blk
    seg0_blocks = seg_tokens[0] // blk
    blocks_per_seq = tuple(t // blk for t in seq_lens)
    assert min(blocks_per_seq) >= 2
    nset = QB // HEAD_BLOCK

    def kv_spec(t):
        return pl.BlockSpec((blk, HEAD_BLOCK), lambda c, ib: (jnp.clip(ib - 1 + t, 0, nblk - 1), c))

    kern = functools.partial(_na_kernel, seg0_blocks=seg0_blocks, blocks_per_seq=blocks_per_seq)
    return pl.pallas_call(
        kern,
        grid=(nset, nblk),
        in_specs=[pl.BlockSpec(memory_space=pltpu.SMEM), pl.BlockSpec((blk, HEAD_BLOCK), lambda c, ib: (ib, c))]
        + [kv_spec(t) for t in range(3)] + [kv_spec(t) for t in range(3)],
        out_specs=pl.BlockSpec((blk, HEAD_BLOCK), lambda c, ib: (ib, c)),
        out_shape=jax.ShapeDtypeStruct((n, QB), jnp.bfloat16),
        scratch_shapes=[pltpu.VMEM((3 * blk, HEAD_BLOCK), jnp.bfloat16),
                        pltpu.VMEM((3 * blk, HEAD_BLOCK), jnp.bfloat16),
                        pltpu.VMEM((4, NA_PAIRS, GRID_W, 2 * GRID_W), jnp.float32)],
        compiler_params=_cparams(("arbitrary", "arbitrary")),
        name="neighborhood_attention",
    )(rpb_flat, qb, kb, kb, kb, vb, vb, vb)


def _post_kernel(xp_ref, xs_ref, lng_ref, lnb_ref, o0_ref, o1_ref, o2_ref, l0_ref, l1_ref, l2_ref, ob_ref,
                 ga_ref, gb_ref, wa_ref, wb_ref, wo_ref, bo_ref, g1_ref, b1_ref, wr_ref, br_ref, tri_ref,
                 x1_ref, route_ref, counts_ref, carry_ref, perm_scr, *, n0_tiles):
    i = pl.program_id(0)

    @pl.when(i == 0)
    def _():
        carry_ref[...] = jnp.zeros_like(carry_ref)

    def token_order(ref, slab0):
        dil, per = ref.shape[1], ref.shape[2]
        if dil == 1:
            return ref[0, 0].astype(jnp.float32)
        for r in range(dil):
            val = ref[0, r].astype(jnp.float32)
            for s in range(2):
                perm_scr[slab0 + s, pl.ds(r, per, stride=dil), :] = val[:, s * LANES:(s + 1) * LANES]
        return jnp.concatenate([perm_scr[slab0], perm_scr[slab0 + 1]], axis=1)

    l0, l1, l2 = (token_order(ref, 2 * j) for j, ref in enumerate((l0_ref, l1_ref, l2_ref)))
    lm = jnp.maximum(jnp.maximum(l0, l1), l2)
    e0, e1, e2 = jnp.exp(l0 - lm), jnp.exp(l1 - lm), jnp.exp(l2 - lm)
    inv = 1.0 / (e0 + e1 + e2)
    o0, o1, o2 = (token_order(ref, 6 + 2 * j) for j, ref in enumerate((o0_ref, o1_ref, o2_ref)))
    o_a = (e0 * inv) * o0 + (e1 * inv) * o1 + (e2 * inv) * o2
    y_a = jnp.dot(o_a.astype(jnp.bfloat16), wa_ref[...], preferred_element_type=jnp.float32)
    y_b = jnp.dot(ob_ref[...], wb_ref[...], preferred_element_type=jnp.float32)
    mix = jax.nn.sigmoid(ga_ref[...].astype(jnp.float32)) * y_a + jax.nn.sigmoid(gb_ref[...].astype(jnp.float32)) * y_b
    out = jnp.dot(mix.astype(jnp.bfloat16), wo_ref[...], preferred_element_type=jnp.float32) + bo_ref[...]
    x = jnp.where(i < n0_tiles, xp_ref[...], xs_ref[...])
    x0 = _layer_norm(x, lng_ref[...], lnb_ref[...])
    x1 = _layer_norm(DN_ALPHA * x0 + out, g1_ref[...], b1_ref[...])
    x1_ref[...] = x1

    r = jnp.dot(x1.astype(jnp.bfloat16), wr_ref[...], preferred_element_type=jnp.float32) + br_ref[...]
    tm = r.shape[0]
    lane = lax.broadcasted_iota(jnp.int32, (tm, ROUTE_LANES), 1)
    neg = jnp.float32(-jnp.inf)
    lg = jnp.where(lane < N_GROUPS, r, neg)
    gmax = jnp.max(lg, axis=-1, keepdims=True)
    gi = jnp.min(jnp.where(lg == gmax, lane, ROUTE_LANES), axis=-1, keepdims=True)
    wg = 1.0 / jnp.sum(jnp.exp(lg - gmax), axis=-1, keepdims=True)
    lo = EXPERT_LANE0 + EXPERTS_PER_GROUP * gi
    le = jnp.where((lane >= lo) & (lane < lo + EXPERTS_PER_GROUP), r, neg)
    m1 = jnp.max(le, axis=-1, keepdims=True)
    i1 = jnp.min(jnp.where(le == m1, lane, ROUTE_LANES), axis=-1, keepdims=True)
    le2 = jnp.where(lane == i1, neg, le)
    m2 = jnp.max(le2, axis=-1, keepdims=True)
    i2 = jnp.min(jnp.where(le2 == m2, lane, ROUTE_LANES), axis=-1, keepdims=True)
    t2 = jnp.exp(m2 - m1)
    w1 = wg / (1.0 + t2)
    w2 = wg * t2 / (1.0 + t2)

    hot1 = lane == i1
    hot2 = lane == i2
    hot = (hot1 | hot2).astype(jnp.float32)
    before = jnp.dot(tri_ref[...], hot.astype(jnp.bfloat16), preferred_element_type=jnp.float32) + carry_ref[0:1, :]
    rank1 = jnp.sum(jnp.where(hot1, before, 0.0), axis=-1, keepdims=True)
    rank2 = jnp.sum(jnp.where(hot2, before, 0.0), axis=-1, keepdims=True)
    total = carry_ref[0:1, :] + jnp.sum(hot, axis=0, keepdims=True)
    carry_ref[...] = jnp.broadcast_to(total, carry_ref.shape)
    counts_ref[...] = jnp.broadcast_to(total, counts_ref.shape)

    e1f = (i1 - EXPERT_LANE0).astype(jnp.float32)
    e2f = (i2 - EXPERT_LANE0).astype(jnp.float32)
    route = jnp.zeros((tm, ROUTE_LANES), jnp.float32)
    for idx, val in enumerate((e1f, e2f, w1, w2, rank1, rank2)):
        route = jnp.where(lane == idx, val, route)
    route_ref[...] = route


def _post_attention(xp, xs, ln_in_g, ln_in_b, o_groups, lse_groups, ob, ga, gb, wa, wb, wo, bo, g1, b1, wr, br):
    n0, d = xp.shape
    n = n0 + xs.shape[0]
    tm = POST_TM
    n0_tiles = n0 // tm
    nt = n // tm
    tri = jnp.asarray(np.tril(np.ones((tm, tm), np.float32), -1), jnp.bfloat16)

    def tok(width):
        return pl.BlockSpec((tm, width), lambda i: (i, 0))

    def full(a):
        return pl.BlockSpec(a.shape, lambda i: (0,) * a.ndim)

    kern = functools.partial(_post_kernel, n0_tiles=n0_tiles)
    return pl.pallas_call(
        kern,
        grid=(nt,),
        in_specs=[
            pl.BlockSpec((tm, d), lambda i: (jnp.minimum(i, n0_tiles - 1), 0)),
            pl.BlockSpec((tm, d), lambda i: (jnp.maximum(i - n0_tiles, 0), 0)),
            full(ln_in_g), full(ln_in_b),
            *[pl.BlockSpec((1,) + a.shape[1:], lambda i: (i, 0, 0, 0)) for a in (*o_groups, *lse_groups)],
            tok(QB), tok(d), tok(d),
            full(wa), full(wb), full(wo), full(bo), full(g1), full(b1), full(wr), full(br), full(tri),
        ],
        out_specs=[tok(d), tok(ROUTE_LANES), pl.BlockSpec((8, ROUTE_LANES), lambda i: (0, 0))],
        out_shape=[jax.ShapeDtypeStruct((n, d), jnp.float32),
                   jax.ShapeDtypeStruct((n, ROUTE_LANES), jnp.float32),
                   jax.ShapeDtypeStruct((8, ROUTE_LANES), jnp.float32)],
        scratch_shapes=[pltpu.VMEM((8, ROUTE_LANES), jnp.float32), pltpu.VMEM((12, tm, LANES), jnp.float32)],
        compiler_params=_cparams(("arbitrary",)),
        name="post_attention_router",
    )(xp, xs, ln_in_g, ln_in_b, *o_groups, *lse_groups, ob, ga, gb, wa, wb, wo, bo, g1, b1, wr, br, tri)


def _dispatch_kernel(pos_ref, x_ref, xs_ref, sem):
    tt = x_ref.shape[0]

    def issue(t, carry):
        for slot in range(2):
            p = pos_ref[0, slot, t]
            pltpu.make_async_copy(x_ref.at[pl.ds(t, 1)], xs_ref.at[pl.ds(p, 1)], sem).start()
        return carry

    lax.fori_loop(0, tt, issue, 0, unroll=8)
    for _ in range(2):
        pltpu.make_async_copy(x_ref, xs_ref.at[pl.ds(0, tt)], sem).wait()


def _dispatch(x1, pos, n_rows):
    n, d = x1.shape
    tt = DISP_TT
    return pl.pallas_call(
        _dispatch_kernel,
        grid=(n // tt,),
        in_specs=[pl.BlockSpec((1, 2, tt), lambda i: (i, 0, 0), memory_space=pltpu.SMEM),
                  pl.BlockSpec((tt, d), lambda i: (i, 0))],
        out_specs=pl.BlockSpec(memory_space=pl.ANY),
        out_shape=jax.ShapeDtypeStruct((n_rows, d), x1.dtype),
        scratch_shapes=[pltpu.SemaphoreType.DMA(())],
        compiler_params=_cparams(("arbitrary",)),
        name="moe_dispatch",
    )(pos.reshape(n // tt, tt, 2).transpose(0, 2, 1), x1)


def _ffn_kernel(item_expert_ref, item_tile_ref, item_lo_ref, item_hi_ref, n_items_ref,
                x_ref, wg_ref, wu_ref, wd_ref, y_ref):
    j = pl.program_id(0)

    @pl.when(j < n_items_ref[0])
    def _():
        x = x_ref[...].astype(jnp.bfloat16)
        a = jnp.dot(x, wg_ref[0], preferred_element_type=jnp.float32)
        u = jnp.dot(x, wu_ref[0], preferred_element_type=jnp.float32)
        h = (a * jax.nn.sigmoid(a) * u).astype(jnp.bfloat16)
        y = jnp.dot(h, wd_ref[0], preferred_element_type=jnp.float32)
        rows = lax.broadcasted_iota(jnp.int32, (y.shape[0], 1), 0)
        y = jnp.where((rows >= item_lo_ref[j]) & (rows < item_hi_ref[j]), y, 0.0)
        first_of_tile = jnp.logical_or(j == 0, item_tile_ref[j] != item_tile_ref[jnp.maximum(j - 1, 0)])

        @pl.when(first_of_tile)
        def _():
            y_ref[...] = y

        @pl.when(jnp.logical_not(first_of_tile))
        def _():
            y_ref[...] += y


def _ffn_items(cnt, n_rows):
    tm = FFN_TM
    n_tiles = n_rows // tm
    max_items = n_tiles + N_EXPERTS - 1
    ends = jnp.cumsum(cnt)
    starts = ends - cnt
    first_tile = starts // tm
    last_tile = jnp.maximum(ends - 1, 0) // tm
    items_per_expert = jnp.where(cnt > 0, last_tile - first_tile + 1, 0)
    item_ends = jnp.cumsum(items_per_expert)
    n_items = item_ends[-1:]
    j = jnp.minimum(jnp.arange(max_items, dtype=jnp.int32), n_items[0] - 1)
    expert = jnp.minimum(jnp.searchsorted(item_ends, j, side="right"), N_EXPERTS - 1).astype(jnp.int32)
    tile = first_tile[expert] + (j - (item_ends - items_per_expert)[expert])
    lo = jnp.clip(starts[expert] - tile * tm, 0, tm)
    hi = jnp.clip(ends[expert] - tile * tm, 0, tm)
    i32 = lambda a: a.astype(jnp.int32)
    return i32(expert), i32(tile), i32(lo), i32(hi), i32(n_items), starts


def _grouped_ffn(xs, items, w_gate, w_up, w_down):
    n_rows, d = xs.shape
    tm = FFN_TM
    de = w_gate.shape[2]
    item_expert, item_tile, item_lo, item_hi, n_items = items
    max_items = item_expert.shape[0]

    def row_map(j, ie, it, lo, hi, ni):
        return (it[j], 0)

    def w_map(j, ie, it, lo, hi, ni):
        return (ie[j], 0, 0)

    grid_spec = pltpu.PrefetchScalarGridSpec(
        num_scalar_prefetch=5,
        grid=(max_items,),
        in_specs=[pl.BlockSpec((tm, d), row_map),
                  pl.BlockSpec((1, d, de), w_map),
                  pl.BlockSpec((1, d, de), w_map),
                  pl.BlockSpec((1, de, d), w_map)],
        out_specs=pl.BlockSpec((tm, d), row_map),
    )
    return pl.pallas_call(
        _ffn_kernel,
        grid_spec=grid_spec,
        out_shape=jax.ShapeDtypeStruct((n_rows, d), jnp.float32),
        compiler_params=_cparams(("arbitrary",)),
        name="moe_grouped_ffn",
    )(item_expert, item_tile, item_lo, item_hi, n_items, xs, w_gate, w_up, w_down)


def _combine_kernel(pos_ref, x1_ref, route_ref, g2_ref, b2_ref, ys_ref, o_ref, buf, sem):
    tt = x1_ref.shape[0]

    def issue(t, carry):
        for slot in range(2):
            p = pos_ref[0, slot, t]
            pltpu.make_async_copy(ys_ref.at[pl.ds(p, 1)], buf.at[slot, pl.ds(t, 1)], sem).start()
        return carry

    lax.fori_loop(0, tt, issue, 0, unroll=8)
    for slot in range(2):
        pltpu.make_async_copy(ys_ref.at[pl.ds(0, tt)], buf.at[slot], sem).wait()
    route = route_ref[...]
    y = route[:, 2:3] * buf[0] + route[:, 3:4] * buf[1]
    o_ref[...] = _layer_norm(DN_ALPHA * x1_ref[...] + y, g2_ref[...], b2_ref[...])


def _combine(x1, route, pos, ys, g2, b2, tok0, n_tok):
    d = x1.shape[1]
    tt = COMB_TT
    t0 = tok0 // tt
    pos3 = pos.reshape(pos.shape[0] // tt, tt, 2).transpose(0, 2, 1)
    return pl.pallas_call(
        _combine_kernel,
        grid=(n_tok // tt,),
        in_specs=[pl.BlockSpec((1, 2, tt), lambda i: (i + t0, 0, 0), memory_space=pltpu.SMEM),
                  pl.BlockSpec((tt, d), lambda i: (i + t0, 0)),
                  pl.BlockSpec((tt, ROUTE_LANES), lambda i: (i + t0, 0)),
                  pl.BlockSpec((1, d), lambda i: (0, 0)),
                  pl.BlockSpec((1, d), lambda i: (0, 0)),
                  pl.BlockSpec(memory_space=pl.ANY)],
        out_specs=pl.BlockSpec((tt, d), lambda i: (i, 0)),
        out_shape=jax.ShapeDtypeStruct((n_tok, d), jnp.float32),
        scratch_shapes=[pltpu.VMEM((2, tt, d), ys.dtype), pltpu.SemaphoreType.DMA(())],
        compiler_params=_cparams(("arbitrary",)),
        name="moe_combine_ln",
    )(pos3, x1, route, g2, b2, ys)


def _layer(x_prompt, x_sample, ln_in_g, ln_in_b, w_in, b_in, rel_bias_t5, na_rpb, w_branch_a, w_branch_b,
           w_out, b_out, ln1_g, ln1_b, w_router_group, b_router_group, w_router_expert, b_router_expert,
           w_exp_gate, w_exp_up, w_exp_down, ln2_g, ln2_b):
    bp, tp, d = x_prompt.shape
    bs, ts, _ = x_sample.shape
    seg_tokens = (bp * tp, bs * ts)
    seq_lens = (tp, ts)
    n = sum(seg_tokens)
    xp = x_prompt.reshape(seg_tokens[0], d)
    xs = x_sample.reshape(seg_tokens[1], d)
    row = lambda a: a.reshape(1, -1).astype(jnp.float32)
    bf = lambda a: a.astype(jnp.bfloat16)

    scale = HEAD_DIM ** -0.5
    col_scale = np.ones((w_in.shape[2],), np.float32)
    col_scale[0:QA] = scale
    col_scale[3 * QA:3 * QA + QB] = scale
    w_proj = bf(w_in[0] * col_scale)
    b_proj = row(b_in[0] * col_scale)
    proj = _in_projection(xp, xs, row(ln_in_g), row(ln_in_b), w_proj, b_proj)
    qkv_a, (qb, kb, vb, ga, gb) = proj[:3 * N_DIL_GROUPS], proj[3 * N_DIL_GROUPS:]

    o_groups, lse_groups = [], []
    for g, (window, dil) in enumerate(DIL_PATTERNS):
        assert window // (2 * dil) == HALF_SPAN
        table = rel_bias_t5[:, g * HEADS_PER_DIL_GROUP:(g + 1) * HEADS_PER_DIL_GROUP].astype(jnp.float32).reshape(-1)
        o, lse = _dilated_group(qkv_a[g], qkv_a[N_DIL_GROUPS + g], qkv_a[2 * N_DIL_GROUPS + g], table, dil,
                                seg_tokens, seq_lens)
        o_groups.append(o)
        lse_groups.append(lse)
    ob = _neighborhood(qb, kb, vb, na_rpb[0].astype(jnp.float32).reshape(-1), seg_tokens, seq_lens)

    wr = jnp.zeros((d, ROUTE_LANES), jnp.float32)
    wr = wr.at[:, 0:N_GROUPS].set(w_router_group[0]).at[:, EXPERT_LANE0:EXPERT_LANE0 + N_EXPERTS].set(w_router_expert[0])
    br = jnp.zeros((1, ROUTE_LANES), jnp.float32)
    br = br.at[0, 0:N_GROUPS].set(b_router_group[0]).at[0, EXPERT_LANE0:EXPERT_LANE0 + N_EXPERTS].set(b_router_expert[0])
    x1, route, counts = _post_attention(
        xp, xs, row(ln_in_g), row(ln_in_b), o_groups, lse_groups, ob, ga, gb, bf(w_branch_a[0]), bf(w_branch_b[0]),
        bf(w_out[0]), row(b_out[0]), row(ln1_g[0]), row(ln1_b[0]), bf(wr), br)

    experts = route[:, 0:2].astype(jnp.int32)
    ranks = route[:, 4:6].astype(jnp.int32)
    cnt = counts[0, EXPERT_LANE0:EXPERT_LANE0 + N_EXPERTS].astype(jnp.int32)
    n_rows = 2 * n
    *items, starts = _ffn_items(cnt, n_rows)
    pos = starts[experts] + ranks

    xsorted = _dispatch(x1, pos, n_rows)
    ys = _grouped_ffn(xsorted, items, bf(w_exp_gate[0]), bf(w_exp_up[0]), bf(w_exp_down[0]))
    y_prompt = _combine(x1, route, pos, ys, row(ln2_g[0]), row(ln2_b[0]), 0, seg_tokens[0])
    y_sample = _combine(x1, route, pos, ys, row(ln2_g[0]), row(ln2_b[0]), seg_tokens[0], seg_tokens[1])
    return y_prompt.reshape(bp, tp, d), y_sample.reshape(bs, ts, d)


def kernel(x_prompt, x_sample, ln_in_g, ln_in_b, w_in, b_in, rel_bias_t5, na_rpb, w_branch_a, w_branch_b, w_out, b_out, ln1_g, ln1_b, w_router_group, b_router_group, w_router_expert, b_router_expert, w_exp_gate, w_exp_up, w_exp_down, ln2_g, ln2_b):
    return _layer(x_prompt, x_sample, ln_in_g, ln_in_b, w_in, b_in, rel_bias_t5, na_rpb, w_branch_a, w_branch_b,
                  w_out, b_out, ln1_g, ln1_b, w_router_group, b_router_group, w_router_expert, b_router_expert,
                  w_exp_gate, w_exp_up, w_exp_down, ln2_g, ln2_b)
```

```python
import functools
import math

import numpy as np
import jax
import jax.numpy as jnp
from jax import lax
from jax.experimental import pallas as pl
from jax.experimental.pallas import tpu as pltpu

HEAD_DIM = 64
DIL_PATTERNS = ((128, 1), (512, 4), (2048, 16))
HEADS_PER_DIL_GROUP = 4
N_DIL_GROUPS = 3
H_A = N_DIL_GROUPS * HEADS_PER_DIL_GROUP
H_B = 8
QA = H_A * HEAD_DIM
QB = H_B * HEAD_DIM
HALF_SPAN = 64
GRID_W = 64
NA_KH = 8
NA_KW = 16
N_BUCKETS = 32
REL_MAX_DIST = 1024
N_GROUPS = 4
EXPERTS_PER_GROUP = 8
N_EXPERTS = N_GROUPS * EXPERTS_PER_GROUP
LN_EPS = 1e-5
NEG_INF = -1e30
DEPTH = 1
DN_ALPHA = (2.0 * DEPTH) ** 0.25

LANES = 128
HEAD_BLOCK = 4 * HEAD_DIM
VMEM_LIMIT_BYTES = 56 * 1024 * 1024

PROJ_TM = 512
DIL_M = 128
DIL_KBLK = 64
NA_ROWS = 8
POST_TM = 512
FFN_TM = 512
DISP_TT = 2048
COMB_TT = 512
ROUTE_LANES = LANES
EXPERT_LANE0 = 32


def _cparams(sem):
    return pltpu.CompilerParams(dimension_semantics=sem, vmem_limit_bytes=VMEM_LIMIT_BYTES)


def _layer_norm(x, g, b):
    mu = jnp.mean(x, axis=-1, keepdims=True)
    xc = x - mu
    var = jnp.mean(xc * xc, axis=-1, keepdims=True)
    return xc * lax.rsqrt(var + LN_EPS) * g + b


def _strided_rows(scr, slab0, r, count, stride):
    return jnp.concatenate([scr[slab0 + s, pl.ds(r, count, stride=stride), :] for s in range(2)], axis=1)


def _inproj_kernel(xp_ref, xs_ref, g_ref, b_ref, w_ref, bias_ref, *refs, n0_tiles):
    out_refs, scr = refs[:-1], refs[-1]
    i = pl.program_id(0)
    x = jnp.where(i < n0_tiles, xp_ref[...], xs_ref[...])
    xn = _layer_norm(x, g_ref[...], b_ref[...]).astype(jnp.bfloat16)
    tm = xn.shape[0]

    def proj(c0, cw):
        return jnp.dot(xn, w_ref[:, c0:c0 + cw], preferred_element_type=jnp.float32) + bias_ref[:, c0:c0 + cw]

    n_dil = 3 * N_DIL_GROUPS
    for idx in range(n_dil):
        g = idx % N_DIL_GROUPS
        dil = DIL_PATTERNS[g][1]
        o_ref = out_refs[idx]
        y = proj(idx * HEAD_BLOCK, HEAD_BLOCK)
        if dil == 1:
            o_ref[0, 0] = y.astype(o_ref.dtype)
        else:
            slab0 = 2 * (idx % 2)
            scr[slab0] = y[:, :LANES]
            scr[slab0 + 1] = y[:, LANES:]
            for r in range(dil):
                o_ref[0, r] = _strided_rows(scr, slab0, r, tm // dil, dil).astype(o_ref.dtype)
    col = n_dil * HEAD_BLOCK
    for o_ref in out_refs[n_dil:]:
        width = o_ref.shape[1]
        for c in range(0, width, 512):
            cw = min(512, width - c)
            o_ref[:, c:c + cw] = proj(col + c, cw).astype(o_ref.dtype)
        col += width


def _in_projection(xp, xs, ln_g, ln_b, w, b):
    n0, d_model = xp.shape
    n1 = xs.shape[0]
    n = n0 + n1
    tm = PROJ_TM
    n0_tiles, n1_tiles = n0 // tm, n1 // tm
    nt = n0_tiles + n1_tiles
    out_shape, out_specs = [], []
    for _ in range(3):
        for _, dil in DIL_PATTERNS:
            out_shape.append(jax.ShapeDtypeStruct((nt, dil, tm // dil, HEAD_BLOCK), jnp.bfloat16))
            out_specs.append(pl.BlockSpec((1, dil, tm // dil, HEAD_BLOCK), lambda i: (i, 0, 0, 0)))
    for wd in (QB, QB, QB, d_model, d_model):
        out_shape.append(jax.ShapeDtypeStruct((n, wd), jnp.bfloat16))
        out_specs.append(pl.BlockSpec((tm, wd), lambda i: (i, 0)))
    d_in = w.shape[1]
    kern = functools.partial(_inproj_kernel, n0_tiles=n0_tiles)
    return pl.pallas_call(
        kern,
        grid=(nt,),
        in_specs=[
            pl.BlockSpec((tm, d_model), lambda i: (jnp.minimum(i, n0_tiles - 1), 0)),
            pl.BlockSpec((tm, d_model), lambda i: (jnp.maximum(i - n0_tiles, 0), 0)),
            pl.BlockSpec((1, d_model), lambda i: (0, 0)),
            pl.BlockSpec((1, d_model), lambda i: (0, 0)),
            pl.BlockSpec((d_model, d_in), lambda i: (0, 0)),
            pl.BlockSpec((1, d_in), lambda i: (0, 0)),
        ],
        out_specs=out_specs,
        out_shape=out_shape,
        scratch_shapes=[pltpu.VMEM((4, tm, LANES), jnp.float32)],
        compiler_params=_cparams(("arbitrary",)),
        name="in_projection",
    )(xp, xs, ln_g, ln_b, w, b)


def _t5_bucket_np(rel):
    half = N_BUCKETS // 2
    max_exact = half // 2
    ret = np.where(rel > 0, half, 0)
    n = np.abs(rel)
    nf = np.maximum(n, 1).astype(np.float32)
    large = max_exact + (np.log(nf / np.float32(max_exact)) / np.float32(math.log(REL_MAX_DIST / max_exact))
                         * np.float32(half - max_exact)).astype(np.int32)
    large = np.minimum(large, half - 1)
    return ret + np.where(n < max_exact, n, large)


def _t5_bucket_starts():
    half = N_BUCKETS // 2
    dist = np.arange(0, HALF_SPAN * max(d for _, d in DIL_PATTERNS) + 1)
    buckets = _t5_bucket_np(-dist)
    return [int(np.argmax(buckets >= k)) for k in range(half)]


def _build_dilated_bias(table_ref, bias_scr, dil):
    half = N_BUCKETS // 2
    starts = _t5_bucket_starts()
    kb = DIL_M + 2 * HALF_SPAN
    i = lax.broadcasted_iota(jnp.int32, (DIL_M, kb), 0)
    m = lax.broadcasted_iota(jnp.int32, (DIL_M, kb), 1)
    delta = m - HALF_SPAN - i
    dist = jnp.abs(delta) * dil
    band = jnp.abs(delta) <= HALF_SPAN
    for h in range(HEADS_PER_DIL_GROUP):
        sides = []
        for side in range(2):
            val = jnp.full((DIL_M, kb), table_ref[(side * half + half - 1) * HEADS_PER_DIL_GROUP + h], jnp.float32)
            for k in range(half - 1, 0, -1):
                val = jnp.where(dist < starts[k], table_ref[(side * half + k - 1) * HEADS_PER_DIL_GROUP + h], val)
            sides.append(val)
        base = jnp.where(band, jnp.where(delta > 0, sides[1], sides[0]), NEG_INF)
        rows = slice(h * DIL_M, (h + 1) * DIL_M)
        bias_scr[0, rows, :] = base
        bias_scr[1, rows, :] = jnp.where(m >= HALF_SPAN, base, NEG_INF)
        bias_scr[2, rows, :] = jnp.where(m < kb - HALF_SPAN, base, NEG_INF)


def _stack_heads(q):
    head_of_col = lax.broadcasted_iota(jnp.int32, (1, HEAD_BLOCK), 1) // HEAD_DIM
    zero = jnp.zeros_like(q)
    return jnp.concatenate([jnp.where(head_of_col == h, q, zero) for h in range(4)], axis=0)


def _merge_heads(x, m):
    head_of_col = lax.broadcasted_iota(jnp.int32, (1, HEAD_BLOCK), 1) // HEAD_DIM
    out = jnp.zeros((m, HEAD_BLOCK), x.dtype)
    for h in range(4):
        out = jnp.where(head_of_col == h, x[h * m:(h + 1) * m], out)
    return out


def _softmax_pv(s, v, m_rows):
    mx = jnp.max(s, axis=-1, keepdims=True)
    p = jnp.exp(s - mx)
    l = jnp.sum(p, axis=-1, keepdims=True)
    pv = jnp.dot(p.astype(jnp.bfloat16), v, preferred_element_type=jnp.float32)
    o = _merge_heads(pv * (1.0 / l), m_rows)
    return o, mx + jnp.log(l)


def _dilated_kernel(table_ref, q_ref, k0_ref, k1_ref, k2_ref, k3_ref, v0_ref, v1_ref, v2_ref, v3_ref,
                    o_ref, lse_ref, bias_scr, *, dil, seg0_blocks, blocks_per_seq):
    jb = pl.program_id(1)

    @pl.when((pl.program_id(0) == 0) & (jb == 0))
    def _():
        _build_dilated_bias(table_ref, bias_scr, dil)

    in0 = jb < seg0_blocks
    jl = jnp.where(in0, jb, jb - seg0_blocks)
    nbs = jnp.where(in0, blocks_per_seq[0], blocks_per_seq[1])
    pos = lax.rem(jl, nbs)
    variant = jnp.where(pos == 0, 1, jnp.where(pos == nbs - 1, 2, 0))
    rows2d = lambda ref: ref[...].reshape(-1, HEAD_BLOCK)
    k = jnp.concatenate([rows2d(r) for r in (k0_ref, k1_ref, k2_ref, k3_ref)], axis=0)
    v = jnp.concatenate([rows2d(r) for r in (v0_ref, v1_ref, v2_ref, v3_ref)], axis=0)
    qs = _stack_heads(rows2d(q_ref))
    s = lax.dot_general(qs, k, (((1,), (1,)), ((), ())), preferred_element_type=jnp.float32)
    s = s + bias_scr[variant]
    o, lse = _softmax_pv(s, v, DIL_M)
    o_ref[...] = o.astype(o_ref.dtype).reshape(o_ref.shape)
    lse_ref[...] = _merge_heads(jnp.broadcast_to(lse, (4 * DIL_M, HEAD_BLOCK)), DIL_M).reshape(lse_ref.shape)


def _residue_spec(rows_blk, dil, block_of):
    per_tile = PROJ_TM // dil
    if per_tile >= rows_blk:
        per = per_tile // rows_blk
        return pl.BlockSpec((1, None, rows_blk, HEAD_BLOCK),
                            lambda r, jb: (block_of(jb) // per, r, block_of(jb) % per, 0))
    return pl.BlockSpec((rows_blk // per_tile, None, per_tile, HEAD_BLOCK), lambda r, jb: (block_of(jb), r, 0, 0))


def _dilated_group(q, k, v, table, dil, seg_tokens, seq_lens):
    n = q.shape[0] * PROJ_TM
    rows = n // dil
    nb = rows // DIL_M
    nkb = rows // DIL_KBLK
    ratio = DIL_M // DIL_KBLK
    seg0_blocks = seg_tokens[0] // dil // DIL_M
    blocks_per_seq = tuple(t // dil // DIL_M for t in seq_lens)
    assert min(blocks_per_seq) >= 2

    def kv_spec(t):
        return _residue_spec(DIL_KBLK, dil, lambda jb: jnp.clip(ratio * jb - 1 + t, 0, nkb - 1))

    qo_spec = _residue_spec(DIL_M, dil, lambda jb: jb)
    kern = functools.partial(_dilated_kernel, dil=dil, seg0_blocks=seg0_blocks, blocks_per_seq=blocks_per_seq)
    return pl.pallas_call(
        kern,
        grid=(dil, nb),
        in_specs=[pl.BlockSpec(memory_space=pltpu.SMEM), qo_spec]
        + [kv_spec(t) for t in range(4)] + [kv_spec(t) for t in range(4)],
        out_specs=[qo_spec, qo_spec],
        out_shape=[jax.ShapeDtypeStruct(q.shape, jnp.bfloat16), jax.ShapeDtypeStruct(q.shape, jnp.float32)],
        scratch_shapes=[pltpu.VMEM((3, HEADS_PER_DIL_GROUP * DIL_M, DIL_M + 2 * HALF_SPAN), jnp.float32)],
        compiler_params=_cparams(("arbitrary", "arbitrary")),
        name=f"dilated_attention_d{dil}",
    )(table, q, k, k, k, k, v, v, v, v)


NA_DR = 2 * NA_KH - 1
NA_DC = 2 * NA_KW - 1
NA_PAIRS = NA_DR - 1


def _build_na_bias(rpb_ref, pair_scr, head0):
    qc = lax.broadcasted_iota(jnp.int32, (GRID_W, 2 * GRID_W), 0)
    lane = lax.broadcasted_iota(jnp.int32, (GRID_W, 2 * GRID_W), 1)
    kc = lane % GRID_W
    upper = lane >= GRID_W
    qstart = jnp.clip(qc - NA_KW // 2, 0, GRID_W - NA_KW)
    valid = (kc >= qstart) & (kc < qstart + NA_KW)
    dc = jnp.clip(kc - qc, -(NA_KW - 1), NA_KW - 1) + NA_KW - 1

    def one_pair(idx, carry):
        h = idx // NA_PAIRS
        dr = idx % NA_PAIRS
        base = ((head0 + h) * NA_DR + dr) * NA_DC
        val = jnp.zeros((GRID_W, 2 * GRID_W), jnp.float32)
        for c in range(NA_DC):
            val = jnp.where(dc == c, jnp.where(upper, rpb_ref[base + NA_DC + c], rpb_ref[base + c]), val)
        pair_scr[h, dr] = jnp.where(valid, val, NEG_INF)
        return carry

    lax.fori_loop(0, 4 * NA_PAIRS, one_pair, 0)


def _na_kernel(rpb_ref, q_ref, kp_ref, kc_ref, kn_ref, vp_ref, vc_ref, vn_ref, o_ref, k_scr, v_scr, pair_scr,
               *, seg0_blocks, blocks_per_seq):
    ib = pl.program_id(1)

    @pl.when(ib == 0)
    def _():
        _build_na_bias(rpb_ref, pair_scr, pl.program_id(0) * 4)

    in0 = ib < seg0_blocks
    il = jnp.where(in0, ib, ib - seg0_blocks)
    nbs = jnp.where(in0, blocks_per_seq[0], blocks_per_seq[1])
    pos = lax.rem(il, nbs)
    first = pos == 0
    last = pos == nbs - 1
    blk = NA_ROWS * GRID_W
    k_scr[0:blk] = kp_ref[...]
    k_scr[blk:2 * blk] = kc_ref[...]
    k_scr[2 * blk:3 * blk] = kn_ref[...]
    v_scr[0:blk] = vp_ref[...]
    v_scr[blk:2 * blk] = vc_ref[...]
    v_scr[2 * blk:3 * blk] = vn_ref[...]
    half = NA_KH // 2
    for rr in range(NA_ROWS):
        start = jnp.where(first, max(rr + half, NA_ROWS), jnp.where(last, min(rr + half, NA_ROWS), rr + half))
        var = jnp.where(first, min(rr, half), jnp.where(last, max(rr, half), half))
        off = pl.multiple_of(start * GRID_W, GRID_W)
        kk = k_scr[pl.ds(off, NA_KH * GRID_W), :]
        vv = v_scr[pl.ds(off, NA_KH * GRID_W), :]
        qs = _stack_heads(q_ref[rr * GRID_W:(rr + 1) * GRID_W, :])
        s = lax.dot_general(qs, kk, (((1,), (1,)), ((), ())), preferred_element_type=jnp.float32)
        bias = jnp.concatenate(
            [jnp.concatenate([pair_scr[h, 2 * p - var + NA_KH - 1] for p in range(NA_KH // 2)], axis=1)
             for h in range(4)], axis=0)
        s = s + bias
        o, _ = _softmax_pv(s, vv, GRID_W)
        o_ref[rr * GRID_W:(rr + 1) * GRID_W, :] = o.astype(o_ref.dtype)


def _neighborhood(qb, kb, vb, rpb_flat, seg_tokens, seq_lens):
    n = qb.shape[0]
    blk = NA_ROWS * GRID_W
    nblk = n // blk
    seg0_blocks = seg_tokens[0] // blk
    blocks_per_seq = tuple(t // blk for t in seq_lens)
    assert min(blocks_per_seq) >= 2
    nset = QB // HEAD_BLOCK

    def kv_spec(t):
        return pl.BlockSpec((blk, HEAD_BLOCK), lambda c, ib: (jnp.clip(ib - 1 + t, 0, nblk - 1), c))

    kern = functools.partial(_na_kernel, seg0_blocks=seg0_blocks, blocks_per_seq=blocks_per_seq)
    return pl.pallas_call(
        kern,
        grid=(nset, nblk),
        in_specs=[pl.BlockSpec(memory_space=pltpu.SMEM), pl.BlockSpec((blk, HEAD_BLOCK), lambda c, ib: (ib, c))]
        + [kv_spec(t) for t in range(3)] + [kv_spec(t) for t in range(3)],
        out_specs=pl.BlockSpec((blk, HEAD_BLOCK), lambda c, ib: (ib, c)),
        out_shape=jax.ShapeDtypeStruct((n, QB), jnp.bfloat16),
        scratch_shapes=[pltpu.VMEM((3 * blk, HEAD_BLOCK), jnp.bfloat16),
                        pltpu.VMEM((3 * blk, HEAD_BLOCK), jnp.bfloat16),
                        pltpu.VMEM((4, NA_PAIRS, GRID_W, 2 * GRID_W), jnp.float32)],
        compiler_params=_cparams(("arbitrary", "arbitrary")),
        name="neighborhood_attention",
    )(rpb_flat, qb, kb, kb, kb, vb, vb, vb)


def _post_kernel(xp_ref, xs_ref, lng_ref, lnb_ref, o0_ref, o1_ref, o2_ref, l0_ref, l1_ref, l2_ref, ob_ref,
                 ga_ref, gb_ref, wa_ref, wb_ref, wo_ref, bo_ref, g1_ref, b1_ref, wr_ref, br_ref, tri_ref,
                 x1_ref, route_ref, counts_ref, carry_ref, perm_scr, *, n0_tiles):
    i = pl.program_id(0)

    @pl.when(i == 0)
    def _():
        carry_ref[...] = jnp.zeros_like(carry_ref)

    def token_order(ref, slab0):
        dil, per = ref.shape[1], ref.shape[2]
        if dil == 1:
            return ref[0, 0].astype(jnp.float32)
        for r in range(dil):
            val = ref[0, r].astype(jnp.float32)
            for s in range(2):
                perm_scr[slab0 + s, pl.ds(r, per, stride=dil), :] = val[:, s * LANES:(s + 1) * LANES]
        return jnp.concatenate([perm_scr[slab0], perm_scr[slab0 + 1]], axis=1)

    l0, l1, l2 = (token_order(ref, 2 * j) for j, ref in enumerate((l0_ref, l1_ref, l2_ref)))
    lm = jnp.maximum(jnp.maximum(l0, l1), l2)
    e0, e1, e2 = jnp.exp(l0 - lm), jnp.exp(l1 - lm), jnp.exp(l2 - lm)
    inv = 1.0 / (e0 + e1 + e2)
    o0, o1, o2 = (token_order(ref, 6 + 2 * j) for j, ref in enumerate((o0_ref, o1_ref, o2_ref)))
    o_a = (e0 * inv) * o0 + (e1 * inv) * o1 + (e2 * inv) * o2
    y_a = jnp.dot(o_a.astype(jnp.bfloat16), wa_ref[...], preferred_element_type=jnp.float32)
    y_b = jnp.dot(ob_ref[...], wb_ref[...], preferred_element_type=jnp.float32)
    mix = jax.nn.sigmoid(ga_ref[...].astype(jnp.float32)) * y_a + jax.nn.sigmoid(gb_ref[...].astype(jnp.float32)) * y_b
    out = jnp.dot(mix.astype(jnp.bfloat16), wo_ref[...], preferred_element_type=jnp.float32) + bo_ref[...]
    x = jnp.where(i < n0_tiles, xp_ref[...], xs_ref[...])
    x0 = _layer_norm(x, lng_ref[...], lnb_ref[...])
    x1 = _layer_norm(DN_ALPHA * x0 + out, g1_ref[...], b1_ref[...])
    x1_ref[...] = x1

    r = jnp.dot(x1.astype(jnp.bfloat16), wr_ref[...], preferred_element_type=jnp.float32) + br_ref[...]
    tm = r.shape[0]
    lane = lax.broadcasted_iota(jnp.int32, (tm, ROUTE_LANES), 1)
    neg = jnp.float32(-jnp.inf)
    lg = jnp.where(lane < N_GROUPS, r, neg)
    gmax = jnp.max(lg, axis=-1, keepdims=True)
    gi = jnp.min(jnp.where(lg == gmax, lane, ROUTE_LANES), axis=-1, keepdims=True)
    wg = 1.0 / jnp.sum(jnp.exp(lg - gmax), axis=-1, keepdims=True)
    lo = EXPERT_LANE0 + EXPERTS_PER_GROUP * gi
    le = jnp.where((lane >= lo) & (lane < lo + EXPERTS_PER_GROUP), r, neg)
    m1 = jnp.max(le, axis=-1, keepdims=True)
    i1 = jnp.min(jnp.where(le == m1, lane, ROUTE_LANES), axis=-1, keepdims=True)
    le2 = jnp.where(lane == i1, neg, le)
    m2 = jnp.max(le2, axis=-1, keepdims=True)
    i2 = jnp.min(jnp.where(le2 == m2, lane, ROUTE_LANES), axis=-1, keepdims=True)
    t2 = jnp.exp(m2 - m1)
    w1 = wg / (1.0 + t2)
    w2 = wg * t2 / (1.0 + t2)

    hot1 = lane == i1
    hot2 = lane == i2
    hot = (hot1 | hot2).astype(jnp.float32)
    before = jnp.dot(tri_ref[...], hot.astype(jnp.bfloat16), preferred_element_type=jnp.float32) + carry_ref[0:1, :]
    rank1 = jnp.sum(jnp.where(hot1, before, 0.0), axis=-1, keepdims=True)
    rank2 = jnp.sum(jnp.where(hot2, before, 0.0), axis=-1, keepdims=True)
    total = carry_ref[0:1, :] + jnp.sum(hot, axis=0, keepdims=True)
    carry_ref[...] = jnp.broadcast_to(total, carry_ref.shape)
    counts_ref[...] = jnp.broadcast_to(total, counts_ref.shape)

    e1f = (i1 - EXPERT_LANE0).astype(jnp.float32)
    e2f = (i2 - EXPERT_LANE0).astype(jnp.float32)
    route = jnp.zeros((tm, ROUTE_LANES), jnp.float32)
    for idx, val in enumerate((e1f, e2f, w1, w2, rank1, rank2)):
        route = jnp.where(lane == idx, val, route)
    route_ref[...] = route


def _post_attention(xp, xs, ln_in_g, ln_in_b, o_groups, lse_groups, ob, ga, gb, wa, wb, wo, bo, g1, b1, wr, br):
    n0, d = xp.shape
    n = n0 + xs.shape[0]
    tm = POST_TM
    n0_tiles = n0 // tm
    nt = n // tm
    tri = jnp.asarray(np.tril(np.ones((tm, tm), np.float32), -1), jnp.bfloat16)

    def tok(width):
        return pl.BlockSpec((tm, width), lambda i: (i, 0))

    def full(a):
        return pl.BlockSpec(a.shape, lambda i: (0,) * a.ndim)

    kern = functools.partial(_post_kernel, n0_tiles=n0_tiles)
    return pl.pallas_call(
        kern,
        grid=(nt,),
        in_specs=[
            pl.BlockSpec((tm, d), lambda i: (jnp.minimum(i, n0_tiles - 1), 0)),
            pl.BlockSpec((tm, d), lambda i: (jnp.maximum(i - n0_tiles, 0), 0)),
            full(ln_in_g), full(ln_in_b),
            *[pl.BlockSpec((1,) + a.shape[1:], lambda i: (i, 0, 0, 0)) for a in (*o_groups, *lse_groups)],
            tok(QB), tok(d), tok(d),
            full(wa), full(wb), full(wo), full(bo), full(g1), full(b1), full(wr), full(br), full(tri),
        ],
        out_specs=[tok(d), tok(ROUTE_LANES), pl.BlockSpec((8, ROUTE_LANES), lambda i: (0, 0))],
        out_shape=[jax.ShapeDtypeStruct((n, d), jnp.float32),
                   jax.ShapeDtypeStruct((n, ROUTE_LANES), jnp.float32),
                   jax.ShapeDtypeStruct((8, ROUTE_LANES), jnp.float32)],
        scratch_shapes=[pltpu.VMEM((8, ROUTE_LANES), jnp.float32), pltpu.VMEM((12, tm, LANES), jnp.float32)],
        compiler_params=_cparams(("arbitrary",)),
        name="post_attention_router",
    )(xp, xs, ln_in_g, ln_in_b, *o_groups, *lse_groups, ob, ga, gb, wa, wb, wo, bo, g1, b1, wr, br, tri)


def _dispatch_kernel(pos_ref, x_ref, xs_ref, sem):
    tt = x_ref.shape[0]

    def issue(j, carry):
        for u in range(2):
            t = 2 * j + u
            for slot in range(2):
                p = pos_ref[0, slot, t]
                pltpu.make_async_copy(x_ref.at[pl.ds(t, 1)], xs_ref.at[pl.ds(p, 1)], sem).start(priority=u)
        return carry

    lax.fori_loop(0, tt // 2, issue, 0, unroll=4)
    for _ in range(2):
        pltpu.make_async_copy(x_ref, xs_ref.at[pl.ds(0, tt)], sem).wait()


def _dispatch(x1, pos, n_rows):
    n, d = x1.shape
    tt = DISP_TT
    return pl.pallas_call(
        _dispatch_kernel,
        grid=(n // tt,),
        in_specs=[pl.BlockSpec((1, 2, tt), lambda i: (i, 0, 0), memory_space=pltpu.SMEM),
                  pl.BlockSpec((tt, d), lambda i: (i, 0))],
        out_specs=pl.BlockSpec(memory_space=pl.ANY),
        out_shape=jax.ShapeDtypeStruct((n_rows, d), x1.dtype),
        scratch_shapes=[pltpu.SemaphoreType.DMA(())],
        compiler_params=_cparams(("arbitrary",)),
        name="moe_dispatch",
    )(pos.reshape(n // tt, tt, 2).transpose(0, 2, 1), x1)


def _ffn_kernel(item_expert_ref, item_tile_ref, item_lo_ref, item_hi_ref, n_items_ref,
                x_ref, wg_ref, wu_ref, wd_ref, y_ref):
    j = pl.program_id(0)

    @pl.when(j < n_items_ref[0])
    def _():
        x = x_ref[...].astype(jnp.bfloat16)
        a = jnp.dot(x, wg_ref[0], preferred_element_type=jnp.float32)
        u = jnp.dot(x, wu_ref[0], preferred_element_type=jnp.float32)
        h = (a * jax.nn.sigmoid(a) * u).astype(jnp.bfloat16)
        y = jnp.dot(h, wd_ref[0], preferred_element_type=jnp.float32)
        rows = lax.broadcasted_iota(jnp.int32, (y.shape[0], 1), 0)
        y = jnp.where((rows >= item_lo_ref[j]) & (rows < item_hi_ref[j]), y, 0.0)
        first_of_tile = jnp.logical_or(j == 0, item_tile_ref[j] != item_tile_ref[jnp.maximum(j - 1, 0)])

        @pl.when(first_of_tile)
        def _():
            y_ref[...] = y

        @pl.when(jnp.logical_not(first_of_tile))
        def _():
            y_ref[...] += y


def _ffn_items(cnt, n_rows):
    tm = FFN_TM
    n_tiles = n_rows // tm
    max_items = n_tiles + N_EXPERTS - 1
    ends = jnp.cumsum(cnt)
    starts = ends - cnt
    first_tile = starts // tm
    last_tile = jnp.maximum(ends - 1, 0) // tm
    items_per_expert = jnp.where(cnt > 0, last_tile - first_tile + 1, 0)
    item_ends = jnp.cumsum(items_per_expert)
    n_items = item_ends[-1:]
    j = jnp.minimum(jnp.arange(max_items, dtype=jnp.int32), n_items[0] - 1)
    expert = jnp.minimum(jnp.searchsorted(item_ends, j, side="right"), N_EXPERTS - 1).astype(jnp.int32)
    tile = first_tile[expert] + (j - (item_ends - items_per_expert)[expert])
    lo = jnp.clip(starts[expert] - tile * tm, 0, tm)
    hi = jnp.clip(ends[expert] - tile * tm, 0, tm)
    i32 = lambda a: a.astype(jnp.int32)
    return i32(expert), i32(tile), i32(lo), i32(hi), i32(n_items), starts


def _grouped_ffn(xs, items, w_gate, w_up, w_down):
    n_rows, d = xs.shape
    tm = FFN_TM
    de = w_gate.shape[2]
    item_expert, item_tile, item_lo, item_hi, n_items = items
    max_items = item_expert.shape[0]

    def row_map(j, ie, it, lo, hi, ni):
        return (it[j], 0)

    def w_map(j, ie, it, lo, hi, ni):
        return (ie[j], 0, 0)

    grid_spec = pltpu.PrefetchScalarGridSpec(
        num_scalar_prefetch=5,
        grid=(max_items,),
        in_specs=[pl.BlockSpec((tm, d), row_map),
                  pl.BlockSpec((1, d, de), w_map),
                  pl.BlockSpec((1, d, de), w_map),
                  pl.BlockSpec((1, de, d), w_map)],
        out_specs=pl.BlockSpec((tm, d), row_map),
    )
    return pl.pallas_call(
        _ffn_kernel,
        grid_spec=grid_spec,
        out_shape=jax.ShapeDtypeStruct((n_rows, d), jnp.float32),
        compiler_params=_cparams(("arbitrary",)),
        name="moe_grouped_ffn",
    )(item_expert, item_tile, item_lo, item_hi, n_items, xs, w_gate, w_up, w_down)


def _combine_kernel(pos_ref, pos_next_ref, x1_ref, route_ref, g2_ref, b2_ref, ys_ref, o_ref, buf, sems):
    tt = x1_ref.shape[0]
    i = pl.program_id(0)
    cur = lax.rem(i, 2)

    def issue_tile(p_ref, b):
        def issue(j, carry):
            for u in range(2):
                t = 2 * j + u
                for slot in range(2):
                    p = p_ref[0, slot, t]
                    pltpu.make_async_copy(ys_ref.at[pl.ds(p, 1)], buf.at[b, slot, pl.ds(t, 1)],
                                          sems.at[b]).start(priority=u)
            return carry

        lax.fori_loop(0, tt // 2, issue, 0, unroll=4)

    @pl.when(i == 0)
    def _():
        issue_tile(pos_ref, 0)

    @pl.when(i + 1 < pl.num_programs(0))
    def _():
        issue_tile(pos_next_ref, 1 - cur)

    for slot in range(2):
        pltpu.make_async_copy(ys_ref.at[pl.ds(0, tt)], buf.at[cur, slot], sems.at[cur]).wait()
    route = route_ref[...]
    y = route[:, 2:3] * buf[cur, 0] + route[:, 3:4] * buf[cur, 1]
    o_ref[...] = _layer_norm(DN_ALPHA * x1_ref[...] + y, g2_ref[...], b2_ref[...])


def _combine(x1, route, pos, ys, g2, b2, tok0, n_tok):
    d = x1.shape[1]
    tt = COMB_TT
    t0 = tok0 // tt
    pos3 = pos.reshape(pos.shape[0] // tt, tt, 2).transpose(0, 2, 1)
    nt = n_tok // tt
    return pl.pallas_call(
        _combine_kernel,
        grid=(nt,),
        in_specs=[pl.BlockSpec((1, 2, tt), lambda i: (i + t0, 0, 0), memory_space=pltpu.SMEM),
                  pl.BlockSpec((1, 2, tt), lambda i: (jnp.minimum(i + 1, nt - 1) + t0, 0, 0), memory_space=pltpu.SMEM),
                  pl.BlockSpec((tt, d), lambda i: (i + t0, 0)),
                  pl.BlockSpec((tt, ROUTE_LANES), lambda i: (i + t0, 0)),
                  pl.BlockSpec((1, d), lambda i: (0, 0)),
                  pl.BlockSpec((1, d), lambda i: (0, 0)),
                  pl.BlockSpec(memory_space=pl.ANY)],
        out_specs=pl.BlockSpec((tt, d), lambda i: (i, 0)),
        out_shape=jax.ShapeDtypeStruct((n_tok, d), jnp.float32),
        scratch_shapes=[pltpu.VMEM((2, 2, tt, d), ys.dtype), pltpu.SemaphoreType.DMA((2,))],
        compiler_params=_cparams(("arbitrary",)),
        name="moe_combine_ln",
    )(pos3, pos3, x1, route, g2, b2, ys)


def _layer(x_prompt, x_sample, ln_in_g, ln_in_b, w_in, b_in, rel_bias_t5, na_rpb, w_branch_a, w_branch_b,
           w_out, b_out, ln1_g, ln1_b, w_router_group, b_router_group, w_router_expert, b_router_expert,
           w_exp_gate, w_exp_up, w_exp_down, ln2_g, ln2_b):
    bp, tp, d = x_prompt.shape
    bs, ts, _ = x_sample.shape
    seg_tokens = (bp * tp, bs * ts)
    seq_lens = (tp, ts)
    n = sum(seg_tokens)
    xp = x_prompt.reshape(seg_tokens[0], d)
    xs = x_sample.reshape(seg_tokens[1], d)
    row = lambda a: a.reshape(1, -1).astype(jnp.float32)
    bf = lambda a: a.astype(jnp.bfloat16)

    scale = HEAD_DIM ** -0.5
    col_scale = np.ones((w_in.shape[2],), np.float32)
    col_scale[0:QA] = scale
    col_scale[3 * QA:3 * QA + QB] = scale
    w_proj = bf(w_in[0] * col_scale)
    b_proj = row(b_in[0] * col_scale)
    proj = _in_projection(xp, xs, row(ln_in_g), row(ln_in_b), w_proj, b_proj)
    qkv_a, (qb, kb, vb, ga, gb) = proj[:3 * N_DIL_GROUPS], proj[3 * N_DIL_GROUPS:]

    o_groups, lse_groups = [], []
    for g, (window, dil) in enumerate(DIL_PATTERNS):
        assert window // (2 * dil) == HALF_SPAN
        table = rel_bias_t5[:, g * HEADS_PER_DIL_GROUP:(g + 1) * HEADS_PER_DIL_GROUP].astype(jnp.float32).reshape(-1)
        o, lse = _dilated_group(qkv_a[g], qkv_a[N_DIL_GROUPS + g], qkv_a[2 * N_DIL_GROUPS + g], table, dil,
                                seg_tokens, seq_lens)
        o_groups.append(o)
        lse_groups.append(lse)
    ob = _neighborhood(qb, kb, vb, na_rpb[0].astype(jnp.float32).reshape(-1), seg_tokens, seq_lens)

    wr = jnp.zeros((d, ROUTE_LANES), jnp.float32)
    wr = wr.at[:, 0:N_GROUPS].set(w_router_group[0]).at[:, EXPERT_LANE0:EXPERT_LANE0 + N_EXPERTS].set(w_router_expert[0])
    br = jnp.zeros((1, ROUTE_LANES), jnp.float32)
    br = br.at[0, 0:N_GROUPS].set(b_router_group[0]).at[0, EXPERT_LANE0:EXPERT_LANE0 + N_EXPERTS].set(b_router_expert[0])
    x1, route, counts = _post_attention(
        xp, xs, row(ln_in_g), row(ln_in_b), o_groups, lse_groups, ob, ga, gb, bf(w_branch_a[0]), bf(w_branch_b[0]),
        bf(w_out[0]), row(b_out[0]), row(ln1_g[0]), row(ln1_b[0]), bf(wr), br)

    experts = route[:, 0:2].astype(jnp.int32)
    ranks = route[:, 4:6].astype(jnp.int32)
    cnt = counts[0, EXPERT_LANE0:EXPERT_LANE0 + N_EXPERTS].astype(jnp.int32)
    n_rows = 2 * n
    *items, starts = _ffn_items(cnt, n_rows)
    pos = starts[experts] + ranks

    xsorted = _dispatch(x1, pos, n_rows)
    ys = _grouped_ffn(xsorted, items, bf(w_exp_gate[0]), bf(w_exp_up[0]), bf(w_exp_down[0]))
    y_prompt = _combine(x1, route, pos, ys, row(ln2_g[0]), row(ln2_b[0]), 0, seg_tokens[0])
    y_sample = _combine(x1, route, pos, ys, row(ln2_g[0]), row(ln2_b[0]), seg_tokens[0], seg_tokens[1])
    return y_prompt.reshape(bp, tp, d), y_sample.reshape(bs, ts, d)


def kernel(x_prompt, x_sample, ln_in_g, ln_in_b, w_in, b_in, rel_bias_t5, na_rpb, w_branch_a, w_branch_b, w_out, b_out, ln1_g, ln1_b, w_router_group, b_router_group, w_router_expert, b_router_expert, w_exp_gate, w_exp_up, w_exp_down, ln2_g, ln2_b):
    return _layer(x_prompt, x_sample, ln_in_g, ln_in_b, w_in, b_in, rel_bias_t5, na_rpb, w_branch_a, w_branch_b,
                  w_out, b_out, ln1_g, ln1_b, w_router_group, b_router_group, w_router_expert, b_router_expert,
                  w_exp_gate, w_exp_up, w_exp_down, ln2_g, ln2_b)
```

```python
import functools
import math

import numpy as np
import jax
import jax.numpy as jnp
from jax import lax
from jax.experimental import pallas as pl
from jax.experimental.pallas import tpu as pltpu

HEAD_DIM = 64
DIL_PATTERNS = ((128, 1), (512, 4), (2048, 16))
HEADS_PER_DIL_GROUP = 4
N_DIL_GROUPS = 3
H_A = N_DIL_GROUPS * HEADS_PER_DIL_GROUP
H_B = 8
QA = H_A * HEAD_DIM
QB = H_B * HEAD_DIM
HALF_SPAN = 64
GRID_W = 64
NA_KH = 8
NA_KW = 16
N_BUCKETS = 32
REL_MAX_DIST = 1024
N_GROUPS = 4
EXPERTS_PER_GROUP = 8
N_EXPERTS = N_GROUPS * EXPERTS_PER_GROUP
LN_EPS = 1e-5
NEG_INF = -1e30
DEPTH = 1
DN_ALPHA = (2.0 * DEPTH) ** 0.25

LANES = 128
HEAD_BLOCK = 4 * HEAD_DIM
VMEM_LIMIT_BYTES = 56 * 1024 * 1024

PROJ_TM = 512
DIL_M = 128
DIL_KBLK = 64
NA_ROWS = 8
POST_TM = 512
FFN_TM = 512
DISP_TT = 2048
COMB_TT = 512
ROUTE_LANES = LANES
EXPERT_LANE0 = 32


def _cparams(sem):
    return pltpu.CompilerParams(dimension_semantics=sem, vmem_limit_bytes=VMEM_LIMIT_BYTES)


SUBLANES = 8


def _store_row_tiles(ref, x, accumulate=False):
    rows = x.shape[0]
    for c in range(SUBLANES):
        idx = pl.ds(c, rows, stride=SUBLANES)
        piece = x[:, c * LANES:(c + 1) * LANES]
        ref[idx, :] = ref[idx, :] + piece if accumulate else piece


def _load_row_tiles(ref, rows):
    return jnp.concatenate([ref[pl.ds(c, rows, stride=SUBLANES), :] for c in range(SUBLANES)], axis=1)


def _layer_norm(x, g, b):
    mu = jnp.mean(x, axis=-1, keepdims=True)
    xc = x - mu
    var = jnp.mean(xc * xc, axis=-1, keepdims=True)
    return xc * lax.rsqrt(var + LN_EPS) * g + b


def _strided_rows(scr, slab0, r, count, stride):
    return jnp.concatenate([scr[slab0 + s, pl.ds(r, count, stride=stride), :] for s in range(2)], axis=1)


def _inproj_kernel(xp_ref, xs_ref, g_ref, b_ref, w_ref, bias_ref, *refs, n0_tiles):
    out_refs, scr = refs[:-1], refs[-1]
    i = pl.program_id(0)
    x = jnp.where(i < n0_tiles, xp_ref[...], xs_ref[...])
    xn = _layer_norm(x, g_ref[...], b_ref[...]).astype(jnp.bfloat16)
    tm = xn.shape[0]

    def proj(c0, cw):
        return jnp.dot(xn, w_ref[:, c0:c0 + cw], preferred_element_type=jnp.float32) + bias_ref[:, c0:c0 + cw]

    n_dil = 3 * N_DIL_GROUPS
    for idx in range(n_dil):
        g = idx % N_DIL_GROUPS
        dil = DIL_PATTERNS[g][1]
        o_ref = out_refs[idx]
        y = proj(idx * HEAD_BLOCK, HEAD_BLOCK)
        if dil == 1:
            o_ref[0, 0] = y.astype(o_ref.dtype)
        else:
            slab0 = 2 * (idx % 2)
            scr[slab0] = y[:, :LANES]
            scr[slab0 + 1] = y[:, LANES:]
            for r in range(dil):
                o_ref[0, r] = _strided_rows(scr, slab0, r, tm // dil, dil).astype(o_ref.dtype)
    col = n_dil * HEAD_BLOCK
    for o_ref in out_refs[n_dil:]:
        width = o_ref.shape[1]
        for c in range(0, width, 512):
            cw = min(512, width - c)
            o_ref[:, c:c + cw] = proj(col + c, cw).astype(o_ref.dtype)
        col += width


def _in_projection(xp, xs, ln_g, ln_b, w, b):
    n0, d_model = xp.shape
    n1 = xs.shape[0]
    n = n0 + n1
    tm = PROJ_TM
    n0_tiles, n1_tiles = n0 // tm, n1 // tm
    nt = n0_tiles + n1_tiles
    out_shape, out_specs = [], []
    for _ in range(3):
        for _, dil in DIL_PATTERNS:
            out_shape.append(jax.ShapeDtypeStruct((nt, dil, tm // dil, HEAD_BLOCK), jnp.bfloat16))
            out_specs.append(pl.BlockSpec((1, dil, tm // dil, HEAD_BLOCK), lambda i: (i, 0, 0, 0)))
    for wd in (QB, QB, QB, d_model, d_model):
        out_shape.append(jax.ShapeDtypeStruct((n, wd), jnp.bfloat16))
        out_specs.append(pl.BlockSpec((tm, wd), lambda i: (i, 0)))
    d_in = w.shape[1]
    kern = functools.partial(_inproj_kernel, n0_tiles=n0_tiles)
    return pl.pallas_call(
        kern,
        grid=(nt,),
        in_specs=[
            pl.BlockSpec((tm, d_model), lambda i: (jnp.minimum(i, n0_tiles - 1), 0)),
            pl.BlockSpec((tm, d_model), lambda i: (jnp.maximum(i - n0_tiles, 0), 0)),
            pl.BlockSpec((1, d_model), lambda i: (0, 0)),
            pl.BlockSpec((1, d_model), lambda i: (0, 0)),
            pl.BlockSpec((d_model, d_in), lambda i: (0, 0)),
            pl.BlockSpec((1, d_in), lambda i: (0, 0)),
        ],
        out_specs=out_specs,
        out_shape=out_shape,
        scratch_shapes=[pltpu.VMEM((4, tm, LANES), jnp.float32)],
        compiler_params=_cparams(("arbitrary",)),
        name="in_projection",
    )(xp, xs, ln_g, ln_b, w, b)


def _t5_bucket_np(rel):
    half = N_BUCKETS // 2
    max_exact = half // 2
    ret = np.where(rel > 0, half, 0)
    n = np.abs(rel)
    nf = np.maximum(n, 1).astype(np.float32)
    large = max_exact + (np.log(nf / np.float32(max_exact)) / np.float32(math.log(REL_MAX_DIST / max_exact))
                         * np.float32(half - max_exact)).astype(np.int32)
    large = np.minimum(large, half - 1)
    return ret + np.where(n < max_exact, n, large)


def _t5_bucket_starts():
    half = N_BUCKETS // 2
    dist = np.arange(0, HALF_SPAN * max(d for _, d in DIL_PATTERNS) + 1)
    buckets = _t5_bucket_np(-dist)
    return [int(np.argmax(buckets >= k)) for k in range(half)]


def _build_dilated_bias(table_ref, bias_scr, dil):
    half = N_BUCKETS // 2
    starts = _t5_bucket_starts()
    kb = DIL_M + 2 * HALF_SPAN
    i = lax.broadcasted_iota(jnp.int32, (DIL_M, kb), 0)
    m = lax.broadcasted_iota(jnp.int32, (DIL_M, kb), 1)
    delta = m - HALF_SPAN - i
    dist = jnp.abs(delta) * dil
    band = jnp.abs(delta) <= HALF_SPAN
    for h in range(HEADS_PER_DIL_GROUP):
        sides = []
        for side in range(2):
            val = jnp.full((DIL_M, kb), table_ref[(side * half + half - 1) * HEADS_PER_DIL_GROUP + h], jnp.float32)
            for k in range(half - 1, 0, -1):
                val = jnp.where(dist < starts[k], table_ref[(side * half + k - 1) * HEADS_PER_DIL_GROUP + h], val)
            sides.append(val)
        base = jnp.where(band, jnp.where(delta > 0, sides[1], sides[0]), NEG_INF)
        rows = slice(h * DIL_M, (h + 1) * DIL_M)
        bias_scr[0, rows, :] = base
        bias_scr[1, rows, :] = jnp.where(m >= HALF_SPAN, base, NEG_INF)
        bias_scr[2, rows, :] = jnp.where(m < kb - HALF_SPAN, base, NEG_INF)


def _stack_heads(q):
    head_of_col = lax.broadcasted_iota(jnp.int32, (1, HEAD_BLOCK), 1) // HEAD_DIM
    zero = jnp.zeros_like(q)
    return jnp.concatenate([jnp.where(head_of_col == h, q, zero) for h in range(4)], axis=0)


def _merge_heads(x, m):
    head_of_col = lax.broadcasted_iota(jnp.int32, (1, HEAD_BLOCK), 1) // HEAD_DIM
    out = jnp.zeros((m, HEAD_BLOCK), x.dtype)
    for h in range(4):
        out = jnp.where(head_of_col == h, x[h * m:(h + 1) * m], out)
    return out


def _softmax_pv(s, v, m_rows):
    mx = jnp.max(s, axis=-1, keepdims=True)
    p = jnp.exp(s - mx)
    l = jnp.sum(p, axis=-1, keepdims=True)
    pv = jnp.dot(p.astype(jnp.bfloat16), v, preferred_element_type=jnp.float32)
    o = _merge_heads(pv * (1.0 / l), m_rows)
    return o, mx + jnp.log(l)


def _dilated_kernel(table_ref, q_ref, k0_ref, k1_ref, k2_ref, k3_ref, v0_ref, v1_ref, v2_ref, v3_ref,
                    o_ref, lse_ref, bias_scr, *, dil, seg0_blocks, blocks_per_seq):
    jb = pl.program_id(1)

    @pl.when((pl.program_id(0) == 0) & (jb == 0))
    def _():
        _build_dilated_bias(table_ref, bias_scr, dil)

    in0 = jb < seg0_blocks
    jl = jnp.where(in0, jb, jb - seg0_blocks)
    nbs = jnp.where(in0, blocks_per_seq[0], blocks_per_seq[1])
    pos = lax.rem(jl, nbs)
    variant = jnp.where(pos == 0, 1, jnp.where(pos == nbs - 1, 2, 0))
    rows2d = lambda ref: ref[...].reshape(-1, HEAD_BLOCK)
    k = jnp.concatenate([rows2d(r) for r in (k0_ref, k1_ref, k2_ref, k3_ref)], axis=0)
    v = jnp.concatenate([rows2d(r) for r in (v0_ref, v1_ref, v2_ref, v3_ref)], axis=0)
    qs = _stack_heads(rows2d(q_ref))
    s = lax.dot_general(qs, k, (((1,), (1,)), ((), ())), preferred_element_type=jnp.float32)
    s = s + bias_scr[variant]
    o, lse = _softmax_pv(s, v, DIL_M)
    o_ref[...] = o.astype(o_ref.dtype).reshape(o_ref.shape)
    lse_ref[...] = _merge_heads(jnp.broadcast_to(lse, (4 * DIL_M, HEAD_BLOCK)), DIL_M).reshape(lse_ref.shape)


def _residue_spec(rows_blk, dil, block_of):
    per_tile = PROJ_TM // dil
    if per_tile >= rows_blk:
        per = per_tile // rows_blk
        return pl.BlockSpec((1, None, rows_blk, HEAD_BLOCK),
                            lambda r, jb: (block_of(jb) // per, r, block_of(jb) % per, 0))
    return pl.BlockSpec((rows_blk // per_tile, None, per_tile, HEAD_BLOCK), lambda r, jb: (block_of(jb), r, 0, 0))


def _dilated_group(q, k, v, table, dil, seg_tokens, seq_lens):
    n = q.shape[0] * PROJ_TM
    rows = n // dil
    nb = rows // DIL_M
    nkb = rows // DIL_KBLK
    ratio = DIL_M // DIL_KBLK
    seg0_blocks = seg_tokens[0] // dil // DIL_M
    blocks_per_seq = tuple(t // dil // DIL_M for t in seq_lens)
    assert min(blocks_per_seq) >= 2

    def kv_spec(t):
        return _residue_spec(DIL_KBLK, dil, lambda jb: jnp.clip(ratio * jb - 1 + t, 0, nkb - 1))

    qo_spec = _residue_spec(DIL_M, dil, lambda jb: jb)
    kern = functools.partial(_dilated_kernel, dil=dil, seg0_blocks=seg0_blocks, blocks_per_seq=blocks_per_seq)
    return pl.pallas_call(
        kern,
        grid=(dil, nb),
        in_specs=[pl.BlockSpec(memory_space=pltpu.SMEM), qo_spec]
        + [kv_spec(t) for t in range(4)] + [kv_spec(t) for t in range(4)],
        out_specs=[qo_spec, qo_spec],
        out_shape=[jax.ShapeDtypeStruct(q.shape, jnp.bfloat16), jax.ShapeDtypeStruct(q.shape, jnp.float32)],
        scratch_shapes=[pltpu.VMEM((3, HEADS_PER_DIL_GROUP * DIL_M, DIL_M + 2 * HALF_SPAN), jnp.float32)],
        compiler_params=_cparams(("arbitrary", "arbitrary")),
        name=f"dilated_attention_d{dil}",
    )(table, q, k, k, k, k, v, v, v, v)


NA_DR = 2 * NA_KH - 1
NA_DC = 2 * NA_KW - 1
NA_PAIRS = NA_DR - 1


def _build_na_bias(rpb_ref, pair_scr, head0):
    qc = lax.broadcasted_iota(jnp.int32, (GRID_W, 2 * GRID_W), 0)
    lane = lax.broadcasted_iota(jnp.int32, (GRID_W, 2 * GRID_W), 1)
    kc = lane % GRID_W
    upper = lane >= GRID_W
    qstart = jnp.clip(qc - NA_KW // 2, 0, GRID_W - NA_KW)
    valid = (kc >= qstart) & (kc < qstart + NA_KW)
    dc = jnp.clip(kc - qc, -(NA_KW - 1), NA_KW - 1) + NA_KW - 1

    def one_pair(idx, carry):
        h = idx // NA_PAIRS
        dr = idx % NA_PAIRS
        base = ((head0 + h) * NA_DR + dr) * NA_DC
        val = jnp.zeros((GRID_W, 2 * GRID_W), jnp.float32)
        for c in range(NA_DC):
            val = jnp.where(dc == c, jnp.where(upper, rpb_ref[base + NA_DC + c], rpb_ref[base + c]), val)
        pair_scr[h, dr] = jnp.where(valid, val, NEG_INF)
        return carry

    lax.fori_loop(0, 4 * NA_PAIRS, one_pair, 0)


def _na_kernel(rpb_ref, q_ref, kp_ref, kc_ref, kn_ref, vp_ref, vc_ref, vn_ref, o_ref, k_scr, v_scr, pair_scr,
               *, seg0_blocks, blocks_per_seq):
    ib = pl.program_id(1)

    @pl.when(ib == 0)
    def _():
        _build_na_bias(rpb_ref, pair_scr, pl.program_id(0) * 4)

    in0 = ib < seg0_blocks
    il = jnp.where(in0, ib, ib - seg0_blocks)
    nbs = jnp.where(in0, blocks_per_seq[0], blocks_per_seq[1])
    pos = lax.rem(il, nbs)
    first = pos == 0
    last = pos == nbs - 1
    blk = NA_ROWS * GRID_W
    k_scr[0:blk] = kp_ref[...]
    k_scr[blk:2 * blk] = kc_ref[...]
    k_scr[2 * blk:3 * blk] = kn_ref[...]
    v_scr[0:blk] = vp_ref[...]
    v_scr[blk:2 * blk] = vc_ref[...]
    v_scr[2 * blk:3 * blk] = vn_ref[...]
    half = NA_KH // 2
    for rr in range(NA_ROWS):
        start = jnp.where(first, max(rr + half, NA_ROWS), jnp.where(last, min(rr + half, NA_ROWS), rr + half))
        var = jnp.where(first, min(rr, half), jnp.where(last, max(rr, half), half))
        off = pl.multiple_of(start * GRID_W, GRID_W)
        kk = k_scr[pl.ds(off, NA_KH * GRID_W), :]
        vv = v_scr[pl.ds(off, NA_KH * GRID_W), :]
        qs = _stack_heads(q_ref[rr * GRID_W:(rr + 1) * GRID_W, :])
        s = lax.dot_general(qs, kk, (((1,), (1,)), ((), ())), preferred_element_type=jnp.float32)
        bias = jnp.concatenate(
            [jnp.concatenate([pair_scr[h, 2 * p - var + NA_KH - 1] for p in range(NA_KH // 2)], axis=1)
             for h in range(4)], axis=0)
        s = s + bias
        o, _ = _softmax_pv(s, vv, GRID_W)
        o_ref[rr * GRID_W:(rr + 1) * GRID_W, :] = o.astype(o_ref.dtype)


def _neighborhood(qb, kb, vb, rpb_flat, seg_tokens, seq_lens):
    n = qb.shape[0]
    blk = NA_ROWS * GRID_W
    nblk = n // blk
    seg0_blocks = seg_tokens[0] // blk
    blocks_per_seq = tuple(t // blk for t in seq_lens)
    assert min(blocks_per_seq) >= 2
    nset = QB // HEAD_BLOCK

    def kv_spec(t):
        return pl.BlockSpec((blk, HEAD_BLOCK), lambda c, ib: (jnp.clip(ib - 1 + t, 0, nblk - 1), c))

    kern = functools.partial(_na_kernel, seg0_blocks=seg0_blocks, blocks_per_seq=blocks_per_seq)
    return pl.pallas_call(
        kern,
        grid=(nset, nblk),
        in_specs=[pl.BlockSpec(memory_space=pltpu.SMEM), pl.BlockSpec((blk, HEAD_BLOCK), lambda c, ib: (ib, c))]
        + [kv_spec(t) for t in range(3)] + [kv_spec(t) for t in range(3)],
        out_specs=pl.BlockSpec((blk, HEAD_BLOCK), lambda c, ib: (ib, c)),
        out_shape=jax.ShapeDtypeStruct((n, QB), jnp.bfloat16),
        scratch_shapes=[pltpu.VMEM((3 * blk, HEAD_BLOCK), jnp.bfloat16),
                        pltpu.VMEM((3 * blk, HEAD_BLOCK), jnp.bfloat16),
                        pltpu.VMEM((4, NA_PAIRS, GRID_W, 2 * GRID_W), jnp.float32)],
        compiler_params=_cparams(("arbitrary", "arbitrary")),
        name="neighborhood_attention",
    )(rpb_flat, qb, kb, kb, kb, vb, vb, vb)


def _post_kernel(xp_ref, xs_ref, lng_ref, lnb_ref, o0_ref, o1_ref, o2_ref, l0_ref, l1_ref, l2_ref, ob_ref,
                 ga_ref, gb_ref, wa_ref, wb_ref, wo_ref, bo_ref, g1_ref, b1_ref, wr_ref, br_ref, tri_ref,
                 x1_ref, route_ref, counts_ref, carry_ref, perm_scr, *, n0_tiles):
    i = pl.program_id(0)

    @pl.when(i == 0)
    def _():
        carry_ref[...] = jnp.zeros_like(carry_ref)

    def token_order(ref, slab0):
        dil, per = ref.shape[1], ref.shape[2]
        if dil == 1:
            return ref[0, 0].astype(jnp.float32)
        for r in range(dil):
            val = ref[0, r].astype(jnp.float32)
            for s in range(2):
                perm_scr[slab0 + s, pl.ds(r, per, stride=dil), :] = val[:, s * LANES:(s + 1) * LANES]
        return jnp.concatenate([perm_scr[slab0], perm_scr[slab0 + 1]], axis=1)

    l0, l1, l2 = (token_order(ref, 2 * j) for j, ref in enumerate((l0_ref, l1_ref, l2_ref)))
    lm = jnp.maximum(jnp.maximum(l0, l1), l2)
    e0, e1, e2 = jnp.exp(l0 - lm), jnp.exp(l1 - lm), jnp.exp(l2 - lm)
    inv = 1.0 / (e0 + e1 + e2)
    o0, o1, o2 = (token_order(ref, 6 + 2 * j) for j, ref in enumerate((o0_ref, o1_ref, o2_ref)))
    o_a = (e0 * inv) * o0 + (e1 * inv) * o1 + (e2 * inv) * o2
    y_a = jnp.dot(o_a.astype(jnp.bfloat16), wa_ref[...], preferred_element_type=jnp.float32)
    y_b = jnp.dot(ob_ref[...], wb_ref[...], preferred_element_type=jnp.float32)
    mix = jax.nn.sigmoid(ga_ref[...].astype(jnp.float32)) * y_a + jax.nn.sigmoid(gb_ref[...].astype(jnp.float32)) * y_b
    out = jnp.dot(mix.astype(jnp.bfloat16), wo_ref[...], preferred_element_type=jnp.float32) + bo_ref[...]
    x = jnp.where(i < n0_tiles, xp_ref[...], xs_ref[...])
    x0 = _layer_norm(x, lng_ref[...], lnb_ref[...])
    x1 = _layer_norm(DN_ALPHA * x0 + out, g1_ref[...], b1_ref[...])
    _store_row_tiles(x1_ref, x1)

    r = jnp.dot(x1.astype(jnp.bfloat16), wr_ref[...], preferred_element_type=jnp.float32) + br_ref[...]
    tm = r.shape[0]
    lane = lax.broadcasted_iota(jnp.int32, (tm, ROUTE_LANES), 1)
    neg = jnp.float32(-jnp.inf)
    lg = jnp.where(lane < N_GROUPS, r, neg)
    gmax = jnp.max(lg, axis=-1, keepdims=True)
    gi = jnp.min(jnp.where(lg == gmax, lane, ROUTE_LANES), axis=-1, keepdims=True)
    wg = 1.0 / jnp.sum(jnp.exp(lg - gmax), axis=-1, keepdims=True)
    lo = EXPERT_LANE0 + EXPERTS_PER_GROUP * gi
    le = jnp.where((lane >= lo) & (lane < lo + EXPERTS_PER_GROUP), r, neg)
    m1 = jnp.max(le, axis=-1, keepdims=True)
    i1 = jnp.min(jnp.where(le == m1, lane, ROUTE_LANES), axis=-1, keepdims=True)
    le2 = jnp.where(lane == i1, neg, le)
    m2 = jnp.max(le2, axis=-1, keepdims=True)
    i2 = jnp.min(jnp.where(le2 == m2, lane, ROUTE_LANES), axis=-1, keepdims=True)
    t2 = jnp.exp(m2 - m1)
    w1 = wg / (1.0 + t2)
    w2 = wg * t2 / (1.0 + t2)

    hot1 = lane == i1
    hot2 = lane == i2
    hot = (hot1 | hot2).astype(jnp.float32)
    before = jnp.dot(tri_ref[...], hot.astype(jnp.bfloat16), preferred_element_type=jnp.float32) + carry_ref[0:1, :]
    rank1 = jnp.sum(jnp.where(hot1, before, 0.0), axis=-1, keepdims=True)
    rank2 = jnp.sum(jnp.where(hot2, before, 0.0), axis=-1, keepdims=True)
    total = carry_ref[0:1, :] + jnp.sum(hot, axis=0, keepdims=True)
    carry_ref[...] = jnp.broadcast_to(total, carry_ref.shape)
    counts_ref[...] = jnp.broadcast_to(total, counts_ref.shape)

    e1f = (i1 - EXPERT_LANE0).astype(jnp.float32)
    e2f = (i2 - EXPERT_LANE0).astype(jnp.float32)
    route = jnp.zeros((tm, ROUTE_LANES), jnp.float32)
    for idx, val in enumerate((e1f, e2f, w1, w2, rank1, rank2)):
        route = jnp.where(lane == idx, val, route)
    route_ref[...] = route


def _post_attention(xp, xs, ln_in_g, ln_in_b, o_groups, lse_groups, ob, ga, gb, wa, wb, wo, bo, g1, b1, wr, br):
    n0, d = xp.shape
    n = n0 + xs.shape[0]
    tm = POST_TM
    n0_tiles = n0 // tm
    nt = n // tm
    tri = jnp.asarray(np.tril(np.ones((tm, tm), np.float32), -1), jnp.bfloat16)

    def tok(width):
        return pl.BlockSpec((tm, width), lambda i: (i, 0))

    def full(a):
        return pl.BlockSpec(a.shape, lambda i: (0,) * a.ndim)

    kern = functools.partial(_post_kernel, n0_tiles=n0_tiles)
    return pl.pallas_call(
        kern,
        grid=(nt,),
        in_specs=[
            pl.BlockSpec((tm, d), lambda i: (jnp.minimum(i, n0_tiles - 1), 0)),
            pl.BlockSpec((tm, d), lambda i: (jnp.maximum(i - n0_tiles, 0), 0)),
            full(ln_in_g), full(ln_in_b),
            *[pl.BlockSpec((1,) + a.shape[1:], lambda i: (i, 0, 0, 0)) for a in (*o_groups, *lse_groups)],
            tok(QB), tok(d), tok(d),
            full(wa), full(wb), full(wo), full(bo), full(g1), full(b1), full(wr), full(br), full(tri),
        ],
        out_specs=[pl.BlockSpec((tm * SUBLANES, LANES), lambda i: (i, 0)), tok(ROUTE_LANES),
                   pl.BlockSpec((8, ROUTE_LANES), lambda i: (0, 0))],
        out_shape=[jax.ShapeDtypeStruct((n * SUBLANES, LANES), jnp.float32),
                   jax.ShapeDtypeStruct((n, ROUTE_LANES), jnp.float32),
                   jax.ShapeDtypeStruct((8, ROUTE_LANES), jnp.float32)],
        scratch_shapes=[pltpu.VMEM((8, ROUTE_LANES), jnp.float32), pltpu.VMEM((12, tm, LANES), jnp.float32)],
        compiler_params=_cparams(("arbitrary",)),
        name="post_attention_router",
    )(xp, xs, ln_in_g, ln_in_b, *o_groups, *lse_groups, ob, ga, gb, wa, wb, wo, bo, g1, b1, wr, br, tri)


def _dispatch_kernel(pos_ref, x_ref, xs_ref, sem):
    tt = x_ref.shape[0] // SUBLANES

    def issue(j, carry):
        for u in range(2):
            t = 2 * j + u
            src = x_ref.at[pl.ds(pl.multiple_of(t * SUBLANES, SUBLANES), SUBLANES)]
            for slot in range(2):
                p = pos_ref[0, slot, t]
                dst = xs_ref.at[pl.ds(pl.multiple_of(p * SUBLANES, SUBLANES), SUBLANES)]
                pltpu.make_async_copy(src, dst, sem).start(priority=u)
        return carry

    lax.fori_loop(0, tt // 2, issue, 0, unroll=4)
    for _ in range(2):
        pltpu.make_async_copy(x_ref, xs_ref.at[pl.ds(0, tt * SUBLANES)], sem).wait()


def _dispatch(x1t, pos, n_rows):
    n = x1t.shape[0] // SUBLANES
    tt = DISP_TT
    return pl.pallas_call(
        _dispatch_kernel,
        grid=(n // tt,),
        in_specs=[pl.BlockSpec((1, 2, tt), lambda i: (i, 0, 0), memory_space=pltpu.SMEM),
                  pl.BlockSpec((tt * SUBLANES, LANES), lambda i: (i, 0))],
        out_specs=pl.BlockSpec(memory_space=pl.ANY),
        out_shape=jax.ShapeDtypeStruct((n_rows * SUBLANES, LANES), x1t.dtype),
        scratch_shapes=[pltpu.SemaphoreType.DMA(())],
        compiler_params=_cparams(("arbitrary",)),
        name="moe_dispatch",
    )(pos.reshape(n // tt, tt, 2).transpose(0, 2, 1), x1t)


def _ffn_kernel(item_expert_ref, item_tile_ref, item_lo_ref, item_hi_ref, n_items_ref,
                x_ref, wg_ref, wu_ref, wd_ref, y_ref):
    j = pl.program_id(0)

    @pl.when(j < n_items_ref[0])
    def _():
        tm = x_ref.shape[0] // SUBLANES
        x = _load_row_tiles(x_ref, tm).astype(jnp.bfloat16)
        a = jnp.dot(x, wg_ref[0], preferred_element_type=jnp.float32)
        u = jnp.dot(x, wu_ref[0], preferred_element_type=jnp.float32)
        h = (a * jax.nn.sigmoid(a) * u).astype(jnp.bfloat16)
        y = jnp.dot(h, wd_ref[0], preferred_element_type=jnp.float32)
        rows = lax.broadcasted_iota(jnp.int32, (tm, 1), 0)
        y = jnp.where((rows >= item_lo_ref[j]) & (rows < item_hi_ref[j]), y, 0.0)
        first_of_tile = jnp.logical_or(j == 0, item_tile_ref[j] != item_tile_ref[jnp.maximum(j - 1, 0)])

        @pl.when(first_of_tile)
        def _():
            _store_row_tiles(y_ref, y)

        @pl.when(jnp.logical_not(first_of_tile))
        def _():
            _store_row_tiles(y_ref, y, accumulate=True)


def _ffn_items(cnt, n_rows):
    tm = FFN_TM
    n_tiles = n_rows // tm
    max_items = n_tiles + N_EXPERTS - 1
    ends = jnp.cumsum(cnt)
    starts = ends - cnt
    first_tile = starts // tm
    last_tile = jnp.maximum(ends - 1, 0) // tm
    items_per_expert = jnp.where(cnt > 0, last_tile - first_tile + 1, 0)
    item_ends = jnp.cumsum(items_per_expert)
    n_items = item_ends[-1:]
    j = jnp.minimum(jnp.arange(max_items, dtype=jnp.int32), n_items[0] - 1)
    expert = jnp.minimum(jnp.searchsorted(item_ends, j, side="right"), N_EXPERTS - 1).astype(jnp.int32)
    tile = first_tile[expert] + (j - (item_ends - items_per_expert)[expert])
    lo = jnp.clip(starts[expert] - tile * tm, 0, tm)
    hi = jnp.clip(ends[expert] - tile * tm, 0, tm)
    i32 = lambda a: a.astype(jnp.int32)
    return i32(expert), i32(tile), i32(lo), i32(hi), i32(n_items), starts


def _grouped_ffn(xs, items, w_gate, w_up, w_down):
    tm = FFN_TM
    d, de = w_gate.shape[1:]
    item_expert, item_tile, item_lo, item_hi, n_items = items
    max_items = item_expert.shape[0]

    def row_map(j, ie, it, lo, hi, ni):
        return (it[j], 0)

    def w_map(j, ie, it, lo, hi, ni):
        return (ie[j], 0, 0)

    grid_spec = pltpu.PrefetchScalarGridSpec(
        num_scalar_prefetch=5,
        grid=(max_items,),
        in_specs=[pl.BlockSpec((tm * SUBLANES, LANES), row_map),
                  pl.BlockSpec((1, d, de), w_map),
                  pl.BlockSpec((1, d, de), w_map),
                  pl.BlockSpec((1, de, d), w_map)],
        out_specs=pl.BlockSpec((tm * SUBLANES, LANES), row_map),
    )
    return pl.pallas_call(
        _ffn_kernel,
        grid_spec=grid_spec,
        out_shape=jax.ShapeDtypeStruct(xs.shape, jnp.float32),
        compiler_params=_cparams(("arbitrary",)),
        name="moe_grouped_ffn",
    )(item_expert, item_tile, item_lo, item_hi, n_items, xs, w_gate, w_up, w_down)


def _combine_kernel(pos_ref, pos_next_ref, x1_ref, route_ref, g2_ref, b2_ref, ys_ref, o_ref, buf, sems):
    tt = x1_ref.shape[0] // SUBLANES
    i = pl.program_id(0)
    cur = lax.rem(i, 2)

    def issue_tile(p_ref, b):
        def issue(j, carry):
            for u in range(2):
                t = 2 * j + u
                for slot in range(2):
                    p = p_ref[0, slot, t]
                    src = ys_ref.at[pl.ds(pl.multiple_of(p * SUBLANES, SUBLANES), SUBLANES)]
                    dst = buf.at[b, slot, pl.ds(pl.multiple_of(t * SUBLANES, SUBLANES), SUBLANES)]
                    pltpu.make_async_copy(src, dst, sems.at[b]).start(priority=u)
            return carry

        lax.fori_loop(0, tt // 2, issue, 0, unroll=4)

    @pl.when(i == 0)
    def _():
        issue_tile(pos_ref, 0)

    @pl.when(i + 1 < pl.num_programs(0))
    def _():
        issue_tile(pos_next_ref, 1 - cur)

    for slot in range(2):
        pltpu.make_async_copy(ys_ref.at[pl.ds(0, tt * SUBLANES)], buf.at[cur, slot], sems.at[cur]).wait()
    route = route_ref[...]
    y = route[:, 2:3] * _load_row_tiles(buf.at[cur, 0], tt) + route[:, 3:4] * _load_row_tiles(buf.at[cur, 1], tt)
    o_ref[...] = _layer_norm(DN_ALPHA * _load_row_tiles(x1_ref, tt) + y, g2_ref[...], b2_ref[...])


def _combine(x1t, route, pos, ys, g2, b2, tok0, n_tok):
    d = g2.shape[1]
    tt = COMB_TT
    t0 = tok0 // tt
    pos3 = pos.reshape(pos.shape[0] // tt, tt, 2).transpose(0, 2, 1)
    nt = n_tok // tt
    return pl.pallas_call(
        _combine_kernel,
        grid=(nt,),
        in_specs=[pl.BlockSpec((1, 2, tt), lambda i: (i + t0, 0, 0), memory_space=pltpu.SMEM),
                  pl.BlockSpec((1, 2, tt), lambda i: (jnp.minimum(i + 1, nt - 1) + t0, 0, 0), memory_space=pltpu.SMEM),
                  pl.BlockSpec((tt * SUBLANES, LANES), lambda i: (i + t0, 0)),
                  pl.BlockSpec((tt, ROUTE_LANES), lambda i: (i + t0, 0)),
                  pl.BlockSpec((1, d), lambda i: (0, 0)),
                  pl.BlockSpec((1, d), lambda i: (0, 0)),
                  pl.BlockSpec(memory_space=pl.ANY)],
        out_specs=pl.BlockSpec((tt, d), lambda i: (i, 0)),
        out_shape=jax.ShapeDtypeStruct((n_tok, d), jnp.float32),
        scratch_shapes=[pltpu.VMEM((2, 2, tt * SUBLANES, LANES), ys.dtype), pltpu.SemaphoreType.DMA((2,))],
        compiler_params=_cparams(("arbitrary",)),
        name="moe_combine_ln",
    )(pos3, pos3, x1t, route, g2, b2, ys)


def _layer(x_prompt, x_sample, ln_in_g, ln_in_b, w_in, b_in, rel_bias_t5, na_rpb, w_branch_a, w_branch_b,
           w_out, b_out, ln1_g, ln1_b, w_router_group, b_router_group, w_router_expert, b_router_expert,
           w_exp_gate, w_exp_up, w_exp_down, ln2_g, ln2_b):
    bp, tp, d = x_prompt.shape
    bs, ts, _ = x_sample.shape
    seg_tokens = (bp * tp, bs * ts)
    seq_lens = (tp, ts)
    n = sum(seg_tokens)
    xp = x_prompt.reshape(seg_tokens[0], d)
    xs = x_sample.reshape(seg_tokens[1], d)
    row = lambda a: a.reshape(1, -1).astype(jnp.float32)
    bf = lambda a: a.astype(jnp.bfloat16)

    scale = HEAD_DIM ** -0.5
    col_scale = np.ones((w_in.shape[2],), np.float32)
    col_scale[0:QA] = scale
    col_scale[3 * QA:3 * QA + QB] = scale
    w_proj = bf(w_in[0] * col_scale)
    b_proj = row(b_in[0] * col_scale)
    proj = _in_projection(xp, xs, row(ln_in_g), row(ln_in_b), w_proj, b_proj)
    qkv_a, (qb, kb, vb, ga, gb) = proj[:3 * N_DIL_GROUPS], proj[3 * N_DIL_GROUPS:]

    o_groups, lse_groups = [], []
    for g, (window, dil) in enumerate(DIL_PATTERNS):
        assert window // (2 * dil) == HALF_SPAN
        table = rel_bias_t5[:, g * HEADS_PER_DIL_GROUP:(g + 1) * HEADS_PER_DIL_GROUP].astype(jnp.float32).reshape(-1)
        o, lse = _dilated_group(qkv_a[g], qkv_a[N_DIL_GROUPS + g], qkv_a[2 * N_DIL_GROUPS + g], table, dil,
                                seg_tokens, seq_lens)
        o_groups.append(o)
        lse_groups.append(lse)
    ob = _neighborhood(qb, kb, vb, na_rpb[0].astype(jnp.float32).reshape(-1), seg_tokens, seq_lens)

    wr = jnp.zeros((d, ROUTE_LANES), jnp.float32)
    wr = wr.at[:, 0:N_GROUPS].set(w_router_group[0]).at[:, EXPERT_LANE0:EXPERT_LANE0 + N_EXPERTS].set(w_router_expert[0])
    br = jnp.zeros((1, ROUTE_LANES), jnp.float32)
    br = br.at[0, 0:N_GROUPS].set(b_router_group[0]).at[0, EXPERT_LANE0:EXPERT_LANE0 + N_EXPERTS].set(b_router_expert[0])
    x1, route, counts = _post_attention(
        xp, xs, row(ln_in_g), row(ln_in_b), o_groups, lse_groups, ob, ga, gb, bf(w_branch_a[0]), bf(w_branch_b[0]),
        bf(w_out[0]), row(b_out[0]), row(ln1_g[0]), row(ln1_b[0]), bf(wr), br)

    experts = route[:, 0:2].astype(jnp.int32)
    ranks = route[:, 4:6].astype(jnp.int32)
    cnt = counts[0, EXPERT_LANE0:EXPERT_LANE0 + N_EXPERTS].astype(jnp.int32)
    n_rows = 2 * n
    *items, starts = _ffn_items(cnt, n_rows)
    pos = starts[experts] + ranks

    xsorted = _dispatch(x1, pos, n_rows)
    ys = _grouped_ffn(xsorted, items, bf(w_exp_gate[0]), bf(w_exp_up[0]), bf(w_exp_down[0]))
    y_prompt = _combine(x1, route, pos, ys, row(ln2_g[0]), row(ln2_b[0]), 0, seg_tokens[0])
    y_sample = _combine(x1, route, pos, ys, row(ln2_g[0]), row(ln2_b[0]), seg_tokens[0], seg_tokens[1])
    return y_prompt.reshape(bp, tp, d), y_sample.reshape(bs, ts, d)


def kernel(x_prompt, x_sample, ln_in_g, ln_in_b, w_in, b_in, rel_bias_t5, na_rpb, w_branch_a, w_branch_b, w_out, b_out, ln1_g, ln1_b, w_router_group, b_router_group, w_router_expert, b_router_expert, w_exp_gate, w_exp_up, w_exp_down, ln2_g, ln2_b):
    return _layer(x_prompt, x_sample, ln_in_g, ln_in_b, w_in, b_in, rel_bias_t5, na_rpb, w_branch_a, w_branch_b,
                  w_out, b_out, ln1_g, ln1_b, w_router_group, b_router_group, w_router_expert, b_router_expert,
                  w_exp_gate, w_exp_up, w_exp_down, ln2_g, ln2_b)
```

```python
import functools
import math

import numpy as np
import jax
import jax.numpy as jnp
from jax import lax
from jax.experimental import pallas as pl
from jax.experimental.pallas import tpu as pltpu

HEAD_DIM = 64
DIL_PATTERNS = ((128, 1), (512, 4), (2048, 16))
HEADS_PER_DIL_GROUP = 4
N_DIL_GROUPS = 3
H_A = N_DIL_GROUPS * HEADS_PER_DIL_GROUP
H_B = 8
QA = H_A * HEAD_DIM
QB = H_B * HEAD_DIM
HALF_SPAN = 64
GRID_W = 64
NA_KH = 8
NA_KW = 16
N_BUCKETS = 32
REL_MAX_DIST = 1024
N_GROUPS = 4
EXPERTS_PER_GROUP = 8
N_EXPERTS = N_GROUPS * EXPERTS_PER_GROUP
LN_EPS = 1e-5
NEG_INF = -1e30
DEPTH = 1
DN_ALPHA = (2.0 * DEPTH) ** 0.25

LANES = 128
HEAD_BLOCK = 4 * HEAD_DIM
VMEM_LIMIT_BYTES = 56 * 1024 * 1024

PROJ_TM = 512
DIL_M = 128
DIL_GROUP = 4
NA_ROWS = 8
POST_TM = 512
FFN_TM = 512
DISP_TT = 2048
COMB_TT = 512
ROUTE_LANES = LANES
EXPERT_LANE0 = 32


def _cparams(sem):
    return pltpu.CompilerParams(dimension_semantics=sem, vmem_limit_bytes=VMEM_LIMIT_BYTES)


SUBLANES = 8


def _store_row_tiles(ref, x, accumulate=False):
    rows = x.shape[0]
    for c in range(SUBLANES):
        idx = pl.ds(c, rows, stride=SUBLANES)
        piece = x[:, c * LANES:(c + 1) * LANES]
        ref[idx, :] = ref[idx, :] + piece if accumulate else piece


def _load_row_tiles(ref, rows):
    return jnp.concatenate([ref[pl.ds(c, rows, stride=SUBLANES), :] for c in range(SUBLANES)], axis=1)


def _layer_norm(x, g, b):
    mu = jnp.mean(x, axis=-1, keepdims=True)
    xc = x - mu
    var = jnp.mean(xc * xc, axis=-1, keepdims=True)
    return xc * lax.rsqrt(var + LN_EPS) * g + b


def _strided_rows(scr, slab0, r, count, stride):
    return jnp.concatenate([scr[slab0 + s, pl.ds(r, count, stride=stride), :] for s in range(2)], axis=1)


def _inproj_kernel(xp_ref, xs_ref, g_ref, b_ref, w_ref, bias_ref, *refs, n0_tiles):
    out_refs, scr = refs[:-1], refs[-1]
    i = pl.program_id(0)
    x = jnp.where(i < n0_tiles, xp_ref[...], xs_ref[...])
    xn = _layer_norm(x, g_ref[...], b_ref[...]).astype(jnp.bfloat16)
    tm = xn.shape[0]

    def proj(c0, cw):
        return jnp.dot(xn, w_ref[:, c0:c0 + cw], preferred_element_type=jnp.float32) + bias_ref[:, c0:c0 + cw]

    n_dil = 3 * N_DIL_GROUPS
    for idx in range(n_dil):
        g = idx % N_DIL_GROUPS
        dil = DIL_PATTERNS[g][1]
        o_ref = out_refs[idx]
        y = proj(idx * HEAD_BLOCK, HEAD_BLOCK)
        if dil == 1:
            o_ref[0, 0] = y.astype(o_ref.dtype)
        else:
            slab0 = 2 * (idx % 2)
            scr[slab0] = y[:, :LANES]
            scr[slab0 + 1] = y[:, LANES:]
            for r in range(dil):
                o_ref[0, r] = _strided_rows(scr, slab0, r, tm // dil, dil).astype(o_ref.dtype)
    col = n_dil * HEAD_BLOCK
    for o_ref in out_refs[n_dil:]:
        width = o_ref.shape[1]
        for c in range(0, width, 512):
            cw = min(512, width - c)
            o_ref[:, c:c + cw] = proj(col + c, cw).astype(o_ref.dtype)
        col += width


def _in_projection(xp, xs, ln_g, ln_b, w, b):
    n0, d_model = xp.shape
    n1 = xs.shape[0]
    n = n0 + n1
    tm = PROJ_TM
    n0_tiles, n1_tiles = n0 // tm, n1 // tm
    nt = n0_tiles + n1_tiles
    out_shape, out_specs = [], []
    for _ in range(3):
        for _, dil in DIL_PATTERNS:
            out_shape.append(jax.ShapeDtypeStruct((nt, dil, tm // dil, HEAD_BLOCK), jnp.bfloat16))
            out_specs.append(pl.BlockSpec((1, dil, tm // dil, HEAD_BLOCK), lambda i: (i, 0, 0, 0)))
    for wd in (QB, QB, QB, d_model, d_model):
        out_shape.append(jax.ShapeDtypeStruct((n, wd), jnp.bfloat16))
        out_specs.append(pl.BlockSpec((tm, wd), lambda i: (i, 0)))
    d_in = w.shape[1]
    kern = functools.partial(_inproj_kernel, n0_tiles=n0_tiles)
    return pl.pallas_call(
        kern,
        grid=(nt,),
        in_specs=[
            pl.BlockSpec((tm, d_model), lambda i: (jnp.minimum(i, n0_tiles - 1), 0)),
            pl.BlockSpec((tm, d_model), lambda i: (jnp.maximum(i - n0_tiles, 0), 0)),
            pl.BlockSpec((1, d_model), lambda i: (0, 0)),
            pl.BlockSpec((1, d_model), lambda i: (0, 0)),
            pl.BlockSpec((d_model, d_in), lambda i: (0, 0)),
            pl.BlockSpec((1, d_in), lambda i: (0, 0)),
        ],
        out_specs=out_specs,
        out_shape=out_shape,
        scratch_shapes=[pltpu.VMEM((4, tm, LANES), jnp.float32)],
        compiler_params=_cparams(("arbitrary",)),
        name="in_projection",
    )(xp, xs, ln_g, ln_b, w, b)


def _t5_bucket_np(rel):
    half = N_BUCKETS // 2
    max_exact = half // 2
    ret = np.where(rel > 0, half, 0)
    n = np.abs(rel)
    nf = np.maximum(n, 1).astype(np.float32)
    large = max_exact + (np.log(nf / np.float32(max_exact)) / np.float32(math.log(REL_MAX_DIST / max_exact))
                         * np.float32(half - max_exact)).astype(np.int32)
    large = np.minimum(large, half - 1)
    return ret + np.where(n < max_exact, n, large)


def _t5_bucket_starts():
    half = N_BUCKETS // 2
    dist = np.arange(0, HALF_SPAN * max(d for _, d in DIL_PATTERNS) + 1)
    buckets = _t5_bucket_np(-dist)
    return [int(np.argmax(buckets >= k)) for k in range(half)]


def _build_dilated_bias(table_ref, bias_scr, dil):
    half = N_BUCKETS // 2
    starts = _t5_bucket_starts()
    kb = DIL_M + 2 * HALF_SPAN
    i = lax.broadcasted_iota(jnp.int32, (DIL_M, kb), 0)
    m = lax.broadcasted_iota(jnp.int32, (DIL_M, kb), 1)
    delta = m - HALF_SPAN - i
    dist = jnp.abs(delta) * dil
    band = jnp.abs(delta) <= HALF_SPAN
    for h in range(HEADS_PER_DIL_GROUP):
        sides = []
        for side in range(2):
            val = jnp.full((DIL_M, kb), table_ref[(side * half + half - 1) * HEADS_PER_DIL_GROUP + h], jnp.float32)
            for k in range(half - 1, 0, -1):
                val = jnp.where(dist < starts[k], table_ref[(side * half + k - 1) * HEADS_PER_DIL_GROUP + h], val)
            sides.append(val)
        base = jnp.where(band, jnp.where(delta > 0, sides[1], sides[0]), NEG_INF)
        rows = slice(h * DIL_M, (h + 1) * DIL_M)
        bias_scr[0, rows, :] = base
        bias_scr[1, rows, :] = jnp.where(m >= HALF_SPAN, base, NEG_INF)
        bias_scr[2, rows, :] = jnp.where(m < kb - HALF_SPAN, base, NEG_INF)


def _stack_heads(q):
    head_of_col = lax.broadcasted_iota(jnp.int32, (1, HEAD_BLOCK), 1) // HEAD_DIM
    zero = jnp.zeros_like(q)
    return jnp.concatenate([jnp.where(head_of_col == h, q, zero) for h in range(4)], axis=0)


def _merge_heads(x, m):
    head_of_col = lax.broadcasted_iota(jnp.int32, (1, HEAD_BLOCK), 1) // HEAD_DIM
    out = jnp.zeros((m, HEAD_BLOCK), x.dtype)
    for h in range(4):
        out = jnp.where(head_of_col == h, x[h * m:(h + 1) * m], out)
    return out


def _softmax_pv(s, v, m_rows):
    mx = jnp.max(s, axis=-1, keepdims=True)
    p = jnp.exp(s - mx)
    l = jnp.sum(p, axis=-1, keepdims=True)
    pv = jnp.dot(p.astype(jnp.bfloat16), v, preferred_element_type=jnp.float32)
    o = _merge_heads(pv * (1.0 / l), m_rows)
    return o, mx + jnp.log(l)


def _dilated_kernel(table_ref, q_ref, kp_ref, kc_ref, kn_ref, vp_ref, vc_ref, vn_ref,
                    o_ref, lse_ref, bias_scr, *, dil, seg0_blocks, blocks_per_seq):
    step = pl.program_id(1)

    @pl.when((pl.program_id(0) == 0) & (step == 0))
    def _():
        _build_dilated_bias(table_ref, bias_scr, dil)

    rows2d = lambda ref: ref[...].reshape(-1, HEAD_BLOCK)
    k = jnp.concatenate([rows2d(r) for r in (kp_ref, kc_ref, kn_ref)], axis=0)
    v = jnp.concatenate([rows2d(r) for r in (vp_ref, vc_ref, vn_ref)], axis=0)
    q = rows2d(q_ref)
    o_parts, lse_parts = [], []
    for g in range(DIL_GROUP):
        jb = step * DIL_GROUP + g
        in0 = jb < seg0_blocks
        jl = jnp.where(in0, jb, jb - seg0_blocks)
        nbs = jnp.where(in0, blocks_per_seq[0], blocks_per_seq[1])
        pos = lax.rem(jl, nbs)
        variant = jnp.where(pos == 0, 1, jnp.where(pos == nbs - 1, 2, 0))
        keys = slice(g * DIL_M, (g + 1) * DIL_M + 2 * HALF_SPAN)
        qs = _stack_heads(q[g * DIL_M:(g + 1) * DIL_M])
        s = lax.dot_general(qs, k[keys], (((1,), (1,)), ((), ())), preferred_element_type=jnp.float32)
        s = s + bias_scr[variant]
        o, lse = _softmax_pv(s, v[keys], DIL_M)
        o_parts.append(o.astype(o_ref.dtype))
        lse_parts.append(_merge_heads(jnp.broadcast_to(lse, (4 * DIL_M, HEAD_BLOCK)), DIL_M))
    o_ref[...] = jnp.concatenate(o_parts, axis=0).reshape(o_ref.shape)
    lse_ref[...] = jnp.concatenate(lse_parts, axis=0).reshape(lse_ref.shape)


def _residue_spec(rows_blk, dil, block_of):
    per_tile = PROJ_TM // dil
    if per_tile >= rows_blk:
        per = per_tile // rows_blk
        return pl.BlockSpec((1, None, rows_blk, HEAD_BLOCK),
                            lambda r, jb: (block_of(jb) // per, r, block_of(jb) % per, 0))
    return pl.BlockSpec((rows_blk // per_tile, None, per_tile, HEAD_BLOCK), lambda r, jb: (block_of(jb), r, 0, 0))


def _dilated_group(q, k, v, table, dil, seg_tokens, seq_lens):
    n = q.shape[0] * PROJ_TM
    rows = n // dil
    step_rows = DIL_GROUP * DIL_M
    assert rows % step_rows == 0 and all(t % (dil * DIL_M) == 0 for t in seq_lens)
    nsteps = rows // step_rows
    nkb = rows // HALF_SPAN
    ratio = step_rows // HALF_SPAN
    seg0_blocks = seg_tokens[0] // dil // DIL_M
    blocks_per_seq = tuple(t // dil // DIL_M for t in seq_lens)
    assert min(blocks_per_seq) >= 2

    qo_spec = _residue_spec(step_rows, dil, lambda jb: jb)
    prev_spec = _residue_spec(HALF_SPAN, dil, lambda jb: jnp.maximum(ratio * jb - 1, 0))
    next_spec = _residue_spec(HALF_SPAN, dil, lambda jb: jnp.minimum(ratio * (jb + 1), nkb - 1))
    kv_specs = [prev_spec, qo_spec, next_spec]
    kern = functools.partial(_dilated_kernel, dil=dil, seg0_blocks=seg0_blocks, blocks_per_seq=blocks_per_seq)
    return pl.pallas_call(
        kern,
        grid=(dil, nsteps),
        in_specs=[pl.BlockSpec(memory_space=pltpu.SMEM), qo_spec] + kv_specs + kv_specs,
        out_specs=[qo_spec, qo_spec],
        out_shape=[jax.ShapeDtypeStruct(q.shape, jnp.bfloat16), jax.ShapeDtypeStruct(q.shape, jnp.float32)],
        scratch_shapes=[pltpu.VMEM((3, HEADS_PER_DIL_GROUP * DIL_M, DIL_M + 2 * HALF_SPAN), jnp.float32)],
        compiler_params=_cparams(("arbitrary", "arbitrary")),
        name=f"dilated_attention_d{dil}",
    )(table, q, k, k, k, v, v, v)


NA_DR = 2 * NA_KH - 1
NA_DC = 2 * NA_KW - 1
NA_PAIRS = NA_DR - 1


def _build_na_bias(rpb_ref, pair_scr, head0):
    qc = lax.broadcasted_iota(jnp.int32, (GRID_W, 2 * GRID_W), 0)
    lane = lax.broadcasted_iota(jnp.int32, (GRID_W, 2 * GRID_W), 1)
    kc = lane % GRID_W
    upper = lane >= GRID_W
    qstart = jnp.clip(qc - NA_KW // 2, 0, GRID_W - NA_KW)
    valid = (kc >= qstart) & (kc < qstart + NA_KW)
    dc = jnp.clip(kc - qc, -(NA_KW - 1), NA_KW - 1) + NA_KW - 1

    def one_pair(idx, carry):
        h = idx // NA_PAIRS
        dr = idx % NA_PAIRS
        base = ((head0 + h) * NA_DR + dr) * NA_DC
        val = jnp.zeros((GRID_W, 2 * GRID_W), jnp.float32)
        for c in range(NA_DC):
            val = jnp.where(dc == c, jnp.where(upper, rpb_ref[base + NA_DC + c], rpb_ref[base + c]), val)
        pair_scr[h, dr] = jnp.where(valid, val, NEG_INF)
        return carry

    lax.fori_loop(0, 4 * NA_PAIRS, one_pair, 0)


def _na_kernel(rpb_ref, q_ref, kp_ref, kc_ref, kn_ref, vp_ref, vc_ref, vn_ref, o_ref, k_scr, v_scr, pair_scr,
               *, seg0_blocks, blocks_per_seq):
    ib = pl.program_id(1)

    @pl.when(ib == 0)
    def _():
        _build_na_bias(rpb_ref, pair_scr, pl.program_id(0) * 4)

    in0 = ib < seg0_blocks
    il = jnp.where(in0, ib, ib - seg0_blocks)
    nbs = jnp.where(in0, blocks_per_seq[0], blocks_per_seq[1])
    pos = lax.rem(il, nbs)
    first = pos == 0
    last = pos == nbs - 1
    blk = NA_ROWS * GRID_W
    k_scr[0:blk] = kp_ref[...]
    k_scr[blk:2 * blk] = kc_ref[...]
    k_scr[2 * blk:3 * blk] = kn_ref[...]
    v_scr[0:blk] = vp_ref[...]
    v_scr[blk:2 * blk] = vc_ref[...]
    v_scr[2 * blk:3 * blk] = vn_ref[...]
    half = NA_KH // 2
    for rr in range(NA_ROWS):
        start = jnp.where(first, max(rr + half, NA_ROWS), jnp.where(last, min(rr + half, NA_ROWS), rr + half))
        var = jnp.where(first, min(rr, half), jnp.where(last, max(rr, half), half))
        off = pl.multiple_of(start * GRID_W, GRID_W)
        kk = k_scr[pl.ds(off, NA_KH * GRID_W), :]
        vv = v_scr[pl.ds(off, NA_KH * GRID_W), :]
        qs = _stack_heads(q_ref[rr * GRID_W:(rr + 1) * GRID_W, :])
        s = lax.dot_general(qs, kk, (((1,), (1,)), ((), ())), preferred_element_type=jnp.float32)
        bias = jnp.concatenate(
            [jnp.concatenate([pair_scr[h, 2 * p - var + NA_KH - 1] for p in range(NA_KH // 2)], axis=1)
             for h in range(4)], axis=0)
        s = s + bias
        o, _ = _softmax_pv(s, vv, GRID_W)
        o_ref[rr * GRID_W:(rr + 1) * GRID_W, :] = o.astype(o_ref.dtype)


def _neighborhood(qb, kb, vb, rpb_flat, seg_tokens, seq_lens):
    n = qb.shape[0]
    blk = NA_ROWS * GRID_W
    nblk = n // blk
    seg0_blocks = seg_tokens[0] // blk
    blocks_per_seq = tuple(t // blk for t in seq_lens)
    assert min(blocks_per_seq) >= 2
    nset = QB // HEAD_BLOCK

    def kv_spec(t):
        return pl.BlockSpec((blk, HEAD_BLOCK), lambda c, ib: (jnp.clip(ib - 1 + t, 0, nblk - 1), c))

    kern = functools.partial(_na_kernel, seg0_blocks=seg0_blocks, blocks_per_seq=blocks_per_seq)
    return pl.pallas_call(
        kern,
        grid=(nset, nblk),
        in_specs=[pl.BlockSpec(memory_space=pltpu.SMEM), pl.BlockSpec((blk, HEAD_BLOCK), lambda c, ib: (ib, c))]
        + [kv_spec(t) for t in range(3)] + [kv_spec(t) for t in range(3)],
        out_specs=pl.BlockSpec((blk, HEAD_BLOCK), lambda c, ib: (ib, c)),
        out_shape=jax.ShapeDtypeStruct((n, QB), jnp.bfloat16),
        scratch_shapes=[pltpu.VMEM((3 * blk, HEAD_BLOCK), jnp.bfloat16),
                        pltpu.VMEM((3 * blk, HEAD_BLOCK), jnp.bfloat16),
                        pltpu.VMEM((4, NA_PAIRS, GRID_W, 2 * GRID_W), jnp.float32)],
        compiler_params=_cparams(("arbitrary", "arbitrary")),
        name="neighborhood_attention",
    )(rpb_flat, qb, kb, kb, kb, vb, vb, vb)


def _post_kernel(xp_ref, xs_ref, lng_ref, lnb_ref, o0_ref, o1_ref, o2_ref, l0_ref, l1_ref, l2_ref, ob_ref,
                 ga_ref, gb_ref, wa_ref, wb_ref, wo_ref, bo_ref, g1_ref, b1_ref, wr_ref, br_ref, tri_ref,
                 x1_ref, route_ref, counts_ref, carry_ref, perm_scr, *, n0_tiles):
    i = pl.program_id(0)

    @pl.when(i == 0)
    def _():
        carry_ref[...] = jnp.zeros_like(carry_ref)

    def token_order(ref, slab0):
        dil, per = ref.shape[1], ref.shape[2]
        if dil == 1:
            return ref[0, 0].astype(jnp.float32)
        for r in range(dil):
            val = ref[0, r].astype(jnp.float32)
            for s in range(2):
                perm_scr[slab0 + s, pl.ds(r, per, stride=dil), :] = val[:, s * LANES:(s + 1) * LANES]
        return jnp.concatenate([perm_scr[slab0], perm_scr[slab0 + 1]], axis=1)

    l0, l1, l2 = (token_order(ref, 2 * j) for j, ref in enumerate((l0_ref, l1_ref, l2_ref)))
    lm = jnp.maximum(jnp.maximum(l0, l1), l2)
    e0, e1, e2 = jnp.exp(l0 - lm), jnp.exp(l1 - lm), jnp.exp(l2 - lm)
    inv = 1.0 / (e0 + e1 + e2)
    o0, o1, o2 = (token_order(ref, 6 + 2 * j) for j, ref in enumerate((o0_ref, o1_ref, o2_ref)))
    o_a = (e0 * inv) * o0 + (e1 * inv) * o1 + (e2 * inv) * o2
    y_a = jnp.dot(o_a.astype(jnp.bfloat16), wa_ref[...], preferred_element_type=jnp.float32)
    y_b = jnp.dot(ob_ref[...], wb_ref[...], preferred_element_type=jnp.float32)
    mix = jax.nn.sigmoid(ga_ref[...].astype(jnp.float32)) * y_a + jax.nn.sigmoid(gb_ref[...].astype(jnp.float32)) * y_b
    out = jnp.dot(mix.astype(jnp.bfloat16), wo_ref[...], preferred_element_type=jnp.float32) + bo_ref[...]
    x = jnp.where(i < n0_tiles, xp_ref[...], xs_ref[...])
    x0 = _layer_norm(x, lng_ref[...], lnb_ref[...])
    x1 = _layer_norm(DN_ALPHA * x0 + out, g1_ref[...], b1_ref[...])
    _store_row_tiles(x1_ref, x1)

    r = jnp.dot(x1.astype(jnp.bfloat16), wr_ref[...], preferred_element_type=jnp.float32) + br_ref[...]
    tm = r.shape[0]
    lane = lax.broadcasted_iota(jnp.int32, (tm, ROUTE_LANES), 1)
    neg = jnp.float32(-jnp.inf)
    lg = jnp.where(lane < N_GROUPS, r, neg)
    gmax = jnp.max(lg, axis=-1, keepdims=True)
    gi = jnp.min(jnp.where(lg == gmax, lane, ROUTE_LANES), axis=-1, keepdims=True)
    wg = 1.0 / jnp.sum(jnp.exp(lg - gmax), axis=-1, keepdims=True)
    lo = EXPERT_LANE0 + EXPERTS_PER_GROUP * gi
    le = jnp.where((lane >= lo) & (lane < lo + EXPERTS_PER_GROUP), r, neg)
    m1 = jnp.max(le, axis=-1, keepdims=True)
    i1 = jnp.min(jnp.where(le == m1, lane, ROUTE_LANES), axis=-1, keepdims=True)
    le2 = jnp.where(lane == i1, neg, le)
    m2 = jnp.max(le2, axis=-1, keepdims=True)
    i2 = jnp.min(jnp.where(le2 == m2, lane, ROUTE_LANES), axis=-1, keepdims=True)
    t2 = jnp.exp(m2 - m1)
    w1 = wg / (1.0 + t2)
    w2 = wg * t2 / (1.0 + t2)

    hot1 = lane == i1
    hot2 = lane == i2
    hot = (hot1 | hot2).astype(jnp.float32)
    before = jnp.dot(tri_ref[...], hot.astype(jnp.bfloat16), preferred_element_type=jnp.float32) + carry_ref[0:1, :]
    rank1 = jnp.sum(jnp.where(hot1, before, 0.0), axis=-1, keepdims=True)
    rank2 = jnp.sum(jnp.where(hot2, before, 0.0), axis=-1, keepdims=True)
    total = carry_ref[0:1, :] + jnp.sum(hot, axis=0, keepdims=True)
    carry_ref[...] = jnp.broadcast_to(total, carry_ref.shape)
    counts_ref[...] = jnp.broadcast_to(total, counts_ref.shape)

    e1f = (i1 - EXPERT_LANE0).astype(jnp.float32)
    e2f = (i2 - EXPERT_LANE0).astype(jnp.float32)
    route = jnp.zeros((tm, ROUTE_LANES), jnp.float32)
    for idx, val in enumerate((e1f, e2f, w1, w2, rank1, rank2)):
        route = jnp.where(lane == idx, val, route)
    route_ref[...] = route


def _post_attention(xp, xs, ln_in_g, ln_in_b, o_groups, lse_groups, ob, ga, gb, wa, wb, wo, bo, g1, b1, wr, br):
    n0, d = xp.shape
    n = n0 + xs.shape[0]
    tm = POST_TM
    n0_tiles = n0 // tm
    nt = n // tm
    tri = jnp.asarray(np.tril(np.ones((tm, tm), np.float32), -1), jnp.bfloat16)

    def tok(width):
        return pl.BlockSpec((tm, width), lambda i: (i, 0))

    def full(a):
        return pl.BlockSpec(a.shape, lambda i: (0,) * a.ndim)

    kern = functools.partial(_post_kernel, n0_tiles=n0_tiles)
    return pl.pallas_call(
        kern,
        grid=(nt,),
        in_specs=[
            pl.BlockSpec((tm, d), lambda i: (jnp.minimum(i, n0_tiles - 1), 0)),
            pl.BlockSpec((tm, d), lambda i: (jnp.maximum(i - n0_tiles, 0), 0)),
            full(ln_in_g), full(ln_in_b),
            *[pl.BlockSpec((1,) + a.shape[1:], lambda i: (i, 0, 0, 0)) for a in (*o_groups, *lse_groups)],
            tok(QB), tok(d), tok(d),
            full(wa), full(wb), full(wo), full(bo), full(g1), full(b1), full(wr), full(br), full(tri),
        ],
        out_specs=[pl.BlockSpec((tm * SUBLANES, LANES), lambda i: (i, 0)), tok(ROUTE_LANES),
                   pl.BlockSpec((8, ROUTE_LANES), lambda i: (0, 0))],
        out_shape=[jax.ShapeDtypeStruct((n * SUBLANES, LANES), jnp.float32),
                   jax.ShapeDtypeStruct((n, ROUTE_LANES), jnp.float32),
                   jax.ShapeDtypeStruct((8, ROUTE_LANES), jnp.float32)],
        scratch_shapes=[pltpu.VMEM((8, ROUTE_LANES), jnp.float32), pltpu.VMEM((12, tm, LANES), jnp.float32)],
        compiler_params=_cparams(("arbitrary",)),
        name="post_attention_router",
    )(xp, xs, ln_in_g, ln_in_b, *o_groups, *lse_groups, ob, ga, gb, wa, wb, wo, bo, g1, b1, wr, br, tri)


def _dispatch_kernel(pos_ref, x_ref, xs_ref, sem):
    tt = x_ref.shape[0] // SUBLANES

    def issue(j, carry):
        for u in range(2):
            t = 2 * j + u
            src = x_ref.at[pl.ds(pl.multiple_of(t * SUBLANES, SUBLANES), SUBLANES)]
            for slot in range(2):
                p = pos_ref[0, slot, t]
                dst = xs_ref.at[pl.ds(pl.multiple_of(p * SUBLANES, SUBLANES), SUBLANES)]
                pltpu.make_async_copy(src, dst, sem).start(priority=u)
        return carry

    lax.fori_loop(0, tt // 2, issue, 0, unroll=4)
    for _ in range(2):
        pltpu.make_async_copy(x_ref, xs_ref.at[pl.ds(0, tt * SUBLANES)], sem).wait()


def _dispatch(x1t, pos, n_rows):
    n = x1t.shape[0] // SUBLANES
    tt = DISP_TT
    return pl.pallas_call(
        _dispatch_kernel,
        grid=(n // tt,),
        in_specs=[pl.BlockSpec((1, 2, tt), lambda i: (i, 0, 0), memory_space=pltpu.SMEM),
                  pl.BlockSpec((tt * SUBLANES, LANES), lambda i: (i, 0))],
        out_specs=pl.BlockSpec(memory_space=pl.ANY),
        out_shape=jax.ShapeDtypeStruct((n_rows * SUBLANES, LANES), x1t.dtype),
        scratch_shapes=[pltpu.SemaphoreType.DMA(())],
        compiler_params=_cparams(("arbitrary",)),
        name="moe_dispatch",
    )(pos.reshape(n // tt, tt, 2).transpose(0, 2, 1), x1t)


def _ffn_kernel(item_expert_ref, item_tile_ref, item_lo_ref, item_hi_ref, n_items_ref,
                x_ref, wg_ref, wu_ref, wd_ref, y_ref):
    j = pl.program_id(0)

    @pl.when(j < n_items_ref[0])
    def _():
        tm = x_ref.shape[0] // SUBLANES
        x = _load_row_tiles(x_ref, tm).astype(jnp.bfloat16)
        a = jnp.dot(x, wg_ref[0], preferred_element_type=jnp.float32)
        u = jnp.dot(x, wu_ref[0], preferred_element_type=jnp.float32)
        h = (a * jax.nn.sigmoid(a) * u).astype(jnp.bfloat16)
        y = jnp.dot(h, wd_ref[0], preferred_element_type=jnp.float32)
        rows = lax.broadcasted_iota(jnp.int32, (tm, 1), 0)
        y = jnp.where((rows >= item_lo_ref[j]) & (rows < item_hi_ref[j]), y, 0.0)
        first_of_tile = jnp.logical_or(j == 0, item_tile_ref[j] != item_tile_ref[jnp.maximum(j - 1, 0)])

        @pl.when(first_of_tile)
        def _():
            _store_row_tiles(y_ref, y)

        @pl.when(jnp.logical_not(first_of_tile))
        def _():
            _store_row_tiles(y_ref, y, accumulate=True)


def _ffn_items(cnt, n_rows):
    tm = FFN_TM
    n_tiles = n_rows // tm
    max_items = n_tiles + N_EXPERTS - 1
    ends = jnp.cumsum(cnt)
    starts = ends - cnt
    first_tile = starts // tm
    last_tile = jnp.maximum(ends - 1, 0) // tm
    items_per_expert = jnp.where(cnt > 0, last_tile - first_tile + 1, 0)
    item_ends = jnp.cumsum(items_per_expert)
    n_items = item_ends[-1:]
    j = jnp.minimum(jnp.arange(max_items, dtype=jnp.int32), n_items[0] - 1)
    expert = jnp.minimum(jnp.searchsorted(item_ends, j, side="right"), N_EXPERTS - 1).astype(jnp.int32)
    tile = first_tile[expert] + (j - (item_ends - items_per_expert)[expert])
    lo = jnp.clip(starts[expert] - tile * tm, 0, tm)
    hi = jnp.clip(ends[expert] - tile * tm, 0, tm)
    i32 = lambda a: a.astype(jnp.int32)
    return i32(expert), i32(tile), i32(lo), i32(hi), i32(n_items), starts


def _grouped_ffn(xs, items, w_gate, w_up, w_down):
    tm = FFN_TM
    d, de = w_gate.shape[1:]
    item_expert, item_tile, item_lo, item_hi, n_items = items
    max_items = item_expert.shape[0]

    def row_map(j, ie, it, lo, hi, ni):
        return (it[j], 0)

    def w_map(j, ie, it, lo, hi, ni):
        return (ie[j], 0, 0)

    grid_spec = pltpu.PrefetchScalarGridSpec(
        num_scalar_prefetch=5,
        grid=(max_items,),
        in_specs=[pl.BlockSpec((tm * SUBLANES, LANES), row_map),
                  pl.BlockSpec((1, d, de), w_map),
                  pl.BlockSpec((1, d, de), w_map),
                  pl.BlockSpec((1, de, d), w_map)],
        out_specs=pl.BlockSpec((tm * SUBLANES, LANES), row_map),
    )
    return pl.pallas_call(
        _ffn_kernel,
        grid_spec=grid_spec,
        out_shape=jax.ShapeDtypeStruct(xs.shape, jnp.float32),
        compiler_params=_cparams(("arbitrary",)),
        name="moe_grouped_ffn",
    )(item_expert, item_tile, item_lo, item_hi, n_items, xs, w_gate, w_up, w_down)


def _combine_kernel(pos_ref, pos_next_ref, x1_ref, route_ref, g2_ref, b2_ref, ys_ref, o_ref, buf, sems):
    tt = x1_ref.shape[0] // SUBLANES
    i = pl.program_id(0)
    cur = lax.rem(i, 2)

    def issue_tile(p_ref, b):
        def issue(j, carry):
            for u in range(2):
                t = 2 * j + u
                for slot in range(2):
                    p = p_ref[0, slot, t]
                    src = ys_ref.at[pl.ds(pl.multiple_of(p * SUBLANES, SUBLANES), SUBLANES)]
                    dst = buf.at[b, slot, pl.ds(pl.multiple_of(t * SUBLANES, SUBLANES), SUBLANES)]
                    pltpu.make_async_copy(src, dst, sems.at[b]).start(priority=u)
            return carry

        lax.fori_loop(0, tt // 2, issue, 0, unroll=4)

    @pl.when(i == 0)
    def _():
        issue_tile(pos_ref, 0)

    @pl.when(i + 1 < pl.num_programs(0))
    def _():
        issue_tile(pos_next_ref, 1 - cur)

    for slot in range(2):
        pltpu.make_async_copy(ys_ref.at[pl.ds(0, tt * SUBLANES)], buf.at[cur, slot], sems.at[cur]).wait()
    route = route_ref[...]
    y = route[:, 2:3] * _load_row_tiles(buf.at[cur, 0], tt) + route[:, 3:4] * _load_row_tiles(buf.at[cur, 1], tt)
    o_ref[...] = _layer_norm(DN_ALPHA * _load_row_tiles(x1_ref, tt) + y, g2_ref[...], b2_ref[...])


def _combine(x1t, route, pos, ys, g2, b2, tok0, n_tok):
    d = g2.shape[1]
    tt = COMB_TT
    t0 = tok0 // tt
    pos3 = pos.reshape(pos.shape[0] // tt, tt, 2).transpose(0, 2, 1)
    nt = n_tok // tt
    return pl.pallas_call(
        _combine_kernel,
        grid=(nt,),
        in_specs=[pl.BlockSpec((1, 2, tt), lambda i: (i + t0, 0, 0), memory_space=pltpu.SMEM),
                  pl.BlockSpec((1, 2, tt), lambda i: (jnp.minimum(i + 1, nt - 1) + t0, 0, 0), memory_space=pltpu.SMEM),
                  pl.BlockSpec((tt * SUBLANES, LANES), lambda i: (i + t0, 0)),
                  pl.BlockSpec((tt, ROUTE_LANES), lambda i: (i + t0, 0)),
                  pl.BlockSpec((1, d), lambda i: (0, 0)),
                  pl.BlockSpec((1, d), lambda i: (0, 0)),
                  pl.BlockSpec(memory_space=pl.ANY)],
        out_specs=pl.BlockSpec((tt, d), lambda i: (i, 0)),
        out_shape=jax.ShapeDtypeStruct((n_tok, d), jnp.float32),
        scratch_shapes=[pltpu.VMEM((2, 2, tt * SUBLANES, LANES), ys.dtype), pltpu.SemaphoreType.DMA((2,))],
        compiler_params=_cparams(("arbitrary",)),
        name="moe_combine_ln",
    )(pos3, pos3, x1t, route, g2, b2, ys)


def _layer(x_prompt, x_sample, ln_in_g, ln_in_b, w_in, b_in, rel_bias_t5, na_rpb, w_branch_a, w_branch_b,
           w_out, b_out, ln1_g, ln1_b, w_router_group, b_router_group, w_router_expert, b_router_expert,
           w_exp_gate, w_exp_up, w_exp_down, ln2_g, ln2_b):
    bp, tp, d = x_prompt.shape
    bs, ts, _ = x_sample.shape
    seg_tokens = (bp * tp, bs * ts)
    seq_lens = (tp, ts)
    n = sum(seg_tokens)
    xp = x_prompt.reshape(seg_tokens[0], d)
    xs = x_sample.reshape(seg_tokens[1], d)
    row = lambda a: a.reshape(1, -1).astype(jnp.float32)
    bf = lambda a: a.astype(jnp.bfloat16)

    scale = HEAD_DIM ** -0.5
    col_scale = np.ones((w_in.shape[2],), np.float32)
    col_scale[0:QA] = scale
    col_scale[3 * QA:3 * QA + QB] = scale
    w_proj = bf(w_in[0] * col_scale)
    b_proj = row(b_in[0] * col_scale)
    proj = _in_projection(xp, xs, row(ln_in_g), row(ln_in_b), w_proj, b_proj)
    qkv_a, (qb, kb, vb, ga, gb) = proj[:3 * N_DIL_GROUPS], proj[3 * N_DIL_GROUPS:]

    o_groups, lse_groups = [], []
    for g, (window, dil) in enumerate(DIL_PATTERNS):
        assert window // (2 * dil) == HALF_SPAN
        table = rel_bias_t5[:, g * HEADS_PER_DIL_GROUP:(g + 1) * HEADS_PER_DIL_GROUP].astype(jnp.float32).reshape(-1)
        o, lse = _dilated_group(qkv_a[g], qkv_a[N_DIL_GROUPS + g], qkv_a[2 * N_DIL_GROUPS + g], table, dil,
                                seg_tokens, seq_lens)
        o_groups.append(o)
        lse_groups.append(lse)
    ob = _neighborhood(qb, kb, vb, na_rpb[0].astype(jnp.float32).reshape(-1), seg_tokens, seq_lens)

    wr = jnp.zeros((d, ROUTE_LANES), jnp.float32)
    wr = wr.at[:, 0:N_GROUPS].set(w_router_group[0]).at[:, EXPERT_LANE0:EXPERT_LANE0 + N_EXPERTS].set(w_router_expert[0])
    br = jnp.zeros((1, ROUTE_LANES), jnp.float32)
    br = br.at[0, 0:N_GROUPS].set(b_router_group[0]).at[0, EXPERT_LANE0:EXPERT_LANE0 + N_EXPERTS].set(b_router_expert[0])
    x1, route, counts = _post_attention(
        xp, xs, row(ln_in_g), row(ln_in_b), o_groups, lse_groups, ob, ga, gb, bf(w_branch_a[0]), bf(w_branch_b[0]),
        bf(w_out[0]), row(b_out[0]), row(ln1_g[0]), row(ln1_b[0]), bf(wr), br)

    experts = route[:, 0:2].astype(jnp.int32)
    ranks = route[:, 4:6].astype(jnp.int32)
    cnt = counts[0, EXPERT_LANE0:EXPERT_LANE0 + N_EXPERTS].astype(jnp.int32)
    n_rows = 2 * n
    *items, starts = _ffn_items(cnt, n_rows)
    pos = starts[experts] + ranks

    xsorted = _dispatch(x1, pos, n_rows)
    ys = _grouped_ffn(xsorted, items, bf(w_exp_gate[0]), bf(w_exp_up[0]), bf(w_exp_down[0]))
    y_prompt = _combine(x1, route, pos, ys, row(ln2_g[0]), row(ln2_b[0]), 0, seg_tokens[0])
    y_sample = _combine(x1, route, pos, ys, row(ln2_g[0]), row(ln2_b[0]), seg_tokens[0], seg_tokens[1])
    return y_prompt.reshape(bp, tp, d), y_sample.reshape(bs, ts, d)


def kernel(x_prompt, x_sample, ln_in_g, ln_in_b, w_in, b_in, rel_bias_t5, na_rpb, w_branch_a, w_branch_b, w_out, b_out, ln1_g, ln1_b, w_router_group, b_router_group, w_router_expert, b_router_expert, w_exp_gate, w_exp_up, w_exp_down, ln2_g, ln2_b):
    return _layer(x_prompt, x_sample, ln_in_g, ln_in_b, w_in, b_in, rel_bias_t5, na_rpb, w_branch_a, w_branch_b,
                  w_out, b_out, ln1_g, ln1_b, w_router_group, b_router_group, w_router_expert, b_router_expert,
                  w_exp_gate, w_exp_up, w_exp_down, ln2_g, ln2_b)
```

```python
import functools
import math

import numpy as np
import jax
import jax.numpy as jnp
from jax import lax
from jax.experimental import pallas as pl
from jax.experimental.pallas import tpu as pltpu

HEAD_DIM = 64
DIL_PATTERNS = ((128, 1), (512, 4), (2048, 16))
HEADS_PER_DIL_GROUP = 4
N_DIL_GROUPS = 3
H_A = N_DIL_GROUPS * HEADS_PER_DIL_GROUP
H_B = 8
QA = H_A * HEAD_DIM
QB = H_B * HEAD_DIM
HALF_SPAN = 64
GRID_W = 64
NA_KH = 8
NA_KW = 16
N_BUCKETS = 32
REL_MAX_DIST = 1024
N_GROUPS = 4
EXPERTS_PER_GROUP = 8
N_EXPERTS = N_GROUPS * EXPERTS_PER_GROUP
LN_EPS = 1e-5
NEG_INF = -1e30
DEPTH = 1
DN_ALPHA = (2.0 * DEPTH) ** 0.25

LANES = 128
HEAD_BLOCK = 4 * HEAD_DIM
VMEM_LIMIT_BYTES = 56 * 1024 * 1024

PROJ_TM = 512
DIL_M = 128
DIL_GROUP = 4
NA_ROWS = 8
POST_TM = 512
FFN_TM = 512
DISP_TT = 2048
COMB_TT = 512
ROUTE_LANES = LANES
EXPERT_LANE0 = 32


def _cparams(sem):
    return pltpu.CompilerParams(dimension_semantics=sem, vmem_limit_bytes=VMEM_LIMIT_BYTES)


SUBLANES = 8


def _store_row_tiles(ref, x, accumulate=False):
    rows = x.shape[0]
    for c in range(SUBLANES):
        idx = pl.ds(c, rows, stride=SUBLANES)
        piece = x[:, c * LANES:(c + 1) * LANES]
        ref[idx, :] = ref[idx, :] + piece if accumulate else piece


def _load_row_tiles(ref, rows):
    return jnp.concatenate([ref[pl.ds(c, rows, stride=SUBLANES), :] for c in range(SUBLANES)], axis=1)


def _layer_norm(x, g, b):
    mu = jnp.mean(x, axis=-1, keepdims=True)
    xc = x - mu
    var = jnp.mean(xc * xc, axis=-1, keepdims=True)
    return xc * lax.rsqrt(var + LN_EPS) * g + b


def _strided_rows(scr, slab0, r, count, stride):
    return jnp.concatenate([scr[slab0 + s, pl.ds(r, count, stride=stride), :] for s in range(2)], axis=1)


def _inproj_kernel(xp_ref, xs_ref, g_ref, b_ref, w_ref, bias_ref, *refs, n0_tiles):
    out_refs, scr = refs[:-1], refs[-1]
    i = pl.program_id(0)
    x = jnp.where(i < n0_tiles, xp_ref[...], xs_ref[...])
    xn = _layer_norm(x, g_ref[...], b_ref[...]).astype(jnp.bfloat16)
    tm = xn.shape[0]

    def proj(c0, cw):
        return jnp.dot(xn, w_ref[:, c0:c0 + cw], preferred_element_type=jnp.float32) + bias_ref[:, c0:c0 + cw]

    n_dil = 3 * N_DIL_GROUPS
    for idx in range(n_dil):
        g = idx % N_DIL_GROUPS
        dil = DIL_PATTERNS[g][1]
        o_ref = out_refs[idx]
        y = proj(idx * HEAD_BLOCK, HEAD_BLOCK)
        if dil == 1:
            o_ref[0, 0] = y.astype(o_ref.dtype)
        else:
            slab0 = 2 * (idx % 2)
            scr[slab0] = y[:, :LANES]
            scr[slab0 + 1] = y[:, LANES:]
            for r in range(dil):
                o_ref[0, r] = _strided_rows(scr, slab0, r, tm // dil, dil).astype(o_ref.dtype)
    col = n_dil * HEAD_BLOCK
    for o_ref in out_refs[n_dil:]:
        width = o_ref.shape[1]
        for c in range(0, width, 512):
            cw = min(512, width - c)
            o_ref[:, c:c + cw] = proj(col + c, cw).astype(o_ref.dtype)
        col += width


def _in_projection(xp, xs, ln_g, ln_b, w, b):
    n0, d_model = xp.shape
    n1 = xs.shape[0]
    n = n0 + n1
    tm = PROJ_TM
    n0_tiles, n1_tiles = n0 // tm, n1 // tm
    nt = n0_tiles + n1_tiles
    out_shape, out_specs = [], []
    for _ in range(3):
        for _, dil in DIL_PATTERNS:
            out_shape.append(jax.ShapeDtypeStruct((nt, dil, tm // dil, HEAD_BLOCK), jnp.bfloat16))
            out_specs.append(pl.BlockSpec((1, dil, tm // dil, HEAD_BLOCK), lambda i: (i, 0, 0, 0)))
    for wd in (QB, QB, QB, d_model, d_model):
        out_shape.append(jax.ShapeDtypeStruct((n, wd), jnp.bfloat16))
        out_specs.append(pl.BlockSpec((tm, wd), lambda i: (i, 0)))
    d_in = w.shape[1]
    kern = functools.partial(_inproj_kernel, n0_tiles=n0_tiles)
    return pl.pallas_call(
        kern,
        grid=(nt,),
        in_specs=[
            pl.BlockSpec((tm, d_model), lambda i: (jnp.minimum(i, n0_tiles - 1), 0)),
            pl.BlockSpec((tm, d_model), lambda i: (jnp.maximum(i - n0_tiles, 0), 0)),
            pl.BlockSpec((1, d_model), lambda i: (0, 0)),
            pl.BlockSpec((1, d_model), lambda i: (0, 0)),
            pl.BlockSpec((d_model, d_in), lambda i: (0, 0)),
            pl.BlockSpec((1, d_in), lambda i: (0, 0)),
        ],
        out_specs=out_specs,
        out_shape=out_shape,
        scratch_shapes=[pltpu.VMEM((4, tm, LANES), jnp.float32)],
        compiler_params=_cparams(("arbitrary",)),
        name="in_projection",
    )(xp, xs, ln_g, ln_b, w, b)


def _t5_bucket_np(rel):
    half = N_BUCKETS // 2
    max_exact = half // 2
    ret = np.where(rel > 0, half, 0)
    n = np.abs(rel)
    nf = np.maximum(n, 1).astype(np.float32)
    large = max_exact + (np.log(nf / np.float32(max_exact)) / np.float32(math.log(REL_MAX_DIST / max_exact))
                         * np.float32(half - max_exact)).astype(np.int32)
    large = np.minimum(large, half - 1)
    return ret + np.where(n < max_exact, n, large)


def _t5_bucket_starts():
    half = N_BUCKETS // 2
    dist = np.arange(0, HALF_SPAN * max(d for _, d in DIL_PATTERNS) + 1)
    buckets = _t5_bucket_np(-dist)
    return [int(np.argmax(buckets >= k)) for k in range(half)]


def _build_dilated_bias(table_ref, bias_scr, dil):
    half = N_BUCKETS // 2
    starts = _t5_bucket_starts()
    kb = DIL_M + 2 * HALF_SPAN
    i = lax.broadcasted_iota(jnp.int32, (DIL_M, kb), 0)
    m = lax.broadcasted_iota(jnp.int32, (DIL_M, kb), 1)
    delta = m - HALF_SPAN - i
    dist = jnp.abs(delta) * dil
    band = jnp.abs(delta) <= HALF_SPAN
    for h in range(HEADS_PER_DIL_GROUP):
        sides = []
        for side in range(2):
            val = jnp.full((DIL_M, kb), table_ref[(side * half + half - 1) * HEADS_PER_DIL_GROUP + h], jnp.float32)
            for k in range(half - 1, 0, -1):
                val = jnp.where(dist < starts[k], table_ref[(side * half + k - 1) * HEADS_PER_DIL_GROUP + h], val)
            sides.append(val)
        base = jnp.where(band, jnp.where(delta > 0, sides[1], sides[0]), NEG_INF)
        rows = slice(h * DIL_M, (h + 1) * DIL_M)
        bias_scr[0, rows, :] = base
        bias_scr[1, rows, :] = jnp.where(m >= HALF_SPAN, base, NEG_INF)
        bias_scr[2, rows, :] = jnp.where(m < kb - HALF_SPAN, base, NEG_INF)


def _stack_heads(q):
    head_of_col = lax.broadcasted_iota(jnp.int32, (1, HEAD_BLOCK), 1) // HEAD_DIM
    zero = jnp.zeros_like(q)
    return jnp.concatenate([jnp.where(head_of_col == h, q, zero) for h in range(4)], axis=0)


def _merge_heads(x, m):
    head_of_col = lax.broadcasted_iota(jnp.int32, (1, HEAD_BLOCK), 1) // HEAD_DIM
    out = jnp.zeros((m, HEAD_BLOCK), x.dtype)
    for h in range(4):
        out = jnp.where(head_of_col == h, x[h * m:(h + 1) * m], out)
    return out


def _softmax_pv(s, v, m_rows):
    mx = jnp.max(s, axis=-1, keepdims=True)
    p = jnp.exp(s - mx)
    l = jnp.sum(p, axis=-1, keepdims=True)
    pv = jnp.dot(p.astype(jnp.bfloat16), v, preferred_element_type=jnp.float32)
    o = _merge_heads(pv * (1.0 / l), m_rows)
    return o, mx + jnp.log(l)


def _dilated_kernel(table_ref, q_ref, kp_ref, kc_ref, kn_ref, vp_ref, vc_ref, vn_ref,
                    o_ref, lse_ref, bias_scr, *, dil, seg0_blocks, blocks_per_seq):
    step = pl.program_id(1)

    @pl.when((pl.program_id(0) == 0) & (step == 0))
    def _():
        _build_dilated_bias(table_ref, bias_scr, dil)

    rows2d = lambda ref: ref[...].reshape(-1, HEAD_BLOCK)
    k = jnp.concatenate([rows2d(r) for r in (kp_ref, kc_ref, kn_ref)], axis=0)
    v = jnp.concatenate([rows2d(r) for r in (vp_ref, vc_ref, vn_ref)], axis=0)
    q = rows2d(q_ref)
    o_parts, lse_parts = [], []
    for g in range(DIL_GROUP):
        jb = step * DIL_GROUP + g
        in0 = jb < seg0_blocks
        jl = jnp.where(in0, jb, jb - seg0_blocks)
        nbs = jnp.where(in0, blocks_per_seq[0], blocks_per_seq[1])
        pos = lax.rem(jl, nbs)
        variant = jnp.where(pos == 0, 1, jnp.where(pos == nbs - 1, 2, 0))
        keys = slice(g * DIL_M, (g + 1) * DIL_M + 2 * HALF_SPAN)
        qs = _stack_heads(q[g * DIL_M:(g + 1) * DIL_M])
        s = lax.dot_general(qs, k[keys], (((1,), (1,)), ((), ())), preferred_element_type=jnp.float32)
        s = s + bias_scr[variant]
        o, lse = _softmax_pv(s, v[keys], DIL_M)
        o_parts.append(o.astype(o_ref.dtype))
        lse_parts.append(_merge_heads(jnp.broadcast_to(lse, (4 * DIL_M, HEAD_BLOCK)), DIL_M))
    o_ref[...] = jnp.concatenate(o_parts, axis=0).reshape(o_ref.shape)
    lse_ref[...] = jnp.concatenate(lse_parts, axis=0).reshape(lse_ref.shape)


def _residue_spec(rows_blk, dil, block_of):
    per_tile = PROJ_TM // dil
    if per_tile >= rows_blk:
        per = per_tile // rows_blk
        return pl.BlockSpec((1, None, rows_blk, HEAD_BLOCK),
                            lambda r, jb: (block_of(jb) // per, r, block_of(jb) % per, 0))
    return pl.BlockSpec((rows_blk // per_tile, None, per_tile, HEAD_BLOCK), lambda r, jb: (block_of(jb), r, 0, 0))


def _dilated_group(q, k, v, table, dil, seg_tokens, seq_lens):
    n = q.shape[0] * PROJ_TM
    rows = n // dil
    step_rows = DIL_GROUP * DIL_M
    assert rows % step_rows == 0 and all(t % (dil * DIL_M) == 0 for t in seq_lens)
    nsteps = rows // step_rows
    nkb = rows // HALF_SPAN
    ratio = step_rows // HALF_SPAN
    seg0_blocks = seg_tokens[0] // dil // DIL_M
    blocks_per_seq = tuple(t // dil // DIL_M for t in seq_lens)
    assert min(blocks_per_seq) >= 2

    qo_spec = _residue_spec(step_rows, dil, lambda jb: jb)
    prev_spec = _residue_spec(HALF_SPAN, dil, lambda jb: jnp.maximum(ratio * jb - 1, 0))
    next_spec = _residue_spec(HALF_SPAN, dil, lambda jb: jnp.minimum(ratio * (jb + 1), nkb - 1))
    kv_specs = [prev_spec, qo_spec, next_spec]
    kern = functools.partial(_dilated_kernel, dil=dil, seg0_blocks=seg0_blocks, blocks_per_seq=blocks_per_seq)
    return pl.pallas_call(
        kern,
        grid=(dil, nsteps),
        in_specs=[pl.BlockSpec(memory_space=pltpu.SMEM), qo_spec] + kv_specs + kv_specs,
        out_specs=[qo_spec, qo_spec],
        out_shape=[jax.ShapeDtypeStruct(q.shape, jnp.bfloat16), jax.ShapeDtypeStruct(q.shape, jnp.float32)],
        scratch_shapes=[pltpu.VMEM((3, HEADS_PER_DIL_GROUP * DIL_M, DIL_M + 2 * HALF_SPAN), jnp.float32)],
        compiler_params=_cparams(("arbitrary", "arbitrary")),
        name=f"dilated_attention_d{dil}",
    )(table, q, k, k, k, v, v, v)


NA_DR = 2 * NA_KH - 1
NA_DC = 2 * NA_KW - 1
NA_PAIRS = NA_DR - 1


def _build_na_bias(rpb_ref, pair_scr, head0):
    qc = lax.broadcasted_iota(jnp.int32, (GRID_W, 2 * GRID_W), 0)
    lane = lax.broadcasted_iota(jnp.int32, (GRID_W, 2 * GRID_W), 1)
    kc = lane % GRID_W
    upper = lane >= GRID_W
    qstart = jnp.clip(qc - NA_KW // 2, 0, GRID_W - NA_KW)
    valid = (kc >= qstart) & (kc < qstart + NA_KW)
    dc = jnp.clip(kc - qc, -(NA_KW - 1), NA_KW - 1) + NA_KW - 1

    def one_pair(idx, carry):
        h = idx // NA_PAIRS
        dr = idx % NA_PAIRS
        base = ((head0 + h) * NA_DR + dr) * NA_DC
        val = jnp.zeros((GRID_W, 2 * GRID_W), jnp.float32)
        for c in range(NA_DC):
            val = jnp.where(dc == c, jnp.where(upper, rpb_ref[base + NA_DC + c], rpb_ref[base + c]), val)
        pair_scr[h, dr] = jnp.where(valid, val, NEG_INF)
        return carry

    lax.fori_loop(0, 4 * NA_PAIRS, one_pair, 0)


def _na_kernel(rpb_ref, q_ref, kp_ref, kc_ref, kn_ref, vp_ref, vc_ref, vn_ref, o_ref, k_scr, v_scr, pair_scr,
               *, seg0_blocks, blocks_per_seq):
    ib = pl.program_id(1)

    @pl.when(ib == 0)
    def _():
        _build_na_bias(rpb_ref, pair_scr, pl.program_id(0) * 4)

    in0 = ib < seg0_blocks
    il = jnp.where(in0, ib, ib - seg0_blocks)
    nbs = jnp.where(in0, blocks_per_seq[0], blocks_per_seq[1])
    pos = lax.rem(il, nbs)
    first = pos == 0
    last = pos == nbs - 1
    blk = NA_ROWS * GRID_W
    k_scr[0:blk] = kp_ref[...]
    k_scr[blk:2 * blk] = kc_ref[...]
    k_scr[2 * blk:3 * blk] = kn_ref[...]
    v_scr[0:blk] = vp_ref[...]
    v_scr[blk:2 * blk] = vc_ref[...]
    v_scr[2 * blk:3 * blk] = vn_ref[...]
    half = NA_KH // 2
    for rr in range(NA_ROWS):
        start = jnp.where(first, max(rr + half, NA_ROWS), jnp.where(last, min(rr + half, NA_ROWS), rr + half))
        var = jnp.where(first, min(rr, half), jnp.where(last, max(rr, half), half))
        off = pl.multiple_of(start * GRID_W, GRID_W)
        kk = k_scr[pl.ds(off, NA_KH * GRID_W), :]
        vv = v_scr[pl.ds(off, NA_KH * GRID_W), :]
        qs = _stack_heads(q_ref[rr * GRID_W:(rr + 1) * GRID_W, :])
        s = lax.dot_general(qs, kk, (((1,), (1,)), ((), ())), preferred_element_type=jnp.float32)
        bias = jnp.concatenate(
            [jnp.concatenate([pair_scr[h, 2 * p - var + NA_KH - 1] for p in range(NA_KH // 2)], axis=1)
             for h in range(4)], axis=0)
        s = s + bias
        o, _ = _softmax_pv(s, vv, GRID_W)
        o_ref[rr * GRID_W:(rr + 1) * GRID_W, :] = o.astype(o_ref.dtype)


def _neighborhood(qb, kb, vb, rpb_flat, seg_tokens, seq_lens):
    n = qb.shape[0]
    blk = NA_ROWS * GRID_W
    nblk = n // blk
    seg0_blocks = seg_tokens[0] // blk
    blocks_per_seq = tuple(t // blk for t in seq_lens)
    assert min(blocks_per_seq) >= 2
    nset = QB // HEAD_BLOCK

    def kv_spec(t):
        return pl.BlockSpec((blk, HEAD_BLOCK), lambda c, ib: (jnp.clip(ib - 1 + t, 0, nblk - 1), c))

    kern = functools.partial(_na_kernel, seg0_blocks=seg0_blocks, blocks_per_seq=blocks_per_seq)
    return pl.pallas_call(
        kern,
        grid=(nset, nblk),
        in_specs=[pl.BlockSpec(memory_space=pltpu.SMEM), pl.BlockSpec((blk, HEAD_BLOCK), lambda c, ib: (ib, c))]
        + [kv_spec(t) for t in range(3)] + [kv_spec(t) for t in range(3)],
        out_specs=pl.BlockSpec((blk, HEAD_BLOCK), lambda c, ib: (ib, c)),
        out_shape=jax.ShapeDtypeStruct((n, QB), jnp.bfloat16),
        scratch_shapes=[pltpu.VMEM((3 * blk, HEAD_BLOCK), jnp.bfloat16),
                        pltpu.VMEM((3 * blk, HEAD_BLOCK), jnp.bfloat16),
                        pltpu.VMEM((4, NA_PAIRS, GRID_W, 2 * GRID_W), jnp.float32)],
        compiler_params=_cparams(("arbitrary", "arbitrary")),
        name="neighborhood_attention",
    )(rpb_flat, qb, kb, kb, kb, vb, vb, vb)


def _post_kernel(xp_ref, xs_ref, lng_ref, lnb_ref, o0_ref, o1_ref, o2_ref, l0_ref, l1_ref, l2_ref, ob_ref,
                 ga_ref, gb_ref, wa_ref, wb_ref, wo_ref, bo_ref, g1_ref, b1_ref, wr_ref, br_ref, tri_ref,
                 x1_ref, route_ref, route_t_ref, counts_ref, carry_ref, perm_scr, *, n0_tiles):
    i = pl.program_id(0)

    @pl.when(i == 0)
    def _():
        carry_ref[...] = jnp.zeros_like(carry_ref)

    def token_order(ref, slab0):
        dil, per = ref.shape[1], ref.shape[2]
        if dil == 1:
            return ref[0, 0].astype(jnp.float32)
        for r in range(dil):
            val = ref[0, r].astype(jnp.float32)
            for s in range(2):
                perm_scr[slab0 + s, pl.ds(r, per, stride=dil), :] = val[:, s * LANES:(s + 1) * LANES]
        return jnp.concatenate([perm_scr[slab0], perm_scr[slab0 + 1]], axis=1)

    l0, l1, l2 = (token_order(ref, 2 * j) for j, ref in enumerate((l0_ref, l1_ref, l2_ref)))
    lm = jnp.maximum(jnp.maximum(l0, l1), l2)
    e0, e1, e2 = jnp.exp(l0 - lm), jnp.exp(l1 - lm), jnp.exp(l2 - lm)
    inv = 1.0 / (e0 + e1 + e2)
    o0, o1, o2 = (token_order(ref, 6 + 2 * j) for j, ref in enumerate((o0_ref, o1_ref, o2_ref)))
    o_a = (e0 * inv) * o0 + (e1 * inv) * o1 + (e2 * inv) * o2
    y_a = jnp.dot(o_a.astype(jnp.bfloat16), wa_ref[...], preferred_element_type=jnp.float32)
    y_b = jnp.dot(ob_ref[...], wb_ref[...], preferred_element_type=jnp.float32)
    mix = jax.nn.sigmoid(ga_ref[...].astype(jnp.float32)) * y_a + jax.nn.sigmoid(gb_ref[...].astype(jnp.float32)) * y_b
    out = jnp.dot(mix.astype(jnp.bfloat16), wo_ref[...], preferred_element_type=jnp.float32) + bo_ref[...]
    x = jnp.where(i < n0_tiles, xp_ref[...], xs_ref[...])
    x0 = _layer_norm(x, lng_ref[...], lnb_ref[...])
    x1 = _layer_norm(DN_ALPHA * x0 + out, g1_ref[...], b1_ref[...])
    _store_row_tiles(x1_ref, x1)

    r = jnp.dot(x1.astype(jnp.bfloat16), wr_ref[...], preferred_element_type=jnp.float32) + br_ref[...]
    tm = r.shape[0]
    lane = lax.broadcasted_iota(jnp.int32, (tm, ROUTE_LANES), 1)
    neg = jnp.float32(-jnp.inf)
    lg = jnp.where(lane < N_GROUPS, r, neg)
    gmax = jnp.max(lg, axis=-1, keepdims=True)
    gi = jnp.min(jnp.where(lg == gmax, lane, ROUTE_LANES), axis=-1, keepdims=True)
    wg = 1.0 / jnp.sum(jnp.exp(lg - gmax), axis=-1, keepdims=True)
    lo = EXPERT_LANE0 + EXPERTS_PER_GROUP * gi
    le = jnp.where((lane >= lo) & (lane < lo + EXPERTS_PER_GROUP), r, neg)
    m1 = jnp.max(le, axis=-1, keepdims=True)
    i1 = jnp.min(jnp.where(le == m1, lane, ROUTE_LANES), axis=-1, keepdims=True)
    le2 = jnp.where(lane == i1, neg, le)
    m2 = jnp.max(le2, axis=-1, keepdims=True)
    i2 = jnp.min(jnp.where(le2 == m2, lane, ROUTE_LANES), axis=-1, keepdims=True)
    t2 = jnp.exp(m2 - m1)
    w1 = wg / (1.0 + t2)
    w2 = wg * t2 / (1.0 + t2)

    hot1 = lane == i1
    hot2 = lane == i2
    hot = (hot1 | hot2).astype(jnp.float32)
    before = jnp.dot(tri_ref[...], hot.astype(jnp.bfloat16), preferred_element_type=jnp.float32) + carry_ref[0:1, :]
    rank1 = jnp.sum(jnp.where(hot1, before, 0.0), axis=-1, keepdims=True)
    rank2 = jnp.sum(jnp.where(hot2, before, 0.0), axis=-1, keepdims=True)
    total = carry_ref[0:1, :] + jnp.sum(hot, axis=0, keepdims=True)
    carry_ref[...] = jnp.broadcast_to(total, carry_ref.shape)
    counts_ref[...] = jnp.broadcast_to(total, counts_ref.shape)

    e1f = (i1 - EXPERT_LANE0).astype(jnp.float32)
    e2f = (i2 - EXPERT_LANE0).astype(jnp.float32)
    route = jnp.zeros((tm, ROUTE_LANES), jnp.float32)
    for idx, val in enumerate((e1f, e2f, w1, w2, rank1, rank2)):
        route = jnp.where(lane == idx, val, route)
    route_ref[...] = route
    route_t_ref[...] = route.T[0:route_t_ref.shape[0]]


def _post_attention(xp, xs, ln_in_g, ln_in_b, o_groups, lse_groups, ob, ga, gb, wa, wb, wo, bo, g1, b1, wr, br):
    n0, d = xp.shape
    n = n0 + xs.shape[0]
    tm = POST_TM
    n0_tiles = n0 // tm
    nt = n // tm
    tri = jnp.asarray(np.tril(np.ones((tm, tm), np.float32), -1), jnp.bfloat16)

    def tok(width):
        return pl.BlockSpec((tm, width), lambda i: (i, 0))

    def full(a):
        return pl.BlockSpec(a.shape, lambda i: (0,) * a.ndim)

    kern = functools.partial(_post_kernel, n0_tiles=n0_tiles)
    return pl.pallas_call(
        kern,
        grid=(nt,),
        in_specs=[
            pl.BlockSpec((tm, d), lambda i: (jnp.minimum(i, n0_tiles - 1), 0)),
            pl.BlockSpec((tm, d), lambda i: (jnp.maximum(i - n0_tiles, 0), 0)),
            full(ln_in_g), full(ln_in_b),
            *[pl.BlockSpec((1,) + a.shape[1:], lambda i: (i, 0, 0, 0)) for a in (*o_groups, *lse_groups)],
            tok(QB), tok(d), tok(d),
            full(wa), full(wb), full(wo), full(bo), full(g1), full(b1), full(wr), full(br), full(tri),
        ],
        out_specs=[pl.BlockSpec((tm * SUBLANES, LANES), lambda i: (i, 0)), tok(ROUTE_LANES),
                   pl.BlockSpec((8, tm), lambda i: (0, i)), pl.BlockSpec((8, ROUTE_LANES), lambda i: (0, 0))],
        out_shape=[jax.ShapeDtypeStruct((n * SUBLANES, LANES), jnp.float32),
                   jax.ShapeDtypeStruct((n, ROUTE_LANES), jnp.float32),
                   jax.ShapeDtypeStruct((8, n), jnp.float32),
                   jax.ShapeDtypeStruct((8, ROUTE_LANES), jnp.float32)],
        scratch_shapes=[pltpu.VMEM((8, ROUTE_LANES), jnp.float32), pltpu.VMEM((12, tm, LANES), jnp.float32)],
        compiler_params=_cparams(("arbitrary",)),
        name="post_attention_router",
    )(xp, xs, ln_in_g, ln_in_b, *o_groups, *lse_groups, ob, ga, gb, wa, wb, wo, bo, g1, b1, wr, br, tri)


def _dispatch_kernel(pos_ref, x_ref, xs_ref, sem):
    tt = x_ref.shape[0] // SUBLANES

    def issue(j, carry):
        for u in range(2):
            t = 2 * j + u
            src = x_ref.at[pl.ds(pl.multiple_of(t * SUBLANES, SUBLANES), SUBLANES)]
            for slot in range(2):
                p = pos_ref[slot, t]
                dst = xs_ref.at[pl.ds(pl.multiple_of(p * SUBLANES, SUBLANES), SUBLANES)]
                pltpu.make_async_copy(src, dst, sem).start(priority=u)
        return carry

    lax.fori_loop(0, tt // 2, issue, 0, unroll=4)
    for _ in range(2):
        pltpu.make_async_copy(x_ref, xs_ref.at[pl.ds(0, tt * SUBLANES)], sem).wait()


def _dispatch(x1t, pos, n_rows):
    n = x1t.shape[0] // SUBLANES
    tt = DISP_TT
    return pl.pallas_call(
        _dispatch_kernel,
        grid=(n // tt,),
        in_specs=[pl.BlockSpec((2, tt), lambda i: (0, i), memory_space=pltpu.SMEM),
                  pl.BlockSpec((tt * SUBLANES, LANES), lambda i: (i, 0))],
        out_specs=pl.BlockSpec(memory_space=pl.ANY),
        out_shape=jax.ShapeDtypeStruct((n_rows * SUBLANES, LANES), x1t.dtype),
        scratch_shapes=[pltpu.SemaphoreType.DMA(())],
        compiler_params=_cparams(("arbitrary",)),
        name="moe_dispatch",
    )(pos, x1t)


def _ffn_kernel(item_expert_ref, item_tile_ref, item_lo_ref, item_hi_ref, n_items_ref,
                x_ref, wg_ref, wu_ref, wd_ref, y_ref, wg_bf, wu_bf, wd_bf):
    j = pl.program_id(0)

    @pl.when(j < n_items_ref[0])
    def _():
        prev = jnp.maximum(j - 1, 0)

        @pl.when(jnp.logical_or(j == 0, item_expert_ref[j] != item_expert_ref[prev]))
        def _():
            wg_bf[...] = wg_ref[0].astype(jnp.bfloat16)
            wu_bf[...] = wu_ref[0].astype(jnp.bfloat16)
            wd_bf[...] = wd_ref[0].astype(jnp.bfloat16)

        tm = x_ref.shape[0] // SUBLANES
        x = _load_row_tiles(x_ref, tm).astype(jnp.bfloat16)
        a = jnp.dot(x, wg_bf[...], preferred_element_type=jnp.float32)
        u = jnp.dot(x, wu_bf[...], preferred_element_type=jnp.float32)
        h = (a * jax.nn.sigmoid(a) * u).astype(jnp.bfloat16)
        y = jnp.dot(h, wd_bf[...], preferred_element_type=jnp.float32)
        rows = lax.broadcasted_iota(jnp.int32, (tm, 1), 0)
        y = jnp.where((rows >= item_lo_ref[j]) & (rows < item_hi_ref[j]), y, 0.0)
        first_of_tile = jnp.logical_or(j == 0, item_tile_ref[j] != item_tile_ref[prev])

        @pl.when(first_of_tile)
        def _():
            _store_row_tiles(y_ref, y)

        @pl.when(jnp.logical_not(first_of_tile))
        def _():
            _store_row_tiles(y_ref, y, accumulate=True)


def _ffn_items(cnt, n_rows):
    tm = FFN_TM
    n_tiles = n_rows // tm
    max_items = n_tiles + N_EXPERTS - 1
    ends = jnp.cumsum(cnt)
    starts = ends - cnt
    first_tile = starts // tm
    last_tile = jnp.maximum(ends - 1, 0) // tm
    items_per_expert = jnp.where(cnt > 0, last_tile - first_tile + 1, 0)
    item_ends = jnp.cumsum(items_per_expert)
    n_items = item_ends[-1:]
    j = jnp.minimum(jnp.arange(max_items, dtype=jnp.int32), n_items[0] - 1)
    expert = jnp.minimum(jnp.sum(item_ends[None, :] <= j[:, None], axis=1), N_EXPERTS - 1).astype(jnp.int32)
    tile = first_tile[expert] + (j - (item_ends - items_per_expert)[expert])
    lo = jnp.clip(starts[expert] - tile * tm, 0, tm)
    hi = jnp.clip(ends[expert] - tile * tm, 0, tm)
    i32 = lambda a: a.astype(jnp.int32)
    return i32(expert), i32(tile), i32(lo), i32(hi), i32(n_items), starts


def _grouped_ffn(xs, items, w_gate, w_up, w_down):
    tm = FFN_TM
    d, de = w_gate.shape[1:]
    item_expert, item_tile, item_lo, item_hi, n_items = items
    max_items = item_expert.shape[0]

    def row_map(j, ie, it, lo, hi, ni):
        return (it[j], 0)

    def w_map(j, ie, it, lo, hi, ni):
        return (ie[j], 0, 0)

    grid_spec = pltpu.PrefetchScalarGridSpec(
        num_scalar_prefetch=5,
        grid=(max_items,),
        in_specs=[pl.BlockSpec((tm * SUBLANES, LANES), row_map),
                  pl.BlockSpec((1, d, de), w_map),
                  pl.BlockSpec((1, d, de), w_map),
                  pl.BlockSpec((1, de, d), w_map)],
        out_specs=pl.BlockSpec((tm * SUBLANES, LANES), row_map),
        scratch_shapes=[pltpu.VMEM((d, de), jnp.bfloat16), pltpu.VMEM((d, de), jnp.bfloat16),
                        pltpu.VMEM((de, d), jnp.bfloat16)],
    )
    return pl.pallas_call(
        _ffn_kernel,
        grid_spec=grid_spec,
        out_shape=jax.ShapeDtypeStruct(xs.shape, jnp.float32),
        compiler_params=_cparams(("arbitrary",)),
        name="moe_grouped_ffn",
    )(item_expert, item_tile, item_lo, item_hi, n_items, xs, w_gate, w_up, w_down)


def _combine_kernel(pos_ref, pos_next_ref, x1_ref, route_ref, g2_ref, b2_ref, ys_ref, o_ref, buf, sems):
    tt = x1_ref.shape[0] // SUBLANES
    i = pl.program_id(0)
    cur = lax.rem(i, 2)

    def issue_tile(p_ref, b):
        def issue(j, carry):
            for u in range(2):
                t = 2 * j + u
                for slot in range(2):
                    p = p_ref[slot, t]
                    src = ys_ref.at[pl.ds(pl.multiple_of(p * SUBLANES, SUBLANES), SUBLANES)]
                    dst = buf.at[b, slot, pl.ds(pl.multiple_of(t * SUBLANES, SUBLANES), SUBLANES)]
                    pltpu.make_async_copy(src, dst, sems.at[b]).start(priority=u)
            return carry

        lax.fori_loop(0, tt // 2, issue, 0, unroll=4)

    @pl.when(i == 0)
    def _():
        issue_tile(pos_ref, 0)

    @pl.when(i + 1 < pl.num_programs(0))
    def _():
        issue_tile(pos_next_ref, 1 - cur)

    for slot in range(2):
        pltpu.make_async_copy(ys_ref.at[pl.ds(0, tt * SUBLANES)], buf.at[cur, slot], sems.at[cur]).wait()
    route = route_ref[...]
    y = route[:, 2:3] * _load_row_tiles(buf.at[cur, 0], tt) + route[:, 3:4] * _load_row_tiles(buf.at[cur, 1], tt)
    o_ref[...] = _layer_norm(DN_ALPHA * _load_row_tiles(x1_ref, tt) + y, g2_ref[...], b2_ref[...])


def _combine(x1t, route, pos, ys, g2, b2, tok0, n_tok):
    d = g2.shape[1]
    tt = COMB_TT
    t0 = tok0 // tt
    nt = n_tok // tt
    return pl.pallas_call(
        _combine_kernel,
        grid=(nt,),
        in_specs=[pl.BlockSpec((2, tt), lambda i: (0, i + t0), memory_space=pltpu.SMEM),
                  pl.BlockSpec((2, tt), lambda i: (0, jnp.minimum(i + 1, nt - 1) + t0), memory_space=pltpu.SMEM),
                  pl.BlockSpec((tt * SUBLANES, LANES), lambda i: (i + t0, 0)),
                  pl.BlockSpec((tt, ROUTE_LANES), lambda i: (i + t0, 0)),
                  pl.BlockSpec((1, d), lambda i: (0, 0)),
                  pl.BlockSpec((1, d), lambda i: (0, 0)),
                  pl.BlockSpec(memory_space=pl.ANY)],
        out_specs=pl.BlockSpec((tt, d), lambda i: (i, 0)),
        out_shape=jax.ShapeDtypeStruct((n_tok, d), jnp.float32),
        scratch_shapes=[pltpu.VMEM((2, 2, tt * SUBLANES, LANES), ys.dtype), pltpu.SemaphoreType.DMA((2,))],
        compiler_params=_cparams(("arbitrary",)),
        name="moe_combine_ln",
    )(pos, pos, x1t, route, g2, b2, ys)


def _layer(x_prompt, x_sample, ln_in_g, ln_in_b, w_in, b_in, rel_bias_t5, na_rpb, w_branch_a, w_branch_b,
           w_out, b_out, ln1_g, ln1_b, w_router_group, b_router_group, w_router_expert, b_router_expert,
           w_exp_gate, w_exp_up, w_exp_down, ln2_g, ln2_b):
    bp, tp, d = x_prompt.shape
    bs, ts, _ = x_sample.shape
    seg_tokens = (bp * tp, bs * ts)
    seq_lens = (tp, ts)
    n = sum(seg_tokens)
    xp = x_prompt.reshape(seg_tokens[0], d)
    xs = x_sample.reshape(seg_tokens[1], d)
    row = lambda a: a.reshape(1, -1).astype(jnp.float32)
    bf = lambda a: a.astype(jnp.bfloat16)

    scale = HEAD_DIM ** -0.5
    col_scale = np.ones((w_in.shape[2],), np.float32)
    col_scale[0:QA] = scale
    col_scale[3 * QA:3 * QA + QB] = scale
    w_proj = bf(w_in[0] * col_scale)
    b_proj = row(b_in[0] * col_scale)
    proj = _in_projection(xp, xs, row(ln_in_g), row(ln_in_b), w_proj, b_proj)
    qkv_a, (qb, kb, vb, ga, gb) = proj[:3 * N_DIL_GROUPS], proj[3 * N_DIL_GROUPS:]

    o_groups, lse_groups = [], []
    for g, (window, dil) in enumerate(DIL_PATTERNS):
        assert window // (2 * dil) == HALF_SPAN
        table = rel_bias_t5[:, g * HEADS_PER_DIL_GROUP:(g + 1) * HEADS_PER_DIL_GROUP].astype(jnp.float32).reshape(-1)
        o, lse = _dilated_group(qkv_a[g], qkv_a[N_DIL_GROUPS + g], qkv_a[2 * N_DIL_GROUPS + g], table, dil,
                                seg_tokens, seq_lens)
        o_groups.append(o)
        lse_groups.append(lse)
    ob = _neighborhood(qb, kb, vb, na_rpb[0].astype(jnp.float32).reshape(-1), seg_tokens, seq_lens)

    wr = jnp.zeros((d, ROUTE_LANES), jnp.float32)
    wr = wr.at[:, 0:N_GROUPS].set(w_router_group[0]).at[:, EXPERT_LANE0:EXPERT_LANE0 + N_EXPERTS].set(w_router_expert[0])
    br = jnp.zeros((1, ROUTE_LANES), jnp.float32)
    br = br.at[0, 0:N_GROUPS].set(b_router_group[0]).at[0, EXPERT_LANE0:EXPERT_LANE0 + N_EXPERTS].set(b_router_expert[0])
    x1, route, route_t, counts = _post_attention(
        xp, xs, row(ln_in_g), row(ln_in_b), o_groups, lse_groups, ob, ga, gb, bf(w_branch_a[0]), bf(w_branch_b[0]),
        bf(w_out[0]), row(b_out[0]), row(ln1_g[0]), row(ln1_b[0]), bf(wr), br)

    experts = route_t[0:2].astype(jnp.int32)
    ranks = route_t[4:6].astype(jnp.int32)
    cnt = counts[0, EXPERT_LANE0:EXPERT_LANE0 + N_EXPERTS].astype(jnp.int32)
    n_rows = 2 * n
    *items, starts = _ffn_items(cnt, n_rows)
    pos = starts[experts] + ranks

    xsorted = _dispatch(x1, pos, n_rows)
    ys = _grouped_ffn(xsorted, items, w_exp_gate[0], w_exp_up[0], w_exp_down[0])
    y_prompt = _combine(x1, route, pos, ys, row(ln2_g[0]), row(ln2_b[0]), 0, seg_tokens[0])
    y_sample = _combine(x1, route, pos, ys, row(ln2_g[0]), row(ln2_b[0]), seg_tokens[0], seg_tokens[1])
    return y_prompt.reshape(bp, tp, d), y_sample.reshape(bs, ts, d)


def kernel(x_prompt, x_sample, ln_in_g, ln_in_b, w_in, b_in, rel_bias_t5, na_rpb, w_branch_a, w_branch_b, w_out, b_out, ln1_g, ln1_b, w_router_group, b_router_group, w_router_expert, b_router_expert, w_exp_gate, w_exp_up, w_exp_down, ln2_g, ln2_b):
    return _layer(x_prompt, x_sample, ln_in_g, ln_in_b, w_in, b_in, rel_bias_t5, na_rpb, w_branch_a, w_branch_b,
                  w_out, b_out, ln1_g, ln1_b, w_router_group, b_router_group, w_router_expert, b_router_expert,
                  w_exp_gate, w_exp_up, w_exp_down, ln2_g, ln2_b)
```

```python
import functools
import math

import numpy as np
import jax
import jax.numpy as jnp
from jax import lax
from jax.experimental import pallas as pl
from jax.experimental.pallas import tpu as pltpu

HEAD_DIM = 64
DIL_PATTERNS = ((128, 1), (512, 4), (2048, 16))
HEADS_PER_DIL_GROUP = 4
N_DIL_GROUPS = 3
H_A = N_DIL_GROUPS * HEADS_PER_DIL_GROUP
H_B = 8
QA = H_A * HEAD_DIM
QB = H_B * HEAD_DIM
HALF_SPAN = 64
GRID_W = 64
NA_KH = 8
NA_KW = 16
N_BUCKETS = 32
REL_MAX_DIST = 1024
N_GROUPS = 4
EXPERTS_PER_GROUP = 8
N_EXPERTS = N_GROUPS * EXPERTS_PER_GROUP
LN_EPS = 1e-5
NEG_INF = -1e30
DEPTH = 1
DN_ALPHA = (2.0 * DEPTH) ** 0.25

LANES = 128
HEAD_BLOCK = 4 * HEAD_DIM
VMEM_LIMIT_BYTES = 56 * 1024 * 1024

PROJ_TM = 512
DIL_M = 128
DIL_GROUP = 4
NA_ROWS = 8
POST_TM = 512
FFN_TM = 512
DISP_TT = 2048
COMB_TT = 512
ROUTE_LANES = LANES
EXPERT_LANE0 = 32


def _cparams(sem):
    return pltpu.CompilerParams(dimension_semantics=sem, vmem_limit_bytes=VMEM_LIMIT_BYTES)


SUBLANES = 8


def _store_row_tiles(ref, x, accumulate=False):
    rows = x.shape[0]
    for c in range(SUBLANES):
        idx = pl.ds(c, rows, stride=SUBLANES)
        piece = x[:, c * LANES:(c + 1) * LANES]
        ref[idx, :] = ref[idx, :] + piece if accumulate else piece


def _load_row_tiles(ref, rows):
    return jnp.concatenate([ref[pl.ds(c, rows, stride=SUBLANES), :] for c in range(SUBLANES)], axis=1)


def _layer_norm(x, g, b):
    mu = jnp.mean(x, axis=-1, keepdims=True)
    xc = x - mu
    var = jnp.mean(xc * xc, axis=-1, keepdims=True)
    return xc * lax.rsqrt(var + LN_EPS) * g + b


def _strided_rows(scr, slab0, r, count, stride):
    return jnp.concatenate([scr[slab0 + s, pl.ds(r, count, stride=stride), :] for s in range(2)], axis=1)


def _inproj_kernel(xp_ref, xs_ref, g_ref, b_ref, w_ref, bias_ref, *refs, n0_tiles):
    out_refs, scr = refs[:-1], refs[-1]
    i = pl.program_id(0)
    x = jnp.where(i < n0_tiles, xp_ref[...], xs_ref[...])
    xn = _layer_norm(x, g_ref[...], b_ref[...]).astype(jnp.bfloat16)
    tm = xn.shape[0]

    def proj(c0, cw):
        return jnp.dot(xn, w_ref[:, c0:c0 + cw], preferred_element_type=jnp.float32) + bias_ref[:, c0:c0 + cw]

    n_dil = 3 * N_DIL_GROUPS
    for idx in range(n_dil):
        g = idx % N_DIL_GROUPS
        dil = DIL_PATTERNS[g][1]
        o_ref = out_refs[idx]
        y = proj(idx * HEAD_BLOCK, HEAD_BLOCK)
        if dil == 1:
            o_ref[0, 0] = y.astype(o_ref.dtype)
        else:
            slab0 = 2 * (idx % 2)
            scr[slab0] = y[:, :LANES]
            scr[slab0 + 1] = y[:, LANES:]
            for r in range(dil):
                o_ref[0, r] = _strided_rows(scr, slab0, r, tm // dil, dil).astype(o_ref.dtype)
    col = n_dil * HEAD_BLOCK
    for o_ref in out_refs[n_dil:]:
        width = o_ref.shape[1]
        for c in range(0, width, 512):
            cw = min(512, width - c)
            o_ref[:, c:c + cw] = proj(col + c, cw).astype(o_ref.dtype)
        col += width


def _in_projection(xp, xs, ln_g, ln_b, w, b):
    n0, d_model = xp.shape
    n1 = xs.shape[0]
    n = n0 + n1
    tm = PROJ_TM
    n0_tiles, n1_tiles = n0 // tm, n1 // tm
    nt = n0_tiles + n1_tiles
    out_shape, out_specs = [], []
    for _ in range(3):
        for _, dil in DIL_PATTERNS:
            out_shape.append(jax.ShapeDtypeStruct((nt, dil, tm // dil, HEAD_BLOCK), jnp.bfloat16))
            out_specs.append(pl.BlockSpec((1, dil, tm // dil, HEAD_BLOCK), lambda i: (i, 0, 0, 0)))
    for wd in (QB, QB, QB, d_model, d_model):
        out_shape.append(jax.ShapeDtypeStruct((n, wd), jnp.bfloat16))
        out_specs.append(pl.BlockSpec((tm, wd), lambda i: (i, 0)))
    d_in = w.shape[1]
    kern = functools.partial(_inproj_kernel, n0_tiles=n0_tiles)
    return pl.pallas_call(
        kern,
        grid=(nt,),
        in_specs=[
            pl.BlockSpec((tm, d_model), lambda i: (jnp.minimum(i, n0_tiles - 1), 0)),
            pl.BlockSpec((tm, d_model), lambda i: (jnp.maximum(i - n0_tiles, 0), 0)),
            pl.BlockSpec((1, d_model), lambda i: (0, 0)),
            pl.BlockSpec((1, d_model), lambda i: (0, 0)),
            pl.BlockSpec((d_model, d_in), lambda i: (0, 0)),
            pl.BlockSpec((1, d_in), lambda i: (0, 0)),
        ],
        out_specs=out_specs,
        out_shape=out_shape,
        scratch_shapes=[pltpu.VMEM((4, tm, LANES), jnp.float32)],
        compiler_params=_cparams(("arbitrary",)),
        name="in_projection",
    )(xp, xs, ln_g, ln_b, w, b)


def _t5_bucket_np(rel):
    half = N_BUCKETS // 2
    max_exact = half // 2
    ret = np.where(rel > 0, half, 0)
    n = np.abs(rel)
    nf = np.maximum(n, 1).astype(np.float32)
    large = max_exact + (np.log(nf / np.float32(max_exact)) / np.float32(math.log(REL_MAX_DIST / max_exact))
                         * np.float32(half - max_exact)).astype(np.int32)
    large = np.minimum(large, half - 1)
    return ret + np.where(n < max_exact, n, large)


def _t5_bucket_starts():
    half = N_BUCKETS // 2
    dist = np.arange(0, HALF_SPAN * max(d for _, d in DIL_PATTERNS) + 1)
    buckets = _t5_bucket_np(-dist)
    return [int(np.argmax(buckets >= k)) for k in range(half)]


def _build_dilated_bias(table_ref, bias_scr, dil):
    half = N_BUCKETS // 2
    starts = _t5_bucket_starts()
    kb = DIL_M + 2 * HALF_SPAN
    i = lax.broadcasted_iota(jnp.int32, (DIL_M, kb), 0)
    m = lax.broadcasted_iota(jnp.int32, (DIL_M, kb), 1)
    delta = m - HALF_SPAN - i
    dist = jnp.abs(delta) * dil
    band = jnp.abs(delta) <= HALF_SPAN
    for h in range(HEADS_PER_DIL_GROUP):
        sides = []
        for side in range(2):
            val = jnp.full((DIL_M, kb), table_ref[(side * half + half - 1) * HEADS_PER_DIL_GROUP + h], jnp.float32)
            for k in range(half - 1, 0, -1):
                val = jnp.where(dist < starts[k], table_ref[(side * half + k - 1) * HEADS_PER_DIL_GROUP + h], val)
            sides.append(val)
        base = jnp.where(band, jnp.where(delta > 0, sides[1], sides[0]), NEG_INF)
        rows = slice(h * DIL_M, (h + 1) * DIL_M)
        bias_scr[0, rows, :] = base
        bias_scr[1, rows, :] = jnp.where(m >= HALF_SPAN, base, NEG_INF)
        bias_scr[2, rows, :] = jnp.where(m < kb - HALF_SPAN, base, NEG_INF)


def _stack_heads(q):
    head_of_col = lax.broadcasted_iota(jnp.int32, (1, HEAD_BLOCK), 1) // HEAD_DIM
    zero = jnp.zeros_like(q)
    return jnp.concatenate([jnp.where(head_of_col == h, q, zero) for h in range(4)], axis=0)


def _merge_heads(x, m):
    head_of_col = lax.broadcasted_iota(jnp.int32, (1, HEAD_BLOCK), 1) // HEAD_DIM
    out = jnp.zeros((m, HEAD_BLOCK), x.dtype)
    for h in range(4):
        out = jnp.where(head_of_col == h, x[h * m:(h + 1) * m], out)
    return out


def _softmax_pv(s, v, m_rows):
    mx = jnp.max(s, axis=-1, keepdims=True)
    p = jnp.exp(s - mx)
    l = jnp.sum(p, axis=-1, keepdims=True)
    pv = jnp.dot(p.astype(jnp.bfloat16), v, preferred_element_type=jnp.float32)
    o = _merge_heads(pv * (1.0 / l), m_rows)
    return o, mx + jnp.log(l)


def _dilated_kernel(table_ref, q_ref, kp_ref, kc_ref, kn_ref, vp_ref, vc_ref, vn_ref,
                    o_ref, lse_ref, bias_scr, *, dil, seg0_blocks, blocks_per_seq):
    step = pl.program_id(1)

    @pl.when((pl.program_id(0) == 0) & (step == 0))
    def _():
        _build_dilated_bias(table_ref, bias_scr, dil)

    rows2d = lambda ref: ref[...].reshape(-1, HEAD_BLOCK)
    k = jnp.concatenate([rows2d(r) for r in (kp_ref, kc_ref, kn_ref)], axis=0)
    v = jnp.concatenate([rows2d(r) for r in (vp_ref, vc_ref, vn_ref)], axis=0)
    q = rows2d(q_ref)
    o_parts, lse_parts = [], []
    for g in range(DIL_GROUP):
        jb = step * DIL_GROUP + g
        in0 = jb < seg0_blocks
        jl = jnp.where(in0, jb, jb - seg0_blocks)
        nbs = jnp.where(in0, blocks_per_seq[0], blocks_per_seq[1])
        pos = lax.rem(jl, nbs)
        variant = jnp.where(pos == 0, 1, jnp.where(pos == nbs - 1, 2, 0))
        keys = slice(g * DIL_M, (g + 1) * DIL_M + 2 * HALF_SPAN)
        qs = _stack_heads(q[g * DIL_M:(g + 1) * DIL_M])
        s = lax.dot_general(qs, k[keys], (((1,), (1,)), ((), ())), preferred_element_type=jnp.float32)
        s = s + bias_scr[variant]
        o, lse = _softmax_pv(s, v[keys], DIL_M)
        o_parts.append(o.astype(o_ref.dtype))
        lse_parts.append(_merge_heads(jnp.broadcast_to(lse, (4 * DIL_M, HEAD_BLOCK)), DIL_M))
    o_ref[...] = jnp.concatenate(o_parts, axis=0).reshape(o_ref.shape)
    lse_ref[...] = jnp.concatenate(lse_parts, axis=0).reshape(lse_ref.shape)


def _residue_spec(rows_blk, dil, block_of):
    per_tile = PROJ_TM // dil
    if per_tile >= rows_blk:
        per = per_tile // rows_blk
        return pl.BlockSpec((1, None, rows_blk, HEAD_BLOCK),
                            lambda r, jb: (block_of(jb) // per, r, block_of(jb) % per, 0))
    return pl.BlockSpec((rows_blk // per_tile, None, per_tile, HEAD_BLOCK), lambda r, jb: (block_of(jb), r, 0, 0))


def _dilated_group(q, k, v, table, dil, seg_tokens, seq_lens):
    n = q.shape[0] * PROJ_TM
    rows = n // dil
    step_rows = DIL_GROUP * DIL_M
    assert rows % step_rows == 0 and all(t % (dil * DIL_M) == 0 for t in seq_lens)
    nsteps = rows // step_rows
    nkb = rows // HALF_SPAN
    ratio = step_rows // HALF_SPAN
    seg0_blocks = seg_tokens[0] // dil // DIL_M
    blocks_per_seq = tuple(t // dil // DIL_M for t in seq_lens)
    assert min(blocks_per_seq) >= 2

    qo_spec = _residue_spec(step_rows, dil, lambda jb: jb)
    prev_spec = _residue_spec(HALF_SPAN, dil, lambda jb: jnp.maximum(ratio * jb - 1, 0))
    next_spec = _residue_spec(HALF_SPAN, dil, lambda jb: jnp.minimum(ratio * (jb + 1), nkb - 1))
    kv_specs = [prev_spec, qo_spec, next_spec]
    kern = functools.partial(_dilated_kernel, dil=dil, seg0_blocks=seg0_blocks, blocks_per_seq=blocks_per_seq)
    return pl.pallas_call(
        kern,
        grid=(dil, nsteps),
        in_specs=[pl.BlockSpec(memory_space=pltpu.SMEM), qo_spec] + kv_specs + kv_specs,
        out_specs=[qo_spec, qo_spec],
        out_shape=[jax.ShapeDtypeStruct(q.shape, jnp.bfloat16), jax.ShapeDtypeStruct(q.shape, jnp.float32)],
        scratch_shapes=[pltpu.VMEM((3, HEADS_PER_DIL_GROUP * DIL_M, DIL_M + 2 * HALF_SPAN), jnp.float32)],
        compiler_params=_cparams(("arbitrary", "arbitrary")),
        name=f"dilated_attention_d{dil}",
    )(table, q, k, k, k, v, v, v)


NA_DR = 2 * NA_KH - 1
NA_DC = 2 * NA_KW - 1
NA_PAIRS = NA_DR - 1


def _build_na_bias(rpb_ref, pair_scr, head0):
    qc = lax.broadcasted_iota(jnp.int32, (GRID_W, 2 * GRID_W), 0)
    lane = lax.broadcasted_iota(jnp.int32, (GRID_W, 2 * GRID_W), 1)
    kc = lane % GRID_W
    upper = lane >= GRID_W
    qstart = jnp.clip(qc - NA_KW // 2, 0, GRID_W - NA_KW)
    valid = (kc >= qstart) & (kc < qstart + NA_KW)
    dc = jnp.clip(kc - qc, -(NA_KW - 1), NA_KW - 1) + NA_KW - 1

    def one_pair(idx, carry):
        h = idx // NA_PAIRS
        dr = idx % NA_PAIRS
        base = ((head0 + h) * NA_DR + dr) * NA_DC
        val = jnp.zeros((GRID_W, 2 * GRID_W), jnp.float32)
        for c in range(NA_DC):
            val = jnp.where(dc == c, jnp.where(upper, rpb_ref[base + NA_DC + c], rpb_ref[base + c]), val)
        pair_scr[h, dr] = jnp.where(valid, val, NEG_INF)
        return carry

    lax.fori_loop(0, 4 * NA_PAIRS, one_pair, 0)


def _na_kernel(rpb_ref, q_ref, kp_ref, kc_ref, kn_ref, vp_ref, vc_ref, vn_ref, o_ref, k_scr, v_scr, pair_scr,
               *, seg0_blocks, blocks_per_seq):
    ib = pl.program_id(1)

    @pl.when(ib == 0)
    def _():
        _build_na_bias(rpb_ref, pair_scr, pl.program_id(0) * 4)

    in0 = ib < seg0_blocks
    il = jnp.where(in0, ib, ib - seg0_blocks)
    nbs = jnp.where(in0, blocks_per_seq[0], blocks_per_seq[1])
    pos = lax.rem(il, nbs)
    first = pos == 0
    last = pos == nbs - 1
    blk = NA_ROWS * GRID_W
    k_scr[0:blk] = kp_ref[...]
    k_scr[blk:2 * blk] = kc_ref[...]
    k_scr[2 * blk:3 * blk] = kn_ref[...]
    v_scr[0:blk] = vp_ref[...]
    v_scr[blk:2 * blk] = vc_ref[...]
    v_scr[2 * blk:3 * blk] = vn_ref[...]
    half = NA_KH // 2
    for rr in range(NA_ROWS):
        start = jnp.where(first, max(rr + half, NA_ROWS), jnp.where(last, min(rr + half, NA_ROWS), rr + half))
        var = jnp.where(first, min(rr, half), jnp.where(last, max(rr, half), half))
        off = pl.multiple_of(start * GRID_W, GRID_W)
        kk = k_scr[pl.ds(off, NA_KH * GRID_W), :]
        vv = v_scr[pl.ds(off, NA_KH * GRID_W), :]
        qs = _stack_heads(q_ref[rr * GRID_W:(rr + 1) * GRID_W, :])
        s = lax.dot_general(qs, kk, (((1,), (1,)), ((), ())), preferred_element_type=jnp.float32)
        bias = jnp.concatenate(
            [jnp.concatenate([pair_scr[h, 2 * p - var + NA_KH - 1] for p in range(NA_KH // 2)], axis=1)
             for h in range(4)], axis=0)
        s = s + bias
        o, _ = _softmax_pv(s, vv, GRID_W)
        o_ref[rr * GRID_W:(rr + 1) * GRID_W, :] = o.astype(o_ref.dtype)


def _neighborhood(qb, kb, vb, rpb_flat, seg_tokens, seq_lens):
    n = qb.shape[0]
    blk = NA_ROWS * GRID_W
    nblk = n // blk
    seg0_blocks = seg_tokens[0] // blk
    blocks_per_seq = tuple(t // blk for t in seq_lens)
    assert min(blocks_per_seq) >= 2
    nset = QB // HEAD_BLOCK

    def kv_spec(t):
        return pl.BlockSpec((blk, HEAD_BLOCK), lambda c, ib: (jnp.clip(ib - 1 + t, 0, nblk - 1), c))

    kern = functools.partial(_na_kernel, seg0_blocks=seg0_blocks, blocks_per_seq=blocks_per_seq)
    return pl.pallas_call(
        kern,
        grid=(nset, nblk),
        in_specs=[pl.BlockSpec(memory_space=pltpu.SMEM), pl.BlockSpec((blk, HEAD_BLOCK), lambda c, ib: (ib, c))]
        + [kv_spec(t) for t in range(3)] + [kv_spec(t) for t in range(3)],
        out_specs=pl.BlockSpec((blk, HEAD_BLOCK), lambda c, ib: (ib, c)),
        out_shape=jax.ShapeDtypeStruct((n, QB), jnp.bfloat16),
        scratch_shapes=[pltpu.VMEM((3 * blk, HEAD_BLOCK), jnp.bfloat16),
                        pltpu.VMEM((3 * blk, HEAD_BLOCK), jnp.bfloat16),
                        pltpu.VMEM((4, NA_PAIRS, GRID_W, 2 * GRID_W), jnp.float32)],
        compiler_params=_cparams(("arbitrary", "arbitrary")),
        name="neighborhood_attention",
    )(rpb_flat, qb, kb, kb, kb, vb, vb, vb)


def _post_kernel(xp_ref, xs_ref, lng_ref, lnb_ref, o0_ref, o1_ref, o2_ref, l0_ref, l1_ref, l2_ref, ob_ref,
                 ga_ref, gb_ref, wa_ref, wb_ref, wo_ref, bo_ref, g1_ref, b1_ref, wr_ref, br_ref, tri_ref,
                 x1_ref, route_ref, route_t_ref, counts_ref, carry_ref, perm_scr, *, n0_tiles):
    i = pl.program_id(0)

    @pl.when(i == 0)
    def _():
        carry_ref[...] = jnp.zeros_like(carry_ref)

    def token_order(ref, slab0):
        dil, per = ref.shape[1], ref.shape[2]
        if dil == 1:
            return ref[0, 0].astype(jnp.float32)
        for r in range(dil):
            val = ref[0, r].astype(jnp.float32)
            for s in range(2):
                perm_scr[slab0 + s, pl.ds(r, per, stride=dil), :] = val[:, s * LANES:(s + 1) * LANES]
        return jnp.concatenate([perm_scr[slab0], perm_scr[slab0 + 1]], axis=1)

    l0, l1, l2 = (token_order(ref, 2 * j) for j, ref in enumerate((l0_ref, l1_ref, l2_ref)))
    lm = jnp.maximum(jnp.maximum(l0, l1), l2)
    e0, e1, e2 = jnp.exp(l0 - lm), jnp.exp(l1 - lm), jnp.exp(l2 - lm)
    inv = 1.0 / (e0 + e1 + e2)
    o0, o1, o2 = (token_order(ref, 6 + 2 * j) for j, ref in enumerate((o0_ref, o1_ref, o2_ref)))
    o_a = (e0 * inv) * o0 + (e1 * inv) * o1 + (e2 * inv) * o2
    y_a = jnp.dot(o_a.astype(jnp.bfloat16), wa_ref[...], preferred_element_type=jnp.float32)
    y_b = jnp.dot(ob_ref[...], wb_ref[...], preferred_element_type=jnp.float32)
    mix = jax.nn.sigmoid(ga_ref[...].astype(jnp.float32)) * y_a + jax.nn.sigmoid(gb_ref[...].astype(jnp.float32)) * y_b
    out = jnp.dot(mix.astype(jnp.bfloat16), wo_ref[...], preferred_element_type=jnp.float32) + bo_ref[...]
    x = jnp.where(i < n0_tiles, xp_ref[...], xs_ref[...])
    x0 = _layer_norm(x, lng_ref[...], lnb_ref[...])
    x1 = _layer_norm(DN_ALPHA * x0 + out, g1_ref[...], b1_ref[...])
    _store_row_tiles(x1_ref, x1)

    r = jnp.dot(x1.astype(jnp.bfloat16), wr_ref[...], preferred_element_type=jnp.float32) + br_ref[...]
    tm = r.shape[0]
    lane = lax.broadcasted_iota(jnp.int32, (tm, ROUTE_LANES), 1)
    neg = jnp.float32(-jnp.inf)
    lg = jnp.where(lane < N_GROUPS, r, neg)
    gmax = jnp.max(lg, axis=-1, keepdims=True)
    gi = jnp.min(jnp.where(lg == gmax, lane, ROUTE_LANES), axis=-1, keepdims=True)
    wg = 1.0 / jnp.sum(jnp.exp(lg - gmax), axis=-1, keepdims=True)
    lo = EXPERT_LANE0 + EXPERTS_PER_GROUP * gi
    le = jnp.where((lane >= lo) & (lane < lo + EXPERTS_PER_GROUP), r, neg)
    m1 = jnp.max(le, axis=-1, keepdims=True)
    i1 = jnp.min(jnp.where(le == m1, lane, ROUTE_LANES), axis=-1, keepdims=True)
    le2 = jnp.where(lane == i1, neg, le)
    m2 = jnp.max(le2, axis=-1, keepdims=True)
    i2 = jnp.min(jnp.where(le2 == m2, lane, ROUTE_LANES), axis=-1, keepdims=True)
    t2 = jnp.exp(m2 - m1)
    w1 = wg / (1.0 + t2)
    w2 = wg * t2 / (1.0 + t2)

    hot1 = lane == i1
    hot2 = lane == i2
    hot = (hot1 | hot2).astype(jnp.float32)
    before = jnp.dot(tri_ref[...], hot.astype(jnp.bfloat16), preferred_element_type=jnp.float32) + carry_ref[0:1, :]
    rank1 = jnp.sum(jnp.where(hot1, before, 0.0), axis=-1, keepdims=True)
    rank2 = jnp.sum(jnp.where(hot2, before, 0.0), axis=-1, keepdims=True)
    total = carry_ref[0:1, :] + jnp.sum(hot, axis=0, keepdims=True)
    carry_ref[...] = jnp.broadcast_to(total, carry_ref.shape)
    counts_ref[...] = jnp.broadcast_to(total, counts_ref.shape)

    e1f = (i1 - EXPERT_LANE0).astype(jnp.float32)
    e2f = (i2 - EXPERT_LANE0).astype(jnp.float32)
    route = jnp.zeros((tm, ROUTE_LANES), jnp.float32)
    for idx, val in enumerate((e1f, e2f, w1, w2, rank1, rank2)):
        route = jnp.where(lane == idx, val, route)
    route_ref[...] = route
    route_t_ref[...] = route.T[0:route_t_ref.shape[0]]


def _post_attention(xp, xs, ln_in_g, ln_in_b, o_groups, lse_groups, ob, ga, gb, wa, wb, wo, bo, g1, b1, wr, br):
    n0, d = xp.shape
    n = n0 + xs.shape[0]
    tm = POST_TM
    n0_tiles = n0 // tm
    nt = n // tm
    tri = jnp.asarray(np.tril(np.ones((tm, tm), np.float32), -1), jnp.bfloat16)

    def tok(width):
        return pl.BlockSpec((tm, width), lambda i: (i, 0))

    def full(a):
        return pl.BlockSpec(a.shape, lambda i: (0,) * a.ndim)

    kern = functools.partial(_post_kernel, n0_tiles=n0_tiles)
    return pl.pallas_call(
        kern,
        grid=(nt,),
        in_specs=[
            pl.BlockSpec((tm, d), lambda i: (jnp.minimum(i, n0_tiles - 1), 0)),
            pl.BlockSpec((tm, d), lambda i: (jnp.maximum(i - n0_tiles, 0), 0)),
            full(ln_in_g), full(ln_in_b),
            *[pl.BlockSpec((1,) + a.shape[1:], lambda i: (i, 0, 0, 0)) for a in (*o_groups, *lse_groups)],
            tok(QB), tok(d), tok(d),
            full(wa), full(wb), full(wo), full(bo), full(g1), full(b1), full(wr), full(br), full(tri),
        ],
        out_specs=[pl.BlockSpec((tm * SUBLANES, LANES), lambda i: (i, 0)), tok(ROUTE_LANES),
                   pl.BlockSpec((8, tm), lambda i: (0, i)), pl.BlockSpec((8, ROUTE_LANES), lambda i: (0, 0))],
        out_shape=[jax.ShapeDtypeStruct((n * SUBLANES, LANES), jnp.float32),
                   jax.ShapeDtypeStruct((n, ROUTE_LANES), jnp.float32),
                   jax.ShapeDtypeStruct((8, n), jnp.float32),
                   jax.ShapeDtypeStruct((8, ROUTE_LANES), jnp.float32)],
        scratch_shapes=[pltpu.VMEM((8, ROUTE_LANES), jnp.float32), pltpu.VMEM((12, tm, LANES), jnp.float32)],
        compiler_params=_cparams(("arbitrary",)),
        name="post_attention_router",
    )(xp, xs, ln_in_g, ln_in_b, *o_groups, *lse_groups, ob, ga, gb, wa, wb, wo, bo, g1, b1, wr, br, tri)


def _dispatch_kernel(pos_ref, x_ref, xs_ref, sem):
    tt = x_ref.shape[0] // SUBLANES

    def issue(j, carry):
        for u in range(2):
            t = 2 * j + u
            src = x_ref.at[pl.ds(pl.multiple_of(t * SUBLANES, SUBLANES), SUBLANES)]
            for slot in range(2):
                p = pos_ref[slot, t]
                dst = xs_ref.at[pl.ds(pl.multiple_of(p * SUBLANES, SUBLANES), SUBLANES)]
                pltpu.make_async_copy(src, dst, sem).start(priority=u)
        return carry

    lax.fori_loop(0, tt // 2, issue, 0, unroll=4)
    for _ in range(2):
        pltpu.make_async_copy(x_ref, xs_ref.at[pl.ds(0, tt * SUBLANES)], sem).wait()


def _dispatch(x1t, pos, n_rows):
    n = x1t.shape[0] // SUBLANES
    tt = DISP_TT
    return pl.pallas_call(
        _dispatch_kernel,
        grid=(n // tt,),
        in_specs=[pl.BlockSpec((2, tt), lambda i: (0, i), memory_space=pltpu.SMEM),
                  pl.BlockSpec((tt * SUBLANES, LANES), lambda i: (i, 0))],
        out_specs=pl.BlockSpec(memory_space=pl.ANY),
        out_shape=jax.ShapeDtypeStruct((n_rows * SUBLANES, LANES), x1t.dtype),
        scratch_shapes=[pltpu.SemaphoreType.DMA(())],
        compiler_params=_cparams(("arbitrary",)),
        name="moe_dispatch",
    )(pos, x1t)


def _ffn_kernel(item_expert_ref, item_tile_ref, item_lo_ref, item_hi_ref, n_items_ref,
                x_ref, wg_ref, wu_ref, wd_ref, y_ref, wg_bf, wu_bf, wd_bf):
    j = pl.program_id(0)

    @pl.when(j < n_items_ref[0])
    def _():
        prev = jnp.maximum(j - 1, 0)

        @pl.when(jnp.logical_or(j == 0, item_expert_ref[j] != item_expert_ref[prev]))
        def _():
            wg_bf[...] = wg_ref[0].astype(jnp.bfloat16)
            wu_bf[...] = wu_ref[0].astype(jnp.bfloat16)
            wd_bf[...] = wd_ref[0].astype(jnp.bfloat16)

        tm = x_ref.shape[0] // SUBLANES
        x = _load_row_tiles(x_ref, tm).astype(jnp.bfloat16)
        a = jnp.dot(x, wg_bf[...], preferred_element_type=jnp.float32)
        u = jnp.dot(x, wu_bf[...], preferred_element_type=jnp.float32)
        h = (a * jax.nn.sigmoid(a) * u).astype(jnp.bfloat16)
        y = jnp.dot(h, wd_bf[...], preferred_element_type=jnp.float32)
        rows = lax.broadcasted_iota(jnp.int32, (tm, 1), 0)
        y = jnp.where((rows >= item_lo_ref[j]) & (rows < item_hi_ref[j]), y, 0.0)
        first_of_tile = jnp.logical_or(j == 0, item_tile_ref[j] != item_tile_ref[prev])

        @pl.when(first_of_tile)
        def _():
            _store_row_tiles(y_ref, y)

        @pl.when(jnp.logical_not(first_of_tile))
        def _():
            _store_row_tiles(y_ref, y, accumulate=True)


def _ffn_items(cnt, n_rows):
    tm = FFN_TM
    n_tiles = n_rows // tm
    max_items = n_tiles + N_EXPERTS - 1
    ends = jnp.cumsum(cnt)
    starts = ends - cnt
    first_tile = starts // tm
    last_tile = jnp.maximum(ends - 1, 0) // tm
    items_per_expert = jnp.where(cnt > 0, last_tile - first_tile + 1, 0)
    item_ends = jnp.cumsum(items_per_expert)
    n_items = item_ends[-1:]
    j = jnp.minimum(jnp.arange(max_items, dtype=jnp.int32), n_items[0] - 1)
    expert = jnp.minimum(jnp.sum(item_ends[None, :] <= j[:, None], axis=1), N_EXPERTS - 1).astype(jnp.int32)
    tile = first_tile[expert] + (j - (item_ends - items_per_expert)[expert])
    lo = jnp.clip(starts[expert] - tile * tm, 0, tm)
    hi = jnp.clip(ends[expert] - tile * tm, 0, tm)
    i32 = lambda a: a.astype(jnp.int32)
    return i32(expert), i32(tile), i32(lo), i32(hi), i32(n_items), starts


def _grouped_ffn(xs, items, w_gate, w_up, w_down):
    tm = FFN_TM
    d, de = w_gate.shape[1:]
    item_expert, item_tile, item_lo, item_hi, n_items = items
    max_items = item_expert.shape[0]

    def row_map(j, ie, it, lo, hi, ni):
        return (it[j], 0)

    def w_map(j, ie, it, lo, hi, ni):
        return (ie[j], 0, 0)

    grid_spec = pltpu.PrefetchScalarGridSpec(
        num_scalar_prefetch=5,
        grid=(max_items,),
        in_specs=[pl.BlockSpec((tm * SUBLANES, LANES), row_map),
                  pl.BlockSpec((1, d, de), w_map),
                  pl.BlockSpec((1, d, de), w_map),
                  pl.BlockSpec((1, de, d), w_map)],
        out_specs=pl.BlockSpec((tm * SUBLANES, LANES), row_map),
        scratch_shapes=[pltpu.VMEM((d, de), jnp.bfloat16), pltpu.VMEM((d, de), jnp.bfloat16),
                        pltpu.VMEM((de, d), jnp.bfloat16)],
    )
    return pl.pallas_call(
        _ffn_kernel,
        grid_spec=grid_spec,
        out_shape=jax.ShapeDtypeStruct(xs.shape, jnp.float32),
        compiler_params=_cparams(("arbitrary",)),
        name="moe_grouped_ffn",
    )(item_expert, item_tile, item_lo, item_hi, n_items, xs, w_gate, w_up, w_down)


def _combine_kernel(pos_ref, pos_next_ref, x1_ref, route_ref, g2_ref, b2_ref, ys_ref, o_ref, buf, sems):
    tt = x1_ref.shape[0] // SUBLANES
    i = pl.program_id(0)
    cur = lax.rem(i, 2)

    def issue_tile(p_ref, b):
        def issue(j, carry):
            for u in range(2):
                t = 2 * j + u
                for slot in range(2):
                    p = p_ref[slot, t]
                    src = ys_ref.at[pl.ds(pl.multiple_of(p * SUBLANES, SUBLANES), SUBLANES)]
                    dst = buf.at[b, slot, pl.ds(pl.multiple_of(t * SUBLANES, SUBLANES), SUBLANES)]
                    pltpu.make_async_copy(src, dst, sems.at[b]).start(priority=u)
            return carry

        lax.fori_loop(0, tt // 2, issue, 0, unroll=4)

    @pl.when(i == 0)
    def _():
        issue_tile(pos_ref, 0)

    @pl.when(i + 1 < pl.num_programs(0))
    def _():
        issue_tile(pos_next_ref, 1 - cur)

    for slot in range(2):
        pltpu.make_async_copy(ys_ref.at[pl.ds(0, tt * SUBLANES)], buf.at[cur, slot], sems.at[cur]).wait()
    route = route_ref[...]
    y = route[:, 2:3] * _load_row_tiles(buf.at[cur, 0], tt) + route[:, 3:4] * _load_row_tiles(buf.at[cur, 1], tt)
    o_ref[...] = _layer_norm(DN_ALPHA * _load_row_tiles(x1_ref, tt) + y, g2_ref[...], b2_ref[...])


def _combine(x1t, route, pos, ys, g2, b2, tok0, n_tok):
    d = g2.shape[1]
    tt = COMB_TT
    t0 = tok0 // tt
    nt = n_tok // tt
    return pl.pallas_call(
        _combine_kernel,
        grid=(nt,),
        in_specs=[pl.BlockSpec((2, tt), lambda i: (0, i + t0), memory_space=pltpu.SMEM),
                  pl.BlockSpec((2, tt), lambda i: (0, jnp.minimum(i + 1, nt - 1) + t0), memory_space=pltpu.SMEM),
                  pl.BlockSpec((tt * SUBLANES, LANES), lambda i: (i + t0, 0)),
                  pl.BlockSpec((tt, ROUTE_LANES), lambda i: (i + t0, 0)),
                  pl.BlockSpec((1, d), lambda i: (0, 0)),
                  pl.BlockSpec((1, d), lambda i: (0, 0)),
                  pl.BlockSpec(memory_space=pl.ANY)],
        out_specs=pl.BlockSpec((tt, d), lambda i: (i, 0)),
        out_shape=jax.ShapeDtypeStruct((n_tok, d), jnp.float32),
        scratch_shapes=[pltpu.VMEM((2, 2, tt * SUBLANES, LANES), ys.dtype), pltpu.SemaphoreType.DMA((2,))],
        compiler_params=_cparams(("arbitrary",)),
        name="moe_combine_ln",
    )(pos, pos, x1t, route, g2, b2, ys)


def _layer(x_prompt, x_sample, ln_in_g, ln_in_b, w_in, b_in, rel_bias_t5, na_rpb, w_branch_a, w_branch_b,
           w_out, b_out, ln1_g, ln1_b, w_router_group, b_router_group, w_router_expert, b_router_expert,
           w_exp_gate, w_exp_up, w_exp_down, ln2_g, ln2_b):
    bp, tp, d = x_prompt.shape
    bs, ts, _ = x_sample.shape
    seg_tokens = (bp * tp, bs * ts)
    seq_lens = (tp, ts)
    n = sum(seg_tokens)
    xp = x_prompt.reshape(seg_tokens[0], d)
    xs = x_sample.reshape(seg_tokens[1], d)
    row = lambda a: a.reshape(1, -1).astype(jnp.float32)
    bf = lambda a: a.astype(jnp.bfloat16)

    scale = HEAD_DIM ** -0.5
    col_scale = np.ones((w_in.shape[2],), np.float32)
    col_scale[0:QA] = scale
    col_scale[3 * QA:3 * QA + QB] = scale
    w_proj = bf(w_in[0] * col_scale)
    b_proj = row(b_in[0] * col_scale)
    proj = _in_projection(xp, xs, row(ln_in_g), row(ln_in_b), w_proj, b_proj)
    qkv_a, (qb, kb, vb, ga, gb) = proj[:3 * N_DIL_GROUPS], proj[3 * N_DIL_GROUPS:]

    o_groups, lse_groups = [], []
    for g, (window, dil) in enumerate(DIL_PATTERNS):
        assert window // (2 * dil) == HALF_SPAN
        table = rel_bias_t5[:, g * HEADS_PER_DIL_GROUP:(g + 1) * HEADS_PER_DIL_GROUP].astype(jnp.float32).reshape(-1)
        o, lse = _dilated_group(qkv_a[g], qkv_a[N_DIL_GROUPS + g], qkv_a[2 * N_DIL_GROUPS + g], table, dil,
                                seg_tokens, seq_lens)
        o_groups.append(o)
        lse_groups.append(lse)
    ob = _neighborhood(qb, kb, vb, na_rpb[0].astype(jnp.float32).reshape(-1), seg_tokens, seq_lens)

    wr = jnp.zeros((d, ROUTE_LANES), jnp.float32)
    wr = wr.at[:, 0:N_GROUPS].set(w_router_group[0]).at[:, EXPERT_LANE0:EXPERT_LANE0 + N_EXPERTS].set(w_router_expert[0])
    br = jnp.zeros((1, ROUTE_LANES), jnp.float32)
    br = br.at[0, 0:N_GROUPS].set(b_router_group[0]).at[0, EXPERT_LANE0:EXPERT_LANE0 + N_EXPERTS].set(b_router_expert[0])
    x1, route, route_t, counts = _post_attention(
        xp, xs, row(ln_in_g), row(ln_in_b), o_groups, lse_groups, ob, ga, gb, bf(w_branch_a[0]), bf(w_branch_b[0]),
        bf(w_out[0]), row(b_out[0]), row(ln1_g[0]), row(ln1_b[0]), bf(wr), br)

    experts = route_t[0:2].astype(jnp.int32)
    ranks = route_t[4:6].astype(jnp.int32)
    cnt = counts[0, EXPERT_LANE0:EXPERT_LANE0 + N_EXPERTS].astype(jnp.int32)
    n_rows = 2 * n
    *items, starts = _ffn_items(cnt, n_rows)
    pos = ranks
    for e in range(N_EXPERTS):
        pos = pos + jnp.where(experts == e, starts[e], 0)

    xsorted = _dispatch(x1, pos, n_rows)
    ys = _grouped_ffn(xsorted, items, w_exp_gate[0], w_exp_up[0], w_exp_down[0])
    y_prompt = _combine(x1, route, pos, ys, row(ln2_g[0]), row(ln2_b[0]), 0, seg_tokens[0])
    y_sample = _combine(x1, route, pos, ys, row(ln2_g[0]), row(ln2_b[0]), seg_tokens[0], seg_tokens[1])
    return y_prompt.reshape(bp, tp, d), y_sample.reshape(bs, ts, d)


def kernel(x_prompt, x_sample, ln_in_g, ln_in_b, w_in, b_in, rel_bias_t5, na_rpb, w_branch_a, w_branch_b, w_out, b_out, ln1_g, ln1_b, w_router_group, b_router_group, w_router_expert, b_router_expert, w_exp_gate, w_exp_up, w_exp_down, ln2_g, ln2_b):
    return _layer(x_prompt, x_sample, ln_in_g, ln_in_b, w_in, b_in, rel_bias_t5, na_rpb, w_branch_a, w_branch_b,
                  w_out, b_out, ln1_g, ln1_b, w_router_group, b_router_group, w_router_expert, b_router_expert,
                  w_exp_gate, w_exp_up, w_exp_down, ln2_g, ln2_b)
```

```python
import functools
import math

import numpy as np
import jax
import jax.numpy as jnp
from jax import lax
from jax.experimental import pallas as pl
from jax.experimental.pallas import tpu as pltpu

HEAD_DIM = 64
DIL_PATTERNS = ((128, 1), (512, 4), (2048, 16))
HEADS_PER_DIL_GROUP = 4
N_DIL_GROUPS = 3
H_A = N_DIL_GROUPS * HEADS_PER_DIL_GROUP
H_B = 8
QA = H_A * HEAD_DIM
QB = H_B * HEAD_DIM
HALF_SPAN = 64
GRID_W = 64
NA_KH = 8
NA_KW = 16
N_BUCKETS = 32
REL_MAX_DIST = 1024
N_GROUPS = 4
EXPERTS_PER_GROUP = 8
N_EXPERTS = N_GROUPS * EXPERTS_PER_GROUP
LN_EPS = 1e-5
NEG_INF = -1e30
DEPTH = 1
DN_ALPHA = (2.0 * DEPTH) ** 0.25

LANES = 128
HEAD_BLOCK = 4 * HEAD_DIM
VMEM_LIMIT_BYTES = 56 * 1024 * 1024

PROJ_TM = 512
DIL_M = 128
DIL_GROUP = 4
NA_ROWS = 8
POST_TM = 512
POST_SPLIT = 1
FFN_TM = 512
DISP_TT = 2048
COMB_TT = 512
ROUTE_LANES = LANES
EXPERT_LANE0 = 32


def _cparams(sem):
    return pltpu.CompilerParams(dimension_semantics=sem, vmem_limit_bytes=VMEM_LIMIT_BYTES)


SUBLANES = 8


def _store_row_tiles(ref, x, accumulate=False, row0=0):
    rows = x.shape[0]
    for c in range(SUBLANES):
        idx = pl.ds(row0 * SUBLANES + c, rows, stride=SUBLANES)
        piece = x[:, c * LANES:(c + 1) * LANES]
        ref[idx, :] = ref[idx, :] + piece if accumulate else piece


def _load_row_tiles(ref, rows):
    return jnp.concatenate([ref[pl.ds(c, rows, stride=SUBLANES), :] for c in range(SUBLANES)], axis=1)


def _layer_norm(x, g, b):
    mu = jnp.mean(x, axis=-1, keepdims=True)
    xc = x - mu
    var = jnp.mean(xc * xc, axis=-1, keepdims=True)
    return xc * lax.rsqrt(var + LN_EPS) * g + b


def _strided_rows(scr, slab0, r, count, stride):
    return jnp.concatenate([scr[slab0 + s, pl.ds(r, count, stride=stride), :] for s in range(2)], axis=1)


def _inproj_kernel(xp_ref, xs_ref, g_ref, b_ref, w_ref, bias_ref, *refs, n0_tiles):
    out_refs, scr = refs[:-1], refs[-1]
    i = pl.program_id(0)
    x = jnp.where(i < n0_tiles, xp_ref[...], xs_ref[...])
    xn = _layer_norm(x, g_ref[...], b_ref[...]).astype(jnp.bfloat16)
    tm = xn.shape[0]

    def proj(c0, cw):
        return jnp.dot(xn, w_ref[:, c0:c0 + cw], preferred_element_type=jnp.float32) + bias_ref[:, c0:c0 + cw]

    n_dil = 3 * N_DIL_GROUPS
    for idx in range(n_dil):
        g = idx % N_DIL_GROUPS
        dil = DIL_PATTERNS[g][1]
        o_ref = out_refs[idx]
        y = proj(idx * HEAD_BLOCK, HEAD_BLOCK)
        if dil == 1:
            o_ref[0, 0] = y.astype(o_ref.dtype)
        else:
            slab0 = 2 * (idx % 2)
            scr[slab0] = y[:, :LANES]
            scr[slab0 + 1] = y[:, LANES:]
            for r in range(dil):
                o_ref[0, r] = _strided_rows(scr, slab0, r, tm // dil, dil).astype(o_ref.dtype)
    col = n_dil * HEAD_BLOCK
    for o_ref in out_refs[n_dil:]:
        width = o_ref.shape[1]
        for c in range(0, width, 512):
            cw = min(512, width - c)
            o_ref[:, c:c + cw] = proj(col + c, cw).astype(o_ref.dtype)
        col += width


def _in_projection(xp, xs, ln_g, ln_b, w, b):
    n0, d_model = xp.shape
    n1 = xs.shape[0]
    n = n0 + n1
    tm = PROJ_TM
    n0_tiles, n1_tiles = n0 // tm, n1 // tm
    nt = n0_tiles + n1_tiles
    out_shape, out_specs = [], []
    for _ in range(3):
        for _, dil in DIL_PATTERNS:
            out_shape.append(jax.ShapeDtypeStruct((nt, dil, tm // dil, HEAD_BLOCK), jnp.bfloat16))
            out_specs.append(pl.BlockSpec((1, dil, tm // dil, HEAD_BLOCK), lambda i: (i, 0, 0, 0)))
    for wd in (QB, QB, QB, d_model, d_model):
        out_shape.append(jax.ShapeDtypeStruct((n, wd), jnp.bfloat16))
        out_specs.append(pl.BlockSpec((tm, wd), lambda i: (i, 0)))
    d_in = w.shape[1]
    kern = functools.partial(_inproj_kernel, n0_tiles=n0_tiles)
    return pl.pallas_call(
        kern,
        grid=(nt,),
        in_specs=[
            pl.BlockSpec((tm, d_model), lambda i: (jnp.minimum(i, n0_tiles - 1), 0)),
            pl.BlockSpec((tm, d_model), lambda i: (jnp.maximum(i - n0_tiles, 0), 0)),
            pl.BlockSpec((1, d_model), lambda i: (0, 0)),
            pl.BlockSpec((1, d_model), lambda i: (0, 0)),
            pl.BlockSpec((d_model, d_in), lambda i: (0, 0)),
            pl.BlockSpec((1, d_in), lambda i: (0, 0)),
        ],
        out_specs=out_specs,
        out_shape=out_shape,
        scratch_shapes=[pltpu.VMEM((4, tm, LANES), jnp.float32)],
        compiler_params=_cparams(("arbitrary",)),
        name="in_projection",
    )(xp, xs, ln_g, ln_b, w, b)


def _t5_bucket_np(rel):
    half = N_BUCKETS // 2
    max_exact = half // 2
    ret = np.where(rel > 0, half, 0)
    n = np.abs(rel)
    nf = np.maximum(n, 1).astype(np.float32)
    large = max_exact + (np.log(nf / np.float32(max_exact)) / np.float32(math.log(REL_MAX_DIST / max_exact))
                         * np.float32(half - max_exact)).astype(np.int32)
    large = np.minimum(large, half - 1)
    return ret + np.where(n < max_exact, n, large)


def _t5_bucket_starts():
    half = N_BUCKETS // 2
    dist = np.arange(0, HALF_SPAN * max(d for _, d in DIL_PATTERNS) + 1)
    buckets = _t5_bucket_np(-dist)
    return [int(np.argmax(buckets >= k)) for k in range(half)]


def _build_dilated_bias(table_ref, bias_scr, dil):
    half = N_BUCKETS // 2
    starts = _t5_bucket_starts()
    kb = DIL_M + 2 * HALF_SPAN
    i = lax.broadcasted_iota(jnp.int32, (DIL_M, kb), 0)
    m = lax.broadcasted_iota(jnp.int32, (DIL_M, kb), 1)
    delta = m - HALF_SPAN - i
    dist = jnp.abs(delta) * dil
    band = jnp.abs(delta) <= HALF_SPAN
    for h in range(HEADS_PER_DIL_GROUP):
        sides = []
        for side in range(2):
            val = jnp.full((DIL_M, kb), table_ref[(side * half + half - 1) * HEADS_PER_DIL_GROUP + h], jnp.float32)
            for k in range(half - 1, 0, -1):
                val = jnp.where(dist < starts[k], table_ref[(side * half + k - 1) * HEADS_PER_DIL_GROUP + h], val)
            sides.append(val)
        base = jnp.where(band, jnp.where(delta > 0, sides[1], sides[0]), NEG_INF)
        rows = slice(h * DIL_M, (h + 1) * DIL_M)
        bias_scr[0, rows, :] = base
        bias_scr[1, rows, :] = jnp.where(m >= HALF_SPAN, base, NEG_INF)
        bias_scr[2, rows, :] = jnp.where(m < kb - HALF_SPAN, base, NEG_INF)


def _stack_heads(q):
    head_of_col = lax.broadcasted_iota(jnp.int32, (1, HEAD_BLOCK), 1) // HEAD_DIM
    zero = jnp.zeros_like(q)
    return jnp.concatenate([jnp.where(head_of_col == h, q, zero) for h in range(4)], axis=0)


def _merge_heads(x, m):
    head_of_col = lax.broadcasted_iota(jnp.int32, (1, HEAD_BLOCK), 1) // HEAD_DIM
    out = jnp.zeros((m, HEAD_BLOCK), x.dtype)
    for h in range(4):
        out = jnp.where(head_of_col == h, x[h * m:(h + 1) * m], out)
    return out


def _softmax_pv(s, v, m_rows):
    mx = jnp.max(s, axis=-1, keepdims=True)
    p = jnp.exp(s - mx)
    l = jnp.sum(p, axis=-1, keepdims=True)
    pv = jnp.dot(p.astype(jnp.bfloat16), v, preferred_element_type=jnp.float32)
    o = _merge_heads(pv * (1.0 / l), m_rows)
    return o, mx + jnp.log(l)


def _dilated_kernel(table_ref, q_ref, kp_ref, kc_ref, kn_ref, vp_ref, vc_ref, vn_ref,
                    o_ref, lse_ref, bias_scr, *, dil, seg0_blocks, blocks_per_seq):
    step = pl.program_id(1)

    @pl.when((pl.program_id(0) == 0) & (step == 0))
    def _():
        _build_dilated_bias(table_ref, bias_scr, dil)

    rows2d = lambda ref: ref[...].reshape(-1, HEAD_BLOCK)
    k = jnp.concatenate([rows2d(r) for r in (kp_ref, kc_ref, kn_ref)], axis=0)
    v = jnp.concatenate([rows2d(r) for r in (vp_ref, vc_ref, vn_ref)], axis=0)
    q = rows2d(q_ref)
    o_parts, lse_parts = [], []
    for g in range(DIL_GROUP):
        jb = step * DIL_GROUP + g
        in0 = jb < seg0_blocks
        jl = jnp.where(in0, jb, jb - seg0_blocks)
        nbs = jnp.where(in0, blocks_per_seq[0], blocks_per_seq[1])
        pos = lax.rem(jl, nbs)
        variant = jnp.where(pos == 0, 1, jnp.where(pos == nbs - 1, 2, 0))
        keys = slice(g * DIL_M, (g + 1) * DIL_M + 2 * HALF_SPAN)
        qs = _stack_heads(q[g * DIL_M:(g + 1) * DIL_M])
        s = lax.dot_general(qs, k[keys], (((1,), (1,)), ((), ())), preferred_element_type=jnp.float32)
        s = s + bias_scr[variant]
        o, lse = _softmax_pv(s, v[keys], DIL_M)
        o_parts.append(o.astype(o_ref.dtype))
        lse_parts.append(_merge_heads(jnp.broadcast_to(lse, (4 * DIL_M, HEAD_BLOCK)), DIL_M))
    o_ref[...] = jnp.concatenate(o_parts, axis=0).reshape(o_ref.shape)
    lse_ref[...] = jnp.concatenate(lse_parts, axis=0).reshape(lse_ref.shape)


def _residue_spec(rows_blk, dil, block_of):
    per_tile = PROJ_TM // dil
    if per_tile >= rows_blk:
        per = per_tile // rows_blk
        return pl.BlockSpec((1, None, rows_blk, HEAD_BLOCK),
                            lambda r, jb: (block_of(jb) // per, r, block_of(jb) % per, 0))
    return pl.BlockSpec((rows_blk // per_tile, None, per_tile, HEAD_BLOCK), lambda r, jb: (block_of(jb), r, 0, 0))


def _dilated_group(q, k, v, table, dil, seg_tokens, seq_lens):
    n = q.shape[0] * PROJ_TM
    rows = n // dil
    step_rows = DIL_GROUP * DIL_M
    assert rows % step_rows == 0 and all(t % (dil * DIL_M) == 0 for t in seq_lens)
    nsteps = rows // step_rows
    nkb = rows // HALF_SPAN
    ratio = step_rows // HALF_SPAN
    seg0_blocks = seg_tokens[0] // dil // DIL_M
    blocks_per_seq = tuple(t // dil // DIL_M for t in seq_lens)
    assert min(blocks_per_seq) >= 2

    qo_spec = _residue_spec(step_rows, dil, lambda jb: jb)
    prev_spec = _residue_spec(HALF_SPAN, dil, lambda jb: jnp.maximum(ratio * jb - 1, 0))
    next_spec = _residue_spec(HALF_SPAN, dil, lambda jb: jnp.minimum(ratio * (jb + 1), nkb - 1))
    kv_specs = [prev_spec, qo_spec, next_spec]
    kern = functools.partial(_dilated_kernel, dil=dil, seg0_blocks=seg0_blocks, blocks_per_seq=blocks_per_seq)
    return pl.pallas_call(
        kern,
        grid=(dil, nsteps),
        in_specs=[pl.BlockSpec(memory_space=pltpu.SMEM), qo_spec] + kv_specs + kv_specs,
        out_specs=[qo_spec, qo_spec],
        out_shape=[jax.ShapeDtypeStruct(q.shape, jnp.bfloat16), jax.ShapeDtypeStruct(q.shape, jnp.float32)],
        scratch_shapes=[pltpu.VMEM((3, HEADS_PER_DIL_GROUP * DIL_M, DIL_M + 2 * HALF_SPAN), jnp.float32)],
        compiler_params=_cparams(("arbitrary", "arbitrary")),
        name=f"dilated_attention_d{dil}",
    )(table, q, k, k, k, v, v, v)


NA_DR = 2 * NA_KH - 1
NA_DC = 2 * NA_KW - 1
NA_PAIRS = NA_DR - 1


def _build_na_bias(rpb_ref, pair_scr, head0):
    qc = lax.broadcasted_iota(jnp.int32, (GRID_W, 2 * GRID_W), 0)
    lane = lax.broadcasted_iota(jnp.int32, (GRID_W, 2 * GRID_W), 1)
    kc = lane % GRID_W
    upper = lane >= GRID_W
    qstart = jnp.clip(qc - NA_KW // 2, 0, GRID_W - NA_KW)
    valid = (kc >= qstart) & (kc < qstart + NA_KW)
    dc = jnp.clip(kc - qc, -(NA_KW - 1), NA_KW - 1) + NA_KW - 1

    def one_pair(idx, carry):
        h = idx // NA_PAIRS
        dr = idx % NA_PAIRS
        base = ((head0 + h) * NA_DR + dr) * NA_DC
        val = jnp.zeros((GRID_W, 2 * GRID_W), jnp.float32)
        for c in range(NA_DC):
            val = jnp.where(dc == c, jnp.where(upper, rpb_ref[base + NA_DC + c], rpb_ref[base + c]), val)
        pair_scr[h, dr] = jnp.where(valid, val, NEG_INF)
        return carry

    lax.fori_loop(0, 4 * NA_PAIRS, one_pair, 0)


def _na_kernel(rpb_ref, q_ref, kp_ref, kc_ref, kn_ref, vp_ref, vc_ref, vn_ref, o_ref, k_scr, v_scr, pair_scr,
               *, seg0_blocks, blocks_per_seq):
    ib = pl.program_id(1)

    @pl.when(ib == 0)
    def _():
        _build_na_bias(rpb_ref, pair_scr, pl.program_id(0) * 4)

    in0 = ib < seg0_blocks
    il = jnp.where(in0, ib, ib - seg0_blocks)
    nbs = jnp.where(in0, blocks_per_seq[0], blocks_per_seq[1])
    pos = lax.rem(il, nbs)
    first = pos == 0
    last = pos == nbs - 1
    blk = NA_ROWS * GRID_W
    k_scr[0:blk] = kp_ref[...]
    k_scr[blk:2 * blk] = kc_ref[...]
    k_scr[2 * blk:3 * blk] = kn_ref[...]
    v_scr[0:blk] = vp_ref[...]
    v_scr[blk:2 * blk] = vc_ref[...]
    v_scr[2 * blk:3 * blk] = vn_ref[...]
    half = NA_KH // 2
    for rr in range(NA_ROWS):
        start = jnp.where(first, max(rr + half, NA_ROWS), jnp.where(last, min(rr + half, NA_ROWS), rr + half))
        var = jnp.where(first, min(rr, half), jnp.where(last, max(rr, half), half))
        off = pl.multiple_of(start * GRID_W, GRID_W)
        kk = k_scr[pl.ds(off, NA_KH * GRID_W), :]
        vv = v_scr[pl.ds(off, NA_KH * GRID_W), :]
        qs = _stack_heads(q_ref[rr * GRID_W:(rr + 1) * GRID_W, :])
        s = lax.dot_general(qs, kk, (((1,), (1,)), ((), ())), preferred_element_type=jnp.float32)
        bias = jnp.concatenate(
            [jnp.concatenate([pair_scr[h, 2 * p - var + NA_KH - 1] for p in range(NA_KH // 2)], axis=1)
             for h in range(4)], axis=0)
        s = s + bias
        o, _ = _softmax_pv(s, vv, GRID_W)
        o_ref[rr * GRID_W:(rr + 1) * GRID_W, :] = o.astype(o_ref.dtype)


def _neighborhood(qb, kb, vb, rpb_flat, seg_tokens, seq_lens):
    n = qb.shape[0]
    blk = NA_ROWS * GRID_W
    nblk = n // blk
    seg0_blocks = seg_tokens[0] // blk
    blocks_per_seq = tuple(t // blk for t in seq_lens)
    assert min(blocks_per_seq) >= 2
    nset = QB // HEAD_BLOCK

    def kv_spec(t):
        return pl.BlockSpec((blk, HEAD_BLOCK), lambda c, ib: (jnp.clip(ib - 1 + t, 0, nblk - 1), c))

    kern = functools.partial(_na_kernel, seg0_blocks=seg0_blocks, blocks_per_seq=blocks_per_seq)
    return pl.pallas_call(
        kern,
        grid=(nset, nblk),
        in_specs=[pl.BlockSpec(memory_space=pltpu.SMEM), pl.BlockSpec((blk, HEAD_BLOCK), lambda c, ib: (ib, c))]
        + [kv_spec(t) for t in range(3)] + [kv_spec(t) for t in range(3)],
        out_specs=pl.BlockSpec((blk, HEAD_BLOCK), lambda c, ib: (ib, c)),
        out_shape=jax.ShapeDtypeStruct((n, QB), jnp.bfloat16),
        scratch_shapes=[pltpu.VMEM((3 * blk, HEAD_BLOCK), jnp.bfloat16),
                        pltpu.VMEM((3 * blk, HEAD_BLOCK), jnp.bfloat16),
                        pltpu.VMEM((4, NA_PAIRS, GRID_W, 2 * GRID_W), jnp.float32)],
        compiler_params=_cparams(("arbitrary", "arbitrary")),
        name="neighborhood_attention",
    )(rpb_flat, qb, kb, kb, kb, vb, vb, vb)


def _post_kernel(xp_ref, xs_ref, lng_ref, lnb_ref, o0_ref, o1_ref, o2_ref, l0_ref, l1_ref, l2_ref, ob_ref,
                 ga_ref, gb_ref, wa_ref, wb_ref, wo_ref, bo_ref, g1_ref, b1_ref, wr_ref, br_ref, tri_ref,
                 x1_ref, route_ref, route_t_ref, counts_ref, carry_ref, perm_scr, *, n0_tiles):
    i = pl.program_id(0)
    tm = route_ref.shape[0]
    hm = tm // POST_SPLIT

    @pl.when(i == 0)
    def _():
        carry_ref[...] = jnp.zeros_like(carry_ref)

    def token_order(ref, slab0):
        dil, per = ref.shape[1], ref.shape[2]
        if dil == 1:
            return lambda rs: ref[0, 0, rs, :].astype(jnp.float32)
        for r in range(dil):
            val = ref[0, r].astype(jnp.float32)
            for s in range(2):
                perm_scr[slab0 + s, pl.ds(r, per, stride=dil), :] = val[:, s * LANES:(s + 1) * LANES]
        return lambda rs: jnp.concatenate([perm_scr[slab0, rs, :], perm_scr[slab0 + 1, rs, :]], axis=1)

    lse_of = [token_order(ref, 2 * j) for j, ref in enumerate((l0_ref, l1_ref, l2_ref))]
    o_of = [token_order(ref, 6 + 2 * j) for j, ref in enumerate((o0_ref, o1_ref, o2_ref))]
    sigmoid = lambda v: 0.5 * jnp.tanh(0.5 * v) + 0.5
    grow = lax.broadcasted_iota(jnp.int32, (8, hm), 0)
    erow = lax.broadcasted_iota(jnp.int32, (N_EXPERTS, hm), 0)
    frow = lax.broadcasted_iota(jnp.int32, (ROUTE_LANES, hm), 0)
    neg = jnp.float32(-jnp.inf)
    carry = carry_ref[:, 0:1]

    for h in range(POST_SPLIT):
        rs = slice(h * hm, (h + 1) * hm)
        l0, l1, l2 = (f(rs) for f in lse_of)
        lm = jnp.maximum(jnp.maximum(l0, l1), l2)
        e0, e1, e2 = jnp.exp(l0 - lm), jnp.exp(l1 - lm), jnp.exp(l2 - lm)
        inv = 1.0 / (e0 + e1 + e2)
        o_a = (e0 * inv) * o_of[0](rs) + (e1 * inv) * o_of[1](rs) + (e2 * inv) * o_of[2](rs)
        y_a = jnp.dot(o_a.astype(jnp.bfloat16), wa_ref[...], preferred_element_type=jnp.float32)
        y_b = jnp.dot(ob_ref[rs, :], wb_ref[...], preferred_element_type=jnp.float32)
        mix = sigmoid(ga_ref[rs, :].astype(jnp.float32)) * y_a + sigmoid(gb_ref[rs, :].astype(jnp.float32)) * y_b
        out = jnp.dot(mix.astype(jnp.bfloat16), wo_ref[...], preferred_element_type=jnp.float32) + bo_ref[...]
        x = jnp.where(i < n0_tiles, xp_ref[rs, :], xs_ref[rs, :])
        x0 = _layer_norm(x, lng_ref[...], lnb_ref[...])
        x1 = _layer_norm(DN_ALPHA * x0 + out, g1_ref[...], b1_ref[...])
        _store_row_tiles(x1_ref, x1, row0=h * hm)

        r = jnp.dot(x1.astype(jnp.bfloat16), wr_ref[...], preferred_element_type=jnp.float32) + br_ref[...]
        rt = r.T
        lg = jnp.where(grow < N_GROUPS, rt[0:8], neg)
        gmax = jnp.max(lg, axis=0, keepdims=True)
        gi = jnp.min(jnp.where(lg == gmax, grow, 8), axis=0, keepdims=True)
        wg = 1.0 / jnp.sum(jnp.exp(lg - gmax), axis=0, keepdims=True)
        lo = EXPERTS_PER_GROUP * gi
        le = jnp.where((erow >= lo) & (erow < lo + EXPERTS_PER_GROUP), rt[EXPERT_LANE0:EXPERT_LANE0 + N_EXPERTS], neg)
        m1 = jnp.max(le, axis=0, keepdims=True)
        i1 = jnp.min(jnp.where(le == m1, erow, N_EXPERTS), axis=0, keepdims=True)
        le2 = jnp.where(erow == i1, neg, le)
        m2 = jnp.max(le2, axis=0, keepdims=True)
        i2 = jnp.min(jnp.where(le2 == m2, erow, N_EXPERTS), axis=0, keepdims=True)
        t2 = jnp.exp(m2 - m1)
        w1 = wg / (1.0 + t2)
        w2 = wg * t2 / (1.0 + t2)

        hot1 = erow == i1
        hot2 = erow == i2
        hot = (hot1 | hot2).astype(jnp.float32)
        before = jnp.dot(hot.astype(jnp.bfloat16), tri_ref[...], preferred_element_type=jnp.float32) + carry
        rank1 = jnp.sum(jnp.where(hot1, before, 0.0), axis=0, keepdims=True)
        rank2 = jnp.sum(jnp.where(hot2, before, 0.0), axis=0, keepdims=True)
        carry = carry + jnp.sum(hot, axis=1, keepdims=True)

        fields = (i1.astype(jnp.float32), i2.astype(jnp.float32), w1, w2, rank1, rank2)
        route_t = jnp.zeros((ROUTE_LANES, hm), jnp.float32)
        for idx, val in enumerate(fields):
            route_t = jnp.where(frow == idx, val, route_t)
        route_t_ref[:, rs] = route_t[0:route_t_ref.shape[0]]
        route_ref[rs, :] = route_t.T

    carry_ref[...] = jnp.broadcast_to(carry, carry_ref.shape)
    counts_ref[...] = jnp.broadcast_to(carry, counts_ref.shape)


def _post_attention(xp, xs, ln_in_g, ln_in_b, o_groups, lse_groups, ob, ga, gb, wa, wb, wo, bo, g1, b1, wr, br):
    n0, d = xp.shape
    n = n0 + xs.shape[0]
    tm = POST_TM
    n0_tiles = n0 // tm
    nt = n // tm
    hm = tm // POST_SPLIT
    tri = jnp.asarray(np.triu(np.ones((hm, hm), np.float32), 1), jnp.bfloat16)

    def tok(width):
        return pl.BlockSpec((tm, width), lambda i: (i, 0))

    def full(a):
        return pl.BlockSpec(a.shape, lambda i: (0,) * a.ndim)

    kern = functools.partial(_post_kernel, n0_tiles=n0_tiles)
    return pl.pallas_call(
        kern,
        grid=(nt,),
        in_specs=[
            pl.BlockSpec((tm, d), lambda i: (jnp.minimum(i, n0_tiles - 1), 0)),
            pl.BlockSpec((tm, d), lambda i: (jnp.maximum(i - n0_tiles, 0), 0)),
            full(ln_in_g), full(ln_in_b),
            *[pl.BlockSpec((1,) + a.shape[1:], lambda i: (i, 0, 0, 0)) for a in (*o_groups, *lse_groups)],
            tok(QB), tok(d), tok(d),
            full(wa), full(wb), full(wo), full(bo), full(g1), full(b1), full(wr), full(br), full(tri),
        ],
        out_specs=[pl.BlockSpec((tm * SUBLANES, LANES), lambda i: (i, 0)), tok(ROUTE_LANES),
                   pl.BlockSpec((8, tm), lambda i: (0, i)), pl.BlockSpec((N_EXPERTS, ROUTE_LANES), lambda i: (0, 0))],
        out_shape=[jax.ShapeDtypeStruct((n * SUBLANES, LANES), jnp.float32),
                   jax.ShapeDtypeStruct((n, ROUTE_LANES), jnp.float32),
                   jax.ShapeDtypeStruct((8, n), jnp.float32),
                   jax.ShapeDtypeStruct((N_EXPERTS, ROUTE_LANES), jnp.float32)],
        scratch_shapes=[pltpu.VMEM((N_EXPERTS, ROUTE_LANES), jnp.float32), pltpu.VMEM((12, tm, LANES), jnp.float32)],
        compiler_params=_cparams(("arbitrary",)),
        name="post_attention_router",
    )(xp, xs, ln_in_g, ln_in_b, *o_groups, *lse_groups, ob, ga, gb, wa, wb, wo, bo, g1, b1, wr, br, tri)


def _dispatch_kernel(pos_ref, x_ref, xs_ref, sem):
    tt = x_ref.shape[0] // SUBLANES

    def issue(j, carry):
        for u in range(2):
            t = 2 * j + u
            src = x_ref.at[pl.ds(pl.multiple_of(t * SUBLANES, SUBLANES), SUBLANES)]
            for slot in range(2):
                p = pos_ref[slot, t]
                dst = xs_ref.at[pl.ds(pl.multiple_of(p * SUBLANES, SUBLANES), SUBLANES)]
                pltpu.make_async_copy(src, dst, sem).start(priority=u)
        return carry

    lax.fori_loop(0, tt // 2, issue, 0, unroll=4)
    for _ in range(2):
        pltpu.make_async_copy(x_ref, xs_ref.at[pl.ds(0, tt * SUBLANES)], sem).wait()


def _dispatch(x1t, pos, n_rows):
    n = x1t.shape[0] // SUBLANES
    tt = DISP_TT
    return pl.pallas_call(
        _dispatch_kernel,
        grid=(n // tt,),
        in_specs=[pl.BlockSpec((2, tt), lambda i: (0, i), memory_space=pltpu.SMEM),
                  pl.BlockSpec((tt * SUBLANES, LANES), lambda i: (i, 0))],
        out_specs=pl.BlockSpec(memory_space=pl.ANY),
        out_shape=jax.ShapeDtypeStruct((n_rows * SUBLANES, LANES), x1t.dtype),
        scratch_shapes=[pltpu.SemaphoreType.DMA(())],
        compiler_params=_cparams(("arbitrary",)),
        name="moe_dispatch",
    )(pos, x1t)


def _ffn_kernel(item_expert_ref, item_tile_ref, item_lo_ref, item_hi_ref, n_items_ref,
                x_ref, wg_ref, wu_ref, wd_ref, y_ref, wg_bf, wu_bf, wd_bf):
    j = pl.program_id(0)

    @pl.when(j < n_items_ref[0])
    def _():
        prev = jnp.maximum(j - 1, 0)

        @pl.when(jnp.logical_or(j == 0, item_expert_ref[j] != item_expert_ref[prev]))
        def _():
            wg_bf[...] = wg_ref[0].astype(jnp.bfloat16)
            wu_bf[...] = wu_ref[0].astype(jnp.bfloat16)
            wd_bf[...] = wd_ref[0].astype(jnp.bfloat16)

        tm = x_ref.shape[0] // SUBLANES
        x = _load_row_tiles(x_ref, tm).astype(jnp.bfloat16)
        a = jnp.dot(x, wg_bf[...], preferred_element_type=jnp.float32)
        u = jnp.dot(x, wu_bf[...], preferred_element_type=jnp.float32)
        h = (a * jax.nn.sigmoid(a) * u).astype(jnp.bfloat16)
        y = jnp.dot(h, wd_bf[...], preferred_element_type=jnp.float32)
        rows = lax.broadcasted_iota(jnp.int32, (tm, 1), 0)
        y = jnp.where((rows >= item_lo_ref[j]) & (rows < item_hi_ref[j]), y, 0.0)
        first_of_tile = jnp.logical_or(j == 0, item_tile_ref[j] != item_tile_ref[prev])

        @pl.when(first_of_tile)
        def _():
            _store_row_tiles(y_ref, y)

        @pl.when(jnp.logical_not(first_of_tile))
        def _():
            _store_row_tiles(y_ref, y, accumulate=True)


def _ffn_items(cnt, n_rows):
    tm = FFN_TM
    n_tiles = n_rows // tm
    max_items = n_tiles + N_EXPERTS - 1
    ends = jnp.cumsum(cnt)
    starts = ends - cnt
    first_tile = starts // tm
    last_tile = jnp.maximum(ends - 1, 0) // tm
    items_per_expert = jnp.where(cnt > 0, last_tile - first_tile + 1, 0)
    item_ends = jnp.cumsum(items_per_expert)
    n_items = item_ends[-1:]
    j = jnp.minimum(jnp.arange(max_items, dtype=jnp.int32), n_items[0] - 1)
    expert = jnp.minimum(jnp.sum(item_ends[None, :] <= j[:, None], axis=1), N_EXPERTS - 1).astype(jnp.int32)
    tile = first_tile[expert] + (j - (item_ends - items_per_expert)[expert])
    lo = jnp.clip(starts[expert] - tile * tm, 0, tm)
    hi = jnp.clip(ends[expert] - tile * tm, 0, tm)
    i32 = lambda a: a.astype(jnp.int32)
    return i32(expert), i32(tile), i32(lo), i32(hi), i32(n_items), starts


def _grouped_ffn(xs, items, w_gate, w_up, w_down):
    tm = FFN_TM
    d, de = w_gate.shape[1:]
    item_expert, item_tile, item_lo, item_hi, n_items = items
    max_items = item_expert.shape[0]

    def row_map(j, ie, it, lo, hi, ni):
        return (it[j], 0)

    def w_map(j, ie, it, lo, hi, ni):
        return (ie[j], 0, 0)

    grid_spec = pltpu.PrefetchScalarGridSpec(
        num_scalar_prefetch=5,
        grid=(max_items,),
        in_specs=[pl.BlockSpec((tm * SUBLANES, LANES), row_map),
                  pl.BlockSpec((1, d, de), w_map),
                  pl.BlockSpec((1, d, de), w_map),
                  pl.BlockSpec((1, de, d), w_map)],
        out_specs=pl.BlockSpec((tm * SUBLANES, LANES), row_map),
        scratch_shapes=[pltpu.VMEM((d, de), jnp.bfloat16), pltpu.VMEM((d, de), jnp.bfloat16),
                        pltpu.VMEM((de, d), jnp.bfloat16)],
    )
    return pl.pallas_call(
        _ffn_kernel,
        grid_spec=grid_spec,
        out_shape=jax.ShapeDtypeStruct(xs.shape, jnp.float32),
        compiler_params=_cparams(("arbitrary",)),
        name="moe_grouped_ffn",
    )(item_expert, item_tile, item_lo, item_hi, n_items, xs, w_gate, w_up, w_down)


def _combine_kernel(pos_ref, pos_next_ref, x1_ref, route_ref, g2_ref, b2_ref, ys_ref, o_ref, buf, sems):
    tt = x1_ref.shape[0] // SUBLANES
    i = pl.program_id(0)
    cur = lax.rem(i, 2)

    def issue_tile(p_ref, b):
        def issue(j, carry):
            for u in range(2):
                t = 2 * j + u
                for slot in range(2):
                    p = p_ref[slot, t]
                    src = ys_ref.at[pl.ds(pl.multiple_of(p * SUBLANES, SUBLANES), SUBLANES)]
                    dst = buf.at[b, slot, pl.ds(pl.multiple_of(t * SUBLANES, SUBLANES), SUBLANES)]
                    pltpu.make_async_copy(src, dst, sems.at[b]).start(priority=u)
            return carry

        lax.fori_loop(0, tt // 2, issue, 0, unroll=4)

    @pl.when(i == 0)
    def _():
        issue_tile(pos_ref, 0)

    @pl.when(i + 1 < pl.num_programs(0))
    def _():
        issue_tile(pos_next_ref, 1 - cur)

    for slot in range(2):
        pltpu.make_async_copy(ys_ref.at[pl.ds(0, tt * SUBLANES)], buf.at[cur, slot], sems.at[cur]).wait()
    route = route_ref[...]
    y = route[:, 2:3] * _load_row_tiles(buf.at[cur, 0], tt) + route[:, 3:4] * _load_row_tiles(buf.at[cur, 1], tt)
    o_ref[...] = _layer_norm(DN_ALPHA * _load_row_tiles(x1_ref, tt) + y, g2_ref[...], b2_ref[...])


def _combine(x1t, route, pos, ys, g2, b2, tok0, n_tok):
    d = g2.shape[1]
    tt = COMB_TT
    t0 = tok0 // tt
    nt = n_tok // tt
    return pl.pallas_call(
        _combine_kernel,
        grid=(nt,),
        in_specs=[pl.BlockSpec((2, tt), lambda i: (0, i + t0), memory_space=pltpu.SMEM),
                  pl.BlockSpec((2, tt), lambda i: (0, jnp.minimum(i + 1, nt - 1) + t0), memory_space=pltpu.SMEM),
                  pl.BlockSpec((tt * SUBLANES, LANES), lambda i: (i + t0, 0)),
                  pl.BlockSpec((tt, ROUTE_LANES), lambda i: (i + t0, 0)),
                  pl.BlockSpec((1, d), lambda i: (0, 0)),
                  pl.BlockSpec((1, d), lambda i: (0, 0)),
                  pl.BlockSpec(memory_space=pl.ANY)],
        out_specs=pl.BlockSpec((tt, d), lambda i: (i, 0)),
        out_shape=jax.ShapeDtypeStruct((n_tok, d), jnp.float32),
        scratch_shapes=[pltpu.VMEM((2, 2, tt * SUBLANES, LANES), ys.dtype), pltpu.SemaphoreType.DMA((2,))],
        compiler_params=_cparams(("arbitrary",)),
        name="moe_combine_ln",
    )(pos, pos, x1t, route, g2, b2, ys)


def _layer(x_prompt, x_sample, ln_in_g, ln_in_b, w_in, b_in, rel_bias_t5, na_rpb, w_branch_a, w_branch_b,
           w_out, b_out, ln1_g, ln1_b, w_router_group, b_router_group, w_router_expert, b_router_expert,
           w_exp_gate, w_exp_up, w_exp_down, ln2_g, ln2_b):
    bp, tp, d = x_prompt.shape
    bs, ts, _ = x_sample.shape
    seg_tokens = (bp * tp, bs * ts)
    seq_lens = (tp, ts)
    n = sum(seg_tokens)
    xp = x_prompt.reshape(seg_tokens[0], d)
    xs = x_sample.reshape(seg_tokens[1], d)
    row = lambda a: a.reshape(1, -1).astype(jnp.float32)
    bf = lambda a: a.astype(jnp.bfloat16)

    scale = HEAD_DIM ** -0.5
    col_scale = np.ones((w_in.shape[2],), np.float32)
    col_scale[0:QA] = scale
    col_scale[3 * QA:3 * QA + QB] = scale
    w_proj = bf(w_in[0] * col_scale)
    b_proj = row(b_in[0] * col_scale)
    proj = _in_projection(xp, xs, row(ln_in_g), row(ln_in_b), w_proj, b_proj)
    qkv_a, (qb, kb, vb, ga, gb) = proj[:3 * N_DIL_GROUPS], proj[3 * N_DIL_GROUPS:]

    o_groups, lse_groups = [], []
    for g, (window, dil) in enumerate(DIL_PATTERNS):
        assert window // (2 * dil) == HALF_SPAN
        table = rel_bias_t5[:, g * HEADS_PER_DIL_GROUP:(g + 1) * HEADS_PER_DIL_GROUP].astype(jnp.float32).reshape(-1)
        o, lse = _dilated_group(qkv_a[g], qkv_a[N_DIL_GROUPS + g], qkv_a[2 * N_DIL_GROUPS + g], table, dil,
                                seg_tokens, seq_lens)
        o_groups.append(o)
        lse_groups.append(lse)
    ob = _neighborhood(qb, kb, vb, na_rpb[0].astype(jnp.float32).reshape(-1), seg_tokens, seq_lens)

    wr = jnp.zeros((d, ROUTE_LANES), jnp.float32)
    wr = wr.at[:, 0:N_GROUPS].set(w_router_group[0]).at[:, EXPERT_LANE0:EXPERT_LANE0 + N_EXPERTS].set(w_router_expert[0])
    br = jnp.zeros((1, ROUTE_LANES), jnp.float32)
    br = br.at[0, 0:N_GROUPS].set(b_router_group[0]).at[0, EXPERT_LANE0:EXPERT_LANE0 + N_EXPERTS].set(b_router_expert[0])
    x1, route, route_t, counts = _post_attention(
        xp, xs, row(ln_in_g), row(ln_in_b), o_groups, lse_groups, ob, ga, gb, bf(w_branch_a[0]), bf(w_branch_b[0]),
        bf(w_out[0]), row(b_out[0]), row(ln1_g[0]), row(ln1_b[0]), bf(wr), br)

    experts = route_t[0:2].astype(jnp.int32)
    ranks = route_t[4:6].astype(jnp.int32)
    cnt = counts[:, 0].astype(jnp.int32)
    n_rows = 2 * n
    *items, starts = _ffn_items(cnt, n_rows)
    pos = ranks
    for e in range(N_EXPERTS):
        pos = pos + jnp.where(experts == e, starts[e], 0)

    xsorted = _dispatch(x1, pos, n_rows)
    ys = _grouped_ffn(xsorted, items, w_exp_gate[0], w_exp_up[0], w_exp_down[0])
    y_prompt = _combine(x1, route, pos, ys, row(ln2_g[0]), row(ln2_b[0]), 0, seg_tokens[0])
    y_sample = _combine(x1, route, pos, ys, row(ln2_g[0]), row(ln2_b[0]), seg_tokens[0], seg_tokens[1])
    return y_prompt.reshape(bp, tp, d), y_sample.reshape(bs, ts, d)


def kernel(x_prompt, x_sample, ln_in_g, ln_in_b, w_in, b_in, rel_bias_t5, na_rpb, w_branch_a, w_branch_b, w_out, b_out, ln1_g, ln1_b, w_router_group, b_router_group, w_router_expert, b_router_expert, w_exp_gate, w_exp_up, w_exp_down, ln2_g, ln2_b):
    return _layer(x_prompt, x_sample, ln_in_g, ln_in_b, w_in, b_in, rel_bias_t5, na_rpb, w_branch_a, w_branch_b,
                  w_out, b_out, ln1_g, ln1_b, w_router_group, b_router_group, w_router_expert, b_router_expert,
                  w_exp_gate, w_exp_up, w_exp_down, ln2_g, ln2_b)
```

```python
import functools
import math

import numpy as np
import jax
import jax.numpy as jnp
from jax import lax
from jax.experimental import pallas as pl
from jax.experimental.pallas import tpu as pltpu

HEAD_DIM = 64
DIL_PATTERNS = ((128, 1), (512, 4), (2048, 16))
HEADS_PER_DIL_GROUP = 4
N_DIL_GROUPS = 3
H_A = N_DIL_GROUPS * HEADS_PER_DIL_GROUP
H_B = 8
QA = H_A * HEAD_DIM
QB = H_B * HEAD_DIM
HALF_SPAN = 64
GRID_W = 64
NA_KH = 8
NA_KW = 16
N_BUCKETS = 32
REL_MAX_DIST = 1024
N_GROUPS = 4
EXPERTS_PER_GROUP = 8
N_EXPERTS = N_GROUPS * EXPERTS_PER_GROUP
LN_EPS = 1e-5
NEG_INF = -1e30
DEPTH = 1
DN_ALPHA = (2.0 * DEPTH) ** 0.25

LANES = 128
HEAD_BLOCK = 4 * HEAD_DIM
VMEM_LIMIT_BYTES = 56 * 1024 * 1024

PROJ_TM = 512
DIL_M = 128
DIL_GROUP = 8
NA_ROWS = 16
POST_TM = 512
POST_SPLIT = 1
FFN_TM = 512
DISP_TT = 2048
COMB_TT = 512
ROUTE_LANES = LANES
EXPERT_LANE0 = 32


def _cparams(sem):
    return pltpu.CompilerParams(dimension_semantics=sem, vmem_limit_bytes=VMEM_LIMIT_BYTES)


SUBLANES = 8


def _store_row_tiles(ref, x, accumulate=False, row0=0):
    rows = x.shape[0]
    for c in range(SUBLANES):
        idx = pl.ds(row0 * SUBLANES + c, rows, stride=SUBLANES)
        piece = x[:, c * LANES:(c + 1) * LANES]
        ref[idx, :] = ref[idx, :] + piece if accumulate else piece


def _load_row_tiles(ref, rows, row0=0):
    return jnp.concatenate([ref[pl.ds(row0 * SUBLANES + c, rows, stride=SUBLANES), :] for c in range(SUBLANES)],
                           axis=1)


def _layer_norm(x, g, b):
    mu = jnp.mean(x, axis=-1, keepdims=True)
    xc = x - mu
    var = jnp.mean(xc * xc, axis=-1, keepdims=True)
    return xc * lax.rsqrt(var + LN_EPS) * g + b


def _strided_rows(scr, slab0, r, count, stride):
    return jnp.concatenate([scr[slab0 + s, pl.ds(r, count, stride=stride), :] for s in range(2)], axis=1)


def _inproj_kernel(xp_ref, xs_ref, g_ref, b_ref, w_ref, bias_ref, *refs, n0_tiles):
    out_refs, scr = refs[:-1], refs[-1]
    i = pl.program_id(0)
    x = jnp.where(i < n0_tiles, xp_ref[...], xs_ref[...])
    xn = _layer_norm(x, g_ref[...], b_ref[...]).astype(jnp.bfloat16)
    tm = xn.shape[0]

    def proj(c0, cw):
        return jnp.dot(xn, w_ref[:, c0:c0 + cw], preferred_element_type=jnp.float32) + bias_ref[:, c0:c0 + cw]

    n_dil = 3 * N_DIL_GROUPS
    for idx in range(n_dil):
        g = idx % N_DIL_GROUPS
        dil = DIL_PATTERNS[g][1]
        o_ref = out_refs[idx]
        y = proj(idx * HEAD_BLOCK, HEAD_BLOCK)
        if dil == 1:
            o_ref[0, 0] = y.astype(o_ref.dtype)
        else:
            slab0 = 2 * (idx % 2)
            scr[slab0] = y[:, :LANES]
            scr[slab0 + 1] = y[:, LANES:]
            for r in range(dil):
                o_ref[0, r] = _strided_rows(scr, slab0, r, tm // dil, dil).astype(o_ref.dtype)
    col = n_dil * HEAD_BLOCK
    for o_ref in out_refs[n_dil:]:
        width = o_ref.shape[1]
        for c in range(0, width, 512):
            cw = min(512, width - c)
            o_ref[:, c:c + cw] = proj(col + c, cw).astype(o_ref.dtype)
        col += width


def _in_projection(xp, xs, ln_g, ln_b, w, b):
    n0, d_model = xp.shape
    n1 = xs.shape[0]
    n = n0 + n1
    tm = PROJ_TM
    n0_tiles, n1_tiles = n0 // tm, n1 // tm
    nt = n0_tiles + n1_tiles
    out_shape, out_specs = [], []
    for _ in range(3):
        for _, dil in DIL_PATTERNS:
            out_shape.append(jax.ShapeDtypeStruct((nt, dil, tm // dil, HEAD_BLOCK), jnp.bfloat16))
            out_specs.append(pl.BlockSpec((1, dil, tm // dil, HEAD_BLOCK), lambda i: (i, 0, 0, 0)))
    for wd in (QB, QB, QB, d_model, d_model):
        out_shape.append(jax.ShapeDtypeStruct((n, wd), jnp.bfloat16))
        out_specs.append(pl.BlockSpec((tm, wd), lambda i: (i, 0)))
    d_in = w.shape[1]
    kern = functools.partial(_inproj_kernel, n0_tiles=n0_tiles)
    return pl.pallas_call(
        kern,
        grid=(nt,),
        in_specs=[
            pl.BlockSpec((tm, d_model), lambda i: (jnp.minimum(i, n0_tiles - 1), 0)),
            pl.BlockSpec((tm, d_model), lambda i: (jnp.maximum(i - n0_tiles, 0), 0)),
            pl.BlockSpec((1, d_model), lambda i: (0, 0)),
            pl.BlockSpec((1, d_model), lambda i: (0, 0)),
            pl.BlockSpec((d_model, d_in), lambda i: (0, 0)),
            pl.BlockSpec((1, d_in), lambda i: (0, 0)),
        ],
        out_specs=out_specs,
        out_shape=out_shape,
        scratch_shapes=[pltpu.VMEM((4, tm, LANES), jnp.float32)],
        compiler_params=_cparams(("arbitrary",)),
        name="in_projection",
    )(xp, xs, ln_g, ln_b, w, b)


def _t5_bucket_np(rel):
    half = N_BUCKETS // 2
    max_exact = half // 2
    ret = np.where(rel > 0, half, 0)
    n = np.abs(rel)
    nf = np.maximum(n, 1).astype(np.float32)
    large = max_exact + (np.log(nf / np.float32(max_exact)) / np.float32(math.log(REL_MAX_DIST / max_exact))
                         * np.float32(half - max_exact)).astype(np.int32)
    large = np.minimum(large, half - 1)
    return ret + np.where(n < max_exact, n, large)


def _t5_bucket_starts():
    half = N_BUCKETS // 2
    dist = np.arange(0, HALF_SPAN * max(d for _, d in DIL_PATTERNS) + 1)
    buckets = _t5_bucket_np(-dist)
    return [int(np.argmax(buckets >= k)) for k in range(half)]


def _build_dilated_bias(table_ref, bias_scr, dil):
    half = N_BUCKETS // 2
    starts = _t5_bucket_starts()
    kb = DIL_M + 2 * HALF_SPAN
    i = lax.broadcasted_iota(jnp.int32, (DIL_M, kb), 0)
    m = lax.broadcasted_iota(jnp.int32, (DIL_M, kb), 1)
    delta = m - HALF_SPAN - i
    dist = jnp.abs(delta) * dil
    band = jnp.abs(delta) <= HALF_SPAN
    for h in range(HEADS_PER_DIL_GROUP):
        sides = []
        for side in range(2):
            val = jnp.full((DIL_M, kb), table_ref[(side * half + half - 1) * HEADS_PER_DIL_GROUP + h], jnp.float32)
            for k in range(half - 1, 0, -1):
                val = jnp.where(dist < starts[k], table_ref[(side * half + k - 1) * HEADS_PER_DIL_GROUP + h], val)
            sides.append(val)
        base = jnp.where(band, jnp.where(delta > 0, sides[1], sides[0]), NEG_INF)
        rows = slice(h * DIL_M, (h + 1) * DIL_M)
        bias_scr[0, rows, :] = base
        bias_scr[1, rows, :] = jnp.where(m >= HALF_SPAN, base, NEG_INF)
        bias_scr[2, rows, :] = jnp.where(m < kb - HALF_SPAN, base, NEG_INF)


def _stack_heads(q):
    head_of_col = lax.broadcasted_iota(jnp.int32, (1, HEAD_BLOCK), 1) // HEAD_DIM
    zero = jnp.zeros_like(q)
    return jnp.concatenate([jnp.where(head_of_col == h, q, zero) for h in range(4)], axis=0)


def _merge_heads(x, m):
    head_of_col = lax.broadcasted_iota(jnp.int32, (1, HEAD_BLOCK), 1) // HEAD_DIM
    out = jnp.zeros((m, HEAD_BLOCK), x.dtype)
    for h in range(4):
        out = jnp.where(head_of_col == h, x[h * m:(h + 1) * m], out)
    return out


def _softmax_pv(s, v, m_rows):
    mx = jnp.max(s, axis=-1, keepdims=True)
    p = jnp.exp(s - mx)
    l = jnp.sum(p, axis=-1, keepdims=True)
    pv = jnp.dot(p.astype(jnp.bfloat16), v, preferred_element_type=jnp.float32)
    o = _merge_heads(pv * (1.0 / l), m_rows)
    return o, mx + jnp.log(l)


def _dilated_kernel(table_ref, q_ref, kp_ref, kc_ref, kn_ref, vp_ref, vc_ref, vn_ref,
                    o_ref, lse_ref, bias_scr, *, dil, seg0_blocks, blocks_per_seq):
    step = pl.program_id(1)

    @pl.when((pl.program_id(0) == 0) & (step == 0))
    def _():
        _build_dilated_bias(table_ref, bias_scr, dil)

    rows2d = lambda ref: ref[...].reshape(-1, HEAD_BLOCK)
    k = jnp.concatenate([rows2d(r) for r in (kp_ref, kc_ref, kn_ref)], axis=0)
    v = jnp.concatenate([rows2d(r) for r in (vp_ref, vc_ref, vn_ref)], axis=0)
    q = rows2d(q_ref)
    o_parts, lse_parts = [], []
    for g in range(DIL_GROUP):
        jb = step * DIL_GROUP + g
        in0 = jb < seg0_blocks
        jl = jnp.where(in0, jb, jb - seg0_blocks)
        nbs = jnp.where(in0, blocks_per_seq[0], blocks_per_seq[1])
        pos = lax.rem(jl, nbs)
        variant = jnp.where(pos == 0, 1, jnp.where(pos == nbs - 1, 2, 0))
        keys = slice(g * DIL_M, (g + 1) * DIL_M + 2 * HALF_SPAN)
        qs = _stack_heads(q[g * DIL_M:(g + 1) * DIL_M])
        s = lax.dot_general(qs, k[keys], (((1,), (1,)), ((), ())), preferred_element_type=jnp.float32)
        s = s + bias_scr[variant]
        o, lse = _softmax_pv(s, v[keys], DIL_M)
        o_parts.append(o.astype(o_ref.dtype))
        lse_parts.append(_merge_heads(jnp.broadcast_to(lse, (4 * DIL_M, HEAD_BLOCK)), DIL_M))
    o_ref[...] = jnp.concatenate(o_parts, axis=0).reshape(o_ref.shape)
    lse_ref[...] = jnp.concatenate(lse_parts, axis=0).reshape(lse_ref.shape)


def _residue_spec(rows_blk, dil, block_of):
    per_tile = PROJ_TM // dil
    if per_tile >= rows_blk:
        per = per_tile // rows_blk
        return pl.BlockSpec((1, None, rows_blk, HEAD_BLOCK),
                            lambda r, jb: (block_of(jb) // per, r, block_of(jb) % per, 0))
    return pl.BlockSpec((rows_blk // per_tile, None, per_tile, HEAD_BLOCK), lambda r, jb: (block_of(jb), r, 0, 0))


def _dilated_group(q, k, v, table, dil, seg_tokens, seq_lens):
    n = q.shape[0] * PROJ_TM
    rows = n // dil
    step_rows = DIL_GROUP * DIL_M
    assert rows % step_rows == 0 and all(t % (dil * DIL_M) == 0 for t in seq_lens)
    nsteps = rows // step_rows
    nkb = rows // HALF_SPAN
    ratio = step_rows // HALF_SPAN
    seg0_blocks = seg_tokens[0] // dil // DIL_M
    blocks_per_seq = tuple(t // dil // DIL_M for t in seq_lens)
    assert min(blocks_per_seq) >= 2

    qo_spec = _residue_spec(step_rows, dil, lambda jb: jb)
    prev_spec = _residue_spec(HALF_SPAN, dil, lambda jb: jnp.maximum(ratio * jb - 1, 0))
    next_spec = _residue_spec(HALF_SPAN, dil, lambda jb: jnp.minimum(ratio * (jb + 1), nkb - 1))
    kv_specs = [prev_spec, qo_spec, next_spec]
    kern = functools.partial(_dilated_kernel, dil=dil, seg0_blocks=seg0_blocks, blocks_per_seq=blocks_per_seq)
    return pl.pallas_call(
        kern,
        grid=(dil, nsteps),
        in_specs=[pl.BlockSpec(memory_space=pltpu.SMEM), qo_spec] + kv_specs + kv_specs,
        out_specs=[qo_spec, qo_spec],
        out_shape=[jax.ShapeDtypeStruct(q.shape, jnp.bfloat16), jax.ShapeDtypeStruct(q.shape, jnp.float32)],
        scratch_shapes=[pltpu.VMEM((3, HEADS_PER_DIL_GROUP * DIL_M, DIL_M + 2 * HALF_SPAN), jnp.float32)],
        compiler_params=_cparams(("arbitrary", "arbitrary")),
        name=f"dilated_attention_d{dil}",
    )(table, q, k, k, k, v, v, v)


NA_DR = 2 * NA_KH - 1
NA_DC = 2 * NA_KW - 1
NA_PAIRS = NA_DR - 1


def _build_na_bias(rpb_ref, pair_scr, head0):
    qc = lax.broadcasted_iota(jnp.int32, (GRID_W, 2 * GRID_W), 0)
    lane = lax.broadcasted_iota(jnp.int32, (GRID_W, 2 * GRID_W), 1)
    kc = lane % GRID_W
    upper = lane >= GRID_W
    qstart = jnp.clip(qc - NA_KW // 2, 0, GRID_W - NA_KW)
    valid = (kc >= qstart) & (kc < qstart + NA_KW)
    dc = jnp.clip(kc - qc, -(NA_KW - 1), NA_KW - 1) + NA_KW - 1

    def one_pair(idx, carry):
        h = idx // NA_PAIRS
        dr = idx % NA_PAIRS
        base = ((head0 + h) * NA_DR + dr) * NA_DC
        val = jnp.zeros((GRID_W, 2 * GRID_W), jnp.float32)
        for c in range(NA_DC):
            val = jnp.where(dc == c, jnp.where(upper, rpb_ref[base + NA_DC + c], rpb_ref[base + c]), val)
        pair_scr[h, dr] = jnp.where(valid, val, NEG_INF)
        return carry

    lax.fori_loop(0, 4 * NA_PAIRS, one_pair, 0)


def _na_kernel(rpb_ref, q_ref, kp_ref, kc_ref, kn_ref, vp_ref, vc_ref, vn_ref, o_ref, k_scr, v_scr, pair_scr,
               *, seg0_blocks, blocks_per_seq):
    ib = pl.program_id(1)

    @pl.when(ib == 0)
    def _():
        _build_na_bias(rpb_ref, pair_scr, pl.program_id(0) * 4)

    in0 = ib < seg0_blocks
    il = jnp.where(in0, ib, ib - seg0_blocks)
    nbs = jnp.where(in0, blocks_per_seq[0], blocks_per_seq[1])
    pos = lax.rem(il, nbs)
    first = pos == 0
    last = pos == nbs - 1
    blk = NA_ROWS * GRID_W
    k_scr[0:blk] = kp_ref[...]
    k_scr[blk:2 * blk] = kc_ref[...]
    k_scr[2 * blk:3 * blk] = kn_ref[...]
    v_scr[0:blk] = vp_ref[...]
    v_scr[blk:2 * blk] = vc_ref[...]
    v_scr[2 * blk:3 * blk] = vn_ref[...]
    half = NA_KH // 2
    for rr in range(NA_ROWS):
        inner = NA_ROWS + rr - half
        s_first, s_last = max(inner, NA_ROWS), min(inner, 2 * NA_ROWS - NA_KH)
        start = jnp.where(first, s_first, jnp.where(last, s_last, inner))
        var = jnp.where(first, NA_ROWS + rr - s_first, jnp.where(last, NA_ROWS + rr - s_last, half))
        off = pl.multiple_of(start * GRID_W, GRID_W)
        kk = k_scr[pl.ds(off, NA_KH * GRID_W), :]
        vv = v_scr[pl.ds(off, NA_KH * GRID_W), :]
        qs = _stack_heads(q_ref[rr * GRID_W:(rr + 1) * GRID_W, :])
        s = lax.dot_general(qs, kk, (((1,), (1,)), ((), ())), preferred_element_type=jnp.float32)
        bias = jnp.concatenate(
            [jnp.concatenate([pair_scr[h, 2 * p - var + NA_KH - 1] for p in range(NA_KH // 2)], axis=1)
             for h in range(4)], axis=0)
        s = s + bias
        o, _ = _softmax_pv(s, vv, GRID_W)
        o_ref[rr * GRID_W:(rr + 1) * GRID_W, :] = o.astype(o_ref.dtype)


def _neighborhood(qb, kb, vb, rpb_flat, seg_tokens, seq_lens):
    n = qb.shape[0]
    blk = NA_ROWS * GRID_W
    nblk = n // blk
    seg0_blocks = seg_tokens[0] // blk
    blocks_per_seq = tuple(t // blk for t in seq_lens)
    assert min(blocks_per_seq) >= 2
    nset = QB // HEAD_BLOCK

    def kv_spec(t):
        return pl.BlockSpec((blk, HEAD_BLOCK), lambda c, ib: (jnp.clip(ib - 1 + t, 0, nblk - 1), c))

    kern = functools.partial(_na_kernel, seg0_blocks=seg0_blocks, blocks_per_seq=blocks_per_seq)
    return pl.pallas_call(
        kern,
        grid=(nset, nblk),
        in_specs=[pl.BlockSpec(memory_space=pltpu.SMEM), pl.BlockSpec((blk, HEAD_BLOCK), lambda c, ib: (ib, c))]
        + [kv_spec(t) for t in range(3)] + [kv_spec(t) for t in range(3)],
        out_specs=pl.BlockSpec((blk, HEAD_BLOCK), lambda c, ib: (ib, c)),
        out_shape=jax.ShapeDtypeStruct((n, QB), jnp.bfloat16),
        scratch_shapes=[pltpu.VMEM((3 * blk, HEAD_BLOCK), jnp.bfloat16),
                        pltpu.VMEM((3 * blk, HEAD_BLOCK), jnp.bfloat16),
                        pltpu.VMEM((4, NA_PAIRS, GRID_W, 2 * GRID_W), jnp.float32)],
        compiler_params=_cparams(("arbitrary", "arbitrary")),
        name="neighborhood_attention",
    )(rpb_flat, qb, kb, kb, kb, vb, vb, vb)


def _post_kernel(xp_ref, xs_ref, lng_ref, lnb_ref, o0_ref, o1_ref, o2_ref, l0_ref, l1_ref, l2_ref, ob_ref,
                 ga_ref, gb_ref, wa_ref, wb_ref, wo_ref, bo_ref, g1_ref, b1_ref, wr_ref, br_ref, tri_ref,
                 x1_ref, route_ref, route_t_ref, counts_ref, carry_ref, perm_scr, *, n0_tiles):
    i = pl.program_id(0)
    tm = route_ref.shape[0]
    hm = tm // POST_SPLIT

    @pl.when(i == 0)
    def _():
        carry_ref[...] = jnp.zeros_like(carry_ref)

    def token_order(ref, slab0):
        dil, per = ref.shape[1], ref.shape[2]
        if dil == 1:
            return lambda rs: ref[0, 0, rs, :].astype(jnp.float32)
        for r in range(dil):
            val = ref[0, r].astype(jnp.float32)
            for s in range(2):
                perm_scr[slab0 + s, pl.ds(r, per, stride=dil), :] = val[:, s * LANES:(s + 1) * LANES]
        return lambda rs: jnp.concatenate([perm_scr[slab0, rs, :], perm_scr[slab0 + 1, rs, :]], axis=1)

    lse_of = [token_order(ref, 2 * j) for j, ref in enumerate((l0_ref, l1_ref, l2_ref))]
    o_of = [token_order(ref, 6 + 2 * j) for j, ref in enumerate((o0_ref, o1_ref, o2_ref))]
    sigmoid = lambda v: 0.5 * jnp.tanh(0.5 * v) + 0.5
    grow = lax.broadcasted_iota(jnp.int32, (8, hm), 0)
    erow = lax.broadcasted_iota(jnp.int32, (N_EXPERTS, hm), 0)
    frow = lax.broadcasted_iota(jnp.int32, (ROUTE_LANES, hm), 0)
    neg = jnp.float32(-jnp.inf)
    carry = carry_ref[:, 0:1]

    for h in range(POST_SPLIT):
        rs = slice(h * hm, (h + 1) * hm)
        l0, l1, l2 = (f(rs) for f in lse_of)
        lm = jnp.maximum(jnp.maximum(l0, l1), l2)
        e0, e1, e2 = jnp.exp(l0 - lm), jnp.exp(l1 - lm), jnp.exp(l2 - lm)
        inv = 1.0 / (e0 + e1 + e2)
        o_a = (e0 * inv) * o_of[0](rs) + (e1 * inv) * o_of[1](rs) + (e2 * inv) * o_of[2](rs)
        y_a = jnp.dot(o_a.astype(jnp.bfloat16), wa_ref[...], preferred_element_type=jnp.float32)
        y_b = jnp.dot(ob_ref[rs, :], wb_ref[...], preferred_element_type=jnp.float32)
        mix = sigmoid(ga_ref[rs, :].astype(jnp.float32)) * y_a + sigmoid(gb_ref[rs, :].astype(jnp.float32)) * y_b
        out = jnp.dot(mix.astype(jnp.bfloat16), wo_ref[...], preferred_element_type=jnp.float32) + bo_ref[...]
        x = jnp.where(i < n0_tiles, xp_ref[rs, :], xs_ref[rs, :])
        x0 = _layer_norm(x, lng_ref[...], lnb_ref[...])
        x1 = _layer_norm(DN_ALPHA * x0 + out, g1_ref[...], b1_ref[...])
        _store_row_tiles(x1_ref, x1, row0=h * hm)

        r = jnp.dot(x1.astype(jnp.bfloat16), wr_ref[...], preferred_element_type=jnp.float32) + br_ref[...]
        rt = r.T
        lg = jnp.where(grow < N_GROUPS, rt[0:8], neg)
        gmax = jnp.max(lg, axis=0, keepdims=True)
        gi = jnp.min(jnp.where(lg == gmax, grow, 8), axis=0, keepdims=True)
        wg = 1.0 / jnp.sum(jnp.exp(lg - gmax), axis=0, keepdims=True)
        lo = EXPERTS_PER_GROUP * gi
        le = jnp.where((erow >= lo) & (erow < lo + EXPERTS_PER_GROUP), rt[EXPERT_LANE0:EXPERT_LANE0 + N_EXPERTS], neg)
        m1 = jnp.max(le, axis=0, keepdims=True)
        i1 = jnp.min(jnp.where(le == m1, erow, N_EXPERTS), axis=0, keepdims=True)
        le2 = jnp.where(erow == i1, neg, le)
        m2 = jnp.max(le2, axis=0, keepdims=True)
        i2 = jnp.min(jnp.where(le2 == m2, erow, N_EXPERTS), axis=0, keepdims=True)
        t2 = jnp.exp(m2 - m1)
        w1 = wg / (1.0 + t2)
        w2 = wg * t2 / (1.0 + t2)

        hot1 = erow == i1
        hot2 = erow == i2
        hot = (hot1 | hot2).astype(jnp.float32)
        before = jnp.dot(hot.astype(jnp.bfloat16), tri_ref[...], preferred_element_type=jnp.float32) + carry
        rank1 = jnp.sum(jnp.where(hot1, before, 0.0), axis=0, keepdims=True)
        rank2 = jnp.sum(jnp.where(hot2, before, 0.0), axis=0, keepdims=True)
        carry = carry + jnp.sum(hot, axis=1, keepdims=True)

        fields = (i1.astype(jnp.float32), i2.astype(jnp.float32), w1, w2, rank1, rank2)
        route_t = jnp.zeros((ROUTE_LANES, hm), jnp.float32)
        for idx, val in enumerate(fields):
            route_t = jnp.where(frow == idx, val, route_t)
        route_t_ref[:, rs] = route_t[0:route_t_ref.shape[0]]
        route_ref[rs, :] = route_t.T

    carry_ref[...] = jnp.broadcast_to(carry, carry_ref.shape)
    counts_ref[...] = jnp.broadcast_to(carry, counts_ref.shape)


def _post_attention(xp, xs, ln_in_g, ln_in_b, o_groups, lse_groups, ob, ga, gb, wa, wb, wo, bo, g1, b1, wr, br):
    n0, d = xp.shape
    n = n0 + xs.shape[0]
    tm = POST_TM
    n0_tiles = n0 // tm
    nt = n // tm
    hm = tm // POST_SPLIT
    tri = jnp.asarray(np.triu(np.ones((hm, hm), np.float32), 1), jnp.bfloat16)

    def tok(width):
        return pl.BlockSpec((tm, width), lambda i: (i, 0))

    def full(a):
        return pl.BlockSpec(a.shape, lambda i: (0,) * a.ndim)

    kern = functools.partial(_post_kernel, n0_tiles=n0_tiles)
    return pl.pallas_call(
        kern,
        grid=(nt,),
        in_specs=[
            pl.BlockSpec((tm, d), lambda i: (jnp.minimum(i, n0_tiles - 1), 0)),
            pl.BlockSpec((tm, d), lambda i: (jnp.maximum(i - n0_tiles, 0), 0)),
            full(ln_in_g), full(ln_in_b),
            *[pl.BlockSpec((1,) + a.shape[1:], lambda i: (i, 0, 0, 0)) for a in (*o_groups, *lse_groups)],
            tok(QB), tok(d), tok(d),
            full(wa), full(wb), full(wo), full(bo), full(g1), full(b1), full(wr), full(br), full(tri),
        ],
        out_specs=[pl.BlockSpec((tm * SUBLANES, LANES), lambda i: (i, 0)), tok(ROUTE_LANES),
                   pl.BlockSpec((8, tm), lambda i: (0, i)), pl.BlockSpec((N_EXPERTS, ROUTE_LANES), lambda i: (0, 0))],
        out_shape=[jax.ShapeDtypeStruct((n * SUBLANES, LANES), jnp.float32),
                   jax.ShapeDtypeStruct((n, ROUTE_LANES), jnp.float32),
                   jax.ShapeDtypeStruct((8, n), jnp.float32),
                   jax.ShapeDtypeStruct((N_EXPERTS, ROUTE_LANES), jnp.float32)],
        scratch_shapes=[pltpu.VMEM((N_EXPERTS, ROUTE_LANES), jnp.float32), pltpu.VMEM((12, tm, LANES), jnp.float32)],
        compiler_params=_cparams(("arbitrary",)),
        name="post_attention_router",
    )(xp, xs, ln_in_g, ln_in_b, *o_groups, *lse_groups, ob, ga, gb, wa, wb, wo, bo, g1, b1, wr, br, tri)


def _dispatch_kernel(pos_ref, x_ref, xs_ref, sem):
    tt = x_ref.shape[0] // SUBLANES

    def issue(j, carry):
        for u in range(2):
            t = 2 * j + u
            src = x_ref.at[pl.ds(pl.multiple_of(t * SUBLANES, SUBLANES), SUBLANES)]
            for slot in range(2):
                p = pos_ref[slot, t]
                dst = xs_ref.at[pl.ds(pl.multiple_of(p * SUBLANES, SUBLANES), SUBLANES)]
                pltpu.make_async_copy(src, dst, sem).start(priority=u)
        return carry

    lax.fori_loop(0, tt // 2, issue, 0, unroll=4)
    for _ in range(2):
        pltpu.make_async_copy(x_ref, xs_ref.at[pl.ds(0, tt * SUBLANES)], sem).wait()


def _dispatch(x1t, pos, n_rows):
    n = x1t.shape[0] // SUBLANES
    tt = DISP_TT
    return pl.pallas_call(
        _dispatch_kernel,
        grid=(n // tt,),
        in_specs=[pl.BlockSpec((2, tt), lambda i: (0, i), memory_space=pltpu.SMEM),
                  pl.BlockSpec((tt * SUBLANES, LANES), lambda i: (i, 0))],
        out_specs=pl.BlockSpec(memory_space=pl.ANY),
        out_shape=jax.ShapeDtypeStruct((n_rows * SUBLANES, LANES), x1t.dtype),
        scratch_shapes=[pltpu.SemaphoreType.DMA(())],
        compiler_params=_cparams(("arbitrary",)),
        name="moe_dispatch",
    )(pos, x1t)


def _ffn_kernel(item_expert_ref, item_tile_ref, item_lo_ref, item_hi_ref, n_items_ref,
                x_ref, wg_ref, wu_ref, wd_ref, y_ref, wg_bf, wu_bf, wd_bf):
    j = pl.program_id(0)

    @pl.when(j < n_items_ref[0])
    def _():
        prev = jnp.maximum(j - 1, 0)

        @pl.when(jnp.logical_or(j == 0, item_expert_ref[j] != item_expert_ref[prev]))
        def _():
            wg_bf[...] = wg_ref[0].astype(jnp.bfloat16)
            wu_bf[...] = wu_ref[0].astype(jnp.bfloat16)
            wd_bf[...] = wd_ref[0].astype(jnp.bfloat16)

        tm = x_ref.shape[0] // SUBLANES
        x = _load_row_tiles(x_ref, tm).astype(jnp.bfloat16)
        a = jnp.dot(x, wg_bf[...], preferred_element_type=jnp.float32)
        u = jnp.dot(x, wu_bf[...], preferred_element_type=jnp.float32)
        h = (a * jax.nn.sigmoid(a) * u).astype(jnp.bfloat16)
        y = jnp.dot(h, wd_bf[...], preferred_element_type=jnp.float32)
        rows = lax.broadcasted_iota(jnp.int32, (tm, 1), 0)
        y = jnp.where((rows >= item_lo_ref[j]) & (rows < item_hi_ref[j]), y, 0.0)
        first_of_tile = jnp.logical_or(j == 0, item_tile_ref[j] != item_tile_ref[prev])

        @pl.when(first_of_tile)
        def _():
            _store_row_tiles(y_ref, y)

        @pl.when(jnp.logical_not(first_of_tile))
        def _():
            _store_row_tiles(y_ref, y, accumulate=True)


def _ffn_items(cnt, n_rows):
    tm = FFN_TM
    n_tiles = n_rows // tm
    max_items = n_tiles + N_EXPERTS - 1
    ends = jnp.cumsum(cnt)
    starts = ends - cnt
    first_tile = starts // tm
    last_tile = jnp.maximum(ends - 1, 0) // tm
    items_per_expert = jnp.where(cnt > 0, last_tile - first_tile + 1, 0)
    item_ends = jnp.cumsum(items_per_expert)
    n_items = item_ends[-1:]
    j = jnp.minimum(jnp.arange(max_items, dtype=jnp.int32), n_items[0] - 1)
    expert = jnp.minimum(jnp.sum(item_ends[None, :] <= j[:, None], axis=1), N_EXPERTS - 1).astype(jnp.int32)
    tile = first_tile[expert] + (j - (item_ends - items_per_expert)[expert])
    lo = jnp.clip(starts[expert] - tile * tm, 0, tm)
    hi = jnp.clip(ends[expert] - tile * tm, 0, tm)
    i32 = lambda a: a.astype(jnp.int32)
    return i32(expert), i32(tile), i32(lo), i32(hi), i32(n_items), starts


def _grouped_ffn(xs, items, w_gate, w_up, w_down):
    tm = FFN_TM
    d, de = w_gate.shape[1:]
    item_expert, item_tile, item_lo, item_hi, n_items = items
    max_items = item_expert.shape[0]

    def row_map(j, ie, it, lo, hi, ni):
        return (it[j], 0)

    def w_map(j, ie, it, lo, hi, ni):
        return (ie[j], 0, 0)

    grid_spec = pltpu.PrefetchScalarGridSpec(
        num_scalar_prefetch=5,
        grid=(max_items,),
        in_specs=[pl.BlockSpec((tm * SUBLANES, LANES), row_map),
                  pl.BlockSpec((1, d, de), w_map),
                  pl.BlockSpec((1, d, de), w_map),
                  pl.BlockSpec((1, de, d), w_map)],
        out_specs=pl.BlockSpec((tm * SUBLANES, LANES), row_map),
        scratch_shapes=[pltpu.VMEM((d, de), jnp.bfloat16), pltpu.VMEM((d, de), jnp.bfloat16),
                        pltpu.VMEM((de, d), jnp.bfloat16)],
    )
    return pl.pallas_call(
        _ffn_kernel,
        grid_spec=grid_spec,
        out_shape=jax.ShapeDtypeStruct(xs.shape, jnp.float32),
        compiler_params=_cparams(("arbitrary",)),
        name="moe_grouped_ffn",
    )(item_expert, item_tile, item_lo, item_hi, n_items, xs, w_gate, w_up, w_down)


def _combine_kernel(pos_ref, pos_next_ref, x1_ref, route_ref, g2_ref, b2_ref, ys_ref, o_ref, buf, sems):
    tt = x1_ref.shape[0] // SUBLANES
    i = pl.program_id(0)
    cur = lax.rem(i, 2)

    def issue_tile(p_ref, b):
        def issue(j, carry):
            for u in range(2):
                t = 2 * j + u
                for slot in range(2):
                    p = p_ref[slot, t]
                    src = ys_ref.at[pl.ds(pl.multiple_of(p * SUBLANES, SUBLANES), SUBLANES)]
                    dst = buf.at[b, slot, pl.ds(pl.multiple_of(t * SUBLANES, SUBLANES), SUBLANES)]
                    pltpu.make_async_copy(src, dst, sems.at[b]).start(priority=u)
            return carry

        lax.fori_loop(0, tt // 2, issue, 0, unroll=4)

    @pl.when(i == 0)
    def _():
        issue_tile(pos_ref, 0)

    @pl.when(i + 1 < pl.num_programs(0))
    def _():
        issue_tile(pos_next_ref, 1 - cur)

    for slot in range(2):
        pltpu.make_async_copy(ys_ref.at[pl.ds(0, tt * SUBLANES)], buf.at[cur, slot], sems.at[cur]).wait()
    route = route_ref[...]
    y = route[:, 2:3] * _load_row_tiles(buf.at[cur, 0], tt) + route[:, 3:4] * _load_row_tiles(buf.at[cur, 1], tt)
    o_ref[...] = _layer_norm(DN_ALPHA * _load_row_tiles(x1_ref, tt) + y, g2_ref[...], b2_ref[...])


def _combine(x1t, route, pos, ys, g2, b2, tok0, n_tok):
    d = g2.shape[1]
    tt = COMB_TT
    t0 = tok0 // tt
    nt = n_tok // tt
    return pl.pallas_call(
        _combine_kernel,
        grid=(nt,),
        in_specs=[pl.BlockSpec((2, tt), lambda i: (0, i + t0), memory_space=pltpu.SMEM),
                  pl.BlockSpec((2, tt), lambda i: (0, jnp.minimum(i + 1, nt - 1) + t0), memory_space=pltpu.SMEM),
                  pl.BlockSpec((tt * SUBLANES, LANES), lambda i: (i + t0, 0)),
                  pl.BlockSpec((tt, ROUTE_LANES), lambda i: (i + t0, 0)),
                  pl.BlockSpec((1, d), lambda i: (0, 0)),
                  pl.BlockSpec((1, d), lambda i: (0, 0)),
                  pl.BlockSpec(memory_space=pl.ANY)],
        out_specs=pl.BlockSpec((tt, d), lambda i: (i, 0)),
        out_shape=jax.ShapeDtypeStruct((n_tok, d), jnp.float32),
        scratch_shapes=[pltpu.VMEM((2, 2, tt * SUBLANES, LANES), ys.dtype), pltpu.SemaphoreType.DMA((2,))],
        compiler_params=_cparams(("arbitrary",)),
        name="moe_combine_ln",
    )(pos, pos, x1t, route, g2, b2, ys)


def _layer(x_prompt, x_sample, ln_in_g, ln_in_b, w_in, b_in, rel_bias_t5, na_rpb, w_branch_a, w_branch_b,
           w_out, b_out, ln1_g, ln1_b, w_router_group, b_router_group, w_router_expert, b_router_expert,
           w_exp_gate, w_exp_up, w_exp_down, ln2_g, ln2_b):
    bp, tp, d = x_prompt.shape
    bs, ts, _ = x_sample.shape
    seg_tokens = (bp * tp, bs * ts)
    seq_lens = (tp, ts)
    n = sum(seg_tokens)
    xp = x_prompt.reshape(seg_tokens[0], d)
    xs = x_sample.reshape(seg_tokens[1], d)
    row = lambda a: a.reshape(1, -1).astype(jnp.float32)
    bf = lambda a: a.astype(jnp.bfloat16)

    scale = HEAD_DIM ** -0.5
    col_scale = np.ones((w_in.shape[2],), np.float32)
    col_scale[0:QA] = scale
    col_scale[3 * QA:3 * QA + QB] = scale
    w_proj = bf(w_in[0] * col_scale)
    b_proj = row(b_in[0] * col_scale)
    proj = _in_projection(xp, xs, row(ln_in_g), row(ln_in_b), w_proj, b_proj)
    qkv_a, (qb, kb, vb, ga, gb) = proj[:3 * N_DIL_GROUPS], proj[3 * N_DIL_GROUPS:]

    o_groups, lse_groups = [], []
    for g, (window, dil) in enumerate(DIL_PATTERNS):
        assert window // (2 * dil) == HALF_SPAN
        table = rel_bias_t5[:, g * HEADS_PER_DIL_GROUP:(g + 1) * HEADS_PER_DIL_GROUP].astype(jnp.float32).reshape(-1)
        o, lse = _dilated_group(qkv_a[g], qkv_a[N_DIL_GROUPS + g], qkv_a[2 * N_DIL_GROUPS + g], table, dil,
                                seg_tokens, seq_lens)
        o_groups.append(o)
        lse_groups.append(lse)
    ob = _neighborhood(qb, kb, vb, na_rpb[0].astype(jnp.float32).reshape(-1), seg_tokens, seq_lens)

    wr = jnp.zeros((d, ROUTE_LANES), jnp.float32)
    wr = wr.at[:, 0:N_GROUPS].set(w_router_group[0]).at[:, EXPERT_LANE0:EXPERT_LANE0 + N_EXPERTS].set(w_router_expert[0])
    br = jnp.zeros((1, ROUTE_LANES), jnp.float32)
    br = br.at[0, 0:N_GROUPS].set(b_router_group[0]).at[0, EXPERT_LANE0:EXPERT_LANE0 + N_EXPERTS].set(b_router_expert[0])
    x1, route, route_t, counts = _post_attention(
        xp, xs, row(ln_in_g), row(ln_in_b), o_groups, lse_groups, ob, ga, gb, bf(w_branch_a[0]), bf(w_branch_b[0]),
        bf(w_out[0]), row(b_out[0]), row(ln1_g[0]), row(ln1_b[0]), bf(wr), br)

    experts = route_t[0:2].astype(jnp.int32)
    ranks = route_t[4:6].astype(jnp.int32)
    cnt = counts[:, 0].astype(jnp.int32)
    n_rows = 2 * n
    *items, starts = _ffn_items(cnt, n_rows)
    pos = ranks
    for e in range(N_EXPERTS):
        pos = pos + jnp.where(experts == e, starts[e], 0)

    xsorted = _dispatch(x1, pos, n_rows)
    ys = _grouped_ffn(xsorted, items, w_exp_gate[0], w_exp_up[0], w_exp_down[0])
    y_prompt = _combine(x1, route, pos, ys, row(ln2_g[0]), row(ln2_b[0]), 0, seg_tokens[0])
    y_sample = _combine(x1, route, pos, ys, row(ln2_g[0]), row(ln2_b[0]), seg_tokens[0], seg_tokens[1])
    return y_prompt.reshape(bp, tp, d), y_sample.reshape(bs, ts, d)


def kernel(x_prompt, x_sample, ln_in_g, ln_in_b, w_in, b_in, rel_bias_t5, na_rpb, w_branch_a, w_branch_b, w_out, b_out, ln1_g, ln1_b, w_router_group, b_router_group, w_router_expert, b_router_expert, w_exp_gate, w_exp_up, w_exp_down, ln2_g, ln2_b):
    return _layer(x_prompt, x_sample, ln_in_g, ln_in_b, w_in, b_in, rel_bias_t5, na_rpb, w_branch_a, w_branch_b,
                  w_out, b_out, ln1_g, ln1_b, w_router_group, b_router_group, w_router_expert, b_router_expert,
                  w_exp_gate, w_exp_up, w_exp_down, ln2_g, ln2_b)
```

```python
import functools
import math

import numpy as np
import jax
import jax.numpy as jnp
from jax import lax
from jax.experimental import pallas as pl
from jax.experimental.pallas import tpu as pltpu

HEAD_DIM = 64
DIL_PATTERNS = ((128, 1), (512, 4), (2048, 16))
HEADS_PER_DIL_GROUP = 4
N_DIL_GROUPS = 3
H_A = N_DIL_GROUPS * HEADS_PER_DIL_GROUP
H_B = 8
QA = H_A * HEAD_DIM
QB = H_B * HEAD_DIM
HALF_SPAN = 64
GRID_W = 64
NA_KH = 8
NA_KW = 16
N_BUCKETS = 32
REL_MAX_DIST = 1024
N_GROUPS = 4
EXPERTS_PER_GROUP = 8
N_EXPERTS = N_GROUPS * EXPERTS_PER_GROUP
LN_EPS = 1e-5
NEG_INF = -1e30
DEPTH = 1
DN_ALPHA = (2.0 * DEPTH) ** 0.25

LANES = 128
HEAD_BLOCK = 4 * HEAD_DIM
VMEM_LIMIT_BYTES = 56 * 1024 * 1024

PROJ_TM = 512
DIL_M = 128
DIL_GROUP = 8
NA_ROWS = 16
POST_TM = 512
POST_SPLIT = 1
FFN_TM = 512
DISP_TT = 2048
COMB_TT = 512
ROUTE_LANES = LANES
EXPERT_LANE0 = 32


def _cparams(sem):
    return pltpu.CompilerParams(dimension_semantics=sem, vmem_limit_bytes=VMEM_LIMIT_BYTES)


SUBLANES = 8


def _store_row_tiles(ref, x, accumulate=False, row0=0):
    rows = x.shape[0]
    for c in range(SUBLANES):
        idx = pl.ds(row0 * SUBLANES + c, rows, stride=SUBLANES)
        piece = x[:, c * LANES:(c + 1) * LANES]
        ref[idx, :] = ref[idx, :] + piece if accumulate else piece


def _load_row_tiles(ref, rows, row0=0):
    return jnp.concatenate([ref[pl.ds(row0 * SUBLANES + c, rows, stride=SUBLANES), :] for c in range(SUBLANES)],
                           axis=1)


def _layer_norm(x, g, b):
    mu = jnp.mean(x, axis=-1, keepdims=True)
    xc = x - mu
    var = jnp.mean(xc * xc, axis=-1, keepdims=True)
    return xc * lax.rsqrt(var + LN_EPS) * g + b


def _strided_rows(scr, slab0, r, count, stride):
    return jnp.concatenate([scr[slab0 + s, pl.ds(r, count, stride=stride), :] for s in range(2)], axis=1)


def _inproj_kernel(xp_ref, xs_ref, g_ref, b_ref, w_ref, bias_ref, *refs, n0_tiles):
    out_refs, scr = refs[:-1], refs[-1]
    i = pl.program_id(0)
    x = jnp.where(i < n0_tiles, xp_ref[...], xs_ref[...])
    xn = _layer_norm(x, g_ref[...], b_ref[...]).astype(jnp.bfloat16)
    tm = xn.shape[0]

    def proj(c0, cw):
        return jnp.dot(xn, w_ref[:, c0:c0 + cw], preferred_element_type=jnp.float32) + bias_ref[:, c0:c0 + cw]

    n_dil = 3 * N_DIL_GROUPS
    for idx in range(n_dil):
        g = idx % N_DIL_GROUPS
        dil = DIL_PATTERNS[g][1]
        o_ref = out_refs[idx]
        y = proj(idx * HEAD_BLOCK, HEAD_BLOCK)
        if dil == 1:
            o_ref[0, 0] = y.astype(o_ref.dtype)
        else:
            slab0 = 2 * (idx % 2)
            scr[slab0] = y[:, :LANES]
            scr[slab0 + 1] = y[:, LANES:]
            for r in range(dil):
                o_ref[0, r] = _strided_rows(scr, slab0, r, tm // dil, dil).astype(o_ref.dtype)
    col = n_dil * HEAD_BLOCK
    for o_ref in out_refs[n_dil:]:
        width = o_ref.shape[1]
        for c in range(0, width, 512):
            cw = min(512, width - c)
            o_ref[:, c:c + cw] = proj(col + c, cw).astype(o_ref.dtype)
        col += width


def _in_projection(xp, xs, ln_g, ln_b, w, b):
    n0, d_model = xp.shape
    n1 = xs.shape[0]
    n = n0 + n1
    tm = PROJ_TM
    n0_tiles, n1_tiles = n0 // tm, n1 // tm
    nt = n0_tiles + n1_tiles
    out_shape, out_specs = [], []
    for _ in range(3):
        for _, dil in DIL_PATTERNS:
            out_shape.append(jax.ShapeDtypeStruct((nt, dil, tm // dil, HEAD_BLOCK), jnp.bfloat16))
            out_specs.append(pl.BlockSpec((1, dil, tm // dil, HEAD_BLOCK), lambda i: (i, 0, 0, 0)))
    for wd in (QB, QB, QB, d_model, d_model):
        out_shape.append(jax.ShapeDtypeStruct((n, wd), jnp.bfloat16))
        out_specs.append(pl.BlockSpec((tm, wd), lambda i: (i, 0)))
    d_in = w.shape[1]
    kern = functools.partial(_inproj_kernel, n0_tiles=n0_tiles)
    return pl.pallas_call(
        kern,
        grid=(nt,),
        in_specs=[
            pl.BlockSpec((tm, d_model), lambda i: (jnp.minimum(i, n0_tiles - 1), 0)),
            pl.BlockSpec((tm, d_model), lambda i: (jnp.maximum(i - n0_tiles, 0), 0)),
            pl.BlockSpec((1, d_model), lambda i: (0, 0)),
            pl.BlockSpec((1, d_model), lambda i: (0, 0)),
            pl.BlockSpec((d_model, d_in), lambda i: (0, 0)),
            pl.BlockSpec((1, d_in), lambda i: (0, 0)),
        ],
        out_specs=out_specs,
        out_shape=out_shape,
        scratch_shapes=[pltpu.VMEM((4, tm, LANES), jnp.float32)],
        compiler_params=_cparams(("arbitrary",)),
        name="in_projection",
    )(xp, xs, ln_g, ln_b, w, b)


def _t5_bucket_np(rel):
    half = N_BUCKETS // 2
    max_exact = half // 2
    ret = np.where(rel > 0, half, 0)
    n = np.abs(rel)
    nf = np.maximum(n, 1).astype(np.float32)
    large = max_exact + (np.log(nf / np.float32(max_exact)) / np.float32(math.log(REL_MAX_DIST / max_exact))
                         * np.float32(half - max_exact)).astype(np.int32)
    large = np.minimum(large, half - 1)
    return ret + np.where(n < max_exact, n, large)


def _t5_bucket_starts():
    half = N_BUCKETS // 2
    dist = np.arange(0, HALF_SPAN * max(d for _, d in DIL_PATTERNS) + 1)
    buckets = _t5_bucket_np(-dist)
    return [int(np.argmax(buckets >= k)) for k in range(half)]


def _build_dilated_bias(table_ref, bias_scr, dil):
    half = N_BUCKETS // 2
    starts = _t5_bucket_starts()
    kb = DIL_M + 2 * HALF_SPAN
    i = lax.broadcasted_iota(jnp.int32, (DIL_M, kb), 0)
    m = lax.broadcasted_iota(jnp.int32, (DIL_M, kb), 1)
    delta = m - HALF_SPAN - i
    dist = jnp.abs(delta) * dil
    band = jnp.abs(delta) <= HALF_SPAN
    for h in range(HEADS_PER_DIL_GROUP):
        sides = []
        for side in range(2):
            val = jnp.full((DIL_M, kb), table_ref[(side * half + half - 1) * HEADS_PER_DIL_GROUP + h], jnp.float32)
            for k in range(half - 1, 0, -1):
                val = jnp.where(dist < starts[k], table_ref[(side * half + k - 1) * HEADS_PER_DIL_GROUP + h], val)
            sides.append(val)
        base = jnp.where(band, jnp.where(delta > 0, sides[1], sides[0]), NEG_INF)
        rows = slice(h * DIL_M, (h + 1) * DIL_M)
        bias_scr[0, rows, :] = base
        bias_scr[1, rows, :] = jnp.where(m >= HALF_SPAN, base, NEG_INF)
        bias_scr[2, rows, :] = jnp.where(m < kb - HALF_SPAN, base, NEG_INF)


def _stack_heads(q):
    head_of_col = lax.broadcasted_iota(jnp.int32, (1, HEAD_BLOCK), 1) // HEAD_DIM
    zero = jnp.zeros_like(q)
    return jnp.concatenate([jnp.where(head_of_col == h, q, zero) for h in range(4)], axis=0)


def _merge_heads(x, m):
    head_of_col = lax.broadcasted_iota(jnp.int32, (1, HEAD_BLOCK), 1) // HEAD_DIM
    out = jnp.zeros((m, HEAD_BLOCK), x.dtype)
    for h in range(4):
        out = jnp.where(head_of_col == h, x[h * m:(h + 1) * m], out)
    return out


def _softmax_pv(s, v, m_rows):
    mx = jnp.max(s, axis=-1, keepdims=True)
    p = jnp.exp(s - mx)
    l = jnp.sum(p, axis=-1, keepdims=True)
    pv = jnp.dot(p.astype(jnp.bfloat16), v, preferred_element_type=jnp.float32)
    o = _merge_heads(pv * (1.0 / l), m_rows)
    return o, mx + jnp.log(l)


def _dilated_kernel(table_ref, q_ref, kp_ref, kc_ref, kn_ref, vp_ref, vc_ref, vn_ref,
                    o_ref, lse_ref, bias_scr, *, dil, seg0_blocks, blocks_per_seq):
    step = pl.program_id(1)

    @pl.when((pl.program_id(0) == 0) & (step == 0))
    def _():
        _build_dilated_bias(table_ref, bias_scr, dil)

    rows2d = lambda ref: ref[...].reshape(-1, HEAD_BLOCK)
    k = jnp.concatenate([rows2d(r) for r in (kp_ref, kc_ref, kn_ref)], axis=0)
    v = jnp.concatenate([rows2d(r) for r in (vp_ref, vc_ref, vn_ref)], axis=0)
    q = rows2d(q_ref)
    o_parts, lse_parts = [], []
    for g in range(DIL_GROUP):
        jb = step * DIL_GROUP + g
        in0 = jb < seg0_blocks
        jl = jnp.where(in0, jb, jb - seg0_blocks)
        nbs = jnp.where(in0, blocks_per_seq[0], blocks_per_seq[1])
        pos = lax.rem(jl, nbs)
        variant = jnp.where(pos == 0, 1, jnp.where(pos == nbs - 1, 2, 0))
        keys = slice(g * DIL_M, (g + 1) * DIL_M + 2 * HALF_SPAN)
        qs = _stack_heads(q[g * DIL_M:(g + 1) * DIL_M])
        s = lax.dot_general(qs, k[keys], (((1,), (1,)), ((), ())), preferred_element_type=jnp.float32)
        s = s + bias_scr[variant]
        o, lse = _softmax_pv(s, v[keys], DIL_M)
        o_parts.append(o.astype(o_ref.dtype))
        lse_parts.append(_merge_heads(jnp.broadcast_to(lse, (4 * DIL_M, HEAD_BLOCK)), DIL_M))
    o_ref[...] = jnp.concatenate(o_parts, axis=0).reshape(o_ref.shape)
    lse_ref[...] = jnp.concatenate(lse_parts, axis=0).reshape(lse_ref.shape)


def _residue_spec(rows_blk, dil, block_of):
    per_tile = PROJ_TM // dil
    if per_tile >= rows_blk:
        per = per_tile // rows_blk
        return pl.BlockSpec((1, None, rows_blk, HEAD_BLOCK),
                            lambda r, jb: (block_of(jb) // per, r, block_of(jb) % per, 0))
    return pl.BlockSpec((rows_blk // per_tile, None, per_tile, HEAD_BLOCK), lambda r, jb: (block_of(jb), r, 0, 0))


def _dilated_group(q, k, v, table, dil, seg_tokens, seq_lens):
    n = q.shape[0] * PROJ_TM
    rows = n // dil
    step_rows = DIL_GROUP * DIL_M
    assert rows % step_rows == 0 and all(t % (dil * DIL_M) == 0 for t in seq_lens)
    nsteps = rows // step_rows
    nkb = rows // HALF_SPAN
    ratio = step_rows // HALF_SPAN
    seg0_blocks = seg_tokens[0] // dil // DIL_M
    blocks_per_seq = tuple(t // dil // DIL_M for t in seq_lens)
    assert min(blocks_per_seq) >= 2

    qo_spec = _residue_spec(step_rows, dil, lambda jb: jb)
    prev_spec = _residue_spec(HALF_SPAN, dil, lambda jb: jnp.maximum(ratio * jb - 1, 0))
    next_spec = _residue_spec(HALF_SPAN, dil, lambda jb: jnp.minimum(ratio * (jb + 1), nkb - 1))
    kv_specs = [prev_spec, qo_spec, next_spec]
    kern = functools.partial(_dilated_kernel, dil=dil, seg0_blocks=seg0_blocks, blocks_per_seq=blocks_per_seq)
    return pl.pallas_call(
        kern,
        grid=(dil, nsteps),
        in_specs=[pl.BlockSpec(memory_space=pltpu.SMEM), qo_spec] + kv_specs + kv_specs,
        out_specs=[qo_spec, qo_spec],
        out_shape=[jax.ShapeDtypeStruct(q.shape, jnp.bfloat16), jax.ShapeDtypeStruct(q.shape, jnp.float32)],
        scratch_shapes=[pltpu.VMEM((3, HEADS_PER_DIL_GROUP * DIL_M, DIL_M + 2 * HALF_SPAN), jnp.float32)],
        compiler_params=_cparams(("arbitrary", "arbitrary")),
        name=f"dilated_attention_d{dil}",
    )(table, q, k, k, k, v, v, v)


NA_DR = 2 * NA_KH - 1
NA_DC = 2 * NA_KW - 1
NA_PAIRS = NA_DR - 1


def _build_na_bias(rpb_ref, pair_scr, head0):
    qc = lax.broadcasted_iota(jnp.int32, (GRID_W, 2 * GRID_W), 0)
    lane = lax.broadcasted_iota(jnp.int32, (GRID_W, 2 * GRID_W), 1)
    kc = lane % GRID_W
    upper = lane >= GRID_W
    qstart = jnp.clip(qc - NA_KW // 2, 0, GRID_W - NA_KW)
    valid = (kc >= qstart) & (kc < qstart + NA_KW)
    dc = jnp.clip(kc - qc, -(NA_KW - 1), NA_KW - 1) + NA_KW - 1

    def one_pair(idx, carry):
        h = idx // NA_PAIRS
        dr = idx % NA_PAIRS
        base = ((head0 + h) * NA_DR + dr) * NA_DC
        val = jnp.zeros((GRID_W, 2 * GRID_W), jnp.float32)
        for c in range(NA_DC):
            val = jnp.where(dc == c, jnp.where(upper, rpb_ref[base + NA_DC + c], rpb_ref[base + c]), val)
        pair_scr[h, dr] = jnp.where(valid, val, NEG_INF)
        return carry

    lax.fori_loop(0, 4 * NA_PAIRS, one_pair, 0)


def _na_kernel(rpb_ref, q_ref, kp_ref, kc_ref, kn_ref, vp_ref, vc_ref, vn_ref, o_ref, k_scr, v_scr, pair_scr,
               *, seg0_blocks, blocks_per_seq):
    ib = pl.program_id(1)

    @pl.when(ib == 0)
    def _():
        _build_na_bias(rpb_ref, pair_scr, pl.program_id(0) * 4)

    in0 = ib < seg0_blocks
    il = jnp.where(in0, ib, ib - seg0_blocks)
    nbs = jnp.where(in0, blocks_per_seq[0], blocks_per_seq[1])
    pos = lax.rem(il, nbs)
    first = pos == 0
    last = pos == nbs - 1
    blk = NA_ROWS * GRID_W
    k_scr[0:blk] = kp_ref[...]
    k_scr[blk:2 * blk] = kc_ref[...]
    k_scr[2 * blk:3 * blk] = kn_ref[...]
    v_scr[0:blk] = vp_ref[...]
    v_scr[blk:2 * blk] = vc_ref[...]
    v_scr[2 * blk:3 * blk] = vn_ref[...]
    half = NA_KH // 2
    for rr in range(NA_ROWS):
        inner = NA_ROWS + rr - half
        s_first, s_last = max(inner, NA_ROWS), min(inner, 2 * NA_ROWS - NA_KH)
        start = jnp.where(first, s_first, jnp.where(last, s_last, inner))
        var = jnp.where(first, NA_ROWS + rr - s_first, jnp.where(last, NA_ROWS + rr - s_last, half))
        off = pl.multiple_of(start * GRID_W, GRID_W)
        kk = k_scr[pl.ds(off, NA_KH * GRID_W), :]
        vv = v_scr[pl.ds(off, NA_KH * GRID_W), :]
        qs = _stack_heads(q_ref[rr * GRID_W:(rr + 1) * GRID_W, :])
        s = lax.dot_general(qs, kk, (((1,), (1,)), ((), ())), preferred_element_type=jnp.float32)
        bias = jnp.concatenate(
            [jnp.concatenate([pair_scr[h, 2 * p - var + NA_KH - 1] for p in range(NA_KH // 2)], axis=1)
             for h in range(4)], axis=0)
        s = s + bias
        o, _ = _softmax_pv(s, vv, GRID_W)
        o_ref[rr * GRID_W:(rr + 1) * GRID_W, :] = o.astype(o_ref.dtype)


def _neighborhood(qb, kb, vb, rpb_flat, seg_tokens, seq_lens):
    n = qb.shape[0]
    blk = NA_ROWS * GRID_W
    nblk = n // blk
    seg0_blocks = seg_tokens[0] // blk
    blocks_per_seq = tuple(t // blk for t in seq_lens)
    assert min(blocks_per_seq) >= 2
    nset = QB // HEAD_BLOCK

    def kv_spec(t):
        return pl.BlockSpec((blk, HEAD_BLOCK), lambda c, ib: (jnp.clip(ib - 1 + t, 0, nblk - 1), c))

    kern = functools.partial(_na_kernel, seg0_blocks=seg0_blocks, blocks_per_seq=blocks_per_seq)
    return pl.pallas_call(
        kern,
        grid=(nset, nblk),
        in_specs=[pl.BlockSpec(memory_space=pltpu.SMEM), pl.BlockSpec((blk, HEAD_BLOCK), lambda c, ib: (ib, c))]
        + [kv_spec(t) for t in range(3)] + [kv_spec(t) for t in range(3)],
        out_specs=pl.BlockSpec((blk, HEAD_BLOCK), lambda c, ib: (ib, c)),
        out_shape=jax.ShapeDtypeStruct((n, QB), jnp.bfloat16),
        scratch_shapes=[pltpu.VMEM((3 * blk, HEAD_BLOCK), jnp.bfloat16),
                        pltpu.VMEM((3 * blk, HEAD_BLOCK), jnp.bfloat16),
                        pltpu.VMEM((4, NA_PAIRS, GRID_W, 2 * GRID_W), jnp.float32)],
        compiler_params=_cparams(("arbitrary", "arbitrary")),
        name="neighborhood_attention",
    )(rpb_flat, qb, kb, kb, kb, vb, vb, vb)


def _post_kernel(xp_ref, xs_ref, lng_ref, lnb_ref, o0_ref, o1_ref, o2_ref, l0_ref, l1_ref, l2_ref, ob_ref,
                 ga_ref, gb_ref, wa_ref, wb_ref, wo_ref, bo_ref, g1_ref, b1_ref, wr_ref, br_ref, tri_ref,
                 x1_ref, route_ref, route_t_ref, counts_ref, carry_ref, perm_scr, *, n0_tiles):
    i = pl.program_id(0)
    tm = route_ref.shape[0]
    hm = tm // POST_SPLIT

    @pl.when(i == 0)
    def _():
        carry_ref[...] = jnp.zeros_like(carry_ref)

    def token_order(ref, slab0):
        dil, per = ref.shape[1], ref.shape[2]
        if dil == 1:
            return lambda rs: ref[0, 0, rs, :].astype(jnp.float32)
        for r in range(dil):
            val = ref[0, r].astype(jnp.float32)
            for s in range(2):
                perm_scr[slab0 + s, pl.ds(r, per, stride=dil), :] = val[:, s * LANES:(s + 1) * LANES]
        return lambda rs: jnp.concatenate([perm_scr[slab0, rs, :], perm_scr[slab0 + 1, rs, :]], axis=1)

    lse_of = [token_order(ref, 2 * j) for j, ref in enumerate((l0_ref, l1_ref, l2_ref))]
    o_of = [token_order(ref, 6 + 2 * j) for j, ref in enumerate((o0_ref, o1_ref, o2_ref))]
    sigmoid = lambda v: 0.5 * jnp.tanh(0.5 * v) + 0.5
    grow = lax.broadcasted_iota(jnp.int32, (8, hm), 0)
    erow = lax.broadcasted_iota(jnp.int32, (N_EXPERTS, hm), 0)
    frow = lax.broadcasted_iota(jnp.int32, (ROUTE_LANES, hm), 0)
    neg = jnp.float32(-jnp.inf)
    carry = carry_ref[:, 0:1]

    for h in range(POST_SPLIT):
        rs = slice(h * hm, (h + 1) * hm)
        l0, l1, l2 = (f(rs) for f in lse_of)
        lm = jnp.maximum(jnp.maximum(l0, l1), l2)
        e0, e1, e2 = jnp.exp(l0 - lm), jnp.exp(l1 - lm), jnp.exp(l2 - lm)
        inv = 1.0 / (e0 + e1 + e2)
        o_a = (e0 * inv) * o_of[0](rs) + (e1 * inv) * o_of[1](rs) + (e2 * inv) * o_of[2](rs)
        y_a = jnp.dot(o_a.astype(jnp.bfloat16), wa_ref[...], preferred_element_type=jnp.float32)
        y_b = jnp.dot(ob_ref[rs, :], wb_ref[...], preferred_element_type=jnp.float32)
        mix = sigmoid(ga_ref[rs, :].astype(jnp.float32)) * y_a + sigmoid(gb_ref[rs, :].astype(jnp.float32)) * y_b
        out = jnp.dot(mix.astype(jnp.bfloat16), wo_ref[...], preferred_element_type=jnp.float32) + bo_ref[...]
        x = jnp.where(i < n0_tiles, xp_ref[rs, :], xs_ref[rs, :])
        x0 = _layer_norm(x, lng_ref[...], lnb_ref[...])
        x1 = _layer_norm(DN_ALPHA * x0 + out, g1_ref[...], b1_ref[...])
        _store_row_tiles(x1_ref, x1, row0=h * hm)

        r = jnp.dot(x1.astype(jnp.bfloat16), wr_ref[...], preferred_element_type=jnp.float32) + br_ref[...]
        rt = r.T
        lg = jnp.where(grow < N_GROUPS, rt[0:8], neg)
        gmax = jnp.max(lg, axis=0, keepdims=True)
        gi = jnp.min(jnp.where(lg == gmax, grow, 8), axis=0, keepdims=True)
        wg = 1.0 / jnp.sum(jnp.exp(lg - gmax), axis=0, keepdims=True)
        lo = EXPERTS_PER_GROUP * gi
        le = jnp.where((erow >= lo) & (erow < lo + EXPERTS_PER_GROUP), rt[EXPERT_LANE0:EXPERT_LANE0 + N_EXPERTS], neg)
        m1 = jnp.max(le, axis=0, keepdims=True)
        i1 = jnp.min(jnp.where(le == m1, erow, N_EXPERTS), axis=0, keepdims=True)
        le2 = jnp.where(erow == i1, neg, le)
        m2 = jnp.max(le2, axis=0, keepdims=True)
        i2 = jnp.min(jnp.where(le2 == m2, erow, N_EXPERTS), axis=0, keepdims=True)
        t2 = jnp.exp(m2 - m1)
        w1 = wg / (1.0 + t2)
        w2 = wg * t2 / (1.0 + t2)

        hot1 = erow == i1
        hot2 = erow == i2
        hot = (hot1 | hot2).astype(jnp.float32)
        before = jnp.dot(hot.astype(jnp.bfloat16), tri_ref[...], preferred_element_type=jnp.float32) + carry
        rank1 = jnp.sum(jnp.where(hot1, before, 0.0), axis=0, keepdims=True)
        rank2 = jnp.sum(jnp.where(hot2, before, 0.0), axis=0, keepdims=True)
        carry = carry + jnp.sum(hot, axis=1, keepdims=True)

        fields = (i1.astype(jnp.float32), i2.astype(jnp.float32), w1, w2, rank1, rank2)
        route_t = jnp.zeros((ROUTE_LANES, hm), jnp.float32)
        for idx, val in enumerate(fields):
            route_t = jnp.where(frow == idx, val, route_t)
        route_t_ref[:, rs] = route_t[0:route_t_ref.shape[0]]
        route_ref[rs, :] = route_t.T

    carry_ref[...] = jnp.broadcast_to(carry, carry_ref.shape)
    counts_ref[...] = jnp.broadcast_to(carry, counts_ref.shape)


def _post_attention(xp, xs, ln_in_g, ln_in_b, o_groups, lse_groups, ob, ga, gb, wa, wb, wo, bo, g1, b1, wr, br):
    n0, d = xp.shape
    n = n0 + xs.shape[0]
    tm = POST_TM
    n0_tiles = n0 // tm
    nt = n // tm
    hm = tm // POST_SPLIT
    tri = jnp.asarray(np.triu(np.ones((hm, hm), np.float32), 1), jnp.bfloat16)

    def tok(width):
        return pl.BlockSpec((tm, width), lambda i: (i, 0))

    def full(a):
        return pl.BlockSpec(a.shape, lambda i: (0,) * a.ndim)

    kern = functools.partial(_post_kernel, n0_tiles=n0_tiles)
    return pl.pallas_call(
        kern,
        grid=(nt,),
        in_specs=[
            pl.BlockSpec((tm, d), lambda i: (jnp.minimum(i, n0_tiles - 1), 0)),
            pl.BlockSpec((tm, d), lambda i: (jnp.maximum(i - n0_tiles, 0), 0)),
            full(ln_in_g), full(ln_in_b),
            *[pl.BlockSpec((1,) + a.shape[1:], lambda i: (i, 0, 0, 0)) for a in (*o_groups, *lse_groups)],
            tok(QB), tok(d), tok(d),
            full(wa), full(wb), full(wo), full(bo), full(g1), full(b1), full(wr), full(br), full(tri),
        ],
        out_specs=[pl.BlockSpec((tm * SUBLANES, LANES), lambda i: (i, 0)), tok(ROUTE_LANES),
                   pl.BlockSpec((8, tm), lambda i: (0, i)), pl.BlockSpec((N_EXPERTS, ROUTE_LANES), lambda i: (0, 0))],
        out_shape=[jax.ShapeDtypeStruct((n * SUBLANES, LANES), jnp.float32),
                   jax.ShapeDtypeStruct((n, ROUTE_LANES), jnp.float32),
                   jax.ShapeDtypeStruct((8, n), jnp.float32),
                   jax.ShapeDtypeStruct((N_EXPERTS, ROUTE_LANES), jnp.float32)],
        scratch_shapes=[pltpu.VMEM((N_EXPERTS, ROUTE_LANES), jnp.float32), pltpu.VMEM((12, tm, LANES), jnp.float32)],
        compiler_params=_cparams(("arbitrary",)),
        name="post_attention_router",
    )(xp, xs, ln_in_g, ln_in_b, *o_groups, *lse_groups, ob, ga, gb, wa, wb, wo, bo, g1, b1, wr, br, tri)


def _dispatch_kernel(pos0_ref, pos1_ref, x_ref, xs_ref, sem):
    tt = x_ref.shape[0] // SUBLANES

    def issue(j, carry):
        for u in range(2):
            t = 2 * j + u
            src = x_ref.at[pl.ds(pl.multiple_of(t * SUBLANES, SUBLANES), SUBLANES)]
            for p_ref in (pos0_ref, pos1_ref):
                p = p_ref[t]
                dst = xs_ref.at[pl.ds(pl.multiple_of(p * SUBLANES, SUBLANES), SUBLANES)]
                pltpu.make_async_copy(src, dst, sem).start(priority=u)
        return carry

    lax.fori_loop(0, tt // 2, issue, 0, unroll=4)
    for _ in range(2):
        pltpu.make_async_copy(x_ref, xs_ref.at[pl.ds(0, tt * SUBLANES)], sem).wait()


def _dispatch(x1t, pos, n_rows):
    n = x1t.shape[0] // SUBLANES
    tt = DISP_TT
    return pl.pallas_call(
        _dispatch_kernel,
        grid=(n // tt,),
        in_specs=[pl.BlockSpec((tt,), lambda i: (i,), memory_space=pltpu.SMEM),
                  pl.BlockSpec((tt,), lambda i: (i,), memory_space=pltpu.SMEM),
                  pl.BlockSpec((tt * SUBLANES, LANES), lambda i: (i, 0))],
        out_specs=pl.BlockSpec(memory_space=pl.ANY),
        out_shape=jax.ShapeDtypeStruct((n_rows * SUBLANES, LANES), x1t.dtype),
        scratch_shapes=[pltpu.SemaphoreType.DMA(())],
        compiler_params=_cparams(("arbitrary",)),
        name="moe_dispatch",
    )(pos[0], pos[1], x1t)


def _ffn_kernel(item_expert_ref, item_tile_ref, item_lo_ref, item_hi_ref, n_items_ref,
                x_ref, wg_ref, wu_ref, wd_ref, y_ref, wg_bf, wu_bf, wd_bf):
    j = pl.program_id(0)

    @pl.when(j < n_items_ref[0])
    def _():
        prev = jnp.maximum(j - 1, 0)

        @pl.when(jnp.logical_or(j == 0, item_expert_ref[j] != item_expert_ref[prev]))
        def _():
            wg_bf[...] = wg_ref[0].astype(jnp.bfloat16)
            wu_bf[...] = wu_ref[0].astype(jnp.bfloat16)
            wd_bf[...] = wd_ref[0].astype(jnp.bfloat16)

        tm = x_ref.shape[0] // SUBLANES
        x = _load_row_tiles(x_ref, tm).astype(jnp.bfloat16)
        a = jnp.dot(x, wg_bf[...], preferred_element_type=jnp.float32)
        u = jnp.dot(x, wu_bf[...], preferred_element_type=jnp.float32)
        h = (a * jax.nn.sigmoid(a) * u).astype(jnp.bfloat16)
        y = jnp.dot(h, wd_bf[...], preferred_element_type=jnp.float32)
        rows = lax.broadcasted_iota(jnp.int32, (tm, 1), 0)
        y = jnp.where((rows >= item_lo_ref[j]) & (rows < item_hi_ref[j]), y, 0.0)
        first_of_tile = jnp.logical_or(j == 0, item_tile_ref[j] != item_tile_ref[prev])

        @pl.when(first_of_tile)
        def _():
            _store_row_tiles(y_ref, y)

        @pl.when(jnp.logical_not(first_of_tile))
        def _():
            _store_row_tiles(y_ref, y, accumulate=True)


def _ffn_items(cnt, n_rows):
    tm = FFN_TM
    n_tiles = n_rows // tm
    max_items = n_tiles + N_EXPERTS - 1
    ends = jnp.cumsum(cnt)
    starts = ends - cnt
    first_tile = starts // tm
    last_tile = jnp.maximum(ends - 1, 0) // tm
    items_per_expert = jnp.where(cnt > 0, last_tile - first_tile + 1, 0)
    item_ends = jnp.cumsum(items_per_expert)
    n_items = item_ends[-1:]
    j = jnp.minimum(jnp.arange(max_items, dtype=jnp.int32), n_items[0] - 1)
    expert = jnp.minimum(jnp.sum(item_ends[None, :] <= j[:, None], axis=1), N_EXPERTS - 1).astype(jnp.int32)
    tile = first_tile[expert] + (j - (item_ends - items_per_expert)[expert])
    lo = jnp.clip(starts[expert] - tile * tm, 0, tm)
    hi = jnp.clip(ends[expert] - tile * tm, 0, tm)
    i32 = lambda a: a.astype(jnp.int32)
    return i32(expert), i32(tile), i32(lo), i32(hi), i32(n_items), starts


def _grouped_ffn(xs, items, w_gate, w_up, w_down):
    tm = FFN_TM
    d, de = w_gate.shape[1:]
    item_expert, item_tile, item_lo, item_hi, n_items = items
    max_items = item_expert.shape[0]

    def row_map(j, ie, it, lo, hi, ni):
        return (it[j], 0)

    def w_map(j, ie, it, lo, hi, ni):
        return (ie[j], 0, 0)

    grid_spec = pltpu.PrefetchScalarGridSpec(
        num_scalar_prefetch=5,
        grid=(max_items,),
        in_specs=[pl.BlockSpec((tm * SUBLANES, LANES), row_map),
                  pl.BlockSpec((1, d, de), w_map),
                  pl.BlockSpec((1, d, de), w_map),
                  pl.BlockSpec((1, de, d), w_map)],
        out_specs=pl.BlockSpec((tm * SUBLANES, LANES), row_map),
        scratch_shapes=[pltpu.VMEM((d, de), jnp.bfloat16), pltpu.VMEM((d, de), jnp.bfloat16),
                        pltpu.VMEM((de, d), jnp.bfloat16)],
    )
    return pl.pallas_call(
        _ffn_kernel,
        grid_spec=grid_spec,
        out_shape=jax.ShapeDtypeStruct(xs.shape, jnp.float32),
        compiler_params=_cparams(("arbitrary",)),
        name="moe_grouped_ffn",
    )(item_expert, item_tile, item_lo, item_hi, n_items, xs, w_gate, w_up, w_down)


def _combine_kernel(pos0_ref, pos1_ref, next0_ref, next1_ref, x1_ref, route_ref, g2_ref, b2_ref, ys_ref, o_ref,
                    buf, sems):
    tt = x1_ref.shape[0] // SUBLANES
    i = pl.program_id(0)
    cur = lax.rem(i, 2)

    def issue_tile(p_refs, b):
        def issue(j, carry):
            for u in range(2):
                t = 2 * j + u
                for slot, p_ref in enumerate(p_refs):
                    p = p_ref[t]
                    src = ys_ref.at[pl.ds(pl.multiple_of(p * SUBLANES, SUBLANES), SUBLANES)]
                    dst = buf.at[b, slot, pl.ds(pl.multiple_of(t * SUBLANES, SUBLANES), SUBLANES)]
                    pltpu.make_async_copy(src, dst, sems.at[b]).start(priority=u)
            return carry

        lax.fori_loop(0, tt // 2, issue, 0, unroll=4)

    @pl.when(i == 0)
    def _():
        issue_tile((pos0_ref, pos1_ref), 0)

    @pl.when(i + 1 < pl.num_programs(0))
    def _():
        issue_tile((next0_ref, next1_ref), 1 - cur)

    for slot in range(2):
        pltpu.make_async_copy(ys_ref.at[pl.ds(0, tt * SUBLANES)], buf.at[cur, slot], sems.at[cur]).wait()
    route = route_ref[...]
    y = route[:, 2:3] * _load_row_tiles(buf.at[cur, 0], tt) + route[:, 3:4] * _load_row_tiles(buf.at[cur, 1], tt)
    o_ref[...] = _layer_norm(DN_ALPHA * _load_row_tiles(x1_ref, tt) + y, g2_ref[...], b2_ref[...])


def _combine(x1t, route, pos, ys, g2, b2, tok0, n_tok):
    d = g2.shape[1]
    tt = COMB_TT
    t0 = tok0 // tt
    nt = n_tok // tt
    return pl.pallas_call(
        _combine_kernel,
        grid=(nt,),
        in_specs=[pl.BlockSpec((tt,), lambda i: (i + t0,), memory_space=pltpu.SMEM),
                  pl.BlockSpec((tt,), lambda i: (i + t0,), memory_space=pltpu.SMEM),
                  pl.BlockSpec((tt,), lambda i: (jnp.minimum(i + 1, nt - 1) + t0,), memory_space=pltpu.SMEM),
                  pl.BlockSpec((tt,), lambda i: (jnp.minimum(i + 1, nt - 1) + t0,), memory_space=pltpu.SMEM),
                  pl.BlockSpec((tt * SUBLANES, LANES), lambda i: (i + t0, 0)),
                  pl.BlockSpec((tt, ROUTE_LANES), lambda i: (i + t0, 0)),
                  pl.BlockSpec((1, d), lambda i: (0, 0)),
                  pl.BlockSpec((1, d), lambda i: (0, 0)),
                  pl.BlockSpec(memory_space=pl.ANY)],
        out_specs=pl.BlockSpec((tt, d), lambda i: (i, 0)),
        out_shape=jax.ShapeDtypeStruct((n_tok, d), jnp.float32),
        scratch_shapes=[pltpu.VMEM((2, 2, tt * SUBLANES, LANES), ys.dtype), pltpu.SemaphoreType.DMA((2,))],
        compiler_params=_cparams(("arbitrary",)),
        name="moe_combine_ln",
    )(pos[0], pos[1], pos[0], pos[1], x1t, route, g2, b2, ys)


def _layer(x_prompt, x_sample, ln_in_g, ln_in_b, w_in, b_in, rel_bias_t5, na_rpb, w_branch_a, w_branch_b,
           w_out, b_out, ln1_g, ln1_b, w_router_group, b_router_group, w_router_expert, b_router_expert,
           w_exp_gate, w_exp_up, w_exp_down, ln2_g, ln2_b):
    bp, tp, d = x_prompt.shape
    bs, ts, _ = x_sample.shape
    seg_tokens = (bp * tp, bs * ts)
    seq_lens = (tp, ts)
    n = sum(seg_tokens)
    xp = x_prompt.reshape(seg_tokens[0], d)
    xs = x_sample.reshape(seg_tokens[1], d)
    row = lambda a: a.reshape(1, -1).astype(jnp.float32)
    bf = lambda a: a.astype(jnp.bfloat16)

    scale = HEAD_DIM ** -0.5
    col_scale = np.ones((w_in.shape[2],), np.float32)
    col_scale[0:QA] = scale
    col_scale[3 * QA:3 * QA + QB] = scale
    w_proj = bf(w_in[0] * col_scale)
    b_proj = row(b_in[0] * col_scale)
    proj = _in_projection(xp, xs, row(ln_in_g), row(ln_in_b), w_proj, b_proj)
    qkv_a, (qb, kb, vb, ga, gb) = proj[:3 * N_DIL_GROUPS], proj[3 * N_DIL_GROUPS:]

    o_groups, lse_groups = [], []
    for g, (window, dil) in enumerate(DIL_PATTERNS):
        assert window // (2 * dil) == HALF_SPAN
        table = rel_bias_t5[:, g * HEADS_PER_DIL_GROUP:(g + 1) * HEADS_PER_DIL_GROUP].astype(jnp.float32).reshape(-1)
        o, lse = _dilated_group(qkv_a[g], qkv_a[N_DIL_GROUPS + g], qkv_a[2 * N_DIL_GROUPS + g], table, dil,
                                seg_tokens, seq_lens)
        o_groups.append(o)
        lse_groups.append(lse)
    ob = _neighborhood(qb, kb, vb, na_rpb[0].astype(jnp.float32).reshape(-1), seg_tokens, seq_lens)

    wr = jnp.zeros((d, ROUTE_LANES), jnp.float32)
    wr = wr.at[:, 0:N_GROUPS].set(w_router_group[0]).at[:, EXPERT_LANE0:EXPERT_LANE0 + N_EXPERTS].set(w_router_expert[0])
    br = jnp.zeros((1, ROUTE_LANES), jnp.float32)
    br = br.at[0, 0:N_GROUPS].set(b_router_group[0]).at[0, EXPERT_LANE0:EXPERT_LANE0 + N_EXPERTS].set(b_router_expert[0])
    x1, route, route_t, counts = _post_attention(
        xp, xs, row(ln_in_g), row(ln_in_b), o_groups, lse_groups, ob, ga, gb, bf(w_branch_a[0]), bf(w_branch_b[0]),
        bf(w_out[0]), row(b_out[0]), row(ln1_g[0]), row(ln1_b[0]), bf(wr), br)

    experts = route_t[0:2].astype(jnp.int32)
    ranks = route_t[4:6].astype(jnp.int32)
    cnt = counts[:, 0].astype(jnp.int32)
    n_rows = 2 * n
    *items, starts = _ffn_items(cnt, n_rows)
    pos = ranks
    for e in range(N_EXPERTS):
        pos = pos + jnp.where(experts == e, starts[e], 0)

    xsorted = _dispatch(x1, pos, n_rows)
    ys = _grouped_ffn(xsorted, items, w_exp_gate[0], w_exp_up[0], w_exp_down[0])
    y_prompt = _combine(x1, route, pos, ys, row(ln2_g[0]), row(ln2_b[0]), 0, seg_tokens[0])
    y_sample = _combine(x1, route, pos, ys, row(ln2_g[0]), row(ln2_b[0]), seg_tokens[0], seg_tokens[1])
    return y_prompt.reshape(bp, tp, d), y_sample.reshape(bs, ts, d)


def kernel(x_prompt, x_sample, ln_in_g, ln_in_b, w_in, b_in, rel_bias_t5, na_rpb, w_branch_a, w_branch_b, w_out, b_out, ln1_g, ln1_b, w_router_group, b_router_group, w_router_expert, b_router_expert, w_exp_gate, w_exp_up, w_exp_down, ln2_g, ln2_b):
    return _layer(x_prompt, x_sample, ln_in_g, ln_in_b, w_in, b_in, rel_bias_t5, na_rpb, w_branch_a, w_branch_b,
                  w_out, b_out, ln1_g, ln1_b, w_router_group, b_router_group, w_router_expert, b_router_expert,
                  w_exp_gate, w_exp_up, w_exp_down, ln2_g, ln2_b)
```

```python
import functools
import math

import numpy as np
import jax
import jax.numpy as jnp
from jax import lax
from jax.experimental import pallas as pl
from jax.experimental.pallas import tpu as pltpu

HEAD_DIM = 64
DIL_PATTERNS = ((128, 1), (512, 4), (2048, 16))
HEADS_PER_DIL_GROUP = 4
N_DIL_GROUPS = 3
H_A = N_DIL_GROUPS * HEADS_PER_DIL_GROUP
H_B = 8
QA = H_A * HEAD_DIM
QB = H_B * HEAD_DIM
HALF_SPAN = 64
GRID_W = 64
NA_KH = 8
NA_KW = 16
N_BUCKETS = 32
REL_MAX_DIST = 1024
N_GROUPS = 4
EXPERTS_PER_GROUP = 8
N_EXPERTS = N_GROUPS * EXPERTS_PER_GROUP
LN_EPS = 1e-5
NEG_INF = -1e30
DEPTH = 1
DN_ALPHA = (2.0 * DEPTH) ** 0.25

LANES = 128
HEAD_BLOCK = 4 * HEAD_DIM
VMEM_LIMIT_BYTES = 56 * 1024 * 1024

PROJ_TM = 512
DIL_M = 128
DIL_GROUP = 8
NA_ROWS = 16
POST_TM = 512
POST_SPLIT = 1
FFN_TM = 512
DISP_TT = 2048
COMB_TT = 512
ROUTE_LANES = LANES
EXPERT_LANE0 = 32


def _cparams(sem):
    return pltpu.CompilerParams(dimension_semantics=sem, vmem_limit_bytes=VMEM_LIMIT_BYTES)


SUBLANES = 8


def _store_row_tiles(ref, x, accumulate=False, row0=0):
    rows = x.shape[0]
    for c in range(SUBLANES):
        idx = pl.ds(row0 * SUBLANES + c, rows, stride=SUBLANES)
        piece = x[:, c * LANES:(c + 1) * LANES]
        ref[idx, :] = ref[idx, :] + piece if accumulate else piece


def _load_row_tiles(ref, rows, row0=0):
    return jnp.concatenate([ref[pl.ds(row0 * SUBLANES + c, rows, stride=SUBLANES), :] for c in range(SUBLANES)],
                           axis=1)


def _layer_norm(x, g, b):
    mu = jnp.mean(x, axis=-1, keepdims=True)
    xc = x - mu
    var = jnp.mean(xc * xc, axis=-1, keepdims=True)
    return xc * lax.rsqrt(var + LN_EPS) * g + b


def _strided_rows(scr, slab0, r, count, stride):
    return jnp.concatenate([scr[slab0 + s, pl.ds(r, count, stride=stride), :] for s in range(2)], axis=1)


def _inproj_kernel(xp_ref, xs_ref, g_ref, b_ref, w_ref, bias_ref, *refs, n0_tiles):
    out_refs, scr = refs[:-1], refs[-1]
    i = pl.program_id(0)
    x = jnp.where(i < n0_tiles, xp_ref[...], xs_ref[...])
    x0 = _layer_norm(x, g_ref[...], b_ref[...])
    out_refs[-1][...] = x0
    out_refs = out_refs[:-1]
    xn = x0.astype(jnp.bfloat16)
    tm = xn.shape[0]

    def proj(c0, cw):
        return jnp.dot(xn, w_ref[:, c0:c0 + cw], preferred_element_type=jnp.float32) + bias_ref[:, c0:c0 + cw]

    n_dil = 3 * N_DIL_GROUPS
    for idx in range(n_dil):
        g = idx % N_DIL_GROUPS
        dil = DIL_PATTERNS[g][1]
        o_ref = out_refs[idx]
        y = proj(idx * HEAD_BLOCK, HEAD_BLOCK)
        if dil == 1:
            o_ref[0, 0] = y.astype(o_ref.dtype)
        else:
            slab0 = 2 * (idx % 2)
            scr[slab0] = y[:, :LANES]
            scr[slab0 + 1] = y[:, LANES:]
            for r in range(dil):
                o_ref[0, r] = _strided_rows(scr, slab0, r, tm // dil, dil).astype(o_ref.dtype)
    col = n_dil * HEAD_BLOCK
    for o_ref in out_refs[n_dil:]:
        width = o_ref.shape[1]
        for c in range(0, width, 512):
            cw = min(512, width - c)
            o_ref[:, c:c + cw] = proj(col + c, cw).astype(o_ref.dtype)
        col += width


def _in_projection(xp, xs, ln_g, ln_b, w, b):
    n0, d_model = xp.shape
    n1 = xs.shape[0]
    n = n0 + n1
    tm = PROJ_TM
    n0_tiles, n1_tiles = n0 // tm, n1 // tm
    nt = n0_tiles + n1_tiles
    out_shape, out_specs = [], []
    for _ in range(3):
        for _, dil in DIL_PATTERNS:
            out_shape.append(jax.ShapeDtypeStruct((nt, dil, tm // dil, HEAD_BLOCK), jnp.bfloat16))
            out_specs.append(pl.BlockSpec((1, dil, tm // dil, HEAD_BLOCK), lambda i: (i, 0, 0, 0)))
    for wd, dtype in ((QB, jnp.bfloat16),) * 3 + ((d_model, jnp.bfloat16),) * 2 + ((d_model, jnp.float32),):
        out_shape.append(jax.ShapeDtypeStruct((n, wd), dtype))
        out_specs.append(pl.BlockSpec((tm, wd), lambda i: (i, 0)))
    d_in = w.shape[1]
    kern = functools.partial(_inproj_kernel, n0_tiles=n0_tiles)
    return pl.pallas_call(
        kern,
        grid=(nt,),
        in_specs=[
            pl.BlockSpec((tm, d_model), lambda i: (jnp.minimum(i, n0_tiles - 1), 0)),
            pl.BlockSpec((tm, d_model), lambda i: (jnp.maximum(i - n0_tiles, 0), 0)),
            pl.BlockSpec((1, d_model), lambda i: (0, 0)),
            pl.BlockSpec((1, d_model), lambda i: (0, 0)),
            pl.BlockSpec((d_model, d_in), lambda i: (0, 0)),
            pl.BlockSpec((1, d_in), lambda i: (0, 0)),
        ],
        out_specs=out_specs,
        out_shape=out_shape,
        scratch_shapes=[pltpu.VMEM((4, tm, LANES), jnp.float32)],
        compiler_params=_cparams(("arbitrary",)),
        name="in_projection",
    )(xp, xs, ln_g, ln_b, w, b)


def _t5_bucket_np(rel):
    half = N_BUCKETS // 2
    max_exact = half // 2
    ret = np.where(rel > 0, half, 0)
    n = np.abs(rel)
    nf = np.maximum(n, 1).astype(np.float32)
    large = max_exact + (np.log(nf / np.float32(max_exact)) / np.float32(math.log(REL_MAX_DIST / max_exact))
                         * np.float32(half - max_exact)).astype(np.int32)
    large = np.minimum(large, half - 1)
    return ret + np.where(n < max_exact, n, large)


def _t5_bucket_starts():
    half = N_BUCKETS // 2
    dist = np.arange(0, HALF_SPAN * max(d for _, d in DIL_PATTERNS) + 1)
    buckets = _t5_bucket_np(-dist)
    return [int(np.argmax(buckets >= k)) for k in range(half)]


def _build_dilated_bias(table_ref, bias_scr, dil):
    half = N_BUCKETS // 2
    starts = _t5_bucket_starts()
    kb = DIL_M + 2 * HALF_SPAN
    i = lax.broadcasted_iota(jnp.int32, (DIL_M, kb), 0)
    m = lax.broadcasted_iota(jnp.int32, (DIL_M, kb), 1)
    delta = m - HALF_SPAN - i
    dist = jnp.abs(delta) * dil
    band = jnp.abs(delta) <= HALF_SPAN
    for h in range(HEADS_PER_DIL_GROUP):
        sides = []
        for side in range(2):
            val = jnp.full((DIL_M, kb), table_ref[(side * half + half - 1) * HEADS_PER_DIL_GROUP + h], jnp.float32)
            for k in range(half - 1, 0, -1):
                val = jnp.where(dist < starts[k], table_ref[(side * half + k - 1) * HEADS_PER_DIL_GROUP + h], val)
            sides.append(val)
        base = jnp.where(band, jnp.where(delta > 0, sides[1], sides[0]), NEG_INF)
        rows = slice(h * DIL_M, (h + 1) * DIL_M)
        bias_scr[0, rows, :] = base
        bias_scr[1, rows, :] = jnp.where(m >= HALF_SPAN, base, NEG_INF)
        bias_scr[2, rows, :] = jnp.where(m < kb - HALF_SPAN, base, NEG_INF)


def _stack_heads(q):
    head_of_col = lax.broadcasted_iota(jnp.int32, (1, HEAD_BLOCK), 1) // HEAD_DIM
    zero = jnp.zeros_like(q)
    return jnp.concatenate([jnp.where(head_of_col == h, q, zero) for h in range(4)], axis=0)


def _merge_heads(x, m):
    head_of_col = lax.broadcasted_iota(jnp.int32, (1, HEAD_BLOCK), 1) // HEAD_DIM
    out = jnp.zeros((m, HEAD_BLOCK), x.dtype)
    for h in range(4):
        out = jnp.where(head_of_col == h, x[h * m:(h + 1) * m], out)
    return out


def _softmax_pv(s, v, m_rows):
    mx = jnp.max(s, axis=-1, keepdims=True)
    p = jnp.exp(s - mx)
    l = jnp.sum(p, axis=-1, keepdims=True)
    pv = jnp.dot(p.astype(jnp.bfloat16), v, preferred_element_type=jnp.float32)
    o = _merge_heads(pv * (1.0 / l), m_rows)
    return o, mx + jnp.log(l)


def _dilated_kernel(table_ref, q_ref, kp_ref, kc_ref, kn_ref, vp_ref, vc_ref, vn_ref,
                    o_ref, lse_ref, bias_scr, *, dil, seg0_blocks, blocks_per_seq):
    step = pl.program_id(1)

    @pl.when((pl.program_id(0) == 0) & (step == 0))
    def _():
        _build_dilated_bias(table_ref, bias_scr, dil)

    rows2d = lambda ref: ref[...].reshape(-1, HEAD_BLOCK)
    k = jnp.concatenate([rows2d(r) for r in (kp_ref, kc_ref, kn_ref)], axis=0)
    v = jnp.concatenate([rows2d(r) for r in (vp_ref, vc_ref, vn_ref)], axis=0)
    q = rows2d(q_ref)
    o_parts, lse_parts = [], []
    for g in range(DIL_GROUP):
        jb = step * DIL_GROUP + g
        in0 = jb < seg0_blocks
        jl = jnp.where(in0, jb, jb - seg0_blocks)
        nbs = jnp.where(in0, blocks_per_seq[0], blocks_per_seq[1])
        pos = lax.rem(jl, nbs)
        variant = jnp.where(pos == 0, 1, jnp.where(pos == nbs - 1, 2, 0))
        keys = slice(g * DIL_M, (g + 1) * DIL_M + 2 * HALF_SPAN)
        qs = _stack_heads(q[g * DIL_M:(g + 1) * DIL_M])
        s = lax.dot_general(qs, k[keys], (((1,), (1,)), ((), ())), preferred_element_type=jnp.float32)
        s = s + bias_scr[variant]
        o, lse = _softmax_pv(s, v[keys], DIL_M)
        o_parts.append(o.astype(o_ref.dtype))
        lse_parts.append(_merge_heads(jnp.broadcast_to(lse, (4 * DIL_M, HEAD_BLOCK)), DIL_M))
    o_ref[...] = jnp.concatenate(o_parts, axis=0).reshape(o_ref.shape)
    lse_ref[...] = jnp.concatenate(lse_parts, axis=0).reshape(lse_ref.shape)


def _residue_spec(rows_blk, dil, block_of):
    per_tile = PROJ_TM // dil
    if per_tile >= rows_blk:
        per = per_tile // rows_blk
        return pl.BlockSpec((1, None, rows_blk, HEAD_BLOCK),
                            lambda r, jb: (block_of(jb) // per, r, block_of(jb) % per, 0))
    return pl.BlockSpec((rows_blk // per_tile, None, per_tile, HEAD_BLOCK), lambda r, jb: (block_of(jb), r, 0, 0))


def _dilated_group(q, k, v, table, dil, seg_tokens, seq_lens):
    n = q.shape[0] * PROJ_TM
    rows = n // dil
    step_rows = DIL_GROUP * DIL_M
    assert rows % step_rows == 0 and all(t % (dil * DIL_M) == 0 for t in seq_lens)
    nsteps = rows // step_rows
    nkb = rows // HALF_SPAN
    ratio = step_rows // HALF_SPAN
    seg0_blocks = seg_tokens[0] // dil // DIL_M
    blocks_per_seq = tuple(t // dil // DIL_M for t in seq_lens)
    assert min(blocks_per_seq) >= 2

    qo_spec = _residue_spec(step_rows, dil, lambda jb: jb)
    prev_spec = _residue_spec(HALF_SPAN, dil, lambda jb: jnp.maximum(ratio * jb - 1, 0))
    next_spec = _residue_spec(HALF_SPAN, dil, lambda jb: jnp.minimum(ratio * (jb + 1), nkb - 1))
    kv_specs = [prev_spec, qo_spec, next_spec]
    kern = functools.partial(_dilated_kernel, dil=dil, seg0_blocks=seg0_blocks, blocks_per_seq=blocks_per_seq)
    return pl.pallas_call(
        kern,
        grid=(dil, nsteps),
        in_specs=[pl.BlockSpec(memory_space=pltpu.SMEM), qo_spec] + kv_specs + kv_specs,
        out_specs=[qo_spec, qo_spec],
        out_shape=[jax.ShapeDtypeStruct(q.shape, jnp.bfloat16), jax.ShapeDtypeStruct(q.shape, jnp.float32)],
        scratch_shapes=[pltpu.VMEM((3, HEADS_PER_DIL_GROUP * DIL_M, DIL_M + 2 * HALF_SPAN), jnp.float32)],
        compiler_params=_cparams(("arbitrary", "arbitrary")),
        name=f"dilated_attention_d{dil}",
    )(table, q, k, k, k, v, v, v)


NA_DR = 2 * NA_KH - 1
NA_DC = 2 * NA_KW - 1
NA_PAIRS = NA_DR - 1


def _build_na_bias(rpb_ref, pair_scr, head0):
    qc = lax.broadcasted_iota(jnp.int32, (GRID_W, 2 * GRID_W), 0)
    lane = lax.broadcasted_iota(jnp.int32, (GRID_W, 2 * GRID_W), 1)
    kc = lane % GRID_W
    upper = lane >= GRID_W
    qstart = jnp.clip(qc - NA_KW // 2, 0, GRID_W - NA_KW)
    valid = (kc >= qstart) & (kc < qstart + NA_KW)
    dc = jnp.clip(kc - qc, -(NA_KW - 1), NA_KW - 1) + NA_KW - 1

    def one_pair(idx, carry):
        h = idx // NA_PAIRS
        dr = idx % NA_PAIRS
        base = ((head0 + h) * NA_DR + dr) * NA_DC
        val = jnp.zeros((GRID_W, 2 * GRID_W), jnp.float32)
        for c in range(NA_DC):
            val = jnp.where(dc == c, jnp.where(upper, rpb_ref[base + NA_DC + c], rpb_ref[base + c]), val)
        pair_scr[h, dr] = jnp.where(valid, val, NEG_INF)
        return carry

    lax.fori_loop(0, 4 * NA_PAIRS, one_pair, 0)


def _na_kernel(rpb_ref, q_ref, kp_ref, kc_ref, kn_ref, vp_ref, vc_ref, vn_ref, o_ref, k_scr, v_scr, pair_scr,
               *, seg0_blocks, blocks_per_seq):
    ib = pl.program_id(1)

    @pl.when(ib == 0)
    def _():
        _build_na_bias(rpb_ref, pair_scr, pl.program_id(0) * 4)

    in0 = ib < seg0_blocks
    il = jnp.where(in0, ib, ib - seg0_blocks)
    nbs = jnp.where(in0, blocks_per_seq[0], blocks_per_seq[1])
    pos = lax.rem(il, nbs)
    first = pos == 0
    last = pos == nbs - 1
    blk = NA_ROWS * GRID_W
    k_scr[0:blk] = kp_ref[...]
    k_scr[blk:2 * blk] = kc_ref[...]
    k_scr[2 * blk:3 * blk] = kn_ref[...]
    v_scr[0:blk] = vp_ref[...]
    v_scr[blk:2 * blk] = vc_ref[...]
    v_scr[2 * blk:3 * blk] = vn_ref[...]
    half = NA_KH // 2
    for rr in range(NA_ROWS):
        inner = NA_ROWS + rr - half
        s_first, s_last = max(inner, NA_ROWS), min(inner, 2 * NA_ROWS - NA_KH)
        start = jnp.where(first, s_first, jnp.where(last, s_last, inner))
        var = jnp.where(first, NA_ROWS + rr - s_first, jnp.where(last, NA_ROWS + rr - s_last, half))
        off = pl.multiple_of(start * GRID_W, GRID_W)
        kk = k_scr[pl.ds(off, NA_KH * GRID_W), :]
        vv = v_scr[pl.ds(off, NA_KH * GRID_W), :]
        qs = _stack_heads(q_ref[rr * GRID_W:(rr + 1) * GRID_W, :])
        s = lax.dot_general(qs, kk, (((1,), (1,)), ((), ())), preferred_element_type=jnp.float32)
        bias = jnp.concatenate(
            [jnp.concatenate([pair_scr[h, 2 * p - var + NA_KH - 1] for p in range(NA_KH // 2)], axis=1)
             for h in range(4)], axis=0)
        s = s + bias
        o, _ = _softmax_pv(s, vv, GRID_W)
        o_ref[rr * GRID_W:(rr + 1) * GRID_W, :] = o.astype(o_ref.dtype)


def _neighborhood(qb, kb, vb, rpb_flat, seg_tokens, seq_lens):
    n = qb.shape[0]
    blk = NA_ROWS * GRID_W
    nblk = n // blk
    seg0_blocks = seg_tokens[0] // blk
    blocks_per_seq = tuple(t // blk for t in seq_lens)
    assert min(blocks_per_seq) >= 2
    nset = QB // HEAD_BLOCK

    def kv_spec(t):
        return pl.BlockSpec((blk, HEAD_BLOCK), lambda c, ib: (jnp.clip(ib - 1 + t, 0, nblk - 1), c))

    kern = functools.partial(_na_kernel, seg0_blocks=seg0_blocks, blocks_per_seq=blocks_per_seq)
    return pl.pallas_call(
        kern,
        grid=(nset, nblk),
        in_specs=[pl.BlockSpec(memory_space=pltpu.SMEM), pl.BlockSpec((blk, HEAD_BLOCK), lambda c, ib: (ib, c))]
        + [kv_spec(t) for t in range(3)] + [kv_spec(t) for t in range(3)],
        out_specs=pl.BlockSpec((blk, HEAD_BLOCK), lambda c, ib: (ib, c)),
        out_shape=jax.ShapeDtypeStruct((n, QB), jnp.bfloat16),
        scratch_shapes=[pltpu.VMEM((3 * blk, HEAD_BLOCK), jnp.bfloat16),
                        pltpu.VMEM((3 * blk, HEAD_BLOCK), jnp.bfloat16),
                        pltpu.VMEM((4, NA_PAIRS, GRID_W, 2 * GRID_W), jnp.float32)],
        compiler_params=_cparams(("arbitrary", "arbitrary")),
        name="neighborhood_attention",
    )(rpb_flat, qb, kb, kb, kb, vb, vb, vb)


def _post_kernel(x0_ref, o0_ref, o1_ref, o2_ref, l0_ref, l1_ref, l2_ref, ob_ref,
                 ga_ref, gb_ref, wa_ref, wb_ref, wo_ref, bo_ref, g1_ref, b1_ref, wr_ref, br_ref, tri_ref,
                 x1_ref, route_ref, route_t_ref, counts_ref, carry_ref, perm_scr):
    i = pl.program_id(0)
    tm = route_ref.shape[0]
    hm = tm // POST_SPLIT

    @pl.when(i == 0)
    def _():
        carry_ref[...] = jnp.zeros_like(carry_ref)

    def token_order(ref, slab0):
        dil, per = ref.shape[1], ref.shape[2]
        if dil == 1:
            return lambda rs: ref[0, 0, rs, :].astype(jnp.float32)
        for r in range(dil):
            val = ref[0, r].astype(jnp.float32)
            for s in range(2):
                perm_scr[slab0 + s, pl.ds(r, per, stride=dil), :] = val[:, s * LANES:(s + 1) * LANES]
        return lambda rs: jnp.concatenate([perm_scr[slab0, rs, :], perm_scr[slab0 + 1, rs, :]], axis=1)

    lse_of = [token_order(ref, 2 * j) for j, ref in enumerate((l0_ref, l1_ref, l2_ref))]
    o_of = [token_order(ref, 6 + 2 * j) for j, ref in enumerate((o0_ref, o1_ref, o2_ref))]
    sigmoid = lambda v: 0.5 * jnp.tanh(0.5 * v) + 0.5
    grow = lax.broadcasted_iota(jnp.int32, (8, hm), 0)
    erow = lax.broadcasted_iota(jnp.int32, (N_EXPERTS, hm), 0)
    frow = lax.broadcasted_iota(jnp.int32, (ROUTE_LANES, hm), 0)
    neg = jnp.float32(-jnp.inf)
    carry = carry_ref[:, 0:1]

    for h in range(POST_SPLIT):
        rs = slice(h * hm, (h + 1) * hm)
        l0, l1, l2 = (f(rs) for f in lse_of)
        lm = jnp.maximum(jnp.maximum(l0, l1), l2)
        e0, e1, e2 = jnp.exp(l0 - lm), jnp.exp(l1 - lm), jnp.exp(l2 - lm)
        inv = 1.0 / (e0 + e1 + e2)
        o_a = (e0 * inv) * o_of[0](rs) + (e1 * inv) * o_of[1](rs) + (e2 * inv) * o_of[2](rs)
        y_a = jnp.dot(o_a.astype(jnp.bfloat16), wa_ref[...], preferred_element_type=jnp.float32)
        y_b = jnp.dot(ob_ref[rs, :], wb_ref[...], preferred_element_type=jnp.float32)
        mix = sigmoid(ga_ref[rs, :].astype(jnp.float32)) * y_a + sigmoid(gb_ref[rs, :].astype(jnp.float32)) * y_b
        out = jnp.dot(mix.astype(jnp.bfloat16), wo_ref[...], preferred_element_type=jnp.float32) + bo_ref[...]
        x1 = _layer_norm(DN_ALPHA * x0_ref[rs, :] + out, g1_ref[...], b1_ref[...])
        _store_row_tiles(x1_ref, x1, row0=h * hm)

        r = jnp.dot(x1.astype(jnp.bfloat16), wr_ref[...], preferred_element_type=jnp.float32) + br_ref[...]
        rt = r.T
        lg = jnp.where(grow < N_GROUPS, rt[0:8], neg)
        gmax = jnp.max(lg, axis=0, keepdims=True)
        gi = jnp.min(jnp.where(lg == gmax, grow, 8), axis=0, keepdims=True)
        wg = 1.0 / jnp.sum(jnp.exp(lg - gmax), axis=0, keepdims=True)
        lo = EXPERTS_PER_GROUP * gi
        le = jnp.where((erow >= lo) & (erow < lo + EXPERTS_PER_GROUP), rt[EXPERT_LANE0:EXPERT_LANE0 + N_EXPERTS], neg)
        m1 = jnp.max(le, axis=0, keepdims=True)
        i1 = jnp.min(jnp.where(le == m1, erow, N_EXPERTS), axis=0, keepdims=True)
        le2 = jnp.where(erow == i1, neg, le)
        m2 = jnp.max(le2, axis=0, keepdims=True)
        i2 = jnp.min(jnp.where(le2 == m2, erow, N_EXPERTS), axis=0, keepdims=True)
        t2 = jnp.exp(m2 - m1)
        w1 = wg / (1.0 + t2)
        w2 = wg * t2 / (1.0 + t2)

        hot1 = erow == i1
        hot2 = erow == i2
        hot = (hot1 | hot2).astype(jnp.float32)
        before = jnp.dot(hot.astype(jnp.bfloat16), tri_ref[...], preferred_element_type=jnp.float32) + carry
        rank1 = jnp.sum(jnp.where(hot1, before, 0.0), axis=0, keepdims=True)
        rank2 = jnp.sum(jnp.where(hot2, before, 0.0), axis=0, keepdims=True)
        carry = carry + jnp.sum(hot, axis=1, keepdims=True)

        fields = (i1.astype(jnp.float32), i2.astype(jnp.float32), w1, w2, rank1, rank2)
        route_t = jnp.zeros((ROUTE_LANES, hm), jnp.float32)
        for idx, val in enumerate(fields):
            route_t = jnp.where(frow == idx, val, route_t)
        route_t_ref[:, rs] = route_t[0:route_t_ref.shape[0]]
        route_ref[rs, :] = route_t.T

    carry_ref[...] = jnp.broadcast_to(carry, carry_ref.shape)
    counts_ref[...] = jnp.broadcast_to(carry, counts_ref.shape)


def _post_attention(x0, o_groups, lse_groups, ob, ga, gb, wa, wb, wo, bo, g1, b1, wr, br):
    n, d = x0.shape
    tm = POST_TM
    nt = n // tm
    hm = tm // POST_SPLIT
    tri = jnp.asarray(np.triu(np.ones((hm, hm), np.float32), 1), jnp.bfloat16)

    def tok(width):
        return pl.BlockSpec((tm, width), lambda i: (i, 0))

    def full(a):
        return pl.BlockSpec(a.shape, lambda i: (0,) * a.ndim)

    return pl.pallas_call(
        _post_kernel,
        grid=(nt,),
        in_specs=[
            tok(d),
            *[pl.BlockSpec((1,) + a.shape[1:], lambda i: (i, 0, 0, 0)) for a in (*o_groups, *lse_groups)],
            tok(QB), tok(d), tok(d),
            full(wa), full(wb), full(wo), full(bo), full(g1), full(b1), full(wr), full(br), full(tri),
        ],
        out_specs=[pl.BlockSpec((tm * SUBLANES, LANES), lambda i: (i, 0)), tok(ROUTE_LANES),
                   pl.BlockSpec((8, tm), lambda i: (0, i)), pl.BlockSpec((N_EXPERTS, ROUTE_LANES), lambda i: (0, 0))],
        out_shape=[jax.ShapeDtypeStruct((n * SUBLANES, LANES), jnp.float32),
                   jax.ShapeDtypeStruct((n, ROUTE_LANES), jnp.float32),
                   jax.ShapeDtypeStruct((8, n), jnp.float32),
                   jax.ShapeDtypeStruct((N_EXPERTS, ROUTE_LANES), jnp.float32)],
        scratch_shapes=[pltpu.VMEM((N_EXPERTS, ROUTE_LANES), jnp.float32), pltpu.VMEM((12, tm, LANES), jnp.float32)],
        compiler_params=_cparams(("arbitrary",)),
        name="post_attention_router",
    )(x0, *o_groups, *lse_groups, ob, ga, gb, wa, wb, wo, bo, g1, b1, wr, br, tri)


def _dispatch_kernel(pos0_ref, pos1_ref, x_ref, xs_ref, sem):
    tt = x_ref.shape[0] // SUBLANES

    def issue(j, carry):
        for u in range(2):
            t = 2 * j + u
            src = x_ref.at[pl.ds(pl.multiple_of(t * SUBLANES, SUBLANES), SUBLANES)]
            for p_ref in (pos0_ref, pos1_ref):
                p = p_ref[t]
                dst = xs_ref.at[pl.ds(pl.multiple_of(p * SUBLANES, SUBLANES), SUBLANES)]
                pltpu.make_async_copy(src, dst, sem).start(priority=u)
        return carry

    lax.fori_loop(0, tt // 2, issue, 0, unroll=4)
    for _ in range(2):
        pltpu.make_async_copy(x_ref, xs_ref.at[pl.ds(0, tt * SUBLANES)], sem).wait()


def _dispatch(x1t, pos, n_rows):
    n = x1t.shape[0] // SUBLANES
    tt = DISP_TT
    return pl.pallas_call(
        _dispatch_kernel,
        grid=(n // tt,),
        in_specs=[pl.BlockSpec((tt,), lambda i: (i,), memory_space=pltpu.SMEM),
                  pl.BlockSpec((tt,), lambda i: (i,), memory_space=pltpu.SMEM),
                  pl.BlockSpec((tt * SUBLANES, LANES), lambda i: (i, 0))],
        out_specs=pl.BlockSpec(memory_space=pl.ANY),
        out_shape=jax.ShapeDtypeStruct((n_rows * SUBLANES, LANES), x1t.dtype),
        scratch_shapes=[pltpu.SemaphoreType.DMA(())],
        compiler_params=_cparams(("arbitrary",)),
        name="moe_dispatch",
    )(pos[0], pos[1], x1t)


def _ffn_kernel(item_expert_ref, item_tile_ref, item_lo_ref, item_hi_ref, n_items_ref,
                x_ref, wg_ref, wu_ref, wd_ref, y_ref, wg_bf, wu_bf, wd_bf):
    j = pl.program_id(0)

    @pl.when(j < n_items_ref[0])
    def _():
        prev = jnp.maximum(j - 1, 0)

        @pl.when(jnp.logical_or(j == 0, item_expert_ref[j] != item_expert_ref[prev]))
        def _():
            wg_bf[...] = wg_ref[0].astype(jnp.bfloat16)
            wu_bf[...] = wu_ref[0].astype(jnp.bfloat16)
            wd_bf[...] = wd_ref[0].astype(jnp.bfloat16)

        tm = x_ref.shape[0] // SUBLANES
        x = _load_row_tiles(x_ref, tm).astype(jnp.bfloat16)
        a = jnp.dot(x, wg_bf[...], preferred_element_type=jnp.float32)
        u = jnp.dot(x, wu_bf[...], preferred_element_type=jnp.float32)
        h = (a * jax.nn.sigmoid(a) * u).astype(jnp.bfloat16)
        y = jnp.dot(h, wd_bf[...], preferred_element_type=jnp.float32)
        rows = lax.broadcasted_iota(jnp.int32, (tm, 1), 0)
        y = jnp.where((rows >= item_lo_ref[j]) & (rows < item_hi_ref[j]), y, 0.0)
        first_of_tile = jnp.logical_or(j == 0, item_tile_ref[j] != item_tile_ref[prev])

        @pl.when(first_of_tile)
        def _():
            _store_row_tiles(y_ref, y)

        @pl.when(jnp.logical_not(first_of_tile))
        def _():
            _store_row_tiles(y_ref, y, accumulate=True)


def _ffn_items(cnt, n_rows):
    tm = FFN_TM
    n_tiles = n_rows // tm
    max_items = n_tiles + N_EXPERTS - 1
    ends = jnp.cumsum(cnt)
    starts = ends - cnt
    first_tile = starts // tm
    last_tile = jnp.maximum(ends - 1, 0) // tm
    items_per_expert = jnp.where(cnt > 0, last_tile - first_tile + 1, 0)
    item_ends = jnp.cumsum(items_per_expert)
    n_items = item_ends[-1:]
    j = jnp.minimum(jnp.arange(max_items, dtype=jnp.int32), n_items[0] - 1)
    expert = jnp.minimum(jnp.sum(item_ends[None, :] <= j[:, None], axis=1), N_EXPERTS - 1).astype(jnp.int32)
    tile = first_tile[expert] + (j - (item_ends - items_per_expert)[expert])
    lo = jnp.clip(starts[expert] - tile * tm, 0, tm)
    hi = jnp.clip(ends[expert] - tile * tm, 0, tm)
    i32 = lambda a: a.astype(jnp.int32)
    return i32(expert), i32(tile), i32(lo), i32(hi), i32(n_items), starts


def _grouped_ffn(xs, items, w_gate, w_up, w_down):
    tm = FFN_TM
    d, de = w_gate.shape[1:]
    item_expert, item_tile, item_lo, item_hi, n_items = items
    max_items = item_expert.shape[0]

    def row_map(j, ie, it, lo, hi, ni):
        return (it[j], 0)

    def w_map(j, ie, it, lo, hi, ni):
        return (ie[j], 0, 0)

    grid_spec = pltpu.PrefetchScalarGridSpec(
        num_scalar_prefetch=5,
        grid=(max_items,),
        in_specs=[pl.BlockSpec((tm * SUBLANES, LANES), row_map),
                  pl.BlockSpec((1, d, de), w_map),
                  pl.BlockSpec((1, d, de), w_map),
                  pl.BlockSpec((1, de, d), w_map)],
        out_specs=pl.BlockSpec((tm * SUBLANES, LANES), row_map),
        scratch_shapes=[pltpu.VMEM((d, de), jnp.bfloat16), pltpu.VMEM((d, de), jnp.bfloat16),
                        pltpu.VMEM((de, d), jnp.bfloat16)],
    )
    return pl.pallas_call(
        _ffn_kernel,
        grid_spec=grid_spec,
        out_shape=jax.ShapeDtypeStruct(xs.shape, jnp.float32),
        compiler_params=_cparams(("arbitrary",)),
        name="moe_grouped_ffn",
    )(item_expert, item_tile, item_lo, item_hi, n_items, xs, w_gate, w_up, w_down)


def _combine_kernel(pos0_ref, pos1_ref, next0_ref, next1_ref, x1_ref, route_ref, g2_ref, b2_ref, ys_ref, o_ref,
                    buf, sems):
    tt = x1_ref.shape[0] // SUBLANES
    i = pl.program_id(0)
    cur = lax.rem(i, 2)

    def issue_tile(p_refs, b):
        def issue(j, carry):
            for u in range(2):
                t = 2 * j + u
                for slot, p_ref in enumerate(p_refs):
                    p = p_ref[t]
                    src = ys_ref.at[pl.ds(pl.multiple_of(p * SUBLANES, SUBLANES), SUBLANES)]
                    dst = buf.at[b, slot, pl.ds(pl.multiple_of(t * SUBLANES, SUBLANES), SUBLANES)]
                    pltpu.make_async_copy(src, dst, sems.at[b]).start(priority=u)
            return carry

        lax.fori_loop(0, tt // 2, issue, 0, unroll=4)

    @pl.when(i == 0)
    def _():
        issue_tile((pos0_ref, pos1_ref), 0)

    @pl.when(i + 1 < pl.num_programs(0))
    def _():
        issue_tile((next0_ref, next1_ref), 1 - cur)

    for slot in range(2):
        pltpu.make_async_copy(ys_ref.at[pl.ds(0, tt * SUBLANES)], buf.at[cur, slot], sems.at[cur]).wait()
    route = route_ref[...]
    y = route[:, 2:3] * _load_row_tiles(buf.at[cur, 0], tt) + route[:, 3:4] * _load_row_tiles(buf.at[cur, 1], tt)
    o_ref[...] = _layer_norm(DN_ALPHA * _load_row_tiles(x1_ref, tt) + y, g2_ref[...], b2_ref[...])


def _combine(x1t, route, pos, ys, g2, b2, tok0, n_tok):
    d = g2.shape[1]
    tt = COMB_TT
    t0 = tok0 // tt
    nt = n_tok // tt
    return pl.pallas_call(
        _combine_kernel,
        grid=(nt,),
        in_specs=[pl.BlockSpec((tt,), lambda i: (i + t0,), memory_space=pltpu.SMEM),
                  pl.BlockSpec((tt,), lambda i: (i + t0,), memory_space=pltpu.SMEM),
                  pl.BlockSpec((tt,), lambda i: (jnp.minimum(i + 1, nt - 1) + t0,), memory_space=pltpu.SMEM),
                  pl.BlockSpec((tt,), lambda i: (jnp.minimum(i + 1, nt - 1) + t0,), memory_space=pltpu.SMEM),
                  pl.BlockSpec((tt * SUBLANES, LANES), lambda i: (i + t0, 0)),
                  pl.BlockSpec((tt, ROUTE_LANES), lambda i: (i + t0, 0)),
                  pl.BlockSpec((1, d), lambda i: (0, 0)),
                  pl.BlockSpec((1, d), lambda i: (0, 0)),
                  pl.BlockSpec(memory_space=pl.ANY)],
        out_specs=pl.BlockSpec((tt, d), lambda i: (i, 0)),
        out_shape=jax.ShapeDtypeStruct((n_tok, d), jnp.float32),
        scratch_shapes=[pltpu.VMEM((2, 2, tt * SUBLANES, LANES), ys.dtype), pltpu.SemaphoreType.DMA((2,))],
        compiler_params=_cparams(("arbitrary",)),
        name="moe_combine_ln",
    )(pos[0], pos[1], pos[0], pos[1], x1t, route, g2, b2, ys)


def _layer(x_prompt, x_sample, ln_in_g, ln_in_b, w_in, b_in, rel_bias_t5, na_rpb, w_branch_a, w_branch_b,
           w_out, b_out, ln1_g, ln1_b, w_router_group, b_router_group, w_router_expert, b_router_expert,
           w_exp_gate, w_exp_up, w_exp_down, ln2_g, ln2_b):
    bp, tp, d = x_prompt.shape
    bs, ts, _ = x_sample.shape
    seg_tokens = (bp * tp, bs * ts)
    seq_lens = (tp, ts)
    n = sum(seg_tokens)
    xp = x_prompt.reshape(seg_tokens[0], d)
    xs = x_sample.reshape(seg_tokens[1], d)
    row = lambda a: a.reshape(1, -1).astype(jnp.float32)
    bf = lambda a: a.astype(jnp.bfloat16)

    scale = HEAD_DIM ** -0.5
    col_scale = np.ones((w_in.shape[2],), np.float32)
    col_scale[0:QA] = scale
    col_scale[3 * QA:3 * QA + QB] = scale
    w_proj = bf(w_in[0] * col_scale)
    b_proj = row(b_in[0] * col_scale)
    proj = _in_projection(xp, xs, row(ln_in_g), row(ln_in_b), w_proj, b_proj)
    qkv_a, (qb, kb, vb, ga, gb, x0) = proj[:3 * N_DIL_GROUPS], proj[3 * N_DIL_GROUPS:]

    o_groups, lse_groups = [], []
    for g, (window, dil) in enumerate(DIL_PATTERNS):
        assert window // (2 * dil) == HALF_SPAN
        table = rel_bias_t5[:, g * HEADS_PER_DIL_GROUP:(g + 1) * HEADS_PER_DIL_GROUP].astype(jnp.float32).reshape(-1)
        o, lse = _dilated_group(qkv_a[g], qkv_a[N_DIL_GROUPS + g], qkv_a[2 * N_DIL_GROUPS + g], table, dil,
                                seg_tokens, seq_lens)
        o_groups.append(o)
        lse_groups.append(lse)
    ob = _neighborhood(qb, kb, vb, na_rpb[0].astype(jnp.float32).reshape(-1), seg_tokens, seq_lens)

    wr = jnp.zeros((d, ROUTE_LANES), jnp.float32)
    wr = wr.at[:, 0:N_GROUPS].set(w_router_group[0]).at[:, EXPERT_LANE0:EXPERT_LANE0 + N_EXPERTS].set(w_router_expert[0])
    br = jnp.zeros((1, ROUTE_LANES), jnp.float32)
    br = br.at[0, 0:N_GROUPS].set(b_router_group[0]).at[0, EXPERT_LANE0:EXPERT_LANE0 + N_EXPERTS].set(b_router_expert[0])
    x1, route, route_t, counts = _post_attention(
        x0, o_groups, lse_groups, ob, ga, gb, bf(w_branch_a[0]), bf(w_branch_b[0]),
        bf(w_out[0]), row(b_out[0]), row(ln1_g[0]), row(ln1_b[0]), bf(wr), br)

    experts = route_t[0:2].astype(jnp.int32)
    ranks = route_t[4:6].astype(jnp.int32)
    cnt = counts[:, 0].astype(jnp.int32)
    n_rows = 2 * n
    *items, starts = _ffn_items(cnt, n_rows)
    pos = ranks
    for e in range(N_EXPERTS):
        pos = pos + jnp.where(experts == e, starts[e], 0)

    xsorted = _dispatch(x1, pos, n_rows)
    ys = _grouped_ffn(xsorted, items, w_exp_gate[0], w_exp_up[0], w_exp_down[0])
    y_prompt = _combine(x1, route, pos, ys, row(ln2_g[0]), row(ln2_b[0]), 0, seg_tokens[0])
    y_sample = _combine(x1, route, pos, ys, row(ln2_g[0]), row(ln2_b[0]), seg_tokens[0], seg_tokens[1])
    return y_prompt.reshape(bp, tp, d), y_sample.reshape(bs, ts, d)


def kernel(x_prompt, x_sample, ln_in_g, ln_in_b, w_in, b_in, rel_bias_t5, na_rpb, w_branch_a, w_branch_b, w_out, b_out, ln1_g, ln1_b, w_router_group, b_router_group, w_router_expert, b_router_expert, w_exp_gate, w_exp_up, w_exp_down, ln2_g, ln2_b):
    return _layer(x_prompt, x_sample, ln_in_g, ln_in_b, w_in, b_in, rel_bias_t5, na_rpb, w_branch_a, w_branch_b,
                  w_out, b_out, ln1_g, ln1_b, w_router_group, b_router_group, w_router_expert, b_router_expert,
                  w_exp_gate, w_exp_up, w_exp_down, ln2_g, ln2_b)
```

```python
import functools
import math

import numpy as np
import jax
import jax.numpy as jnp
from jax import lax
from jax.experimental import pallas as pl
from jax.experimental.pallas import tpu as pltpu

HEAD_DIM = 64
DIL_PATTERNS = ((128, 1), (512, 4), (2048, 16))
HEADS_PER_DIL_GROUP = 4
N_DIL_GROUPS = 3
H_A = N_DIL_GROUPS * HEADS_PER_DIL_GROUP
H_B = 8
QA = H_A * HEAD_DIM
QB = H_B * HEAD_DIM
HALF_SPAN = 64
GRID_W = 64
NA_KH = 8
NA_KW = 16
N_BUCKETS = 32
REL_MAX_DIST = 1024
N_GROUPS = 4
EXPERTS_PER_GROUP = 8
N_EXPERTS = N_GROUPS * EXPERTS_PER_GROUP
LN_EPS = 1e-5
NEG_INF = -1e30
DEPTH = 1
DN_ALPHA = (2.0 * DEPTH) ** 0.25

LANES = 128
HEAD_BLOCK = 4 * HEAD_DIM
VMEM_LIMIT_BYTES = 56 * 1024 * 1024

PROJ_TM = 512
DIL_M = 128
DIL_GROUP = 8
NA_ROWS = 16
POST_TM = 512
POST_SPLIT = 1
FFN_TM = 512
DISP_TT = 2048
COMB_TT = 512
ROUTE_LANES = LANES
EXPERT_LANE0 = 32


def _cparams(sem):
    return pltpu.CompilerParams(dimension_semantics=sem, vmem_limit_bytes=VMEM_LIMIT_BYTES)


SUBLANES = 8


def _store_row_tiles(ref, x, row0=0):
    rows = x.shape[0]
    for c in range(SUBLANES):
        ref[pl.ds(row0 * SUBLANES + c, rows, stride=SUBLANES), :] = x[:, c * LANES:(c + 1) * LANES]


def _load_row_tiles(ref, rows, row0=0):
    return jnp.concatenate([ref[pl.ds(row0 * SUBLANES + c, rows, stride=SUBLANES), :] for c in range(SUBLANES)],
                           axis=1)


def _layer_norm(x, g, b):
    mu = jnp.mean(x, axis=-1, keepdims=True)
    xc = x - mu
    var = jnp.mean(xc * xc, axis=-1, keepdims=True)
    return xc * lax.rsqrt(var + LN_EPS) * g + b


def _strided_rows(scr, slab0, r, count, stride):
    return jnp.concatenate([scr[slab0 + s, pl.ds(r, count, stride=stride), :] for s in range(2)], axis=1)


def _inproj_kernel(xp_ref, xs_ref, g_ref, b_ref, w_ref, bias_ref, *refs, n0_tiles):
    out_refs, scr = refs[:-1], refs[-1]
    i = pl.program_id(0)
    x = jnp.where(i < n0_tiles, xp_ref[...], xs_ref[...])
    x0 = _layer_norm(x, g_ref[...], b_ref[...])
    out_refs[-1][...] = x0
    out_refs = out_refs[:-1]
    xn = x0.astype(jnp.bfloat16)
    tm = xn.shape[0]

    def proj(c0, cw):
        return jnp.dot(xn, w_ref[:, c0:c0 + cw], preferred_element_type=jnp.float32) + bias_ref[:, c0:c0 + cw]

    n_dil = 3 * N_DIL_GROUPS
    for idx in range(n_dil):
        g = idx % N_DIL_GROUPS
        dil = DIL_PATTERNS[g][1]
        o_ref = out_refs[idx]
        y = proj(idx * HEAD_BLOCK, HEAD_BLOCK)
        if dil == 1:
            o_ref[0, 0] = y.astype(o_ref.dtype)
        else:
            slab0 = 2 * (idx % 2)
            scr[slab0] = y[:, :LANES]
            scr[slab0 + 1] = y[:, LANES:]
            for r in range(dil):
                o_ref[0, r] = _strided_rows(scr, slab0, r, tm // dil, dil).astype(o_ref.dtype)
    col = n_dil * HEAD_BLOCK
    for o_ref in out_refs[n_dil:]:
        width = o_ref.shape[1]
        for c in range(0, width, 512):
            cw = min(512, width - c)
            o_ref[:, c:c + cw] = proj(col + c, cw).astype(o_ref.dtype)
        col += width


def _in_projection(xp, xs, ln_g, ln_b, w, b):
    n0, d_model = xp.shape
    n1 = xs.shape[0]
    n = n0 + n1
    tm = PROJ_TM
    n0_tiles, n1_tiles = n0 // tm, n1 // tm
    nt = n0_tiles + n1_tiles
    out_shape, out_specs = [], []
    for _ in range(3):
        for _, dil in DIL_PATTERNS:
            out_shape.append(jax.ShapeDtypeStruct((nt, dil, tm // dil, HEAD_BLOCK), jnp.bfloat16))
            out_specs.append(pl.BlockSpec((1, dil, tm // dil, HEAD_BLOCK), lambda i: (i, 0, 0, 0)))
    for wd, dtype in ((QB, jnp.bfloat16),) * 3 + ((d_model, jnp.bfloat16),) * 2 + ((d_model, jnp.float32),):
        out_shape.append(jax.ShapeDtypeStruct((n, wd), dtype))
        out_specs.append(pl.BlockSpec((tm, wd), lambda i: (i, 0)))
    d_in = w.shape[1]
    kern = functools.partial(_inproj_kernel, n0_tiles=n0_tiles)
    return pl.pallas_call(
        kern,
        grid=(nt,),
        in_specs=[
            pl.BlockSpec((tm, d_model), lambda i: (jnp.minimum(i, n0_tiles - 1), 0)),
            pl.BlockSpec((tm, d_model), lambda i: (jnp.maximum(i - n0_tiles, 0), 0)),
            pl.BlockSpec((1, d_model), lambda i: (0, 0)),
            pl.BlockSpec((1, d_model), lambda i: (0, 0)),
            pl.BlockSpec((d_model, d_in), lambda i: (0, 0)),
            pl.BlockSpec((1, d_in), lambda i: (0, 0)),
        ],
        out_specs=out_specs,
        out_shape=out_shape,
        scratch_shapes=[pltpu.VMEM((4, tm, LANES), jnp.float32)],
        compiler_params=_cparams(("arbitrary",)),
        name="in_projection",
    )(xp, xs, ln_g, ln_b, w, b)


def _t5_bucket_np(rel):
    half = N_BUCKETS // 2
    max_exact = half // 2
    ret = np.where(rel > 0, half, 0)
    n = np.abs(rel)
    nf = np.maximum(n, 1).astype(np.float32)
    large = max_exact + (np.log(nf / np.float32(max_exact)) / np.float32(math.log(REL_MAX_DIST / max_exact))
                         * np.float32(half - max_exact)).astype(np.int32)
    large = np.minimum(large, half - 1)
    return ret + np.where(n < max_exact, n, large)


def _t5_bucket_starts():
    half = N_BUCKETS // 2
    dist = np.arange(0, HALF_SPAN * max(d for _, d in DIL_PATTERNS) + 1)
    buckets = _t5_bucket_np(-dist)
    return [int(np.argmax(buckets >= k)) for k in range(half)]


def _build_dilated_bias(table_ref, bias_scr, dil):
    half = N_BUCKETS // 2
    starts = _t5_bucket_starts()
    kb = DIL_M + 2 * HALF_SPAN
    i = lax.broadcasted_iota(jnp.int32, (DIL_M, kb), 0)
    m = lax.broadcasted_iota(jnp.int32, (DIL_M, kb), 1)
    delta = m - HALF_SPAN - i
    dist = jnp.abs(delta) * dil
    band = jnp.abs(delta) <= HALF_SPAN
    for h in range(HEADS_PER_DIL_GROUP):
        sides = []
        for side in range(2):
            val = jnp.full((DIL_M, kb), table_ref[(side * half + half - 1) * HEADS_PER_DIL_GROUP + h], jnp.float32)
            for k in range(half - 1, 0, -1):
                val = jnp.where(dist < starts[k], table_ref[(side * half + k - 1) * HEADS_PER_DIL_GROUP + h], val)
            sides.append(val)
        base = jnp.where(band, jnp.where(delta > 0, sides[1], sides[0]), NEG_INF)
        rows = slice(h * DIL_M, (h + 1) * DIL_M)
        bias_scr[0, rows, :] = base
        bias_scr[1, rows, :] = jnp.where(m >= HALF_SPAN, base, NEG_INF)
        bias_scr[2, rows, :] = jnp.where(m < kb - HALF_SPAN, base, NEG_INF)


def _stack_heads(q):
    head_of_col = lax.broadcasted_iota(jnp.int32, (1, HEAD_BLOCK), 1) // HEAD_DIM
    zero = jnp.zeros_like(q)
    return jnp.concatenate([jnp.where(head_of_col == h, q, zero) for h in range(4)], axis=0)


def _merge_heads(x, m):
    head_of_col = lax.broadcasted_iota(jnp.int32, (1, HEAD_BLOCK), 1) // HEAD_DIM
    out = jnp.zeros((m, HEAD_BLOCK), x.dtype)
    for h in range(4):
        out = jnp.where(head_of_col == h, x[h * m:(h + 1) * m], out)
    return out


def _softmax_pv(s, v, m_rows):
    mx = jnp.max(s, axis=-1, keepdims=True)
    p = jnp.exp(s - mx)
    l = jnp.sum(p, axis=-1, keepdims=True)
    pv = jnp.dot(p.astype(jnp.bfloat16), v, preferred_element_type=jnp.float32)
    o = _merge_heads(pv * (1.0 / l), m_rows)
    return o, mx + jnp.log(l)


def _dilated_kernel(table_ref, q_ref, kp_ref, kc_ref, kn_ref, vp_ref, vc_ref, vn_ref,
                    o_ref, lse_ref, bias_scr, *, dil, seg0_blocks, blocks_per_seq):
    step = pl.program_id(1)

    @pl.when((pl.program_id(0) == 0) & (step == 0))
    def _():
        _build_dilated_bias(table_ref, bias_scr, dil)

    rows2d = lambda ref: ref[...].reshape(-1, HEAD_BLOCK)
    k = jnp.concatenate([rows2d(r) for r in (kp_ref, kc_ref, kn_ref)], axis=0)
    v = jnp.concatenate([rows2d(r) for r in (vp_ref, vc_ref, vn_ref)], axis=0)
    q = rows2d(q_ref)
    o_parts, lse_parts = [], []
    for g in range(DIL_GROUP):
        jb = step * DIL_GROUP + g
        in0 = jb < seg0_blocks
        jl = jnp.where(in0, jb, jb - seg0_blocks)
        nbs = jnp.where(in0, blocks_per_seq[0], blocks_per_seq[1])
        pos = lax.rem(jl, nbs)
        variant = jnp.where(pos == 0, 1, jnp.where(pos == nbs - 1, 2, 0))
        keys = slice(g * DIL_M, (g + 1) * DIL_M + 2 * HALF_SPAN)
        qs = _stack_heads(q[g * DIL_M:(g + 1) * DIL_M])
        s = lax.dot_general(qs, k[keys], (((1,), (1,)), ((), ())), preferred_element_type=jnp.float32)
        s = s + bias_scr[variant]
        o, lse = _softmax_pv(s, v[keys], DIL_M)
        o_parts.append(o.astype(o_ref.dtype))
        lse_parts.append(_merge_heads(jnp.broadcast_to(lse, (4 * DIL_M, HEAD_BLOCK)), DIL_M))
    o_ref[...] = jnp.concatenate(o_parts, axis=0).reshape(o_ref.shape)
    lse_ref[...] = jnp.concatenate(lse_parts, axis=0).reshape(lse_ref.shape)


def _residue_spec(rows_blk, dil, block_of):
    per_tile = PROJ_TM // dil
    if per_tile >= rows_blk:
        per = per_tile // rows_blk
        return pl.BlockSpec((1, None, rows_blk, HEAD_BLOCK),
                            lambda r, jb: (block_of(jb) // per, r, block_of(jb) % per, 0))
    return pl.BlockSpec((rows_blk // per_tile, None, per_tile, HEAD_BLOCK), lambda r, jb: (block_of(jb), r, 0, 0))


def _dilated_group(q, k, v, table, dil, seg_tokens, seq_lens):
    n = q.shape[0] * PROJ_TM
    rows = n // dil
    step_rows = DIL_GROUP * DIL_M
    assert rows % step_rows == 0 and all(t % (dil * DIL_M) == 0 for t in seq_lens)
    nsteps = rows // step_rows
    nkb = rows // HALF_SPAN
    ratio = step_rows // HALF_SPAN
    seg0_blocks = seg_tokens[0] // dil // DIL_M
    blocks_per_seq = tuple(t // dil // DIL_M for t in seq_lens)
    assert min(blocks_per_seq) >= 2

    qo_spec = _residue_spec(step_rows, dil, lambda jb: jb)
    prev_spec = _residue_spec(HALF_SPAN, dil, lambda jb: jnp.maximum(ratio * jb - 1, 0))
    next_spec = _residue_spec(HALF_SPAN, dil, lambda jb: jnp.minimum(ratio * (jb + 1), nkb - 1))
    kv_specs = [prev_spec, qo_spec, next_spec]
    kern = functools.partial(_dilated_kernel, dil=dil, seg0_blocks=seg0_blocks, blocks_per_seq=blocks_per_seq)
    return pl.pallas_call(
        kern,
        grid=(dil, nsteps),
        in_specs=[pl.BlockSpec(memory_space=pltpu.SMEM), qo_spec] + kv_specs + kv_specs,
        out_specs=[qo_spec, qo_spec],
        out_shape=[jax.ShapeDtypeStruct(q.shape, jnp.bfloat16), jax.ShapeDtypeStruct(q.shape, jnp.float32)],
        scratch_shapes=[pltpu.VMEM((3, HEADS_PER_DIL_GROUP * DIL_M, DIL_M + 2 * HALF_SPAN), jnp.float32)],
        compiler_params=_cparams(("arbitrary", "arbitrary")),
        name=f"dilated_attention_d{dil}",
    )(table, q, k, k, k, v, v, v)


NA_DR = 2 * NA_KH - 1
NA_DC = 2 * NA_KW - 1
NA_PAIRS = NA_DR - 1


def _build_na_bias(rpb_ref, pair_scr, head0):
    qc = lax.broadcasted_iota(jnp.int32, (GRID_W, 2 * GRID_W), 0)
    lane = lax.broadcasted_iota(jnp.int32, (GRID_W, 2 * GRID_W), 1)
    kc = lane % GRID_W
    upper = lane >= GRID_W
    qstart = jnp.clip(qc - NA_KW // 2, 0, GRID_W - NA_KW)
    valid = (kc >= qstart) & (kc < qstart + NA_KW)
    dc = jnp.clip(kc - qc, -(NA_KW - 1), NA_KW - 1) + NA_KW - 1

    def one_pair(idx, carry):
        h = idx // NA_PAIRS
        dr = idx % NA_PAIRS
        base = ((head0 + h) * NA_DR + dr) * NA_DC
        val = jnp.zeros((GRID_W, 2 * GRID_W), jnp.float32)
        for c in range(NA_DC):
            val = jnp.where(dc == c, jnp.where(upper, rpb_ref[base + NA_DC + c], rpb_ref[base + c]), val)
        pair_scr[h, dr] = jnp.where(valid, val, NEG_INF)
        return carry

    lax.fori_loop(0, 4 * NA_PAIRS, one_pair, 0)


def _na_kernel(rpb_ref, q_ref, kp_ref, kc_ref, kn_ref, vp_ref, vc_ref, vn_ref, o_ref, k_scr, v_scr, pair_scr,
               *, seg0_blocks, blocks_per_seq):
    ib = pl.program_id(1)

    @pl.when(ib == 0)
    def _():
        _build_na_bias(rpb_ref, pair_scr, pl.program_id(0) * 4)

    in0 = ib < seg0_blocks
    il = jnp.where(in0, ib, ib - seg0_blocks)
    nbs = jnp.where(in0, blocks_per_seq[0], blocks_per_seq[1])
    pos = lax.rem(il, nbs)
    first = pos == 0
    last = pos == nbs - 1
    blk = NA_ROWS * GRID_W
    k_scr[0:blk] = kp_ref[...]
    k_scr[blk:2 * blk] = kc_ref[...]
    k_scr[2 * blk:3 * blk] = kn_ref[...]
    v_scr[0:blk] = vp_ref[...]
    v_scr[blk:2 * blk] = vc_ref[...]
    v_scr[2 * blk:3 * blk] = vn_ref[...]
    half = NA_KH // 2
    for rr in range(NA_ROWS):
        inner = NA_ROWS + rr - half
        s_first, s_last = max(inner, NA_ROWS), min(inner, 2 * NA_ROWS - NA_KH)
        start = jnp.where(first, s_first, jnp.where(last, s_last, inner))
        var = jnp.where(first, NA_ROWS + rr - s_first, jnp.where(last, NA_ROWS + rr - s_last, half))
        off = pl.multiple_of(start * GRID_W, GRID_W)
        kk = k_scr[pl.ds(off, NA_KH * GRID_W), :]
        vv = v_scr[pl.ds(off, NA_KH * GRID_W), :]
        qs = _stack_heads(q_ref[rr * GRID_W:(rr + 1) * GRID_W, :])
        s = lax.dot_general(qs, kk, (((1,), (1,)), ((), ())), preferred_element_type=jnp.float32)
        bias = jnp.concatenate(
            [jnp.concatenate([pair_scr[h, 2 * p - var + NA_KH - 1] for p in range(NA_KH // 2)], axis=1)
             for h in range(4)], axis=0)
        s = s + bias
        o, _ = _softmax_pv(s, vv, GRID_W)
        o_ref[rr * GRID_W:(rr + 1) * GRID_W, :] = o.astype(o_ref.dtype)


def _neighborhood(qb, kb, vb, rpb_flat, seg_tokens, seq_lens):
    n = qb.shape[0]
    blk = NA_ROWS * GRID_W
    nblk = n // blk
    seg0_blocks = seg_tokens[0] // blk
    blocks_per_seq = tuple(t // blk for t in seq_lens)
    assert min(blocks_per_seq) >= 2
    nset = QB // HEAD_BLOCK

    def kv_spec(t):
        return pl.BlockSpec((blk, HEAD_BLOCK), lambda c, ib: (jnp.clip(ib - 1 + t, 0, nblk - 1), c))

    kern = functools.partial(_na_kernel, seg0_blocks=seg0_blocks, blocks_per_seq=blocks_per_seq)
    return pl.pallas_call(
        kern,
        grid=(nset, nblk),
        in_specs=[pl.BlockSpec(memory_space=pltpu.SMEM), pl.BlockSpec((blk, HEAD_BLOCK), lambda c, ib: (ib, c))]
        + [kv_spec(t) for t in range(3)] + [kv_spec(t) for t in range(3)],
        out_specs=pl.BlockSpec((blk, HEAD_BLOCK), lambda c, ib: (ib, c)),
        out_shape=jax.ShapeDtypeStruct((n, QB), jnp.bfloat16),
        scratch_shapes=[pltpu.VMEM((3 * blk, HEAD_BLOCK), jnp.bfloat16),
                        pltpu.VMEM((3 * blk, HEAD_BLOCK), jnp.bfloat16),
                        pltpu.VMEM((4, NA_PAIRS, GRID_W, 2 * GRID_W), jnp.float32)],
        compiler_params=_cparams(("arbitrary", "arbitrary")),
        name="neighborhood_attention",
    )(rpb_flat, qb, kb, kb, kb, vb, vb, vb)


def _post_kernel(x0_ref, o0_ref, o1_ref, o2_ref, l0_ref, l1_ref, l2_ref, ob_ref,
                 ga_ref, gb_ref, wa_ref, wb_ref, wo_ref, bo_ref, g1_ref, b1_ref, wr_ref, br_ref, tri_ref,
                 x1_ref, route_ref, route_t_ref, counts_ref, carry_ref, perm_scr):
    i = pl.program_id(0)
    tm = route_ref.shape[0]
    hm = tm // POST_SPLIT

    @pl.when(i == 0)
    def _():
        carry_ref[...] = jnp.zeros_like(carry_ref)

    def token_order(ref, slab0):
        dil, per = ref.shape[1], ref.shape[2]
        if dil == 1:
            return lambda rs: ref[0, 0, rs, :].astype(jnp.float32)
        for r in range(dil):
            val = ref[0, r].astype(jnp.float32)
            for s in range(2):
                perm_scr[slab0 + s, pl.ds(r, per, stride=dil), :] = val[:, s * LANES:(s + 1) * LANES]
        return lambda rs: jnp.concatenate([perm_scr[slab0, rs, :], perm_scr[slab0 + 1, rs, :]], axis=1)

    lse_of = [token_order(ref, 2 * j) for j, ref in enumerate((l0_ref, l1_ref, l2_ref))]
    o_of = [token_order(ref, 6 + 2 * j) for j, ref in enumerate((o0_ref, o1_ref, o2_ref))]
    sigmoid = lambda v: 0.5 * jnp.tanh(0.5 * v) + 0.5
    grow = lax.broadcasted_iota(jnp.int32, (8, hm), 0)
    erow = lax.broadcasted_iota(jnp.int32, (N_EXPERTS, hm), 0)
    frow = lax.broadcasted_iota(jnp.int32, (ROUTE_LANES, hm), 0)
    neg = jnp.float32(-jnp.inf)
    carry = carry_ref[:, 0:1]

    for h in range(POST_SPLIT):
        rs = slice(h * hm, (h + 1) * hm)
        l0, l1, l2 = (f(rs) for f in lse_of)
        lm = jnp.maximum(jnp.maximum(l0, l1), l2)
        e0, e1, e2 = jnp.exp(l0 - lm), jnp.exp(l1 - lm), jnp.exp(l2 - lm)
        inv = 1.0 / (e0 + e1 + e2)
        o_a = (e0 * inv) * o_of[0](rs) + (e1 * inv) * o_of[1](rs) + (e2 * inv) * o_of[2](rs)
        y_a = jnp.dot(o_a.astype(jnp.bfloat16), wa_ref[...], preferred_element_type=jnp.float32)
        y_b = jnp.dot(ob_ref[rs, :], wb_ref[...], preferred_element_type=jnp.float32)
        mix = sigmoid(ga_ref[rs, :].astype(jnp.float32)) * y_a + sigmoid(gb_ref[rs, :].astype(jnp.float32)) * y_b
        out = jnp.dot(mix.astype(jnp.bfloat16), wo_ref[...], preferred_element_type=jnp.float32) + bo_ref[...]
        x1 = _layer_norm(DN_ALPHA * x0_ref[rs, :] + out, g1_ref[...], b1_ref[...])
        _store_row_tiles(x1_ref, x1, row0=h * hm)

        r = jnp.dot(x1.astype(jnp.bfloat16), wr_ref[...], preferred_element_type=jnp.float32) + br_ref[...]
        rt = r.T
        lg = jnp.where(grow < N_GROUPS, rt[0:8], neg)
        gmax = jnp.max(lg, axis=0, keepdims=True)
        gi = jnp.min(jnp.where(lg == gmax, grow, 8), axis=0, keepdims=True)
        wg = 1.0 / jnp.sum(jnp.exp(lg - gmax), axis=0, keepdims=True)
        lo = EXPERTS_PER_GROUP * gi
        le = jnp.where((erow >= lo) & (erow < lo + EXPERTS_PER_GROUP), rt[EXPERT_LANE0:EXPERT_LANE0 + N_EXPERTS], neg)
        m1 = jnp.max(le, axis=0, keepdims=True)
        i1 = jnp.min(jnp.where(le == m1, erow, N_EXPERTS), axis=0, keepdims=True)
        le2 = jnp.where(erow == i1, neg, le)
        m2 = jnp.max(le2, axis=0, keepdims=True)
        i2 = jnp.min(jnp.where(le2 == m2, erow, N_EXPERTS), axis=0, keepdims=True)
        t2 = jnp.exp(m2 - m1)
        w1 = wg / (1.0 + t2)
        w2 = wg * t2 / (1.0 + t2)

        hot1 = erow == i1
        hot2 = erow == i2
        hot = (hot1 | hot2).astype(jnp.float32)
        before = jnp.dot(hot.astype(jnp.bfloat16), tri_ref[...], preferred_element_type=jnp.float32) + carry
        rank1 = jnp.sum(jnp.where(hot1, before, 0.0), axis=0, keepdims=True)
        rank2 = jnp.sum(jnp.where(hot2, before, 0.0), axis=0, keepdims=True)
        carry = carry + jnp.sum(hot, axis=1, keepdims=True)

        fields = (i1.astype(jnp.float32), i2.astype(jnp.float32), w1, w2, rank1, rank2)
        route_t = jnp.zeros((ROUTE_LANES, hm), jnp.float32)
        for idx, val in enumerate(fields):
            route_t = jnp.where(frow == idx, val, route_t)
        route_t_ref[:, rs] = route_t[0:route_t_ref.shape[0]]
        route_ref[rs, :] = route_t.T

    carry_ref[...] = jnp.broadcast_to(carry, carry_ref.shape)
    counts_ref[...] = jnp.broadcast_to(carry, counts_ref.shape)


def _post_attention(x0, o_groups, lse_groups, ob, ga, gb, wa, wb, wo, bo, g1, b1, wr, br):
    n, d = x0.shape
    tm = POST_TM
    nt = n // tm
    hm = tm // POST_SPLIT
    tri = jnp.asarray(np.triu(np.ones((hm, hm), np.float32), 1), jnp.bfloat16)

    def tok(width):
        return pl.BlockSpec((tm, width), lambda i: (i, 0))

    def full(a):
        return pl.BlockSpec(a.shape, lambda i: (0,) * a.ndim)

    return pl.pallas_call(
        _post_kernel,
        grid=(nt,),
        in_specs=[
            tok(d),
            *[pl.BlockSpec((1,) + a.shape[1:], lambda i: (i, 0, 0, 0)) for a in (*o_groups, *lse_groups)],
            tok(QB), tok(d), tok(d),
            full(wa), full(wb), full(wo), full(bo), full(g1), full(b1), full(wr), full(br), full(tri),
        ],
        out_specs=[pl.BlockSpec((tm * SUBLANES, LANES), lambda i: (i, 0)), tok(ROUTE_LANES),
                   pl.BlockSpec((8, tm), lambda i: (0, i)), pl.BlockSpec((N_EXPERTS, ROUTE_LANES), lambda i: (0, 0))],
        out_shape=[jax.ShapeDtypeStruct((n * SUBLANES, LANES), jnp.float32),
                   jax.ShapeDtypeStruct((n, ROUTE_LANES), jnp.float32),
                   jax.ShapeDtypeStruct((8, n), jnp.float32),
                   jax.ShapeDtypeStruct((N_EXPERTS, ROUTE_LANES), jnp.float32)],
        scratch_shapes=[pltpu.VMEM((N_EXPERTS, ROUTE_LANES), jnp.float32), pltpu.VMEM((12, tm, LANES), jnp.float32)],
        compiler_params=_cparams(("arbitrary",)),
        name="post_attention_router",
    )(x0, *o_groups, *lse_groups, ob, ga, gb, wa, wb, wo, bo, g1, b1, wr, br, tri)


def _dispatch_kernel(pos0_ref, pos1_ref, x_ref, xs_ref, sem):
    tt = x_ref.shape[0] // SUBLANES

    def issue(j, carry):
        for u in range(2):
            t = 2 * j + u
            src = x_ref.at[pl.ds(pl.multiple_of(t * SUBLANES, SUBLANES), SUBLANES)]
            for p_ref in (pos0_ref, pos1_ref):
                p = p_ref[t]
                dst = xs_ref.at[pl.ds(pl.multiple_of(p * SUBLANES, SUBLANES), SUBLANES)]
                pltpu.make_async_copy(src, dst, sem).start(priority=u)
        return carry

    lax.fori_loop(0, tt // 2, issue, 0, unroll=4)
    for _ in range(2):
        pltpu.make_async_copy(x_ref, xs_ref.at[pl.ds(0, tt * SUBLANES)], sem).wait()


def _dispatch(x1t, pos, n_rows):
    n = x1t.shape[0] // SUBLANES
    tt = DISP_TT
    return pl.pallas_call(
        _dispatch_kernel,
        grid=(n // tt,),
        in_specs=[pl.BlockSpec((tt,), lambda i: (i,), memory_space=pltpu.SMEM),
                  pl.BlockSpec((tt,), lambda i: (i,), memory_space=pltpu.SMEM),
                  pl.BlockSpec((tt * SUBLANES, LANES), lambda i: (i, 0))],
        out_specs=pl.BlockSpec(memory_space=pl.ANY),
        out_shape=jax.ShapeDtypeStruct((n_rows * SUBLANES, LANES), x1t.dtype),
        scratch_shapes=[pltpu.SemaphoreType.DMA(())],
        compiler_params=_cparams(("arbitrary",)),
        name="moe_dispatch",
    )(pos[0], pos[1], x1t)


def _ffn_kernel(item_expert_ref, item_tile_ref, item_lo_ref, item_hi_ref, n_items_ref,
                x_ref, wg_ref, wu_ref, wd_ref, y_ref, wg_bf, wu_bf, wd_bf, y_acc):
    j = pl.program_id(0)

    @pl.when(j == 0)
    def _():
        y_acc[...] = jnp.zeros_like(y_acc)

    @pl.when(j < n_items_ref[0])
    def _():
        prev = jnp.maximum(j - 1, 0)

        @pl.when(jnp.logical_or(j == 0, item_expert_ref[j] != item_expert_ref[prev]))
        def _():
            wg_bf[...] = wg_ref[0].astype(jnp.bfloat16)
            wu_bf[...] = wu_ref[0].astype(jnp.bfloat16)
            wd_bf[...] = wd_ref[0].astype(jnp.bfloat16)

        tm = x_ref.shape[0] // SUBLANES
        x = _load_row_tiles(x_ref, tm).astype(jnp.bfloat16)
        a = jnp.dot(x, wg_bf[...], preferred_element_type=jnp.float32)
        u = jnp.dot(x, wu_bf[...], preferred_element_type=jnp.float32)
        h = (a * jax.nn.sigmoid(a) * u).astype(jnp.bfloat16)
        y = jnp.dot(h, wd_bf[...], preferred_element_type=jnp.float32)
        rows = lax.broadcasted_iota(jnp.int32, y.shape, 0)
        pltpu.store(y_acc, y, mask=(rows >= item_lo_ref[j]) & (rows < item_hi_ref[j]))
        _store_row_tiles(y_ref, y_acc[...])


def _ffn_items(cnt, n_rows):
    tm = FFN_TM
    n_tiles = n_rows // tm
    max_items = n_tiles + N_EXPERTS - 1
    ends = jnp.cumsum(cnt)
    starts = ends - cnt
    first_tile = starts // tm
    last_tile = jnp.maximum(ends - 1, 0) // tm
    items_per_expert = jnp.where(cnt > 0, last_tile - first_tile + 1, 0)
    item_ends = jnp.cumsum(items_per_expert)
    n_items = item_ends[-1:]
    j = jnp.minimum(jnp.arange(max_items, dtype=jnp.int32), n_items[0] - 1)
    expert = jnp.minimum(jnp.sum(item_ends[None, :] <= j[:, None], axis=1), N_EXPERTS - 1).astype(jnp.int32)
    tile = first_tile[expert] + (j - (item_ends - items_per_expert)[expert])
    lo = jnp.clip(starts[expert] - tile * tm, 0, tm)
    hi = jnp.clip(ends[expert] - tile * tm, 0, tm)
    i32 = lambda a: a.astype(jnp.int32)
    return i32(expert), i32(tile), i32(lo), i32(hi), i32(n_items), starts


def _grouped_ffn(xs, items, w_gate, w_up, w_down):
    tm = FFN_TM
    d, de = w_gate.shape[1:]
    item_expert, item_tile, item_lo, item_hi, n_items = items
    max_items = item_expert.shape[0]

    def row_map(j, ie, it, lo, hi, ni):
        return (it[j], 0)

    def w_map(j, ie, it, lo, hi, ni):
        return (ie[j], 0, 0)

    grid_spec = pltpu.PrefetchScalarGridSpec(
        num_scalar_prefetch=5,
        grid=(max_items,),
        in_specs=[pl.BlockSpec((tm * SUBLANES, LANES), row_map),
                  pl.BlockSpec((1, d, de), w_map),
                  pl.BlockSpec((1, d, de), w_map),
                  pl.BlockSpec((1, de, d), w_map)],
        out_specs=pl.BlockSpec((tm * SUBLANES, LANES), row_map),
        scratch_shapes=[pltpu.VMEM((d, de), jnp.bfloat16), pltpu.VMEM((d, de), jnp.bfloat16),
                        pltpu.VMEM((de, d), jnp.bfloat16), pltpu.VMEM((tm, d), jnp.float32)],
    )
    return pl.pallas_call(
        _ffn_kernel,
        grid_spec=grid_spec,
        out_shape=jax.ShapeDtypeStruct(xs.shape, jnp.float32),
        compiler_params=_cparams(("arbitrary",)),
        name="moe_grouped_ffn",
    )(item_expert, item_tile, item_lo, item_hi, n_items, xs, w_gate, w_up, w_down)


def _combine_kernel(pos0_ref, pos1_ref, next0_ref, next1_ref, x1_ref, route_ref, g2_ref, b2_ref, ys_ref, o_ref,
                    buf, sems):
    tt = x1_ref.shape[0] // SUBLANES
    i = pl.program_id(0)
    cur = lax.rem(i, 2)

    def issue_tile(p_refs, b):
        def issue(j, carry):
            for u in range(2):
                t = 2 * j + u
                for slot, p_ref in enumerate(p_refs):
                    p = p_ref[t]
                    src = ys_ref.at[pl.ds(pl.multiple_of(p * SUBLANES, SUBLANES), SUBLANES)]
                    dst = buf.at[b, slot, pl.ds(pl.multiple_of(t * SUBLANES, SUBLANES), SUBLANES)]
                    pltpu.make_async_copy(src, dst, sems.at[b]).start(priority=u)
            return carry

        lax.fori_loop(0, tt // 2, issue, 0, unroll=4)

    @pl.when(i == 0)
    def _():
        issue_tile((pos0_ref, pos1_ref), 0)

    @pl.when(i + 1 < pl.num_programs(0))
    def _():
        issue_tile((next0_ref, next1_ref), 1 - cur)

    for slot in range(2):
        pltpu.make_async_copy(ys_ref.at[pl.ds(0, tt * SUBLANES)], buf.at[cur, slot], sems.at[cur]).wait()
    route = route_ref[...]
    y = route[:, 2:3] * _load_row_tiles(buf.at[cur, 0], tt) + route[:, 3:4] * _load_row_tiles(buf.at[cur, 1], tt)
    o_ref[...] = _layer_norm(DN_ALPHA * _load_row_tiles(x1_ref, tt) + y, g2_ref[...], b2_ref[...])


def _combine(x1t, route, pos, ys, g2, b2, tok0, n_tok):
    d = g2.shape[1]
    tt = COMB_TT
    t0 = tok0 // tt
    nt = n_tok // tt
    return pl.pallas_call(
        _combine_kernel,
        grid=(nt,),
        in_specs=[pl.BlockSpec((tt,), lambda i: (i + t0,), memory_space=pltpu.SMEM),
                  pl.BlockSpec((tt,), lambda i: (i + t0,), memory_space=pltpu.SMEM),
                  pl.BlockSpec((tt,), lambda i: (jnp.minimum(i + 1, nt - 1) + t0,), memory_space=pltpu.SMEM),
                  pl.BlockSpec((tt,), lambda i: (jnp.minimum(i + 1, nt - 1) + t0,), memory_space=pltpu.SMEM),
                  pl.BlockSpec((tt * SUBLANES, LANES), lambda i: (i + t0, 0)),
                  pl.BlockSpec((tt, ROUTE_LANES), lambda i: (i + t0, 0)),
                  pl.BlockSpec((1, d), lambda i: (0, 0)),
                  pl.BlockSpec((1, d), lambda i: (0, 0)),
                  pl.BlockSpec(memory_space=pl.ANY)],
        out_specs=pl.BlockSpec((tt, d), lambda i: (i, 0)),
        out_shape=jax.ShapeDtypeStruct((n_tok, d), jnp.float32),
        scratch_shapes=[pltpu.VMEM((2, 2, tt * SUBLANES, LANES), ys.dtype), pltpu.SemaphoreType.DMA((2,))],
        compiler_params=_cparams(("arbitrary",)),
        name="moe_combine_ln",
    )(pos[0], pos[1], pos[0], pos[1], x1t, route, g2, b2, ys)


def _layer(x_prompt, x_sample, ln_in_g, ln_in_b, w_in, b_in, rel_bias_t5, na_rpb, w_branch_a, w_branch_b,
           w_out, b_out, ln1_g, ln1_b, w_router_group, b_router_group, w_router_expert, b_router_expert,
           w_exp_gate, w_exp_up, w_exp_down, ln2_g, ln2_b):
    bp, tp, d = x_prompt.shape
    bs, ts, _ = x_sample.shape
    seg_tokens = (bp * tp, bs * ts)
    seq_lens = (tp, ts)
    n = sum(seg_tokens)
    xp = x_prompt.reshape(seg_tokens[0], d)
    xs = x_sample.reshape(seg_tokens[1], d)
    row = lambda a: a.reshape(1, -1).astype(jnp.float32)
    bf = lambda a: a.astype(jnp.bfloat16)

    scale = HEAD_DIM ** -0.5
    col_scale = np.ones((w_in.shape[2],), np.float32)
    col_scale[0:QA] = scale
    col_scale[3 * QA:3 * QA + QB] = scale
    w_proj = bf(w_in[0] * col_scale)
    b_proj = row(b_in[0] * col_scale)
    proj = _in_projection(xp, xs, row(ln_in_g), row(ln_in_b), w_proj, b_proj)
    qkv_a, (qb, kb, vb, ga, gb, x0) = proj[:3 * N_DIL_GROUPS], proj[3 * N_DIL_GROUPS:]

    o_groups, lse_groups = [], []
    for g, (window, dil) in enumerate(DIL_PATTERNS):
        assert window // (2 * dil) == HALF_SPAN
        table = rel_bias_t5[:, g * HEADS_PER_DIL_GROUP:(g + 1) * HEADS_PER_DIL_GROUP].astype(jnp.float32).reshape(-1)
        o, lse = _dilated_group(qkv_a[g], qkv_a[N_DIL_GROUPS + g], qkv_a[2 * N_DIL_GROUPS + g], table, dil,
                                seg_tokens, seq_lens)
        o_groups.append(o)
        lse_groups.append(lse)
    ob = _neighborhood(qb, kb, vb, na_rpb[0].astype(jnp.float32).reshape(-1), seg_tokens, seq_lens)

    wr = jnp.zeros((d, ROUTE_LANES), jnp.float32)
    wr = wr.at[:, 0:N_GROUPS].set(w_router_group[0]).at[:, EXPERT_LANE0:EXPERT_LANE0 + N_EXPERTS].set(w_router_expert[0])
    br = jnp.zeros((1, ROUTE_LANES), jnp.float32)
    br = br.at[0, 0:N_GROUPS].set(b_router_group[0]).at[0, EXPERT_LANE0:EXPERT_LANE0 + N_EXPERTS].set(b_router_expert[0])
    x1, route, route_t, counts = _post_attention(
        x0, o_groups, lse_groups, ob, ga, gb, bf(w_branch_a[0]), bf(w_branch_b[0]),
        bf(w_out[0]), row(b_out[0]), row(ln1_g[0]), row(ln1_b[0]), bf(wr), br)

    experts = route_t[0:2].astype(jnp.int32)
    ranks = route_t[4:6].astype(jnp.int32)
    cnt = counts[:, 0].astype(jnp.int32)
    n_rows = 2 * n
    *items, starts = _ffn_items(cnt, n_rows)
    pos = ranks
    for e in range(N_EXPERTS):
        pos = pos + jnp.where(experts == e, starts[e], 0)

    xsorted = _dispatch(x1, pos, n_rows)
    ys = _grouped_ffn(xsorted, items, w_exp_gate[0], w_exp_up[0], w_exp_down[0])
    y_prompt = _combine(x1, route, pos, ys, row(ln2_g[0]), row(ln2_b[0]), 0, seg_tokens[0])
    y_sample = _combine(x1, route, pos, ys, row(ln2_g[0]), row(ln2_b[0]), seg_tokens[0], seg_tokens[1])
    return y_prompt.reshape(bp, tp, d), y_sample.reshape(bs, ts, d)


def kernel(x_prompt, x_sample, ln_in_g, ln_in_b, w_in, b_in, rel_bias_t5, na_rpb, w_branch_a, w_branch_b, w_out, b_out, ln1_g, ln1_b, w_router_group, b_router_group, w_router_expert, b_router_expert, w_exp_gate, w_exp_up, w_exp_down, ln2_g, ln2_b):
    return _layer(x_prompt, x_sample, ln_in_g, ln_in_b, w_in, b_in, rel_bias_t5, na_rpb, w_branch_a, w_branch_b,
                  w_out, b_out, ln1_g, ln1_b, w_router_group, b_router_group, w_router_expert, b_router_expert,
                  w_exp_gate, w_exp_up, w_exp_down, ln2_g, ln2_b)
```

```python
import functools
import math

import numpy as np
import jax
import jax.numpy as jnp
from jax import lax
from jax.experimental import pallas as pl
from jax.experimental.pallas import tpu as pltpu

HEAD_DIM = 64
DIL_PATTERNS = ((128, 1), (512, 4), (2048, 16))
HEADS_PER_DIL_GROUP = 4
N_DIL_GROUPS = 3
H_A = N_DIL_GROUPS * HEADS_PER_DIL_GROUP
H_B = 8
QA = H_A * HEAD_DIM
QB = H_B * HEAD_DIM
HALF_SPAN = 64
GRID_W = 64
NA_KH = 8
NA_KW = 16
N_BUCKETS = 32
REL_MAX_DIST = 1024
N_GROUPS = 4
EXPERTS_PER_GROUP = 8
N_EXPERTS = N_GROUPS * EXPERTS_PER_GROUP
LN_EPS = 1e-5
NEG_INF = -1e30
DEPTH = 1
DN_ALPHA = (2.0 * DEPTH) ** 0.25

LANES = 128
HEAD_BLOCK = 4 * HEAD_DIM
VMEM_LIMIT_BYTES = 56 * 1024 * 1024

PROJ_TM = 512
DIL_M = 128
DIL_GROUP = 8
NA_ROWS = 16
NA_HALO = NA_KH // 2
POST_TM = 512
POST_SPLIT = 1
FFN_TM = 512
DISP_TT = 2048
COMB_TT = 512
ROUTE_LANES = LANES
EXPERT_LANE0 = 32


def _cparams(sem):
    return pltpu.CompilerParams(dimension_semantics=sem, vmem_limit_bytes=VMEM_LIMIT_BYTES)


SUBLANES = 8


def _store_row_tiles(ref, x, row0=0):
    rows = x.shape[0]
    for c in range(SUBLANES):
        ref[pl.ds(row0 * SUBLANES + c, rows, stride=SUBLANES), :] = x[:, c * LANES:(c + 1) * LANES]


def _load_row_tiles(ref, rows, row0=0):
    return jnp.concatenate([ref[pl.ds(row0 * SUBLANES + c, rows, stride=SUBLANES), :] for c in range(SUBLANES)],
                           axis=1)


def _layer_norm(x, g, b):
    mu = jnp.mean(x, axis=-1, keepdims=True)
    xc = x - mu
    var = jnp.mean(xc * xc, axis=-1, keepdims=True)
    return xc * lax.rsqrt(var + LN_EPS) * g + b


def _strided_rows(scr, slab0, r, count, stride):
    return jnp.concatenate([scr[slab0 + s, pl.ds(r, count, stride=stride), :] for s in range(2)], axis=1)


def _inproj_kernel(xp_ref, xs_ref, g_ref, b_ref, w_ref, bias_ref, *refs, n0_tiles):
    out_refs, scr = refs[:-1], refs[-1]
    i = pl.program_id(0)
    x = jnp.where(i < n0_tiles, xp_ref[...], xs_ref[...])
    x0 = _layer_norm(x, g_ref[...], b_ref[...])
    out_refs[-1][...] = x0
    out_refs = out_refs[:-1]
    xn = x0.astype(jnp.bfloat16)
    tm = xn.shape[0]

    def proj(c0, cw):
        return jnp.dot(xn, w_ref[:, c0:c0 + cw], preferred_element_type=jnp.float32) + bias_ref[:, c0:c0 + cw]

    n_dil = 3 * N_DIL_GROUPS
    for idx in range(n_dil):
        g = idx % N_DIL_GROUPS
        dil = DIL_PATTERNS[g][1]
        o_ref = out_refs[idx]
        y = proj(idx * HEAD_BLOCK, HEAD_BLOCK)
        if dil == 1:
            o_ref[0, 0] = y.astype(o_ref.dtype)
        else:
            slab0 = 2 * (idx % 2)
            scr[slab0] = y[:, :LANES]
            scr[slab0 + 1] = y[:, LANES:]
            for r in range(dil):
                o_ref[0, r] = _strided_rows(scr, slab0, r, tm // dil, dil).astype(o_ref.dtype)
    col = n_dil * HEAD_BLOCK
    for o_ref in out_refs[n_dil:]:
        width = o_ref.shape[1]
        for c in range(0, width, 512):
            cw = min(512, width - c)
            o_ref[:, c:c + cw] = proj(col + c, cw).astype(o_ref.dtype)
        col += width


def _in_projection(xp, xs, ln_g, ln_b, w, b):
    n0, d_model = xp.shape
    n1 = xs.shape[0]
    n = n0 + n1
    tm = PROJ_TM
    n0_tiles, n1_tiles = n0 // tm, n1 // tm
    nt = n0_tiles + n1_tiles
    out_shape, out_specs = [], []
    for _ in range(3):
        for _, dil in DIL_PATTERNS:
            out_shape.append(jax.ShapeDtypeStruct((nt, dil, tm // dil, HEAD_BLOCK), jnp.bfloat16))
            out_specs.append(pl.BlockSpec((1, dil, tm // dil, HEAD_BLOCK), lambda i: (i, 0, 0, 0)))
    for wd, dtype in ((QB, jnp.bfloat16),) * 3 + ((d_model, jnp.bfloat16),) * 2 + ((d_model, jnp.float32),):
        out_shape.append(jax.ShapeDtypeStruct((n, wd), dtype))
        out_specs.append(pl.BlockSpec((tm, wd), lambda i: (i, 0)))
    d_in = w.shape[1]
    kern = functools.partial(_inproj_kernel, n0_tiles=n0_tiles)
    return pl.pallas_call(
        kern,
        grid=(nt,),
        in_specs=[
            pl.BlockSpec((tm, d_model), lambda i: (jnp.minimum(i, n0_tiles - 1), 0)),
            pl.BlockSpec((tm, d_model), lambda i: (jnp.maximum(i - n0_tiles, 0), 0)),
            pl.BlockSpec((1, d_model), lambda i: (0, 0)),
            pl.BlockSpec((1, d_model), lambda i: (0, 0)),
            pl.BlockSpec((d_model, d_in), lambda i: (0, 0)),
            pl.BlockSpec((1, d_in), lambda i: (0, 0)),
        ],
        out_specs=out_specs,
        out_shape=out_shape,
        scratch_shapes=[pltpu.VMEM((4, tm, LANES), jnp.float32)],
        compiler_params=_cparams(("arbitrary",)),
        name="in_projection",
    )(xp, xs, ln_g, ln_b, w, b)


def _t5_bucket_np(rel):
    half = N_BUCKETS // 2
    max_exact = half // 2
    ret = np.where(rel > 0, half, 0)
    n = np.abs(rel)
    nf = np.maximum(n, 1).astype(np.float32)
    large = max_exact + (np.log(nf / np.float32(max_exact)) / np.float32(math.log(REL_MAX_DIST / max_exact))
                         * np.float32(half - max_exact)).astype(np.int32)
    large = np.minimum(large, half - 1)
    return ret + np.where(n < max_exact, n, large)


def _t5_bucket_starts():
    half = N_BUCKETS // 2
    dist = np.arange(0, HALF_SPAN * max(d for _, d in DIL_PATTERNS) + 1)
    buckets = _t5_bucket_np(-dist)
    return [int(np.argmax(buckets >= k)) for k in range(half)]


def _build_dilated_bias(table_ref, bias_scr, dil):
    half = N_BUCKETS // 2
    starts = _t5_bucket_starts()
    kb = DIL_M + 2 * HALF_SPAN
    i = lax.broadcasted_iota(jnp.int32, (DIL_M, kb), 0)
    m = lax.broadcasted_iota(jnp.int32, (DIL_M, kb), 1)
    delta = m - HALF_SPAN - i
    dist = jnp.abs(delta) * dil
    band = jnp.abs(delta) <= HALF_SPAN
    for h in range(HEADS_PER_DIL_GROUP):
        sides = []
        for side in range(2):
            val = jnp.full((DIL_M, kb), table_ref[(side * half + half - 1) * HEADS_PER_DIL_GROUP + h], jnp.float32)
            for k in range(half - 1, 0, -1):
                val = jnp.where(dist < starts[k], table_ref[(side * half + k - 1) * HEADS_PER_DIL_GROUP + h], val)
            sides.append(val)
        base = jnp.where(band, jnp.where(delta > 0, sides[1], sides[0]), NEG_INF)
        rows = slice(h * DIL_M, (h + 1) * DIL_M)
        bias_scr[0, rows, :] = base
        bias_scr[1, rows, :] = jnp.where(m >= HALF_SPAN, base, NEG_INF)
        bias_scr[2, rows, :] = jnp.where(m < kb - HALF_SPAN, base, NEG_INF)


def _stack_heads(q):
    head_of_col = lax.broadcasted_iota(jnp.int32, (1, HEAD_BLOCK), 1) // HEAD_DIM
    zero = jnp.zeros_like(q)
    return jnp.concatenate([jnp.where(head_of_col == h, q, zero) for h in range(4)], axis=0)


def _merge_heads(x, m):
    head_of_col = lax.broadcasted_iota(jnp.int32, (1, HEAD_BLOCK), 1) // HEAD_DIM
    out = jnp.zeros((m, HEAD_BLOCK), x.dtype)
    for h in range(4):
        out = jnp.where(head_of_col == h, x[h * m:(h + 1) * m], out)
    return out


def _softmax_pv(s, v, m_rows):
    mx = jnp.max(s, axis=-1, keepdims=True)
    p = jnp.exp(s - mx)
    l = jnp.sum(p, axis=-1, keepdims=True)
    pv = jnp.dot(p.astype(jnp.bfloat16), v, preferred_element_type=jnp.float32)
    o = _merge_heads(pv * (1.0 / l), m_rows)
    return o, mx + jnp.log(l)


def _dilated_kernel(table_ref, q_ref, kp_ref, kc_ref, kn_ref, vp_ref, vc_ref, vn_ref,
                    o_ref, lse_ref, bias_scr, *, dil, seg0_blocks, blocks_per_seq):
    step = pl.program_id(1)

    @pl.when((pl.program_id(0) == 0) & (step == 0))
    def _():
        _build_dilated_bias(table_ref, bias_scr, dil)

    rows2d = lambda ref: ref[...].reshape(-1, HEAD_BLOCK)
    k = jnp.concatenate([rows2d(r) for r in (kp_ref, kc_ref, kn_ref)], axis=0)
    v = jnp.concatenate([rows2d(r) for r in (vp_ref, vc_ref, vn_ref)], axis=0)
    q = rows2d(q_ref)
    o_parts, lse_parts = [], []
    for g in range(DIL_GROUP):
        jb = step * DIL_GROUP + g
        in0 = jb < seg0_blocks
        jl = jnp.where(in0, jb, jb - seg0_blocks)
        nbs = jnp.where(in0, blocks_per_seq[0], blocks_per_seq[1])
        pos = lax.rem(jl, nbs)
        variant = jnp.where(pos == 0, 1, jnp.where(pos == nbs - 1, 2, 0))
        keys = slice(g * DIL_M, (g + 1) * DIL_M + 2 * HALF_SPAN)
        qs = _stack_heads(q[g * DIL_M:(g + 1) * DIL_M])
        s = lax.dot_general(qs, k[keys], (((1,), (1,)), ((), ())), preferred_element_type=jnp.float32)
        s = s + bias_scr[variant]
        o, lse = _softmax_pv(s, v[keys], DIL_M)
        o_parts.append(o.astype(o_ref.dtype))
        lse_parts.append(_merge_heads(jnp.broadcast_to(lse, (4 * DIL_M, HEAD_BLOCK)), DIL_M))
    o_ref[...] = jnp.concatenate(o_parts, axis=0).reshape(o_ref.shape)
    lse_ref[...] = jnp.concatenate(lse_parts, axis=0).reshape(lse_ref.shape)


def _residue_spec(rows_blk, dil, block_of):
    per_tile = PROJ_TM // dil
    if per_tile >= rows_blk:
        per = per_tile // rows_blk
        return pl.BlockSpec((1, None, rows_blk, HEAD_BLOCK),
                            lambda r, jb: (block_of(jb) // per, r, block_of(jb) % per, 0))
    return pl.BlockSpec((rows_blk // per_tile, None, per_tile, HEAD_BLOCK), lambda r, jb: (block_of(jb), r, 0, 0))


def _dilated_group(q, k, v, table, dil, seg_tokens, seq_lens):
    n = q.shape[0] * PROJ_TM
    rows = n // dil
    step_rows = DIL_GROUP * DIL_M
    assert rows % step_rows == 0 and all(t % (dil * DIL_M) == 0 for t in seq_lens)
    nsteps = rows // step_rows
    nkb = rows // HALF_SPAN
    ratio = step_rows // HALF_SPAN
    seg0_blocks = seg_tokens[0] // dil // DIL_M
    blocks_per_seq = tuple(t // dil // DIL_M for t in seq_lens)
    assert min(blocks_per_seq) >= 2

    qo_spec = _residue_spec(step_rows, dil, lambda jb: jb)
    prev_spec = _residue_spec(HALF_SPAN, dil, lambda jb: jnp.maximum(ratio * jb - 1, 0))
    next_spec = _residue_spec(HALF_SPAN, dil, lambda jb: jnp.minimum(ratio * (jb + 1), nkb - 1))
    kv_specs = [prev_spec, qo_spec, next_spec]
    kern = functools.partial(_dilated_kernel, dil=dil, seg0_blocks=seg0_blocks, blocks_per_seq=blocks_per_seq)
    return pl.pallas_call(
        kern,
        grid=(dil, nsteps),
        in_specs=[pl.BlockSpec(memory_space=pltpu.SMEM), qo_spec] + kv_specs + kv_specs,
        out_specs=[qo_spec, qo_spec],
        out_shape=[jax.ShapeDtypeStruct(q.shape, jnp.bfloat16), jax.ShapeDtypeStruct(q.shape, jnp.float32)],
        scratch_shapes=[pltpu.VMEM((3, HEADS_PER_DIL_GROUP * DIL_M, DIL_M + 2 * HALF_SPAN), jnp.float32)],
        compiler_params=_cparams(("arbitrary", "arbitrary")),
        name=f"dilated_attention_d{dil}",
    )(table, q, k, k, k, v, v, v)


NA_DR = 2 * NA_KH - 1
NA_DC = 2 * NA_KW - 1
NA_PAIRS = NA_DR - 1


def _build_na_bias(rpb_ref, pair_scr, head0):
    qc = lax.broadcasted_iota(jnp.int32, (GRID_W, 2 * GRID_W), 0)
    lane = lax.broadcasted_iota(jnp.int32, (GRID_W, 2 * GRID_W), 1)
    kc = lane % GRID_W
    upper = lane >= GRID_W
    qstart = jnp.clip(qc - NA_KW // 2, 0, GRID_W - NA_KW)
    valid = (kc >= qstart) & (kc < qstart + NA_KW)
    dc = jnp.clip(kc - qc, -(NA_KW - 1), NA_KW - 1) + NA_KW - 1

    def one_pair(idx, carry):
        h = idx // NA_PAIRS
        dr = idx % NA_PAIRS
        base = ((head0 + h) * NA_DR + dr) * NA_DC
        val = jnp.zeros((GRID_W, 2 * GRID_W), jnp.float32)
        for c in range(NA_DC):
            val = jnp.where(dc == c, jnp.where(upper, rpb_ref[base + NA_DC + c], rpb_ref[base + c]), val)
        pair_scr[h, dr] = jnp.where(valid, val, NEG_INF)
        return carry

    lax.fori_loop(0, 4 * NA_PAIRS, one_pair, 0)


def _na_kernel(rpb_ref, q_ref, kp_ref, kc_ref, kn_ref, vp_ref, vc_ref, vn_ref, o_ref, k_scr, v_scr, pair_scr,
               *, seg0_blocks, blocks_per_seq):
    ib = pl.program_id(1)

    @pl.when(ib == 0)
    def _():
        _build_na_bias(rpb_ref, pair_scr, pl.program_id(0) * 4)

    in0 = ib < seg0_blocks
    il = jnp.where(in0, ib, ib - seg0_blocks)
    nbs = jnp.where(in0, blocks_per_seq[0], blocks_per_seq[1])
    pos = lax.rem(il, nbs)
    first = pos == 0
    last = pos == nbs - 1
    blk = NA_ROWS * GRID_W
    halo = NA_HALO * GRID_W
    for scr, (p_ref, c_ref, n_ref) in ((k_scr, (kp_ref, kc_ref, kn_ref)), (v_scr, (vp_ref, vc_ref, vn_ref))):
        scr[0:halo] = p_ref[...]
        scr[halo:halo + blk] = c_ref[...]
        scr[halo + blk:2 * halo + blk] = n_ref[...]
    half = NA_KH // 2
    for rr in range(NA_ROWS):
        inner = NA_HALO + rr - half
        s_first, s_last = max(inner, NA_HALO), min(inner, NA_HALO + NA_ROWS - NA_KH)
        start = jnp.where(first, s_first, jnp.where(last, s_last, inner))
        var = jnp.where(first, NA_HALO + rr - s_first, jnp.where(last, NA_HALO + rr - s_last, half))
        off = pl.multiple_of(start * GRID_W, GRID_W)
        kk = k_scr[pl.ds(off, NA_KH * GRID_W), :]
        vv = v_scr[pl.ds(off, NA_KH * GRID_W), :]
        qs = _stack_heads(q_ref[rr * GRID_W:(rr + 1) * GRID_W, :])
        s = lax.dot_general(qs, kk, (((1,), (1,)), ((), ())), preferred_element_type=jnp.float32)
        bias = jnp.concatenate(
            [jnp.concatenate([pair_scr[h, 2 * p - var + NA_KH - 1] for p in range(NA_KH // 2)], axis=1)
             for h in range(4)], axis=0)
        s = s + bias
        o, _ = _softmax_pv(s, vv, GRID_W)
        o_ref[rr * GRID_W:(rr + 1) * GRID_W, :] = o.astype(o_ref.dtype)


def _neighborhood(qb, kb, vb, rpb_flat, seg_tokens, seq_lens):
    n = qb.shape[0]
    blk = NA_ROWS * GRID_W
    nblk = n // blk
    seg0_blocks = seg_tokens[0] // blk
    blocks_per_seq = tuple(t // blk for t in seq_lens)
    assert min(blocks_per_seq) >= 2
    nset = QB // HEAD_BLOCK

    per = NA_ROWS // NA_HALO
    halo = NA_HALO * GRID_W
    cur_spec = pl.BlockSpec((blk, HEAD_BLOCK), lambda c, ib: (ib, c))
    prev_spec = pl.BlockSpec((halo, HEAD_BLOCK), lambda c, ib: (jnp.maximum(per * ib - 1, 0), c))
    next_spec = pl.BlockSpec((halo, HEAD_BLOCK), lambda c, ib: (jnp.minimum(per * (ib + 1), per * nblk - 1), c))
    kv_specs = [prev_spec, cur_spec, next_spec]

    kern = functools.partial(_na_kernel, seg0_blocks=seg0_blocks, blocks_per_seq=blocks_per_seq)
    return pl.pallas_call(
        kern,
        grid=(nset, nblk),
        in_specs=[pl.BlockSpec(memory_space=pltpu.SMEM), cur_spec] + kv_specs + kv_specs,
        out_specs=cur_spec,
        out_shape=jax.ShapeDtypeStruct((n, QB), jnp.bfloat16),
        scratch_shapes=[pltpu.VMEM((blk + 2 * halo, HEAD_BLOCK), jnp.bfloat16),
                        pltpu.VMEM((blk + 2 * halo, HEAD_BLOCK), jnp.bfloat16),
                        pltpu.VMEM((4, NA_PAIRS, GRID_W, 2 * GRID_W), jnp.float32)],
        compiler_params=_cparams(("arbitrary", "arbitrary")),
        name="neighborhood_attention",
    )(rpb_flat, qb, kb, kb, kb, vb, vb, vb)


def _post_kernel(x0_ref, o0_ref, o1_ref, o2_ref, l0_ref, l1_ref, l2_ref, ob_ref,
                 ga_ref, gb_ref, wa_ref, wb_ref, wo_ref, bo_ref, g1_ref, b1_ref, wr_ref, br_ref, tri_ref,
                 x1_ref, route_ref, route_t_ref, counts_ref, carry_ref, perm_scr):
    i = pl.program_id(0)
    tm = route_ref.shape[0]
    hm = tm // POST_SPLIT

    @pl.when(i == 0)
    def _():
        carry_ref[...] = jnp.zeros_like(carry_ref)

    def token_order(ref, slab0):
        dil, per = ref.shape[1], ref.shape[2]
        if dil == 1:
            return lambda rs: ref[0, 0, rs, :].astype(jnp.float32)
        for r in range(dil):
            val = ref[0, r].astype(jnp.float32)
            for s in range(2):
                perm_scr[slab0 + s, pl.ds(r, per, stride=dil), :] = val[:, s * LANES:(s + 1) * LANES]
        return lambda rs: jnp.concatenate([perm_scr[slab0, rs, :], perm_scr[slab0 + 1, rs, :]], axis=1)

    lse_of = [token_order(ref, 2 * j) for j, ref in enumerate((l0_ref, l1_ref, l2_ref))]
    o_of = [token_order(ref, 6 + 2 * j) for j, ref in enumerate((o0_ref, o1_ref, o2_ref))]
    sigmoid = lambda v: 0.5 * jnp.tanh(0.5 * v) + 0.5
    grow = lax.broadcasted_iota(jnp.int32, (8, hm), 0)
    erow = lax.broadcasted_iota(jnp.int32, (N_EXPERTS, hm), 0)
    frow = lax.broadcasted_iota(jnp.int32, (ROUTE_LANES, hm), 0)
    neg = jnp.float32(-jnp.inf)
    carry = carry_ref[:, 0:1]

    for h in range(POST_SPLIT):
        rs = slice(h * hm, (h + 1) * hm)
        l0, l1, l2 = (f(rs) for f in lse_of)
        lm = jnp.maximum(jnp.maximum(l0, l1), l2)
        e0, e1, e2 = jnp.exp(l0 - lm), jnp.exp(l1 - lm), jnp.exp(l2 - lm)
        inv = 1.0 / (e0 + e1 + e2)
        o_a = (e0 * inv) * o_of[0](rs) + (e1 * inv) * o_of[1](rs) + (e2 * inv) * o_of[2](rs)
        y_a = jnp.dot(o_a.astype(jnp.bfloat16), wa_ref[...], preferred_element_type=jnp.float32)
        y_b = jnp.dot(ob_ref[rs, :], wb_ref[...], preferred_element_type=jnp.float32)
        mix = sigmoid(ga_ref[rs, :].astype(jnp.float32)) * y_a + sigmoid(gb_ref[rs, :].astype(jnp.float32)) * y_b
        out = jnp.dot(mix.astype(jnp.bfloat16), wo_ref[...], preferred_element_type=jnp.float32) + bo_ref[...]
        x1 = _layer_norm(DN_ALPHA * x0_ref[rs, :] + out, g1_ref[...], b1_ref[...])
        _store_row_tiles(x1_ref, x1, row0=h * hm)

        r = jnp.dot(x1.astype(jnp.bfloat16), wr_ref[...], preferred_element_type=jnp.float32) + br_ref[...]
        rt = r.T
        lg = jnp.where(grow < N_GROUPS, rt[0:8], neg)
        gmax = jnp.max(lg, axis=0, keepdims=True)
        gi = jnp.min(jnp.where(lg == gmax, grow, 8), axis=0, keepdims=True)
        wg = 1.0 / jnp.sum(jnp.exp(lg - gmax), axis=0, keepdims=True)
        lo = EXPERTS_PER_GROUP * gi
        le = jnp.where((erow >= lo) & (erow < lo + EXPERTS_PER_GROUP), rt[EXPERT_LANE0:EXPERT_LANE0 + N_EXPERTS], neg)
        m1 = jnp.max(le, axis=0, keepdims=True)
        i1 = jnp.min(jnp.where(le == m1, erow, N_EXPERTS), axis=0, keepdims=True)
        le2 = jnp.where(erow == i1, neg, le)
        m2 = jnp.max(le2, axis=0, keepdims=True)
        i2 = jnp.min(jnp.where(le2 == m2, erow, N_EXPERTS), axis=0, keepdims=True)
        t2 = jnp.exp(m2 - m1)
        w1 = wg / (1.0 + t2)
        w2 = wg * t2 / (1.0 + t2)

        hot1 = erow == i1
        hot2 = erow == i2
        hot = (hot1 | hot2).astype(jnp.float32)
        before = jnp.dot(hot.astype(jnp.bfloat16), tri_ref[...], preferred_element_type=jnp.float32) + carry
        rank1 = jnp.sum(jnp.where(hot1, before, 0.0), axis=0, keepdims=True)
        rank2 = jnp.sum(jnp.where(hot2, before, 0.0), axis=0, keepdims=True)
        carry = carry + jnp.sum(hot, axis=1, keepdims=True)

        fields = (i1.astype(jnp.float32), i2.astype(jnp.float32), w1, w2, rank1, rank2)
        route_t = jnp.zeros((ROUTE_LANES, hm), jnp.float32)
        for idx, val in enumerate(fields):
            route_t = jnp.where(frow == idx, val, route_t)
        route_t_ref[:, rs] = route_t[0:route_t_ref.shape[0]]
        route_ref[rs, :] = route_t.T

    carry_ref[...] = jnp.broadcast_to(carry, carry_ref.shape)
    counts_ref[...] = jnp.broadcast_to(carry, counts_ref.shape)


def _post_attention(x0, o_groups, lse_groups, ob, ga, gb, wa, wb, wo, bo, g1, b1, wr, br):
    n, d = x0.shape
    tm = POST_TM
    nt = n // tm
    hm = tm // POST_SPLIT
    tri = jnp.asarray(np.triu(np.ones((hm, hm), np.float32), 1), jnp.bfloat16)

    def tok(width):
        return pl.BlockSpec((tm, width), lambda i: (i, 0))

    def full(a):
        return pl.BlockSpec(a.shape, lambda i: (0,) * a.ndim)

    return pl.pallas_call(
        _post_kernel,
        grid=(nt,),
        in_specs=[
            tok(d),
            *[pl.BlockSpec((1,) + a.shape[1:], lambda i: (i, 0, 0, 0)) for a in (*o_groups, *lse_groups)],
            tok(QB), tok(d), tok(d),
            full(wa), full(wb), full(wo), full(bo), full(g1), full(b1), full(wr), full(br), full(tri),
        ],
        out_specs=[pl.BlockSpec((tm * SUBLANES, LANES), lambda i: (i, 0)), tok(ROUTE_LANES),
                   pl.BlockSpec((8, tm), lambda i: (0, i)), pl.BlockSpec((N_EXPERTS, ROUTE_LANES), lambda i: (0, 0))],
        out_shape=[jax.ShapeDtypeStruct((n * SUBLANES, LANES), jnp.float32),
                   jax.ShapeDtypeStruct((n, ROUTE_LANES), jnp.float32),
                   jax.ShapeDtypeStruct((8, n), jnp.float32),
                   jax.ShapeDtypeStruct((N_EXPERTS, ROUTE_LANES), jnp.float32)],
        scratch_shapes=[pltpu.VMEM((N_EXPERTS, ROUTE_LANES), jnp.float32), pltpu.VMEM((12, tm, LANES), jnp.float32)],
        compiler_params=_cparams(("arbitrary",)),
        name="post_attention_router",
    )(x0, *o_groups, *lse_groups, ob, ga, gb, wa, wb, wo, bo, g1, b1, wr, br, tri)


def _dispatch_kernel(pos0_ref, pos1_ref, x_ref, xs_ref, sem):
    tt = x_ref.shape[0] // SUBLANES

    def issue(j, carry):
        for u in range(2):
            t = 2 * j + u
            src = x_ref.at[pl.ds(pl.multiple_of(t * SUBLANES, SUBLANES), SUBLANES)]
            for p_ref in (pos0_ref, pos1_ref):
                p = p_ref[t]
                dst = xs_ref.at[pl.ds(pl.multiple_of(p * SUBLANES, SUBLANES), SUBLANES)]
                pltpu.make_async_copy(src, dst, sem).start(priority=u)
        return carry

    lax.fori_loop(0, tt // 2, issue, 0, unroll=4)
    for _ in range(2):
        pltpu.make_async_copy(x_ref, xs_ref.at[pl.ds(0, tt * SUBLANES)], sem).wait()


def _dispatch(x1t, pos, n_rows):
    n = x1t.shape[0] // SUBLANES
    tt = DISP_TT
    return pl.pallas_call(
        _dispatch_kernel,
        grid=(n // tt,),
        in_specs=[pl.BlockSpec((tt,), lambda i: (i,), memory_space=pltpu.SMEM),
                  pl.BlockSpec((tt,), lambda i: (i,), memory_space=pltpu.SMEM),
                  pl.BlockSpec((tt * SUBLANES, LANES), lambda i: (i, 0))],
        out_specs=pl.BlockSpec(memory_space=pl.ANY),
        out_shape=jax.ShapeDtypeStruct((n_rows * SUBLANES, LANES), x1t.dtype),
        scratch_shapes=[pltpu.SemaphoreType.DMA(())],
        compiler_params=_cparams(("arbitrary",)),
        name="moe_dispatch",
    )(pos[0], pos[1], x1t)


def _ffn_kernel(item_expert_ref, item_tile_ref, item_lo_ref, item_hi_ref, n_items_ref,
                x_ref, wg_ref, wu_ref, wd_ref, y_ref, wg_bf, wu_bf, wd_bf, y_acc):
    j = pl.program_id(0)

    @pl.when(j == 0)
    def _():
        y_acc[...] = jnp.zeros_like(y_acc)

    @pl.when(j < n_items_ref[0])
    def _():
        prev = jnp.maximum(j - 1, 0)

        @pl.when(jnp.logical_or(j == 0, item_expert_ref[j] != item_expert_ref[prev]))
        def _():
            wg_bf[...] = wg_ref[0].astype(jnp.bfloat16)
            wu_bf[...] = wu_ref[0].astype(jnp.bfloat16)
            wd_bf[...] = wd_ref[0].astype(jnp.bfloat16)

        tm = x_ref.shape[0] // SUBLANES
        x = _load_row_tiles(x_ref, tm).astype(jnp.bfloat16)
        a = jnp.dot(x, wg_bf[...], preferred_element_type=jnp.float32)
        u = jnp.dot(x, wu_bf[...], preferred_element_type=jnp.float32)
        h = (a * jax.nn.sigmoid(a) * u).astype(jnp.bfloat16)
        y = jnp.dot(h, wd_bf[...], preferred_element_type=jnp.float32)
        rows = lax.broadcasted_iota(jnp.int32, y.shape, 0)
        pltpu.store(y_acc, y, mask=(rows >= item_lo_ref[j]) & (rows < item_hi_ref[j]))
        _store_row_tiles(y_ref, y_acc[...])


def _ffn_items(cnt, n_rows):
    tm = FFN_TM
    n_tiles = n_rows // tm
    max_items = n_tiles + N_EXPERTS - 1
    ends = jnp.cumsum(cnt)
    starts = ends - cnt
    first_tile = starts // tm
    last_tile = jnp.maximum(ends - 1, 0) // tm
    items_per_expert = jnp.where(cnt > 0, last_tile - first_tile + 1, 0)
    item_ends = jnp.cumsum(items_per_expert)
    n_items = item_ends[-1:]
    j = jnp.minimum(jnp.arange(max_items, dtype=jnp.int32), n_items[0] - 1)
    expert = jnp.minimum(jnp.sum(item_ends[None, :] <= j[:, None], axis=1), N_EXPERTS - 1).astype(jnp.int32)
    tile = first_tile[expert] + (j - (item_ends - items_per_expert)[expert])
    lo = jnp.clip(starts[expert] - tile * tm, 0, tm)
    hi = jnp.clip(ends[expert] - tile * tm, 0, tm)
    i32 = lambda a: a.astype(jnp.int32)
    return i32(expert), i32(tile), i32(lo), i32(hi), i32(n_items), starts


def _grouped_ffn(xs, items, w_gate, w_up, w_down):
    tm = FFN_TM
    d, de = w_gate.shape[1:]
    item_expert, item_tile, item_lo, item_hi, n_items = items
    max_items = item_expert.shape[0]

    def row_map(j, ie, it, lo, hi, ni):
        return (it[j], 0)

    def w_map(j, ie, it, lo, hi, ni):
        return (ie[j], 0, 0)

    grid_spec = pltpu.PrefetchScalarGridSpec(
        num_scalar_prefetch=5,
        grid=(max_items,),
        in_specs=[pl.BlockSpec((tm * SUBLANES, LANES), row_map),
                  pl.BlockSpec((1, d, de), w_map),
                  pl.BlockSpec((1, d, de), w_map),
                  pl.BlockSpec((1, de, d), w_map)],
        out_specs=pl.BlockSpec((tm * SUBLANES, LANES), row_map),
        scratch_shapes=[pltpu.VMEM((d, de), jnp.bfloat16), pltpu.VMEM((d, de), jnp.bfloat16),
                        pltpu.VMEM((de, d), jnp.bfloat16), pltpu.VMEM((tm, d), jnp.float32)],
    )
    return pl.pallas_call(
        _ffn_kernel,
        grid_spec=grid_spec,
        out_shape=jax.ShapeDtypeStruct(xs.shape, jnp.float32),
        compiler_params=_cparams(("arbitrary",)),
        name="moe_grouped_ffn",
    )(item_expert, item_tile, item_lo, item_hi, n_items, xs, w_gate, w_up, w_down)


def _combine_kernel(pos0_ref, pos1_ref, next0_ref, next1_ref, x1_ref, route_ref, g2_ref, b2_ref, ys_ref, o_ref,
                    buf, sems):
    tt = x1_ref.shape[0] // SUBLANES
    i = pl.program_id(0)
    cur = lax.rem(i, 2)

    def issue_tile(p_refs, b):
        def issue(j, carry):
            for u in range(2):
                t = 2 * j + u
                for slot, p_ref in enumerate(p_refs):
                    p = p_ref[t]
                    src = ys_ref.at[pl.ds(pl.multiple_of(p * SUBLANES, SUBLANES), SUBLANES)]
                    dst = buf.at[b, slot, pl.ds(pl.multiple_of(t * SUBLANES, SUBLANES), SUBLANES)]
                    pltpu.make_async_copy(src, dst, sems.at[b]).start(priority=u)
            return carry

        lax.fori_loop(0, tt // 2, issue, 0, unroll=4)

    @pl.when(i == 0)
    def _():
        issue_tile((pos0_ref, pos1_ref), 0)

    @pl.when(i + 1 < pl.num_programs(0))
    def _():
        issue_tile((next0_ref, next1_ref), 1 - cur)

    for slot in range(2):
        pltpu.make_async_copy(ys_ref.at[pl.ds(0, tt * SUBLANES)], buf.at[cur, slot], sems.at[cur]).wait()
    route = route_ref[...]
    y = route[:, 2:3] * _load_row_tiles(buf.at[cur, 0], tt) + route[:, 3:4] * _load_row_tiles(buf.at[cur, 1], tt)
    o_ref[...] = _layer_norm(DN_ALPHA * _load_row_tiles(x1_ref, tt) + y, g2_ref[...], b2_ref[...])


def _combine(x1t, route, pos, ys, g2, b2, tok0, n_tok):
    d = g2.shape[1]
    tt = COMB_TT
    t0 = tok0 // tt
    nt = n_tok // tt
    return pl.pallas_call(
        _combine_kernel,
        grid=(nt,),
        in_specs=[pl.BlockSpec((tt,), lambda i: (i + t0,), memory_space=pltpu.SMEM),
                  pl.BlockSpec((tt,), lambda i: (i + t0,), memory_space=pltpu.SMEM),
                  pl.BlockSpec((tt,), lambda i: (jnp.minimum(i + 1, nt - 1) + t0,), memory_space=pltpu.SMEM),
                  pl.BlockSpec((tt,), lambda i: (jnp.minimum(i + 1, nt - 1) + t0,), memory_space=pltpu.SMEM),
                  pl.BlockSpec((tt * SUBLANES, LANES), lambda i: (i + t0, 0)),
                  pl.BlockSpec((tt, ROUTE_LANES), lambda i: (i + t0, 0)),
                  pl.BlockSpec((1, d), lambda i: (0, 0)),
                  pl.BlockSpec((1, d), lambda i: (0, 0)),
                  pl.BlockSpec(memory_space=pl.ANY)],
        out_specs=pl.BlockSpec((tt, d), lambda i: (i, 0)),
        out_shape=jax.ShapeDtypeStruct((n_tok, d), jnp.float32),
        scratch_shapes=[pltpu.VMEM((2, 2, tt * SUBLANES, LANES), ys.dtype), pltpu.SemaphoreType.DMA((2,))],
        compiler_params=_cparams(("arbitrary",)),
        name="moe_combine_ln",
    )(pos[0], pos[1], pos[0], pos[1], x1t, route, g2, b2, ys)


def _layer(x_prompt, x_sample, ln_in_g, ln_in_b, w_in, b_in, rel_bias_t5, na_rpb, w_branch_a, w_branch_b,
           w_out, b_out, ln1_g, ln1_b, w_router_group, b_router_group, w_router_expert, b_router_expert,
           w_exp_gate, w_exp_up, w_exp_down, ln2_g, ln2_b):
    bp, tp, d = x_prompt.shape
    bs, ts, _ = x_sample.shape
    seg_tokens = (bp * tp, bs * ts)
    seq_lens = (tp, ts)
    n = sum(seg_tokens)
    xp = x_prompt.reshape(seg_tokens[0], d)
    xs = x_sample.reshape(seg_tokens[1], d)
    row = lambda a: a.reshape(1, -1).astype(jnp.float32)
    bf = lambda a: a.astype(jnp.bfloat16)

    scale = HEAD_DIM ** -0.5
    col_scale = np.ones((w_in.shape[2],), np.float32)
    col_scale[0:QA] = scale
    col_scale[3 * QA:3 * QA + QB] = scale
    w_proj = bf(w_in[0] * col_scale)
    b_proj = row(b_in[0] * col_scale)
    proj = _in_projection(xp, xs, row(ln_in_g), row(ln_in_b), w_proj, b_proj)
    qkv_a, (qb, kb, vb, ga, gb, x0) = proj[:3 * N_DIL_GROUPS], proj[3 * N_DIL_GROUPS:]

    o_groups, lse_groups = [], []
    for g, (window, dil) in enumerate(DIL_PATTERNS):
        assert window // (2 * dil) == HALF_SPAN
        table = rel_bias_t5[:, g * HEADS_PER_DIL_GROUP:(g + 1) * HEADS_PER_DIL_GROUP].astype(jnp.float32).reshape(-1)
        o, lse = _dilated_group(qkv_a[g], qkv_a[N_DIL_GROUPS + g], qkv_a[2 * N_DIL_GROUPS + g], table, dil,
                                seg_tokens, seq_lens)
        o_groups.append(o)
        lse_groups.append(lse)
    ob = _neighborhood(qb, kb, vb, na_rpb[0].astype(jnp.float32).reshape(-1), seg_tokens, seq_lens)

    wr = jnp.zeros((d, ROUTE_LANES), jnp.float32)
    wr = wr.at[:, 0:N_GROUPS].set(w_router_group[0]).at[:, EXPERT_LANE0:EXPERT_LANE0 + N_EXPERTS].set(w_router_expert[0])
    br = jnp.zeros((1, ROUTE_LANES), jnp.float32)
    br = br.at[0, 0:N_GROUPS].set(b_router_group[0]).at[0, EXPERT_LANE0:EXPERT_LANE0 + N_EXPERTS].set(b_router_expert[0])
    x1, route, route_t, counts = _post_attention(
        x0, o_groups, lse_groups, ob, ga, gb, bf(w_branch_a[0]), bf(w_branch_b[0]),
        bf(w_out[0]), row(b_out[0]), row(ln1_g[0]), row(ln1_b[0]), bf(wr), br)

    experts = route_t[0:2].astype(jnp.int32)
    ranks = route_t[4:6].astype(jnp.int32)
    cnt = counts[:, 0].astype(jnp.int32)
    n_rows = 2 * n
    *items, starts = _ffn_items(cnt, n_rows)
    one_hot = (experts[..., None] == jnp.arange(N_EXPERTS, dtype=jnp.int32)).astype(jnp.float32)
    pos = ranks + jnp.dot(one_hot, starts.astype(jnp.float32), precision=lax.Precision.HIGHEST).astype(jnp.int32)

    xsorted = _dispatch(x1, pos, n_rows)
    ys = _grouped_ffn(xsorted, items, w_exp_gate[0], w_exp_up[0], w_exp_down[0])
    y_prompt = _combine(x1, route, pos, ys, row(ln2_g[0]), row(ln2_b[0]), 0, seg_tokens[0])
    y_sample = _combine(x1, route, pos, ys, row(ln2_g[0]), row(ln2_b[0]), seg_tokens[0], seg_tokens[1])
    return y_prompt.reshape(bp, tp, d), y_sample.reshape(bs, ts, d)


def kernel(x_prompt, x_sample, ln_in_g, ln_in_b, w_in, b_in, rel_bias_t5, na_rpb, w_branch_a, w_branch_b, w_out, b_out, ln1_g, ln1_b, w_router_group, b_router_group, w_router_expert, b_router_expert, w_exp_gate, w_exp_up, w_exp_down, ln2_g, ln2_b):
    return _layer(x_prompt, x_sample, ln_in_g, ln_in_b, w_in, b_in, rel_bias_t5, na_rpb, w_branch_a, w_branch_b,
                  w_out, b_out, ln1_g, ln1_b, w_router_group, b_router_group, w_router_expert, b_router_expert,
                  w_exp_gate, w_exp_up, w_exp_down, ln2_g, ln2_b)
```

```python
import functools
import math

import numpy as np
import jax
import jax.numpy as jnp
from jax import lax
from jax.experimental import pallas as pl
from jax.experimental.pallas import tpu as pltpu

HEAD_DIM = 64
DIL_PATTERNS = ((128, 1), (512, 4), (2048, 16))
HEADS_PER_DIL_GROUP = 4
N_DIL_GROUPS = 3
H_A = N_DIL_GROUPS * HEADS_PER_DIL_GROUP
H_B = 8
QA = H_A * HEAD_DIM
QB = H_B * HEAD_DIM
HALF_SPAN = 64
GRID_W = 64
NA_KH = 8
NA_KW = 16
N_BUCKETS = 32
REL_MAX_DIST = 1024
N_GROUPS = 4
EXPERTS_PER_GROUP = 8
N_EXPERTS = N_GROUPS * EXPERTS_PER_GROUP
LN_EPS = 1e-5
NEG_INF = -1e30
DEPTH = 1
DN_ALPHA = (2.0 * DEPTH) ** 0.25

LANES = 128
HEAD_BLOCK = 4 * HEAD_DIM
VMEM_LIMIT_BYTES = 56 * 1024 * 1024

PROJ_TM = 512
DIL_M = 128
DIL_GROUP = 16
NA_ROWS = 16
NA_HALO = NA_KH // 2
POST_TM = 512
POST_SPLIT = 1
FFN_TM = 512
DISP_TT = 4096
COMB_TT = 1024
ROUTE_LANES = LANES
EXPERT_LANE0 = 32


def _cparams(sem):
    return pltpu.CompilerParams(dimension_semantics=sem, vmem_limit_bytes=VMEM_LIMIT_BYTES)


SUBLANES = 8


def _store_row_tiles(ref, x, row0=0):
    rows = x.shape[0]
    for c in range(SUBLANES):
        ref[pl.ds(row0 * SUBLANES + c, rows, stride=SUBLANES), :] = x[:, c * LANES:(c + 1) * LANES]


def _load_row_tiles(ref, rows, row0=0):
    return jnp.concatenate([ref[pl.ds(row0 * SUBLANES + c, rows, stride=SUBLANES), :] for c in range(SUBLANES)],
                           axis=1)


def _layer_norm(x, g, b):
    mu = jnp.mean(x, axis=-1, keepdims=True)
    xc = x - mu
    var = jnp.mean(xc * xc, axis=-1, keepdims=True)
    return xc * lax.rsqrt(var + LN_EPS) * g + b


def _strided_rows(scr, slab0, r, count, stride):
    return jnp.concatenate([scr[slab0 + s, pl.ds(r, count, stride=stride), :] for s in range(2)], axis=1)


def _inproj_kernel(xp_ref, xs_ref, g_ref, b_ref, w_ref, bias_ref, *refs, n0_tiles):
    out_refs, scr = refs[:-1], refs[-1]
    i = pl.program_id(0)
    x = jnp.where(i < n0_tiles, xp_ref[...], xs_ref[...])
    x0 = _layer_norm(x, g_ref[...], b_ref[...])
    out_refs[-1][...] = x0
    out_refs = out_refs[:-1]
    xn = x0.astype(jnp.bfloat16)
    tm = xn.shape[0]

    def proj(c0, cw):
        return jnp.dot(xn, w_ref[:, c0:c0 + cw], preferred_element_type=jnp.float32) + bias_ref[:, c0:c0 + cw]

    n_dil = 3 * N_DIL_GROUPS
    for idx in range(n_dil):
        g = idx % N_DIL_GROUPS
        dil = DIL_PATTERNS[g][1]
        o_ref = out_refs[idx]
        y = proj(idx * HEAD_BLOCK, HEAD_BLOCK)
        if dil == 1:
            o_ref[0, 0] = y.astype(o_ref.dtype)
        else:
            slab0 = 2 * (idx % 2)
            scr[slab0] = y[:, :LANES]
            scr[slab0 + 1] = y[:, LANES:]
            for r in range(dil):
                o_ref[0, r] = _strided_rows(scr, slab0, r, tm // dil, dil).astype(o_ref.dtype)
    col = n_dil * HEAD_BLOCK
    for o_ref in out_refs[n_dil:]:
        width = o_ref.shape[1]
        for c in range(0, width, 512):
            cw = min(512, width - c)
            o_ref[:, c:c + cw] = proj(col + c, cw).astype(o_ref.dtype)
        col += width


def _in_projection(xp, xs, ln_g, ln_b, w, b):
    n0, d_model = xp.shape
    n1 = xs.shape[0]
    n = n0 + n1
    tm = PROJ_TM
    n0_tiles, n1_tiles = n0 // tm, n1 // tm
    nt = n0_tiles + n1_tiles
    out_shape, out_specs = [], []
    for _ in range(3):
        for _, dil in DIL_PATTERNS:
            out_shape.append(jax.ShapeDtypeStruct((nt, dil, tm // dil, HEAD_BLOCK), jnp.bfloat16))
            out_specs.append(pl.BlockSpec((1, dil, tm // dil, HEAD_BLOCK), lambda i: (i, 0, 0, 0)))
    for wd, dtype in ((QB, jnp.bfloat16),) * 3 + ((d_model, jnp.bfloat16),) * 2 + ((d_model, jnp.float32),):
        out_shape.append(jax.ShapeDtypeStruct((n, wd), dtype))
        out_specs.append(pl.BlockSpec((tm, wd), lambda i: (i, 0)))
    d_in = w.shape[1]
    kern = functools.partial(_inproj_kernel, n0_tiles=n0_tiles)
    return pl.pallas_call(
        kern,
        grid=(nt,),
        in_specs=[
            pl.BlockSpec((tm, d_model), lambda i: (jnp.minimum(i, n0_tiles - 1), 0)),
            pl.BlockSpec((tm, d_model), lambda i: (jnp.maximum(i - n0_tiles, 0), 0)),
            pl.BlockSpec((1, d_model), lambda i: (0, 0)),
            pl.BlockSpec((1, d_model), lambda i: (0, 0)),
            pl.BlockSpec((d_model, d_in), lambda i: (0, 0)),
            pl.BlockSpec((1, d_in), lambda i: (0, 0)),
        ],
        out_specs=out_specs,
        out_shape=out_shape,
        scratch_shapes=[pltpu.VMEM((4, tm, LANES), jnp.float32)],
        compiler_params=_cparams(("arbitrary",)),
        name="in_projection",
    )(xp, xs, ln_g, ln_b, w, b)


def _t5_bucket_np(rel):
    half = N_BUCKETS // 2
    max_exact = half // 2
    ret = np.where(rel > 0, half, 0)
    n = np.abs(rel)
    nf = np.maximum(n, 1).astype(np.float32)
    large = max_exact + (np.log(nf / np.float32(max_exact)) / np.float32(math.log(REL_MAX_DIST / max_exact))
                         * np.float32(half - max_exact)).astype(np.int32)
    large = np.minimum(large, half - 1)
    return ret + np.where(n < max_exact, n, large)


def _t5_bucket_starts():
    half = N_BUCKETS // 2
    dist = np.arange(0, HALF_SPAN * max(d for _, d in DIL_PATTERNS) + 1)
    buckets = _t5_bucket_np(-dist)
    return [int(np.argmax(buckets >= k)) for k in range(half)]


def _build_dilated_bias(table_ref, bias_scr, dil):
    half = N_BUCKETS // 2
    starts = _t5_bucket_starts()
    kb = DIL_M + 2 * HALF_SPAN
    i = lax.broadcasted_iota(jnp.int32, (DIL_M, kb), 0)
    m = lax.broadcasted_iota(jnp.int32, (DIL_M, kb), 1)
    delta = m - HALF_SPAN - i
    dist = jnp.abs(delta) * dil
    band = jnp.abs(delta) <= HALF_SPAN
    for h in range(HEADS_PER_DIL_GROUP):
        sides = []
        for side in range(2):
            val = jnp.full((DIL_M, kb), table_ref[(side * half + half - 1) * HEADS_PER_DIL_GROUP + h], jnp.float32)
            for k in range(half - 1, 0, -1):
                val = jnp.where(dist < starts[k], table_ref[(side * half + k - 1) * HEADS_PER_DIL_GROUP + h], val)
            sides.append(val)
        base = jnp.where(band, jnp.where(delta > 0, sides[1], sides[0]), NEG_INF)
        rows = slice(h * DIL_M, (h + 1) * DIL_M)
        bias_scr[0, rows, :] = base
        bias_scr[1, rows, :] = jnp.where(m >= HALF_SPAN, base, NEG_INF)
        bias_scr[2, rows, :] = jnp.where(m < kb - HALF_SPAN, base, NEG_INF)


def _stack_heads(q):
    head_of_col = lax.broadcasted_iota(jnp.int32, (1, HEAD_BLOCK), 1) // HEAD_DIM
    zero = jnp.zeros_like(q)
    return jnp.concatenate([jnp.where(head_of_col == h, q, zero) for h in range(4)], axis=0)


def _merge_heads(x, m):
    head_of_col = lax.broadcasted_iota(jnp.int32, (1, HEAD_BLOCK), 1) // HEAD_DIM
    out = jnp.zeros((m, HEAD_BLOCK), x.dtype)
    for h in range(4):
        out = jnp.where(head_of_col == h, x[h * m:(h + 1) * m], out)
    return out


def _softmax_pv(s, v, m_rows):
    mx = jnp.max(s, axis=-1, keepdims=True)
    p = jnp.exp(s - mx)
    l = jnp.sum(p, axis=-1, keepdims=True)
    pv = jnp.dot(p.astype(jnp.bfloat16), v, preferred_element_type=jnp.float32)
    o = _merge_heads(pv * (1.0 / l), m_rows)
    return o, mx + jnp.log(l)


def _dilated_kernel(table_ref, q_ref, kp_ref, kc_ref, kn_ref, vp_ref, vc_ref, vn_ref,
                    o_ref, lse_ref, bias_scr, *, dil, seg0_blocks, blocks_per_seq):
    step = pl.program_id(1)

    @pl.when((pl.program_id(0) == 0) & (step == 0))
    def _():
        _build_dilated_bias(table_ref, bias_scr, dil)

    rows2d = lambda ref: ref[...].reshape(-1, HEAD_BLOCK)
    k = jnp.concatenate([rows2d(r) for r in (kp_ref, kc_ref, kn_ref)], axis=0)
    v = jnp.concatenate([rows2d(r) for r in (vp_ref, vc_ref, vn_ref)], axis=0)
    q = rows2d(q_ref)
    o_parts, lse_parts = [], []
    for g in range(DIL_GROUP):
        jb = step * DIL_GROUP + g
        in0 = jb < seg0_blocks
        jl = jnp.where(in0, jb, jb - seg0_blocks)
        nbs = jnp.where(in0, blocks_per_seq[0], blocks_per_seq[1])
        pos = lax.rem(jl, nbs)
        variant = jnp.where(pos == 0, 1, jnp.where(pos == nbs - 1, 2, 0))
        keys = slice(g * DIL_M, (g + 1) * DIL_M + 2 * HALF_SPAN)
        qs = _stack_heads(q[g * DIL_M:(g + 1) * DIL_M])
        s = lax.dot_general(qs, k[keys], (((1,), (1,)), ((), ())), preferred_element_type=jnp.float32)
        s = s + bias_scr[variant]
        o, lse = _softmax_pv(s, v[keys], DIL_M)
        o_parts.append(o.astype(o_ref.dtype))
        lse_parts.append(_merge_heads(jnp.broadcast_to(lse, (4 * DIL_M, HEAD_BLOCK)), DIL_M))
    o_ref[...] = jnp.concatenate(o_parts, axis=0).reshape(o_ref.shape)
    lse_ref[...] = jnp.concatenate(lse_parts, axis=0).reshape(lse_ref.shape)


def _residue_spec(rows_blk, dil, block_of):
    per_tile = PROJ_TM // dil
    if per_tile >= rows_blk:
        per = per_tile // rows_blk
        return pl.BlockSpec((1, None, rows_blk, HEAD_BLOCK),
                            lambda r, jb: (block_of(jb) // per, r, block_of(jb) % per, 0))
    return pl.BlockSpec((rows_blk // per_tile, None, per_tile, HEAD_BLOCK), lambda r, jb: (block_of(jb), r, 0, 0))


def _dilated_group(q, k, v, table, dil, seg_tokens, seq_lens):
    n = q.shape[0] * PROJ_TM
    rows = n // dil
    step_rows = DIL_GROUP * DIL_M
    assert rows % step_rows == 0 and all(t % (dil * DIL_M) == 0 for t in seq_lens)
    nsteps = rows // step_rows
    nkb = rows // HALF_SPAN
    ratio = step_rows // HALF_SPAN
    seg0_blocks = seg_tokens[0] // dil // DIL_M
    blocks_per_seq = tuple(t // dil // DIL_M for t in seq_lens)
    assert min(blocks_per_seq) >= 2

    qo_spec = _residue_spec(step_rows, dil, lambda jb: jb)
    prev_spec = _residue_spec(HALF_SPAN, dil, lambda jb: jnp.maximum(ratio * jb - 1, 0))
    next_spec = _residue_spec(HALF_SPAN, dil, lambda jb: jnp.minimum(ratio * (jb + 1), nkb - 1))
    kv_specs = [prev_spec, qo_spec, next_spec]
    kern = functools.partial(_dilated_kernel, dil=dil, seg0_blocks=seg0_blocks, blocks_per_seq=blocks_per_seq)
    return pl.pallas_call(
        kern,
        grid=(dil, nsteps),
        in_specs=[pl.BlockSpec(memory_space=pltpu.SMEM), qo_spec] + kv_specs + kv_specs,
        out_specs=[qo_spec, qo_spec],
        out_shape=[jax.ShapeDtypeStruct(q.shape, jnp.bfloat16), jax.ShapeDtypeStruct(q.shape, jnp.float32)],
        scratch_shapes=[pltpu.VMEM((3, HEADS_PER_DIL_GROUP * DIL_M, DIL_M + 2 * HALF_SPAN), jnp.float32)],
        compiler_params=_cparams(("arbitrary", "arbitrary")),
        name=f"dilated_attention_d{dil}",
    )(table, q, k, k, k, v, v, v)


NA_DR = 2 * NA_KH - 1
NA_DC = 2 * NA_KW - 1
NA_PAIRS = NA_DR - 1


def _build_na_bias(rpb_ref, pair_scr, head0):
    qc = lax.broadcasted_iota(jnp.int32, (GRID_W, 2 * GRID_W), 0)
    lane = lax.broadcasted_iota(jnp.int32, (GRID_W, 2 * GRID_W), 1)
    kc = lane % GRID_W
    upper = lane >= GRID_W
    qstart = jnp.clip(qc - NA_KW // 2, 0, GRID_W - NA_KW)
    valid = (kc >= qstart) & (kc < qstart + NA_KW)
    dc = jnp.clip(kc - qc, -(NA_KW - 1), NA_KW - 1) + NA_KW - 1

    def one_pair(idx, carry):
        h = idx // NA_PAIRS
        dr = idx % NA_PAIRS
        base = ((head0 + h) * NA_DR + dr) * NA_DC
        val = jnp.zeros((GRID_W, 2 * GRID_W), jnp.float32)
        for c in range(NA_DC):
            val = jnp.where(dc == c, jnp.where(upper, rpb_ref[base + NA_DC + c], rpb_ref[base + c]), val)
        pair_scr[h, dr] = jnp.where(valid, val, NEG_INF)
        return carry

    lax.fori_loop(0, 4 * NA_PAIRS, one_pair, 0)


def _na_kernel(rpb_ref, q_ref, kp_ref, kc_ref, kn_ref, vp_ref, vc_ref, vn_ref, o_ref, k_scr, v_scr, pair_scr,
               *, seg0_blocks, blocks_per_seq):
    ib = pl.program_id(1)

    @pl.when(ib == 0)
    def _():
        _build_na_bias(rpb_ref, pair_scr, pl.program_id(0) * 4)

    in0 = ib < seg0_blocks
    il = jnp.where(in0, ib, ib - seg0_blocks)
    nbs = jnp.where(in0, blocks_per_seq[0], blocks_per_seq[1])
    pos = lax.rem(il, nbs)
    first = pos == 0
    last = pos == nbs - 1
    blk = NA_ROWS * GRID_W
    halo = NA_HALO * GRID_W
    for scr, (p_ref, c_ref, n_ref) in ((k_scr, (kp_ref, kc_ref, kn_ref)), (v_scr, (vp_ref, vc_ref, vn_ref))):
        scr[0:halo] = p_ref[...]
        scr[halo:halo + blk] = c_ref[...]
        scr[halo + blk:2 * halo + blk] = n_ref[...]
    half = NA_KH // 2
    for rr in range(NA_ROWS):
        inner = NA_HALO + rr - half
        s_first, s_last = max(inner, NA_HALO), min(inner, NA_HALO + NA_ROWS - NA_KH)
        start = jnp.where(first, s_first, jnp.where(last, s_last, inner))
        var = jnp.where(first, NA_HALO + rr - s_first, jnp.where(last, NA_HALO + rr - s_last, half))
        off = pl.multiple_of(start * GRID_W, GRID_W)
        kk = k_scr[pl.ds(off, NA_KH * GRID_W), :]
        vv = v_scr[pl.ds(off, NA_KH * GRID_W), :]
        qs = _stack_heads(q_ref[rr * GRID_W:(rr + 1) * GRID_W, :])
        s = lax.dot_general(qs, kk, (((1,), (1,)), ((), ())), preferred_element_type=jnp.float32)
        bias = jnp.concatenate(
            [jnp.concatenate([pair_scr[h, 2 * p - var + NA_KH - 1] for p in range(NA_KH // 2)], axis=1)
             for h in range(4)], axis=0)
        s = s + bias
        o, _ = _softmax_pv(s, vv, GRID_W)
        o_ref[rr * GRID_W:(rr + 1) * GRID_W, :] = o.astype(o_ref.dtype)


def _neighborhood(qb, kb, vb, rpb_flat, seg_tokens, seq_lens):
    n = qb.shape[0]
    blk = NA_ROWS * GRID_W
    nblk = n // blk
    seg0_blocks = seg_tokens[0] // blk
    blocks_per_seq = tuple(t // blk for t in seq_lens)
    assert min(blocks_per_seq) >= 2
    nset = QB // HEAD_BLOCK

    per = NA_ROWS // NA_HALO
    halo = NA_HALO * GRID_W
    cur_spec = pl.BlockSpec((blk, HEAD_BLOCK), lambda c, ib: (ib, c))
    prev_spec = pl.BlockSpec((halo, HEAD_BLOCK), lambda c, ib: (jnp.maximum(per * ib - 1, 0), c))
    next_spec = pl.BlockSpec((halo, HEAD_BLOCK), lambda c, ib: (jnp.minimum(per * (ib + 1), per * nblk - 1), c))
    kv_specs = [prev_spec, cur_spec, next_spec]

    kern = functools.partial(_na_kernel, seg0_blocks=seg0_blocks, blocks_per_seq=blocks_per_seq)
    return pl.pallas_call(
        kern,
        grid=(nset, nblk),
        in_specs=[pl.BlockSpec(memory_space=pltpu.SMEM), cur_spec] + kv_specs + kv_specs,
        out_specs=cur_spec,
        out_shape=jax.ShapeDtypeStruct((n, QB), jnp.bfloat16),
        scratch_shapes=[pltpu.VMEM((blk + 2 * halo, HEAD_BLOCK), jnp.bfloat16),
                        pltpu.VMEM((blk + 2 * halo, HEAD_BLOCK), jnp.bfloat16),
                        pltpu.VMEM((4, NA_PAIRS, GRID_W, 2 * GRID_W), jnp.float32)],
        compiler_params=_cparams(("arbitrary", "arbitrary")),
        name="neighborhood_attention",
    )(rpb_flat, qb, kb, kb, kb, vb, vb, vb)


def _post_kernel(x0_ref, o0_ref, o1_ref, o2_ref, l0_ref, l1_ref, l2_ref, ob_ref,
                 ga_ref, gb_ref, wa_ref, wb_ref, wo_ref, bo_ref, g1_ref, b1_ref, wr_ref, br_ref, tri_ref,
                 x1_ref, route_ref, route_t_ref, counts_ref, carry_ref, perm_scr):
    i = pl.program_id(0)
    tm = route_ref.shape[0]
    hm = tm // POST_SPLIT

    @pl.when(i == 0)
    def _():
        carry_ref[...] = jnp.zeros_like(carry_ref)

    def token_order(ref, slab0):
        dil, per = ref.shape[1], ref.shape[2]
        if dil == 1:
            return lambda rs: ref[0, 0, rs, :].astype(jnp.float32)
        for r in range(dil):
            val = ref[0, r].astype(jnp.float32)
            for s in range(2):
                perm_scr[slab0 + s, pl.ds(r, per, stride=dil), :] = val[:, s * LANES:(s + 1) * LANES]
        return lambda rs: jnp.concatenate([perm_scr[slab0, rs, :], perm_scr[slab0 + 1, rs, :]], axis=1)

    lse_of = [token_order(ref, 2 * j) for j, ref in enumerate((l0_ref, l1_ref, l2_ref))]
    o_of = [token_order(ref, 6 + 2 * j) for j, ref in enumerate((o0_ref, o1_ref, o2_ref))]
    sigmoid = lambda v: 0.5 * jnp.tanh(0.5 * v) + 0.5
    grow = lax.broadcasted_iota(jnp.int32, (8, hm), 0)
    erow = lax.broadcasted_iota(jnp.int32, (N_EXPERTS, hm), 0)
    frow = lax.broadcasted_iota(jnp.int32, (ROUTE_LANES, hm), 0)
    neg = jnp.float32(-jnp.inf)
    carry = carry_ref[:, 0:1]

    for h in range(POST_SPLIT):
        rs = slice(h * hm, (h + 1) * hm)
        l0, l1, l2 = (f(rs) for f in lse_of)
        lm = jnp.maximum(jnp.maximum(l0, l1), l2)
        e0, e1, e2 = jnp.exp(l0 - lm), jnp.exp(l1 - lm), jnp.exp(l2 - lm)
        inv = 1.0 / (e0 + e1 + e2)
        o_a = (e0 * inv) * o_of[0](rs) + (e1 * inv) * o_of[1](rs) + (e2 * inv) * o_of[2](rs)
        y_a = jnp.dot(o_a.astype(jnp.bfloat16), wa_ref[...], preferred_element_type=jnp.float32)
        y_b = jnp.dot(ob_ref[rs, :], wb_ref[...], preferred_element_type=jnp.float32)
        mix = sigmoid(ga_ref[rs, :].astype(jnp.float32)) * y_a + sigmoid(gb_ref[rs, :].astype(jnp.float32)) * y_b
        out = jnp.dot(mix.astype(jnp.bfloat16), wo_ref[...], preferred_element_type=jnp.float32) + bo_ref[...]
        x1 = _layer_norm(DN_ALPHA * x0_ref[rs, :] + out, g1_ref[...], b1_ref[...])
        _store_row_tiles(x1_ref, x1, row0=h * hm)

        r = jnp.dot(x1.astype(jnp.bfloat16), wr_ref[...], preferred_element_type=jnp.float32) + br_ref[...]
        rt = r.T
        lg = jnp.where(grow < N_GROUPS, rt[0:8], neg)
        gmax = jnp.max(lg, axis=0, keepdims=True)
        gi = jnp.min(jnp.where(lg == gmax, grow, 8), axis=0, keepdims=True)
        wg = 1.0 / jnp.sum(jnp.exp(lg - gmax), axis=0, keepdims=True)
        lo = EXPERTS_PER_GROUP * gi
        le = jnp.where((erow >= lo) & (erow < lo + EXPERTS_PER_GROUP), rt[EXPERT_LANE0:EXPERT_LANE0 + N_EXPERTS], neg)
        m1 = jnp.max(le, axis=0, keepdims=True)
        i1 = jnp.min(jnp.where(le == m1, erow, N_EXPERTS), axis=0, keepdims=True)
        le2 = jnp.where(erow == i1, neg, le)
        m2 = jnp.max(le2, axis=0, keepdims=True)
        i2 = jnp.min(jnp.where(le2 == m2, erow, N_EXPERTS), axis=0, keepdims=True)
        t2 = jnp.exp(m2 - m1)
        w1 = wg / (1.0 + t2)
        w2 = wg * t2 / (1.0 + t2)

        hot1 = erow == i1
        hot2 = erow == i2
        hot = (hot1 | hot2).astype(jnp.float32)
        before = jnp.dot(hot.astype(jnp.bfloat16), tri_ref[...], preferred_element_type=jnp.float32) + carry
        rank1 = jnp.sum(jnp.where(hot1, before, 0.0), axis=0, keepdims=True)
        rank2 = jnp.sum(jnp.where(hot2, before, 0.0), axis=0, keepdims=True)
        carry = carry + jnp.sum(hot, axis=1, keepdims=True)

        fields = (i1.astype(jnp.float32), i2.astype(jnp.float32), w1, w2, rank1, rank2)
        route_t = jnp.zeros((ROUTE_LANES, hm), jnp.float32)
        for idx, val in enumerate(fields):
            route_t = jnp.where(frow == idx, val, route_t)
        route_t_ref[:, rs] = route_t[0:route_t_ref.shape[0]]
        route_ref[rs, :] = route_t.T

    carry_ref[...] = jnp.broadcast_to(carry, carry_ref.shape)
    counts_ref[...] = jnp.broadcast_to(carry, counts_ref.shape)


def _post_attention(x0, o_groups, lse_groups, ob, ga, gb, wa, wb, wo, bo, g1, b1, wr, br):
    n, d = x0.shape
    tm = POST_TM
    nt = n // tm
    hm = tm // POST_SPLIT
    tri = jnp.asarray(np.triu(np.ones((hm, hm), np.float32), 1), jnp.bfloat16)

    def tok(width):
        return pl.BlockSpec((tm, width), lambda i: (i, 0))

    def full(a):
        return pl.BlockSpec(a.shape, lambda i: (0,) * a.ndim)

    return pl.pallas_call(
        _post_kernel,
        grid=(nt,),
        in_specs=[
            tok(d),
            *[pl.BlockSpec((1,) + a.shape[1:], lambda i: (i, 0, 0, 0)) for a in (*o_groups, *lse_groups)],
            tok(QB), tok(d), tok(d),
            full(wa), full(wb), full(wo), full(bo), full(g1), full(b1), full(wr), full(br), full(tri),
        ],
        out_specs=[pl.BlockSpec((tm * SUBLANES, LANES), lambda i: (i, 0)), tok(ROUTE_LANES),
                   pl.BlockSpec((8, tm), lambda i: (0, i)), pl.BlockSpec((N_EXPERTS, ROUTE_LANES), lambda i: (0, 0))],
        out_shape=[jax.ShapeDtypeStruct((n * SUBLANES, LANES), jnp.float32),
                   jax.ShapeDtypeStruct((n, ROUTE_LANES), jnp.float32),
                   jax.ShapeDtypeStruct((8, n), jnp.float32),
                   jax.ShapeDtypeStruct((N_EXPERTS, ROUTE_LANES), jnp.float32)],
        scratch_shapes=[pltpu.VMEM((N_EXPERTS, ROUTE_LANES), jnp.float32), pltpu.VMEM((12, tm, LANES), jnp.float32)],
        compiler_params=_cparams(("arbitrary",)),
        name="post_attention_router",
    )(x0, *o_groups, *lse_groups, ob, ga, gb, wa, wb, wo, bo, g1, b1, wr, br, tri)


def _dispatch_kernel(pos0_ref, pos1_ref, x_ref, xs_ref, sem):
    tt = x_ref.shape[0] // SUBLANES

    def issue(j, carry):
        for u in range(2):
            t = 2 * j + u
            src = x_ref.at[pl.ds(pl.multiple_of(t * SUBLANES, SUBLANES), SUBLANES)]
            for p_ref in (pos0_ref, pos1_ref):
                p = p_ref[t]
                dst = xs_ref.at[pl.ds(pl.multiple_of(p * SUBLANES, SUBLANES), SUBLANES)]
                pltpu.make_async_copy(src, dst, sem).start(priority=u)
        return carry

    lax.fori_loop(0, tt // 2, issue, 0, unroll=4)
    for _ in range(2):
        pltpu.make_async_copy(x_ref, xs_ref.at[pl.ds(0, tt * SUBLANES)], sem).wait()


def _dispatch(x1t, pos, n_rows):
    n = x1t.shape[0] // SUBLANES
    tt = DISP_TT
    return pl.pallas_call(
        _dispatch_kernel,
        grid=(n // tt,),
        in_specs=[pl.BlockSpec((tt,), lambda i: (i,), memory_space=pltpu.SMEM),
                  pl.BlockSpec((tt,), lambda i: (i,), memory_space=pltpu.SMEM),
                  pl.BlockSpec((tt * SUBLANES, LANES), lambda i: (i, 0))],
        out_specs=pl.BlockSpec(memory_space=pl.ANY),
        out_shape=jax.ShapeDtypeStruct((n_rows * SUBLANES, LANES), x1t.dtype),
        scratch_shapes=[pltpu.SemaphoreType.DMA(())],
        compiler_params=_cparams(("arbitrary",)),
        name="moe_dispatch",
    )(pos[0], pos[1], x1t)


def _ffn_kernel(item_expert_ref, item_tile_ref, item_lo_ref, item_hi_ref, n_items_ref,
                x_ref, wg_ref, wu_ref, wd_ref, y_ref, wg_bf, wu_bf, wd_bf, y_acc):
    j = pl.program_id(0)

    @pl.when(j == 0)
    def _():
        y_acc[...] = jnp.zeros_like(y_acc)

    @pl.when(j < n_items_ref[0])
    def _():
        prev = jnp.maximum(j - 1, 0)

        @pl.when(jnp.logical_or(j == 0, item_expert_ref[j] != item_expert_ref[prev]))
        def _():
            wg_bf[...] = wg_ref[0].astype(jnp.bfloat16)
            wu_bf[...] = wu_ref[0].astype(jnp.bfloat16)
            wd_bf[...] = wd_ref[0].astype(jnp.bfloat16)

        tm = x_ref.shape[0] // SUBLANES
        x = _load_row_tiles(x_ref, tm).astype(jnp.bfloat16)
        a = jnp.dot(x, wg_bf[...], preferred_element_type=jnp.float32)
        u = jnp.dot(x, wu_bf[...], preferred_element_type=jnp.float32)
        h = (a * jax.nn.sigmoid(a) * u).astype(jnp.bfloat16)
        y = jnp.dot(h, wd_bf[...], preferred_element_type=jnp.float32)
        rows = lax.broadcasted_iota(jnp.int32, y.shape, 0)
        pltpu.store(y_acc, y, mask=(rows >= item_lo_ref[j]) & (rows < item_hi_ref[j]))
        _store_row_tiles(y_ref, y_acc[...])


def _ffn_items(cnt, n_rows):
    tm = FFN_TM
    n_tiles = n_rows // tm
    max_items = n_tiles + N_EXPERTS - 1
    ends = jnp.cumsum(cnt)
    starts = ends - cnt
    first_tile = starts // tm
    last_tile = jnp.maximum(ends - 1, 0) // tm
    items_per_expert = jnp.where(cnt > 0, last_tile - first_tile + 1, 0)
    item_ends = jnp.cumsum(items_per_expert)
    n_items = item_ends[-1:]
    j = jnp.minimum(jnp.arange(max_items, dtype=jnp.int32), n_items[0] - 1)
    expert = jnp.minimum(jnp.sum(item_ends[None, :] <= j[:, None], axis=1), N_EXPERTS - 1).astype(jnp.int32)
    tile = first_tile[expert] + (j - (item_ends - items_per_expert)[expert])
    lo = jnp.clip(starts[expert] - tile * tm, 0, tm)
    hi = jnp.clip(ends[expert] - tile * tm, 0, tm)
    i32 = lambda a: a.astype(jnp.int32)
    return i32(expert), i32(tile), i32(lo), i32(hi), i32(n_items), starts


def _grouped_ffn(xs, items, w_gate, w_up, w_down):
    tm = FFN_TM
    d, de = w_gate.shape[1:]
    item_expert, item_tile, item_lo, item_hi, n_items = items
    max_items = item_expert.shape[0]

    def row_map(j, ie, it, lo, hi, ni):
        return (it[j], 0)

    def w_map(j, ie, it, lo, hi, ni):
        return (ie[j], 0, 0)

    grid_spec = pltpu.PrefetchScalarGridSpec(
        num_scalar_prefetch=5,
        grid=(max_items,),
        in_specs=[pl.BlockSpec((tm * SUBLANES, LANES), row_map),
                  pl.BlockSpec((1, d, de), w_map),
                  pl.BlockSpec((1, d, de), w_map),
                  pl.BlockSpec((1, de, d), w_map)],
        out_specs=pl.BlockSpec((tm * SUBLANES, LANES), row_map),
        scratch_shapes=[pltpu.VMEM((d, de), jnp.bfloat16), pltpu.VMEM((d, de), jnp.bfloat16),
                        pltpu.VMEM((de, d), jnp.bfloat16), pltpu.VMEM((tm, d), jnp.float32)],
    )
    return pl.pallas_call(
        _ffn_kernel,
        grid_spec=grid_spec,
        out_shape=jax.ShapeDtypeStruct(xs.shape, jnp.float32),
        compiler_params=_cparams(("arbitrary",)),
        name="moe_grouped_ffn",
    )(item_expert, item_tile, item_lo, item_hi, n_items, xs, w_gate, w_up, w_down)


def _combine_kernel(pos0_ref, pos1_ref, next0_ref, next1_ref, x1_ref, route_ref, g2_ref, b2_ref, ys_ref, o_ref,
                    buf, sems):
    tt = x1_ref.shape[0] // SUBLANES
    i = pl.program_id(0)
    cur = lax.rem(i, 2)

    def issue_tile(p_refs, b):
        def issue(j, carry):
            for u in range(2):
                t = 2 * j + u
                for slot, p_ref in enumerate(p_refs):
                    p = p_ref[t]
                    src = ys_ref.at[pl.ds(pl.multiple_of(p * SUBLANES, SUBLANES), SUBLANES)]
                    dst = buf.at[b, slot, pl.ds(pl.multiple_of(t * SUBLANES, SUBLANES), SUBLANES)]
                    pltpu.make_async_copy(src, dst, sems.at[b]).start(priority=u)
            return carry

        lax.fori_loop(0, tt // 2, issue, 0, unroll=4)

    @pl.when(i == 0)
    def _():
        issue_tile((pos0_ref, pos1_ref), 0)

    @pl.when(i + 1 < pl.num_programs(0))
    def _():
        issue_tile((next0_ref, next1_ref), 1 - cur)

    for slot in range(2):
        pltpu.make_async_copy(ys_ref.at[pl.ds(0, tt * SUBLANES)], buf.at[cur, slot], sems.at[cur]).wait()
    route = route_ref[...]
    y = route[:, 2:3] * _load_row_tiles(buf.at[cur, 0], tt) + route[:, 3:4] * _load_row_tiles(buf.at[cur, 1], tt)
    o_ref[...] = _layer_norm(DN_ALPHA * _load_row_tiles(x1_ref, tt) + y, g2_ref[...], b2_ref[...])


def _combine(x1t, route, pos, ys, g2, b2, tok0, n_tok):
    d = g2.shape[1]
    tt = COMB_TT
    t0 = tok0 // tt
    nt = n_tok // tt
    return pl.pallas_call(
        _combine_kernel,
        grid=(nt,),
        in_specs=[pl.BlockSpec((tt,), lambda i: (i + t0,), memory_space=pltpu.SMEM),
                  pl.BlockSpec((tt,), lambda i: (i + t0,), memory_space=pltpu.SMEM),
                  pl.BlockSpec((tt,), lambda i: (jnp.minimum(i + 1, nt - 1) + t0,), memory_space=pltpu.SMEM),
                  pl.BlockSpec((tt,), lambda i: (jnp.minimum(i + 1, nt - 1) + t0,), memory_space=pltpu.SMEM),
                  pl.BlockSpec((tt * SUBLANES, LANES), lambda i: (i + t0, 0)),
                  pl.BlockSpec((tt, ROUTE_LANES), lambda i: (i + t0, 0)),
                  pl.BlockSpec((1, d), lambda i: (0, 0)),
                  pl.BlockSpec((1, d), lambda i: (0, 0)),
                  pl.BlockSpec(memory_space=pl.ANY)],
        out_specs=pl.BlockSpec((tt, d), lambda i: (i, 0)),
        out_shape=jax.ShapeDtypeStruct((n_tok, d), jnp.float32),
        scratch_shapes=[pltpu.VMEM((2, 2, tt * SUBLANES, LANES), ys.dtype), pltpu.SemaphoreType.DMA((2,))],
        compiler_params=_cparams(("arbitrary",)),
        name="moe_combine_ln",
    )(pos[0], pos[1], pos[0], pos[1], x1t, route, g2, b2, ys)


def _layer(x_prompt, x_sample, ln_in_g, ln_in_b, w_in, b_in, rel_bias_t5, na_rpb, w_branch_a, w_branch_b,
           w_out, b_out, ln1_g, ln1_b, w_router_group, b_router_group, w_router_expert, b_router_expert,
           w_exp_gate, w_exp_up, w_exp_down, ln2_g, ln2_b):
    bp, tp, d = x_prompt.shape
    bs, ts, _ = x_sample.shape
    seg_tokens = (bp * tp, bs * ts)
    seq_lens = (tp, ts)
    n = sum(seg_tokens)
    xp = x_prompt.reshape(seg_tokens[0], d)
    xs = x_sample.reshape(seg_tokens[1], d)
    row = lambda a: a.reshape(1, -1).astype(jnp.float32)
    bf = lambda a: a.astype(jnp.bfloat16)

    scale = HEAD_DIM ** -0.5
    col_scale = np.ones((w_in.shape[2],), np.float32)
    col_scale[0:QA] = scale
    col_scale[3 * QA:3 * QA + QB] = scale
    w_proj = bf(w_in[0] * col_scale)
    b_proj = row(b_in[0] * col_scale)
    proj = _in_projection(xp, xs, row(ln_in_g), row(ln_in_b), w_proj, b_proj)
    qkv_a, (qb, kb, vb, ga, gb, x0) = proj[:3 * N_DIL_GROUPS], proj[3 * N_DIL_GROUPS:]

    o_groups, lse_groups = [], []
    for g, (window, dil) in enumerate(DIL_PATTERNS):
        assert window // (2 * dil) == HALF_SPAN
        table = rel_bias_t5[:, g * HEADS_PER_DIL_GROUP:(g + 1) * HEADS_PER_DIL_GROUP].astype(jnp.float32).reshape(-1)
        o, lse = _dilated_group(qkv_a[g], qkv_a[N_DIL_GROUPS + g], qkv_a[2 * N_DIL_GROUPS + g], table, dil,
                                seg_tokens, seq_lens)
        o_groups.append(o)
        lse_groups.append(lse)
    ob = _neighborhood(qb, kb, vb, na_rpb[0].astype(jnp.float32).reshape(-1), seg_tokens, seq_lens)

    wr = jnp.zeros((d, ROUTE_LANES), jnp.float32)
    wr = wr.at[:, 0:N_GROUPS].set(w_router_group[0]).at[:, EXPERT_LANE0:EXPERT_LANE0 + N_EXPERTS].set(w_router_expert[0])
    br = jnp.zeros((1, ROUTE_LANES), jnp.float32)
    br = br.at[0, 0:N_GROUPS].set(b_router_group[0]).at[0, EXPERT_LANE0:EXPERT_LANE0 + N_EXPERTS].set(b_router_expert[0])
    x1, route, route_t, counts = _post_attention(
        x0, o_groups, lse_groups, ob, ga, gb, bf(w_branch_a[0]), bf(w_branch_b[0]),
        bf(w_out[0]), row(b_out[0]), row(ln1_g[0]), row(ln1_b[0]), bf(wr), br)

    experts = route_t[0:2].astype(jnp.int32)
    ranks = route_t[4:6].astype(jnp.int32)
    cnt = counts[:, 0].astype(jnp.int32)
    n_rows = 2 * n
    *items, starts = _ffn_items(cnt, n_rows)
    one_hot = (experts[..., None] == jnp.arange(N_EXPERTS, dtype=jnp.int32)).astype(jnp.float32)
    pos = ranks + jnp.dot(one_hot, starts.astype(jnp.float32), precision=lax.Precision.HIGHEST).astype(jnp.int32)

    xsorted = _dispatch(x1, pos, n_rows)
    ys = _grouped_ffn(xsorted, items, w_exp_gate[0], w_exp_up[0], w_exp_down[0])
    y_prompt = _combine(x1, route, pos, ys, row(ln2_g[0]), row(ln2_b[0]), 0, seg_tokens[0])
    y_sample = _combine(x1, route, pos, ys, row(ln2_g[0]), row(ln2_b[0]), seg_tokens[0], seg_tokens[1])
    return y_prompt.reshape(bp, tp, d), y_sample.reshape(bs, ts, d)


def kernel(x_prompt, x_sample, ln_in_g, ln_in_b, w_in, b_in, rel_bias_t5, na_rpb, w_branch_a, w_branch_b, w_out, b_out, ln1_g, ln1_b, w_router_group, b_router_group, w_router_expert, b_router_expert, w_exp_gate, w_exp_up, w_exp_down, ln2_g, ln2_b):
    return _layer(x_prompt, x_sample, ln_in_g, ln_in_b, w_in, b_in, rel_bias_t5, na_rpb, w_branch_a, w_branch_b,
                  w_out, b_out, ln1_g, ln1_b, w_router_group, b_router_group, w_router_expert, b_router_expert,
                  w_exp_gate, w_exp_up, w_exp_down, ln2_g, ln2_b)
```

```python
import functools
import math

import numpy as np
import jax
import jax.numpy as jnp
from jax import lax
from jax.experimental import pallas as pl
from jax.experimental.pallas import tpu as pltpu

HEAD_DIM = 64
DIL_PATTERNS = ((128, 1), (512, 4), (2048, 16))
HEADS_PER_DIL_GROUP = 4
N_DIL_GROUPS = 3
H_A = N_DIL_GROUPS * HEADS_PER_DIL_GROUP
H_B = 8
QA = H_A * HEAD_DIM
QB = H_B * HEAD_DIM
HALF_SPAN = 64
GRID_W = 64
NA_KH = 8
NA_KW = 16
N_BUCKETS = 32
REL_MAX_DIST = 1024
N_GROUPS = 4
EXPERTS_PER_GROUP = 8
N_EXPERTS = N_GROUPS * EXPERTS_PER_GROUP
LN_EPS = 1e-5
NEG_INF = -1e30
DEPTH = 1
DN_ALPHA = (2.0 * DEPTH) ** 0.25

LANES = 128
HEAD_BLOCK = 4 * HEAD_DIM
VMEM_LIMIT_BYTES = 56 * 1024 * 1024

PROJ_TM = 512
DIL_M = 128
DIL_GROUP = 16
NA_ROWS = 32
NA_HALO = NA_KH // 2
POST_TM = 512
POST_SPLIT = 1
FFN_TM = 512
DISP_TT = 4096
COMB_TT = 512
ROUTE_LANES = LANES
EXPERT_LANE0 = 32


def _cparams(sem):
    return pltpu.CompilerParams(dimension_semantics=sem, vmem_limit_bytes=VMEM_LIMIT_BYTES)


SUBLANES = 8


def _store_row_tiles(ref, x, row0=0):
    rows = x.shape[0]
    for c in range(SUBLANES):
        ref[pl.ds(row0 * SUBLANES + c, rows, stride=SUBLANES), :] = x[:, c * LANES:(c + 1) * LANES]


def _load_row_tiles(ref, rows, row0=0):
    return jnp.concatenate([ref[pl.ds(row0 * SUBLANES + c, rows, stride=SUBLANES), :] for c in range(SUBLANES)],
                           axis=1)


def _layer_norm(x, g, b):
    mu = jnp.mean(x, axis=-1, keepdims=True)
    xc = x - mu
    var = jnp.mean(xc * xc, axis=-1, keepdims=True)
    return xc * lax.rsqrt(var + LN_EPS) * g + b


def _strided_rows(scr, slab0, r, count, stride):
    return jnp.concatenate([scr[slab0 + s, pl.ds(r, count, stride=stride), :] for s in range(2)], axis=1)


def _inproj_kernel(xp_ref, xs_ref, g_ref, b_ref, w_ref, bias_ref, *refs, n0_tiles):
    out_refs, scr = refs[:-1], refs[-1]
    i = pl.program_id(0)
    x = jnp.where(i < n0_tiles, xp_ref[...], xs_ref[...])
    x0 = _layer_norm(x, g_ref[...], b_ref[...])
    out_refs[-1][...] = x0
    out_refs = out_refs[:-1]
    xn = x0.astype(jnp.bfloat16)
    tm = xn.shape[0]

    def proj(c0, cw):
        return jnp.dot(xn, w_ref[:, c0:c0 + cw], preferred_element_type=jnp.float32) + bias_ref[:, c0:c0 + cw]

    n_dil = 3 * N_DIL_GROUPS
    for idx in range(n_dil):
        g = idx % N_DIL_GROUPS
        dil = DIL_PATTERNS[g][1]
        o_ref = out_refs[idx]
        y = proj(idx * HEAD_BLOCK, HEAD_BLOCK)
        if dil == 1:
            o_ref[0, 0] = y.astype(o_ref.dtype)
        else:
            slab0 = 2 * (idx % 2)
            scr[slab0] = y[:, :LANES]
            scr[slab0 + 1] = y[:, LANES:]
            for r in range(dil):
                o_ref[0, r] = _strided_rows(scr, slab0, r, tm // dil, dil).astype(o_ref.dtype)
    col = n_dil * HEAD_BLOCK
    for o_ref in out_refs[n_dil:]:
        width = o_ref.shape[1]
        for c in range(0, width, 512):
            cw = min(512, width - c)
            o_ref[:, c:c + cw] = proj(col + c, cw).astype(o_ref.dtype)
        col += width


def _in_projection(xp, xs, ln_g, ln_b, w, b):
    n0, d_model = xp.shape
    n1 = xs.shape[0]
    n = n0 + n1
    tm = PROJ_TM
    n0_tiles, n1_tiles = n0 // tm, n1 // tm
    nt = n0_tiles + n1_tiles
    out_shape, out_specs = [], []
    for _ in range(3):
        for _, dil in DIL_PATTERNS:
            out_shape.append(jax.ShapeDtypeStruct((nt, dil, tm // dil, HEAD_BLOCK), jnp.bfloat16))
            out_specs.append(pl.BlockSpec((1, dil, tm // dil, HEAD_BLOCK), lambda i: (i, 0, 0, 0)))
    for wd, dtype in ((QB, jnp.bfloat16),) * 3 + ((d_model, jnp.bfloat16),) * 2 + ((d_model, jnp.float32),):
        out_shape.append(jax.ShapeDtypeStruct((n, wd), dtype))
        out_specs.append(pl.BlockSpec((tm, wd), lambda i: (i, 0)))
    d_in = w.shape[1]
    kern = functools.partial(_inproj_kernel, n0_tiles=n0_tiles)
    return pl.pallas_call(
        kern,
        grid=(nt,),
        in_specs=[
            pl.BlockSpec((tm, d_model), lambda i: (jnp.minimum(i, n0_tiles - 1), 0)),
            pl.BlockSpec((tm, d_model), lambda i: (jnp.maximum(i - n0_tiles, 0), 0)),
            pl.BlockSpec((1, d_model), lambda i: (0, 0)),
            pl.BlockSpec((1, d_model), lambda i: (0, 0)),
            pl.BlockSpec((d_model, d_in), lambda i: (0, 0)),
            pl.BlockSpec((1, d_in), lambda i: (0, 0)),
        ],
        out_specs=out_specs,
        out_shape=out_shape,
        scratch_shapes=[pltpu.VMEM((4, tm, LANES), jnp.float32)],
        compiler_params=_cparams(("arbitrary",)),
        name="in_projection",
    )(xp, xs, ln_g, ln_b, w, b)


def _t5_bucket_np(rel):
    half = N_BUCKETS // 2
    max_exact = half // 2
    ret = np.where(rel > 0, half, 0)
    n = np.abs(rel)
    nf = np.maximum(n, 1).astype(np.float32)
    large = max_exact + (np.log(nf / np.float32(max_exact)) / np.float32(math.log(REL_MAX_DIST / max_exact))
                         * np.float32(half - max_exact)).astype(np.int32)
    large = np.minimum(large, half - 1)
    return ret + np.where(n < max_exact, n, large)


def _t5_bucket_starts():
    half = N_BUCKETS // 2
    dist = np.arange(0, HALF_SPAN * max(d for _, d in DIL_PATTERNS) + 1)
    buckets = _t5_bucket_np(-dist)
    return [int(np.argmax(buckets >= k)) for k in range(half)]


def _build_dilated_bias(table_ref, bias_scr, dil):
    half = N_BUCKETS // 2
    starts = _t5_bucket_starts()
    kb = DIL_M + 2 * HALF_SPAN
    i = lax.broadcasted_iota(jnp.int32, (DIL_M, kb), 0)
    m = lax.broadcasted_iota(jnp.int32, (DIL_M, kb), 1)
    delta = m - HALF_SPAN - i
    dist = jnp.abs(delta) * dil
    band = jnp.abs(delta) <= HALF_SPAN
    for h in range(HEADS_PER_DIL_GROUP):
        sides = []
        for side in range(2):
            val = jnp.full((DIL_M, kb), table_ref[(side * half + half - 1) * HEADS_PER_DIL_GROUP + h], jnp.float32)
            for k in range(half - 1, 0, -1):
                val = jnp.where(dist < starts[k], table_ref[(side * half + k - 1) * HEADS_PER_DIL_GROUP + h], val)
            sides.append(val)
        base = jnp.where(band, jnp.where(delta > 0, sides[1], sides[0]), NEG_INF)
        rows = slice(h * DIL_M, (h + 1) * DIL_M)
        bias_scr[0, rows, :] = base
        bias_scr[1, rows, :] = jnp.where(m >= HALF_SPAN, base, NEG_INF)
        bias_scr[2, rows, :] = jnp.where(m < kb - HALF_SPAN, base, NEG_INF)


def _stack_heads(q):
    head_of_col = lax.broadcasted_iota(jnp.int32, (1, HEAD_BLOCK), 1) // HEAD_DIM
    zero = jnp.zeros_like(q)
    return jnp.concatenate([jnp.where(head_of_col == h, q, zero) for h in range(4)], axis=0)


def _merge_heads(x, m):
    head_of_col = lax.broadcasted_iota(jnp.int32, (1, HEAD_BLOCK), 1) // HEAD_DIM
    out = jnp.zeros((m, HEAD_BLOCK), x.dtype)
    for h in range(4):
        out = jnp.where(head_of_col == h, x[h * m:(h + 1) * m], out)
    return out


def _softmax_pv(s, v, m_rows):
    mx = jnp.max(s, axis=-1, keepdims=True)
    p = jnp.exp(s - mx)
    l = jnp.sum(p, axis=-1, keepdims=True)
    pv = jnp.dot(p.astype(jnp.bfloat16), v, preferred_element_type=jnp.float32)
    o = _merge_heads(pv * (1.0 / l), m_rows)
    return o, mx + jnp.log(l)


def _dilated_kernel(table_ref, q_ref, kp_ref, kc_ref, kn_ref, vp_ref, vc_ref, vn_ref,
                    o_ref, lse_ref, bias_scr, *, dil, seg0_blocks, blocks_per_seq):
    step = pl.program_id(1)

    @pl.when((pl.program_id(0) == 0) & (step == 0))
    def _():
        _build_dilated_bias(table_ref, bias_scr, dil)

    rows2d = lambda ref: ref[...].reshape(-1, HEAD_BLOCK)
    k = jnp.concatenate([rows2d(r) for r in (kp_ref, kc_ref, kn_ref)], axis=0)
    v = jnp.concatenate([rows2d(r) for r in (vp_ref, vc_ref, vn_ref)], axis=0)
    q = rows2d(q_ref)
    o_parts, lse_parts = [], []
    for g in range(DIL_GROUP):
        jb = step * DIL_GROUP + g
        in0 = jb < seg0_blocks
        jl = jnp.where(in0, jb, jb - seg0_blocks)
        nbs = jnp.where(in0, blocks_per_seq[0], blocks_per_seq[1])
        pos = lax.rem(jl, nbs)
        variant = jnp.where(pos == 0, 1, jnp.where(pos == nbs - 1, 2, 0))
        keys = slice(g * DIL_M, (g + 1) * DIL_M + 2 * HALF_SPAN)
        qs = _stack_heads(q[g * DIL_M:(g + 1) * DIL_M])
        s = lax.dot_general(qs, k[keys], (((1,), (1,)), ((), ())), preferred_element_type=jnp.float32)
        s = s + bias_scr[variant]
        o, lse = _softmax_pv(s, v[keys], DIL_M)
        o_parts.append(o.astype(o_ref.dtype))
        lse_parts.append(_merge_heads(jnp.broadcast_to(lse, (4 * DIL_M, HEAD_BLOCK)), DIL_M))
    o_ref[...] = jnp.concatenate(o_parts, axis=0).reshape(o_ref.shape)
    lse_ref[...] = jnp.concatenate(lse_parts, axis=0).reshape(lse_ref.shape)


def _residue_spec(rows_blk, dil, block_of):
    per_tile = PROJ_TM // dil
    if per_tile >= rows_blk:
        per = per_tile // rows_blk
        return pl.BlockSpec((1, None, rows_blk, HEAD_BLOCK),
                            lambda r, jb: (block_of(jb) // per, r, block_of(jb) % per, 0))
    return pl.BlockSpec((rows_blk // per_tile, None, per_tile, HEAD_BLOCK), lambda r, jb: (block_of(jb), r, 0, 0))


def _dilated_group(q, k, v, table, dil, seg_tokens, seq_lens):
    n = q.shape[0] * PROJ_TM
    rows = n // dil
    step_rows = DIL_GROUP * DIL_M
    assert rows % step_rows == 0 and all(t % (dil * DIL_M) == 0 for t in seq_lens)
    nsteps = rows // step_rows
    nkb = rows // HALF_SPAN
    ratio = step_rows // HALF_SPAN
    seg0_blocks = seg_tokens[0] // dil // DIL_M
    blocks_per_seq = tuple(t // dil // DIL_M for t in seq_lens)
    assert min(blocks_per_seq) >= 2

    qo_spec = _residue_spec(step_rows, dil, lambda jb: jb)
    prev_spec = _residue_spec(HALF_SPAN, dil, lambda jb: jnp.maximum(ratio * jb - 1, 0))
    next_spec = _residue_spec(HALF_SPAN, dil, lambda jb: jnp.minimum(ratio * (jb + 1), nkb - 1))
    kv_specs = [prev_spec, qo_spec, next_spec]
    kern = functools.partial(_dilated_kernel, dil=dil, seg0_blocks=seg0_blocks, blocks_per_seq=blocks_per_seq)
    return pl.pallas_call(
        kern,
        grid=(dil, nsteps),
        in_specs=[pl.BlockSpec(memory_space=pltpu.SMEM), qo_spec] + kv_specs + kv_specs,
        out_specs=[qo_spec, qo_spec],
        out_shape=[jax.ShapeDtypeStruct(q.shape, jnp.bfloat16), jax.ShapeDtypeStruct(q.shape, jnp.float32)],
        scratch_shapes=[pltpu.VMEM((3, HEADS_PER_DIL_GROUP * DIL_M, DIL_M + 2 * HALF_SPAN), jnp.float32)],
        compiler_params=_cparams(("arbitrary", "arbitrary")),
        name=f"dilated_attention_d{dil}",
    )(table, q, k, k, k, v, v, v)


NA_DR = 2 * NA_KH - 1
NA_DC = 2 * NA_KW - 1
NA_PAIRS = NA_DR - 1


def _build_na_bias(rpb_ref, pair_scr, head0):
    qc = lax.broadcasted_iota(jnp.int32, (GRID_W, 2 * GRID_W), 0)
    lane = lax.broadcasted_iota(jnp.int32, (GRID_W, 2 * GRID_W), 1)
    kc = lane % GRID_W
    upper = lane >= GRID_W
    qstart = jnp.clip(qc - NA_KW // 2, 0, GRID_W - NA_KW)
    valid = (kc >= qstart) & (kc < qstart + NA_KW)
    dc = jnp.clip(kc - qc, -(NA_KW - 1), NA_KW - 1) + NA_KW - 1

    def one_pair(idx, carry):
        h = idx // NA_PAIRS
        dr = idx % NA_PAIRS
        base = ((head0 + h) * NA_DR + dr) * NA_DC
        val = jnp.zeros((GRID_W, 2 * GRID_W), jnp.float32)
        for c in range(NA_DC):
            val = jnp.where(dc == c, jnp.where(upper, rpb_ref[base + NA_DC + c], rpb_ref[base + c]), val)
        pair_scr[h, dr] = jnp.where(valid, val, NEG_INF)
        return carry

    lax.fori_loop(0, 4 * NA_PAIRS, one_pair, 0)


def _na_kernel(rpb_ref, q_ref, kp_ref, kc_ref, kn_ref, vp_ref, vc_ref, vn_ref, o_ref, k_scr, v_scr, pair_scr,
               *, seg0_blocks, blocks_per_seq):
    ib = pl.program_id(1)

    @pl.when(ib == 0)
    def _():
        _build_na_bias(rpb_ref, pair_scr, pl.program_id(0) * 4)

    in0 = ib < seg0_blocks
    il = jnp.where(in0, ib, ib - seg0_blocks)
    nbs = jnp.where(in0, blocks_per_seq[0], blocks_per_seq[1])
    pos = lax.rem(il, nbs)
    first = pos == 0
    last = pos == nbs - 1
    blk = NA_ROWS * GRID_W
    halo = NA_HALO * GRID_W
    for scr, (p_ref, c_ref, n_ref) in ((k_scr, (kp_ref, kc_ref, kn_ref)), (v_scr, (vp_ref, vc_ref, vn_ref))):
        scr[0:halo] = p_ref[...]
        scr[halo:halo + blk] = c_ref[...]
        scr[halo + blk:2 * halo + blk] = n_ref[...]
    half = NA_KH // 2
    for rr in range(NA_ROWS):
        inner = NA_HALO + rr - half
        s_first, s_last = max(inner, NA_HALO), min(inner, NA_HALO + NA_ROWS - NA_KH)
        start = jnp.where(first, s_first, jnp.where(last, s_last, inner))
        var = jnp.where(first, NA_HALO + rr - s_first, jnp.where(last, NA_HALO + rr - s_last, half))
        off = pl.multiple_of(start * GRID_W, GRID_W)
        kk = k_scr[pl.ds(off, NA_KH * GRID_W), :]
        vv = v_scr[pl.ds(off, NA_KH * GRID_W), :]
        qs = _stack_heads(q_ref[rr * GRID_W:(rr + 1) * GRID_W, :])
        s = lax.dot_general(qs, kk, (((1,), (1,)), ((), ())), preferred_element_type=jnp.float32)
        bias = jnp.concatenate(
            [jnp.concatenate([pair_scr[h, 2 * p - var + NA_KH - 1] for p in range(NA_KH // 2)], axis=1)
             for h in range(4)], axis=0)
        s = s + bias
        o, _ = _softmax_pv(s, vv, GRID_W)
        o_ref[rr * GRID_W:(rr + 1) * GRID_W, :] = o.astype(o_ref.dtype)


def _neighborhood(qb, kb, vb, rpb_flat, seg_tokens, seq_lens):
    n = qb.shape[0]
    blk = NA_ROWS * GRID_W
    nblk = n // blk
    seg0_blocks = seg_tokens[0] // blk
    blocks_per_seq = tuple(t // blk for t in seq_lens)
    assert min(blocks_per_seq) >= 2
    nset = QB // HEAD_BLOCK

    per = NA_ROWS // NA_HALO
    halo = NA_HALO * GRID_W
    cur_spec = pl.BlockSpec((blk, HEAD_BLOCK), lambda c, ib: (ib, c))
    prev_spec = pl.BlockSpec((halo, HEAD_BLOCK), lambda c, ib: (jnp.maximum(per * ib - 1, 0), c))
    next_spec = pl.BlockSpec((halo, HEAD_BLOCK), lambda c, ib: (jnp.minimum(per * (ib + 1), per * nblk - 1), c))
    kv_specs = [prev_spec, cur_spec, next_spec]

    kern = functools.partial(_na_kernel, seg0_blocks=seg0_blocks, blocks_per_seq=blocks_per_seq)
    return pl.pallas_call(
        kern,
        grid=(nset, nblk),
        in_specs=[pl.BlockSpec(memory_space=pltpu.SMEM), cur_spec] + kv_specs + kv_specs,
        out_specs=cur_spec,
        out_shape=jax.ShapeDtypeStruct((n, QB), jnp.bfloat16),
        scratch_shapes=[pltpu.VMEM((blk + 2 * halo, HEAD_BLOCK), jnp.bfloat16),
                        pltpu.VMEM((blk + 2 * halo, HEAD_BLOCK), jnp.bfloat16),
                        pltpu.VMEM((4, NA_PAIRS, GRID_W, 2 * GRID_W), jnp.float32)],
        compiler_params=_cparams(("arbitrary", "arbitrary")),
        name="neighborhood_attention",
    )(rpb_flat, qb, kb, kb, kb, vb, vb, vb)


def _post_kernel(x0_ref, o0_ref, o1_ref, o2_ref, l0_ref, l1_ref, l2_ref, ob_ref,
                 ga_ref, gb_ref, wa_ref, wb_ref, wo_ref, bo_ref, g1_ref, b1_ref, wr_ref, br_ref, tri_ref,
                 x1_ref, route_ref, route_t_ref, counts_ref, carry_ref, perm_scr):
    i = pl.program_id(0)
    tm = route_ref.shape[0]
    hm = tm // POST_SPLIT

    @pl.when(i == 0)
    def _():
        carry_ref[...] = jnp.zeros_like(carry_ref)

    def token_order(ref, slab0):
        dil, per = ref.shape[1], ref.shape[2]
        if dil == 1:
            return lambda rs: ref[0, 0, rs, :].astype(jnp.float32)
        for r in range(dil):
            val = ref[0, r].astype(jnp.float32)
            for s in range(2):
                perm_scr[slab0 + s, pl.ds(r, per, stride=dil), :] = val[:, s * LANES:(s + 1) * LANES]
        return lambda rs: jnp.concatenate([perm_scr[slab0, rs, :], perm_scr[slab0 + 1, rs, :]], axis=1)

    lse_of = [token_order(ref, 2 * j) for j, ref in enumerate((l0_ref, l1_ref, l2_ref))]
    o_of = [token_order(ref, 6 + 2 * j) for j, ref in enumerate((o0_ref, o1_ref, o2_ref))]
    sigmoid = lambda v: 0.5 * jnp.tanh(0.5 * v) + 0.5
    grow = lax.broadcasted_iota(jnp.int32, (8, hm), 0)
    erow = lax.broadcasted_iota(jnp.int32, (N_EXPERTS, hm), 0)
    frow = lax.broadcasted_iota(jnp.int32, (ROUTE_LANES, hm), 0)
    neg = jnp.float32(-jnp.inf)
    carry = carry_ref[:, 0:1]

    for h in range(POST_SPLIT):
        rs = slice(h * hm, (h + 1) * hm)
        l0, l1, l2 = (f(rs) for f in lse_of)
        lm = jnp.maximum(jnp.maximum(l0, l1), l2)
        e0, e1, e2 = jnp.exp(l0 - lm), jnp.exp(l1 - lm), jnp.exp(l2 - lm)
        inv = 1.0 / (e0 + e1 + e2)
        o_a = (e0 * inv) * o_of[0](rs) + (e1 * inv) * o_of[1](rs) + (e2 * inv) * o_of[2](rs)
        y_a = jnp.dot(o_a.astype(jnp.bfloat16), wa_ref[...], preferred_element_type=jnp.float32)
        y_b = jnp.dot(ob_ref[rs, :], wb_ref[...], preferred_element_type=jnp.float32)
        mix = sigmoid(ga_ref[rs, :].astype(jnp.float32)) * y_a + sigmoid(gb_ref[rs, :].astype(jnp.float32)) * y_b
        out = jnp.dot(mix.astype(jnp.bfloat16), wo_ref[...], preferred_element_type=jnp.float32) + bo_ref[...]
        x1 = _layer_norm(DN_ALPHA * x0_ref[rs, :] + out, g1_ref[...], b1_ref[...])
        _store_row_tiles(x1_ref, x1, row0=h * hm)

        r = jnp.dot(x1.astype(jnp.bfloat16), wr_ref[...], preferred_element_type=jnp.float32) + br_ref[...]
        rt = r.T
        lg = jnp.where(grow < N_GROUPS, rt[0:8], neg)
        gmax = jnp.max(lg, axis=0, keepdims=True)
        gi = jnp.min(jnp.where(lg == gmax, grow, 8), axis=0, keepdims=True)
        wg = 1.0 / jnp.sum(jnp.exp(lg - gmax), axis=0, keepdims=True)
        lo = EXPERTS_PER_GROUP * gi
        le = jnp.where((erow >= lo) & (erow < lo + EXPERTS_PER_GROUP), rt[EXPERT_LANE0:EXPERT_LANE0 + N_EXPERTS], neg)
        m1 = jnp.max(le, axis=0, keepdims=True)
        i1 = jnp.min(jnp.where(le == m1, erow, N_EXPERTS), axis=0, keepdims=True)
        le2 = jnp.where(erow == i1, neg, le)
        m2 = jnp.max(le2, axis=0, keepdims=True)
        i2 = jnp.min(jnp.where(le2 == m2, erow, N_EXPERTS), axis=0, keepdims=True)
        t2 = jnp.exp(m2 - m1)
        w1 = wg / (1.0 + t2)
        w2 = wg * t2 / (1.0 + t2)

        hot1 = erow == i1
        hot2 = erow == i2
        hot = (hot1 | hot2).astype(jnp.float32)
        before = jnp.dot(hot.astype(jnp.bfloat16), tri_ref[...], preferred_element_type=jnp.float32) + carry
        rank1 = jnp.sum(jnp.where(hot1, before, 0.0), axis=0, keepdims=True)
        rank2 = jnp.sum(jnp.where(hot2, before, 0.0), axis=0, keepdims=True)
        carry = carry + jnp.sum(hot, axis=1, keepdims=True)

        fields = (i1.astype(jnp.float32), i2.astype(jnp.float32), w1, w2, rank1, rank2)
        route_t = jnp.zeros((ROUTE_LANES, hm), jnp.float32)
        for idx, val in enumerate(fields):
            route_t = jnp.where(frow == idx, val, route_t)
        route_t_ref[:, rs] = route_t[0:route_t_ref.shape[0]]
        route_ref[rs, :] = route_t.T

    carry_ref[...] = jnp.broadcast_to(carry, carry_ref.shape)
    counts_ref[...] = jnp.broadcast_to(carry, counts_ref.shape)


def _post_attention(x0, o_groups, lse_groups, ob, ga, gb, wa, wb, wo, bo, g1, b1, wr, br):
    n, d = x0.shape
    tm = POST_TM
    nt = n // tm
    hm = tm // POST_SPLIT
    tri = jnp.asarray(np.triu(np.ones((hm, hm), np.float32), 1), jnp.bfloat16)

    def tok(width):
        return pl.BlockSpec((tm, width), lambda i: (i, 0))

    def full(a):
        return pl.BlockSpec(a.shape, lambda i: (0,) * a.ndim)

    return pl.pallas_call(
        _post_kernel,
        grid=(nt,),
        in_specs=[
            tok(d),
            *[pl.BlockSpec((1,) + a.shape[1:], lambda i: (i, 0, 0, 0)) for a in (*o_groups, *lse_groups)],
            tok(QB), tok(d), tok(d),
            full(wa), full(wb), full(wo), full(bo), full(g1), full(b1), full(wr), full(br), full(tri),
        ],
        out_specs=[pl.BlockSpec((tm * SUBLANES, LANES), lambda i: (i, 0)), tok(ROUTE_LANES),
                   pl.BlockSpec((8, tm), lambda i: (0, i)), pl.BlockSpec((N_EXPERTS, ROUTE_LANES), lambda i: (0, 0))],
        out_shape=[jax.ShapeDtypeStruct((n * SUBLANES, LANES), jnp.float32),
                   jax.ShapeDtypeStruct((n, ROUTE_LANES), jnp.float32),
                   jax.ShapeDtypeStruct((8, n), jnp.float32),
                   jax.ShapeDtypeStruct((N_EXPERTS, ROUTE_LANES), jnp.float32)],
        scratch_shapes=[pltpu.VMEM((N_EXPERTS, ROUTE_LANES), jnp.float32), pltpu.VMEM((12, tm, LANES), jnp.float32)],
        compiler_params=_cparams(("arbitrary",)),
        name="post_attention_router",
    )(x0, *o_groups, *lse_groups, ob, ga, gb, wa, wb, wo, bo, g1, b1, wr, br, tri)


def _dispatch_kernel(pos0_ref, pos1_ref, x_ref, xs_ref, sem):
    tt = x_ref.shape[0] // SUBLANES

    def issue(j, carry):
        for u in range(2):
            t = 2 * j + u
            src = x_ref.at[pl.ds(pl.multiple_of(t * SUBLANES, SUBLANES), SUBLANES)]
            for p_ref in (pos0_ref, pos1_ref):
                p = p_ref[t]
                dst = xs_ref.at[pl.ds(pl.multiple_of(p * SUBLANES, SUBLANES), SUBLANES)]
                pltpu.make_async_copy(src, dst, sem).start(priority=u)
        return carry

    lax.fori_loop(0, tt // 2, issue, 0, unroll=4)
    for _ in range(2):
        pltpu.make_async_copy(x_ref, xs_ref.at[pl.ds(0, tt * SUBLANES)], sem).wait()


def _dispatch(x1t, pos, n_rows):
    n = x1t.shape[0] // SUBLANES
    tt = DISP_TT
    return pl.pallas_call(
        _dispatch_kernel,
        grid=(n // tt,),
        in_specs=[pl.BlockSpec((tt,), lambda i: (i,), memory_space=pltpu.SMEM),
                  pl.BlockSpec((tt,), lambda i: (i,), memory_space=pltpu.SMEM),
                  pl.BlockSpec((tt * SUBLANES, LANES), lambda i: (i, 0))],
        out_specs=pl.BlockSpec(memory_space=pl.ANY),
        out_shape=jax.ShapeDtypeStruct((n_rows * SUBLANES, LANES), x1t.dtype),
        scratch_shapes=[pltpu.SemaphoreType.DMA(())],
        compiler_params=_cparams(("arbitrary",)),
        name="moe_dispatch",
    )(pos[0], pos[1], x1t)


def _ffn_kernel(item_expert_ref, item_tile_ref, item_lo_ref, item_hi_ref, n_items_ref,
                x_ref, wg_ref, wu_ref, wd_ref, y_ref, wg_bf, wu_bf, wd_bf, y_acc):
    j = pl.program_id(0)

    @pl.when(j == 0)
    def _():
        y_acc[...] = jnp.zeros_like(y_acc)

    @pl.when(j < n_items_ref[0])
    def _():
        prev = jnp.maximum(j - 1, 0)

        @pl.when(jnp.logical_or(j == 0, item_expert_ref[j] != item_expert_ref[prev]))
        def _():
            wg_bf[...] = wg_ref[0].astype(jnp.bfloat16)
            wu_bf[...] = wu_ref[0].astype(jnp.bfloat16)
            wd_bf[...] = wd_ref[0].astype(jnp.bfloat16)

        tm = x_ref.shape[0] // SUBLANES
        x = _load_row_tiles(x_ref, tm).astype(jnp.bfloat16)
        a = jnp.dot(x, wg_bf[...], preferred_element_type=jnp.float32)
        u = jnp.dot(x, wu_bf[...], preferred_element_type=jnp.float32)
        h = (a * jax.nn.sigmoid(a) * u).astype(jnp.bfloat16)
        y = jnp.dot(h, wd_bf[...], preferred_element_type=jnp.float32)
        rows = lax.broadcasted_iota(jnp.int32, y.shape, 0)
        pltpu.store(y_acc, y, mask=(rows >= item_lo_ref[j]) & (rows < item_hi_ref[j]))
        _store_row_tiles(y_ref, y_acc[...])


def _ffn_items(cnt, n_rows):
    tm = FFN_TM
    n_tiles = n_rows // tm
    max_items = n_tiles + N_EXPERTS - 1
    ends = jnp.cumsum(cnt)
    starts = ends - cnt
    first_tile = starts // tm
    last_tile = jnp.maximum(ends - 1, 0) // tm
    items_per_expert = jnp.where(cnt > 0, last_tile - first_tile + 1, 0)
    item_ends = jnp.cumsum(items_per_expert)
    n_items = item_ends[-1:]
    j = jnp.minimum(jnp.arange(max_items, dtype=jnp.int32), n_items[0] - 1)
    expert = jnp.minimum(jnp.sum(item_ends[None, :] <= j[:, None], axis=1), N_EXPERTS - 1).astype(jnp.int32)
    tile = first_tile[expert] + (j - (item_ends - items_per_expert)[expert])
    lo = jnp.clip(starts[expert] - tile * tm, 0, tm)
    hi = jnp.clip(ends[expert] - tile * tm, 0, tm)
    i32 = lambda a: a.astype(jnp.int32)
    return i32(expert), i32(tile), i32(lo), i32(hi), i32(n_items), starts


def _grouped_ffn(xs, items, w_gate, w_up, w_down):
    tm = FFN_TM
    d, de = w_gate.shape[1:]
    item_expert, item_tile, item_lo, item_hi, n_items = items
    max_items = item_expert.shape[0]

    def row_map(j, ie, it, lo, hi, ni):
        return (it[j], 0)

    def w_map(j, ie, it, lo, hi, ni):
        return (ie[j], 0, 0)

    grid_spec = pltpu.PrefetchScalarGridSpec(
        num_scalar_prefetch=5,
        grid=(max_items,),
        in_specs=[pl.BlockSpec((tm * SUBLANES, LANES), row_map),
                  pl.BlockSpec((1, d, de), w_map),
                  pl.BlockSpec((1, d, de), w_map),
                  pl.BlockSpec((1, de, d), w_map)],
        out_specs=pl.BlockSpec((tm * SUBLANES, LANES), row_map),
        scratch_shapes=[pltpu.VMEM((d, de), jnp.bfloat16), pltpu.VMEM((d, de), jnp.bfloat16),
                        pltpu.VMEM((de, d), jnp.bfloat16), pltpu.VMEM((tm, d), jnp.float32)],
    )
    return pl.pallas_call(
        _ffn_kernel,
        grid_spec=grid_spec,
        out_shape=jax.ShapeDtypeStruct(xs.shape, jnp.float32),
        compiler_params=_cparams(("arbitrary",)),
        name="moe_grouped_ffn",
    )(item_expert, item_tile, item_lo, item_hi, n_items, xs, w_gate, w_up, w_down)


def _combine_kernel(pos0_ref, pos1_ref, next0_ref, next1_ref, x1_ref, route_ref, g2_ref, b2_ref, ys_ref, o_ref,
                    buf, sems):
    tt = x1_ref.shape[0] // SUBLANES
    i = pl.program_id(0)
    cur = lax.rem(i, 2)

    def issue_tile(p_refs, b):
        def issue(j, carry):
            for u in range(2):
                t = 2 * j + u
                for slot, p_ref in enumerate(p_refs):
                    p = p_ref[t]
                    src = ys_ref.at[pl.ds(pl.multiple_of(p * SUBLANES, SUBLANES), SUBLANES)]
                    dst = buf.at[b, slot, pl.ds(pl.multiple_of(t * SUBLANES, SUBLANES), SUBLANES)]
                    pltpu.make_async_copy(src, dst, sems.at[b]).start(priority=u)
            return carry

        lax.fori_loop(0, tt // 2, issue, 0, unroll=4)

    @pl.when(i == 0)
    def _():
        issue_tile((pos0_ref, pos1_ref), 0)

    @pl.when(i + 1 < pl.num_programs(0))
    def _():
        issue_tile((next0_ref, next1_ref), 1 - cur)

    for slot in range(2):
        pltpu.make_async_copy(ys_ref.at[pl.ds(0, tt * SUBLANES)], buf.at[cur, slot], sems.at[cur]).wait()
    route = route_ref[...]
    y = route[:, 2:3] * _load_row_tiles(buf.at[cur, 0], tt) + route[:, 3:4] * _load_row_tiles(buf.at[cur, 1], tt)
    o_ref[...] = _layer_norm(DN_ALPHA * _load_row_tiles(x1_ref, tt) + y, g2_ref[...], b2_ref[...])


def _combine(x1t, route, pos, ys, g2, b2, tok0, n_tok):
    d = g2.shape[1]
    tt = COMB_TT
    t0 = tok0 // tt
    nt = n_tok // tt
    return pl.pallas_call(
        _combine_kernel,
        grid=(nt,),
        in_specs=[pl.BlockSpec((tt,), lambda i: (i + t0,), memory_space=pltpu.SMEM),
                  pl.BlockSpec((tt,), lambda i: (i + t0,), memory_space=pltpu.SMEM),
                  pl.BlockSpec((tt,), lambda i: (jnp.minimum(i + 1, nt - 1) + t0,), memory_space=pltpu.SMEM),
                  pl.BlockSpec((tt,), lambda i: (jnp.minimum(i + 1, nt - 1) + t0,), memory_space=pltpu.SMEM),
                  pl.BlockSpec((tt * SUBLANES, LANES), lambda i: (i + t0, 0)),
                  pl.BlockSpec((tt, ROUTE_LANES), lambda i: (i + t0, 0)),
                  pl.BlockSpec((1, d), lambda i: (0, 0)),
                  pl.BlockSpec((1, d), lambda i: (0, 0)),
                  pl.BlockSpec(memory_space=pl.ANY)],
        out_specs=pl.BlockSpec((tt, d), lambda i: (i, 0)),
        out_shape=jax.ShapeDtypeStruct((n_tok, d), jnp.float32),
        scratch_shapes=[pltpu.VMEM((2, 2, tt * SUBLANES, LANES), ys.dtype), pltpu.SemaphoreType.DMA((2,))],
        compiler_params=_cparams(("arbitrary",)),
        name="moe_combine_ln",
    )(pos[0], pos[1], pos[0], pos[1], x1t, route, g2, b2, ys)


def _layer(x_prompt, x_sample, ln_in_g, ln_in_b, w_in, b_in, rel_bias_t5, na_rpb, w_branch_a, w_branch_b,
           w_out, b_out, ln1_g, ln1_b, w_router_group, b_router_group, w_router_expert, b_router_expert,
           w_exp_gate, w_exp_up, w_exp_down, ln2_g, ln2_b):
    bp, tp, d = x_prompt.shape
    bs, ts, _ = x_sample.shape
    seg_tokens = (bp * tp, bs * ts)
    seq_lens = (tp, ts)
    n = sum(seg_tokens)
    xp = x_prompt.reshape(seg_tokens[0], d)
    xs = x_sample.reshape(seg_tokens[1], d)
    row = lambda a: a.reshape(1, -1).astype(jnp.float32)
    bf = lambda a: a.astype(jnp.bfloat16)

    scale = HEAD_DIM ** -0.5
    col_scale = np.ones((w_in.shape[2],), np.float32)
    col_scale[0:QA] = scale
    col_scale[3 * QA:3 * QA + QB] = scale
    w_proj = bf(w_in[0] * col_scale)
    b_proj = row(b_in[0] * col_scale)
    proj = _in_projection(xp, xs, row(ln_in_g), row(ln_in_b), w_proj, b_proj)
    qkv_a, (qb, kb, vb, ga, gb, x0) = proj[:3 * N_DIL_GROUPS], proj[3 * N_DIL_GROUPS:]

    o_groups, lse_groups = [], []
    for g, (window, dil) in enumerate(DIL_PATTERNS):
        assert window // (2 * dil) == HALF_SPAN
        table = rel_bias_t5[:, g * HEADS_PER_DIL_GROUP:(g + 1) * HEADS_PER_DIL_GROUP].astype(jnp.float32).reshape(-1)
        o, lse = _dilated_group(qkv_a[g], qkv_a[N_DIL_GROUPS + g], qkv_a[2 * N_DIL_GROUPS + g], table, dil,
                                seg_tokens, seq_lens)
        o_groups.append(o)
        lse_groups.append(lse)
    ob = _neighborhood(qb, kb, vb, na_rpb[0].astype(jnp.float32).reshape(-1), seg_tokens, seq_lens)

    wr = jnp.zeros((d, ROUTE_LANES), jnp.float32)
    wr = wr.at[:, 0:N_GROUPS].set(w_router_group[0]).at[:, EXPERT_LANE0:EXPERT_LANE0 + N_EXPERTS].set(w_router_expert[0])
    br = jnp.zeros((1, ROUTE_LANES), jnp.float32)
    br = br.at[0, 0:N_GROUPS].set(b_router_group[0]).at[0, EXPERT_LANE0:EXPERT_LANE0 + N_EXPERTS].set(b_router_expert[0])
    x1, route, route_t, counts = _post_attention(
        x0, o_groups, lse_groups, ob, ga, gb, bf(w_branch_a[0]), bf(w_branch_b[0]),
        bf(w_out[0]), row(b_out[0]), row(ln1_g[0]), row(ln1_b[0]), bf(wr), br)

    experts = route_t[0:2].astype(jnp.int32)
    ranks = route_t[4:6].astype(jnp.int32)
    cnt = counts[:, 0].astype(jnp.int32)
    n_rows = 2 * n
    *items, starts = _ffn_items(cnt, n_rows)
    one_hot = (experts[..., None] == jnp.arange(N_EXPERTS, dtype=jnp.int32)).astype(jnp.float32)
    pos = ranks + jnp.dot(one_hot, starts.astype(jnp.float32), precision=lax.Precision.HIGHEST).astype(jnp.int32)

    xsorted = _dispatch(x1, pos, n_rows)
    ys = _grouped_ffn(xsorted, items, w_exp_gate[0], w_exp_up[0], w_exp_down[0])
    y_prompt = _combine(x1, route, pos, ys, row(ln2_g[0]), row(ln2_b[0]), 0, seg_tokens[0])
    y_sample = _combine(x1, route, pos, ys, row(ln2_g[0]), row(ln2_b[0]), seg_tokens[0], seg_tokens[1])
    return y_prompt.reshape(bp, tp, d), y_sample.reshape(bs, ts, d)


def kernel(x_prompt, x_sample, ln_in_g, ln_in_b, w_in, b_in, rel_bias_t5, na_rpb, w_branch_a, w_branch_b, w_out, b_out, ln1_g, ln1_b, w_router_group, b_router_group, w_router_expert, b_router_expert, w_exp_gate, w_exp_up, w_exp_down, ln2_g, ln2_b):
    return _layer(x_prompt, x_sample, ln_in_g, ln_in_b, w_in, b_in, rel_bias_t5, na_rpb, w_branch_a, w_branch_b,
                  w_out, b_out, ln1_g, ln1_b, w_router_group, b_router_group, w_router_expert, b_router_expert,
                  w_exp_gate, w_exp_up, w_exp_down, ln2_g, ln2_b)
```

```python
import functools
import math

import numpy as np
import jax
import jax.numpy as jnp
from jax import lax
from jax.experimental import pallas as pl
from jax.experimental.pallas import tpu as pltpu

HEAD_DIM = 64
DIL_PATTERNS = ((128, 1), (512, 4), (2048, 16))
HEADS_PER_DIL_GROUP = 4
N_DIL_GROUPS = 3
H_A = N_DIL_GROUPS * HEADS_PER_DIL_GROUP
H_B = 8
QA = H_A * HEAD_DIM
QB = H_B * HEAD_DIM
HALF_SPAN = 64
GRID_W = 64
NA_KH = 8
NA_KW = 16
N_BUCKETS = 32
REL_MAX_DIST = 1024
N_GROUPS = 4
EXPERTS_PER_GROUP = 8
N_EXPERTS = N_GROUPS * EXPERTS_PER_GROUP
LN_EPS = 1e-5
NEG_INF = -1e30
DEPTH = 1
DN_ALPHA = (2.0 * DEPTH) ** 0.25

LANES = 128
HEAD_BLOCK = 4 * HEAD_DIM
VMEM_LIMIT_BYTES = 56 * 1024 * 1024

PROJ_TM = 512
DIL_M = 128
DIL_GROUP = 16
NA_ROWS = 32
NA_HALO = NA_KH // 2
POST_TM = 512
POST_SPLIT = 1
FFN_TM = 512
DISP_TT = 4096
COMB_TT = 512
ROUTE_LANES = LANES
EXPERT_LANE0 = 32


def _cparams(sem):
    return pltpu.CompilerParams(dimension_semantics=sem, vmem_limit_bytes=VMEM_LIMIT_BYTES)


SUBLANES = 8


def _store_row_tiles(ref, x, row0=0):
    rows = x.shape[0]
    for c in range(SUBLANES):
        ref[pl.ds(row0 * SUBLANES + c, rows, stride=SUBLANES), :] = x[:, c * LANES:(c + 1) * LANES]


def _load_row_tiles(ref, rows, row0=0):
    return jnp.concatenate([ref[pl.ds(row0 * SUBLANES + c, rows, stride=SUBLANES), :] for c in range(SUBLANES)],
                           axis=1)


def _layer_norm(x, g, b):
    mu = jnp.mean(x, axis=-1, keepdims=True)
    xc = x - mu
    var = jnp.mean(xc * xc, axis=-1, keepdims=True)
    return xc * lax.rsqrt(var + LN_EPS) * g + b


def _strided_rows(scr, slab0, r, count, stride):
    return jnp.concatenate([scr[slab0 + s, pl.ds(r, count, stride=stride), :] for s in range(2)], axis=1)


def _inproj_kernel(xp_ref, xs_ref, g_ref, b_ref, w_ref, bias_ref, *refs, n0_tiles):
    out_refs, scr = refs[:-1], refs[-1]
    i = pl.program_id(0)
    x = jnp.where(i < n0_tiles, xp_ref[...], xs_ref[...])
    x0 = _layer_norm(x, g_ref[...], b_ref[...])
    out_refs[-1][...] = x0
    out_refs = out_refs[:-1]
    xn = x0.astype(jnp.bfloat16)
    tm = xn.shape[0]

    def proj(c0, cw):
        return jnp.dot(xn, w_ref[:, c0:c0 + cw], preferred_element_type=jnp.float32) + bias_ref[:, c0:c0 + cw]

    n_dil = 3 * N_DIL_GROUPS
    for idx in range(n_dil):
        g = idx % N_DIL_GROUPS
        dil = DIL_PATTERNS[g][1]
        o_ref = out_refs[idx]
        y = proj(idx * HEAD_BLOCK, HEAD_BLOCK)
        if dil == 1:
            o_ref[0, 0] = y.astype(o_ref.dtype)
        else:
            slab0 = 2 * (idx % 2)
            scr[slab0] = y[:, :LANES]
            scr[slab0 + 1] = y[:, LANES:]
            for r in range(dil):
                o_ref[0, r] = _strided_rows(scr, slab0, r, tm // dil, dil).astype(o_ref.dtype)
    col = n_dil * HEAD_BLOCK
    for o_ref in out_refs[n_dil:]:
        width = o_ref.shape[1]
        for c in range(0, width, 512):
            cw = min(512, width - c)
            o_ref[:, c:c + cw] = proj(col + c, cw).astype(o_ref.dtype)
        col += width


def _in_projection(xp, xs, ln_g, ln_b, w, b):
    n0, d_model = xp.shape
    n1 = xs.shape[0]
    n = n0 + n1
    tm = PROJ_TM
    n0_tiles, n1_tiles = n0 // tm, n1 // tm
    nt = n0_tiles + n1_tiles
    out_shape, out_specs = [], []
    for _ in range(3):
        for _, dil in DIL_PATTERNS:
            out_shape.append(jax.ShapeDtypeStruct((nt, dil, tm // dil, HEAD_BLOCK), jnp.bfloat16))
            out_specs.append(pl.BlockSpec((1, dil, tm // dil, HEAD_BLOCK), lambda i: (i, 0, 0, 0)))
    for wd, dtype in ((QB, jnp.bfloat16),) * 3 + ((d_model, jnp.bfloat16),) * 2 + ((d_model, jnp.float32),):
        out_shape.append(jax.ShapeDtypeStruct((n, wd), dtype))
        out_specs.append(pl.BlockSpec((tm, wd), lambda i: (i, 0)))
    d_in = w.shape[1]
    kern = functools.partial(_inproj_kernel, n0_tiles=n0_tiles)
    return pl.pallas_call(
        kern,
        grid=(nt,),
        in_specs=[
            pl.BlockSpec((tm, d_model), lambda i: (jnp.minimum(i, n0_tiles - 1), 0)),
            pl.BlockSpec((tm, d_model), lambda i: (jnp.maximum(i - n0_tiles, 0), 0)),
            pl.BlockSpec((1, d_model), lambda i: (0, 0)),
            pl.BlockSpec((1, d_model), lambda i: (0, 0)),
            pl.BlockSpec((d_model, d_in), lambda i: (0, 0)),
            pl.BlockSpec((1, d_in), lambda i: (0, 0)),
        ],
        out_specs=out_specs,
        out_shape=out_shape,
        scratch_shapes=[pltpu.VMEM((4, tm, LANES), jnp.float32)],
        compiler_params=_cparams(("arbitrary",)),
        name="in_projection",
    )(xp, xs, ln_g, ln_b, w, b)


def _t5_bucket_np(rel):
    half = N_BUCKETS // 2
    max_exact = half // 2
    ret = np.where(rel > 0, half, 0)
    n = np.abs(rel)
    nf = np.maximum(n, 1).astype(np.float32)
    large = max_exact + (np.log(nf / np.float32(max_exact)) / np.float32(math.log(REL_MAX_DIST / max_exact))
                         * np.float32(half - max_exact)).astype(np.int32)
    large = np.minimum(large, half - 1)
    return ret + np.where(n < max_exact, n, large)


def _t5_bucket_starts():
    half = N_BUCKETS // 2
    dist = np.arange(0, HALF_SPAN * max(d for _, d in DIL_PATTERNS) + 1)
    buckets = _t5_bucket_np(-dist)
    return [int(np.argmax(buckets >= k)) for k in range(half)]


def _build_dilated_bias(table_ref, bias_scr, dil):
    half = N_BUCKETS // 2
    starts = _t5_bucket_starts()
    kb = DIL_M + 2 * HALF_SPAN
    i = lax.broadcasted_iota(jnp.int32, (DIL_M, kb), 0)
    m = lax.broadcasted_iota(jnp.int32, (DIL_M, kb), 1)
    delta = m - HALF_SPAN - i
    dist = jnp.abs(delta) * dil
    band = jnp.abs(delta) <= HALF_SPAN
    for h in range(HEADS_PER_DIL_GROUP):
        sides = []
        for side in range(2):
            val = jnp.full((DIL_M, kb), table_ref[(side * half + half - 1) * HEADS_PER_DIL_GROUP + h], jnp.float32)
            for k in range(half - 1, 0, -1):
                val = jnp.where(dist < starts[k], table_ref[(side * half + k - 1) * HEADS_PER_DIL_GROUP + h], val)
            sides.append(val)
        base = jnp.where(band, jnp.where(delta > 0, sides[1], sides[0]), NEG_INF)
        rows = slice(h * DIL_M, (h + 1) * DIL_M)
        bias_scr[0, rows, :] = base
        bias_scr[1, rows, :] = jnp.where(m >= HALF_SPAN, base, NEG_INF)
        bias_scr[2, rows, :] = jnp.where(m < kb - HALF_SPAN, base, NEG_INF)


def _stack_heads(q):
    head_of_col = lax.broadcasted_iota(jnp.int32, (1, HEAD_BLOCK), 1) // HEAD_DIM
    zero = jnp.zeros_like(q)
    return jnp.concatenate([jnp.where(head_of_col == h, q, zero) for h in range(4)], axis=0)


def _merge_heads(x, m):
    head_of_col = lax.broadcasted_iota(jnp.int32, (1, HEAD_BLOCK), 1) // HEAD_DIM
    out = jnp.zeros((m, HEAD_BLOCK), x.dtype)
    for h in range(4):
        out = jnp.where(head_of_col == h, x[h * m:(h + 1) * m], out)
    return out


def _softmax_pv(s, v, m_rows):
    mx = jnp.max(s, axis=-1, keepdims=True)
    p = jnp.exp(s - mx)
    l = jnp.sum(p, axis=-1, keepdims=True)
    pv = jnp.dot(p.astype(jnp.bfloat16), v, preferred_element_type=jnp.float32)
    o = _merge_heads(pv * (1.0 / l), m_rows)
    return o, mx + jnp.log(l)


def _dilated_kernel(table_ref, q_ref, kp_ref, kc_ref, kn_ref, vp_ref, vc_ref, vn_ref,
                    o_ref, lse_ref, bias_scr, *, dil, seg0_blocks, blocks_per_seq):
    step = pl.program_id(1)

    @pl.when((pl.program_id(0) == 0) & (step == 0))
    def _():
        _build_dilated_bias(table_ref, bias_scr, dil)

    rows2d = lambda ref: ref[...].reshape(-1, HEAD_BLOCK)
    k = jnp.concatenate([rows2d(r) for r in (kp_ref, kc_ref, kn_ref)], axis=0)
    v = jnp.concatenate([rows2d(r) for r in (vp_ref, vc_ref, vn_ref)], axis=0)
    q = rows2d(q_ref)
    o_parts, lse_parts = [], []
    for g in range(DIL_GROUP):
        jb = step * DIL_GROUP + g
        in0 = jb < seg0_blocks
        jl = jnp.where(in0, jb, jb - seg0_blocks)
        nbs = jnp.where(in0, blocks_per_seq[0], blocks_per_seq[1])
        pos = lax.rem(jl, nbs)
        variant = jnp.where(pos == 0, 1, jnp.where(pos == nbs - 1, 2, 0))
        keys = slice(g * DIL_M, (g + 1) * DIL_M + 2 * HALF_SPAN)
        qs = _stack_heads(q[g * DIL_M:(g + 1) * DIL_M])
        s = lax.dot_general(qs, k[keys], (((1,), (1,)), ((), ())), preferred_element_type=jnp.float32)
        s = s + bias_scr[variant]
        o, lse = _softmax_pv(s, v[keys], DIL_M)
        o_parts.append(o.astype(o_ref.dtype))
        lse_parts.append(_merge_heads(jnp.broadcast_to(lse, (4 * DIL_M, HEAD_BLOCK)), DIL_M))
    o_ref[...] = jnp.concatenate(o_parts, axis=0).reshape(o_ref.shape)
    lse_ref[...] = jnp.concatenate(lse_parts, axis=0).reshape(lse_ref.shape)


def _residue_spec(rows_blk, dil, block_of):
    per_tile = PROJ_TM // dil
    if per_tile >= rows_blk:
        per = per_tile // rows_blk
        return pl.BlockSpec((1, None, rows_blk, HEAD_BLOCK),
                            lambda r, jb: (block_of(jb) // per, r, block_of(jb) % per, 0))
    return pl.BlockSpec((rows_blk // per_tile, None, per_tile, HEAD_BLOCK), lambda r, jb: (block_of(jb), r, 0, 0))


def _dilated_group(q, k, v, table, dil, seg_tokens, seq_lens):
    n = q.shape[0] * PROJ_TM
    rows = n // dil
    step_rows = DIL_GROUP * DIL_M
    assert rows % step_rows == 0 and all(t % (dil * DIL_M) == 0 for t in seq_lens)
    nsteps = rows // step_rows
    nkb = rows // HALF_SPAN
    ratio = step_rows // HALF_SPAN
    seg0_blocks = seg_tokens[0] // dil // DIL_M
    blocks_per_seq = tuple(t // dil // DIL_M for t in seq_lens)
    assert min(blocks_per_seq) >= 2

    qo_spec = _residue_spec(step_rows, dil, lambda jb: jb)
    prev_spec = _residue_spec(HALF_SPAN, dil, lambda jb: jnp.maximum(ratio * jb - 1, 0))
    next_spec = _residue_spec(HALF_SPAN, dil, lambda jb: jnp.minimum(ratio * (jb + 1), nkb - 1))
    kv_specs = [prev_spec, qo_spec, next_spec]
    kern = functools.partial(_dilated_kernel, dil=dil, seg0_blocks=seg0_blocks, blocks_per_seq=blocks_per_seq)
    return pl.pallas_call(
        kern,
        grid=(dil, nsteps),
        in_specs=[pl.BlockSpec(memory_space=pltpu.SMEM), qo_spec] + kv_specs + kv_specs,
        out_specs=[qo_spec, qo_spec],
        out_shape=[jax.ShapeDtypeStruct(q.shape, jnp.bfloat16), jax.ShapeDtypeStruct(q.shape, jnp.float32)],
        scratch_shapes=[pltpu.VMEM((3, HEADS_PER_DIL_GROUP * DIL_M, DIL_M + 2 * HALF_SPAN), jnp.float32)],
        compiler_params=_cparams(("arbitrary", "arbitrary")),
        name=f"dilated_attention_d{dil}",
    )(table, q, k, k, k, v, v, v)


NA_DR = 2 * NA_KH - 1
NA_DC = 2 * NA_KW - 1
NA_PAIRS = NA_DR - 1


def _build_na_bias(rpb_ref, pair_scr, head0):
    qc = lax.broadcasted_iota(jnp.int32, (GRID_W, 2 * GRID_W), 0)
    lane = lax.broadcasted_iota(jnp.int32, (GRID_W, 2 * GRID_W), 1)
    kc = lane % GRID_W
    upper = lane >= GRID_W
    qstart = jnp.clip(qc - NA_KW // 2, 0, GRID_W - NA_KW)
    valid = (kc >= qstart) & (kc < qstart + NA_KW)
    dc = jnp.clip(kc - qc, -(NA_KW - 1), NA_KW - 1) + NA_KW - 1

    def one_pair(idx, carry):
        h = idx // NA_PAIRS
        dr = idx % NA_PAIRS
        base = ((head0 + h) * NA_DR + dr) * NA_DC
        val = jnp.zeros((GRID_W, 2 * GRID_W), jnp.float32)
        for c in range(NA_DC):
            val = jnp.where(dc == c, jnp.where(upper, rpb_ref[base + NA_DC + c], rpb_ref[base + c]), val)
        pair_scr[h, dr] = jnp.where(valid, val, NEG_INF)
        return carry

    lax.fori_loop(0, 4 * NA_PAIRS, one_pair, 0)


def _na_kernel(rpb_ref, q_ref, kp_ref, kc_ref, kn_ref, vp_ref, vc_ref, vn_ref, o_ref, k_scr, v_scr, pair_scr,
               *, seg0_blocks, blocks_per_seq):
    ib = pl.program_id(1)

    @pl.when(ib == 0)
    def _():
        _build_na_bias(rpb_ref, pair_scr, pl.program_id(0) * 4)

    in0 = ib < seg0_blocks
    il = jnp.where(in0, ib, ib - seg0_blocks)
    nbs = jnp.where(in0, blocks_per_seq[0], blocks_per_seq[1])
    pos = lax.rem(il, nbs)
    first = pos == 0
    last = pos == nbs - 1
    blk = NA_ROWS * GRID_W
    halo = NA_HALO * GRID_W
    for scr, (p_ref, c_ref, n_ref) in ((k_scr, (kp_ref, kc_ref, kn_ref)), (v_scr, (vp_ref, vc_ref, vn_ref))):
        scr[0:halo] = p_ref[...]
        scr[halo:halo + blk] = c_ref[...]
        scr[halo + blk:2 * halo + blk] = n_ref[...]
    half = NA_KH // 2
    for rr in range(NA_ROWS):
        inner = NA_HALO + rr - half
        s_first, s_last = max(inner, NA_HALO), min(inner, NA_HALO + NA_ROWS - NA_KH)
        start = jnp.where(first, s_first, jnp.where(last, s_last, inner))
        var = jnp.where(first, NA_HALO + rr - s_first, jnp.where(last, NA_HALO + rr - s_last, half))
        off = pl.multiple_of(start * GRID_W, GRID_W)
        kk = k_scr[pl.ds(off, NA_KH * GRID_W), :]
        vv = v_scr[pl.ds(off, NA_KH * GRID_W), :]
        qs = _stack_heads(q_ref[rr * GRID_W:(rr + 1) * GRID_W, :])
        s = lax.dot_general(qs, kk, (((1,), (1,)), ((), ())), preferred_element_type=jnp.float32)
        bias = jnp.concatenate(
            [jnp.concatenate([pair_scr[h, 2 * p - var + NA_KH - 1] for p in range(NA_KH // 2)], axis=1)
             for h in range(4)], axis=0)
        s = s + bias
        o, _ = _softmax_pv(s, vv, GRID_W)
        o_ref[rr * GRID_W:(rr + 1) * GRID_W, :] = o.astype(o_ref.dtype)


def _neighborhood(qb, kb, vb, rpb_flat, seg_tokens, seq_lens):
    n = qb.shape[0]
    blk = NA_ROWS * GRID_W
    nblk = n // blk
    seg0_blocks = seg_tokens[0] // blk
    blocks_per_seq = tuple(t // blk for t in seq_lens)
    assert min(blocks_per_seq) >= 2
    nset = QB // HEAD_BLOCK

    per = NA_ROWS // NA_HALO
    halo = NA_HALO * GRID_W
    cur_spec = pl.BlockSpec((blk, HEAD_BLOCK), lambda c, ib: (ib, c))
    prev_spec = pl.BlockSpec((halo, HEAD_BLOCK), lambda c, ib: (jnp.maximum(per * ib - 1, 0), c))
    next_spec = pl.BlockSpec((halo, HEAD_BLOCK), lambda c, ib: (jnp.minimum(per * (ib + 1), per * nblk - 1), c))
    kv_specs = [prev_spec, cur_spec, next_spec]

    kern = functools.partial(_na_kernel, seg0_blocks=seg0_blocks, blocks_per_seq=blocks_per_seq)
    return pl.pallas_call(
        kern,
        grid=(nset, nblk),
        in_specs=[pl.BlockSpec(memory_space=pltpu.SMEM), cur_spec] + kv_specs + kv_specs,
        out_specs=cur_spec,
        out_shape=jax.ShapeDtypeStruct((n, QB), jnp.bfloat16),
        scratch_shapes=[pltpu.VMEM((blk + 2 * halo, HEAD_BLOCK), jnp.bfloat16),
                        pltpu.VMEM((blk + 2 * halo, HEAD_BLOCK), jnp.bfloat16),
                        pltpu.VMEM((4, NA_PAIRS, GRID_W, 2 * GRID_W), jnp.float32)],
        compiler_params=_cparams(("arbitrary", "arbitrary")),
        name="neighborhood_attention",
    )(rpb_flat, qb, kb, kb, kb, vb, vb, vb)


def _post_kernel(x0_ref, o0_ref, o1_ref, o2_ref, l0_ref, l1_ref, l2_ref, ob_ref,
                 ga_ref, gb_ref, wa_ref, wb_ref, wo_ref, bo_ref, g1_ref, b1_ref, wr_ref, br_ref, tri_ref,
                 x1_ref, route_ref, route_t_ref, counts_ref, carry_ref, perm_scr):
    i = pl.program_id(0)
    tm = route_ref.shape[0]
    hm = tm // POST_SPLIT

    @pl.when(i == 0)
    def _():
        carry_ref[...] = jnp.zeros_like(carry_ref)

    def token_order(ref, slab0):
        dil, per = ref.shape[1], ref.shape[2]
        if dil == 1:
            return lambda rs: ref[0, 0, rs, :].astype(jnp.float32)
        for r in range(dil):
            val = ref[0, r].astype(jnp.float32)
            for s in range(2):
                perm_scr[slab0 + s, pl.ds(r, per, stride=dil), :] = val[:, s * LANES:(s + 1) * LANES]
        return lambda rs: jnp.concatenate([perm_scr[slab0, rs, :], perm_scr[slab0 + 1, rs, :]], axis=1)

    lse_of = [token_order(ref, 2 * j) for j, ref in enumerate((l0_ref, l1_ref, l2_ref))]
    o_of = [token_order(ref, 6 + 2 * j) for j, ref in enumerate((o0_ref, o1_ref, o2_ref))]
    sigmoid = lambda v: 0.5 * jnp.tanh(0.5 * v) + 0.5
    grow = lax.broadcasted_iota(jnp.int32, (8, hm), 0)
    erow = lax.broadcasted_iota(jnp.int32, (N_EXPERTS, hm), 0)
    frow = lax.broadcasted_iota(jnp.int32, (ROUTE_LANES, hm), 0)
    neg = jnp.float32(-jnp.inf)
    carry = carry_ref[:, 0:1]

    for h in range(POST_SPLIT):
        rs = slice(h * hm, (h + 1) * hm)
        l0, l1, l2 = (f(rs) for f in lse_of)
        lm = jnp.maximum(jnp.maximum(l0, l1), l2)
        e0, e1, e2 = jnp.exp(l0 - lm), jnp.exp(l1 - lm), jnp.exp(l2 - lm)
        inv = 1.0 / (e0 + e1 + e2)
        o_a = (e0 * inv) * o_of[0](rs) + (e1 * inv) * o_of[1](rs) + (e2 * inv) * o_of[2](rs)
        y_a = jnp.dot(o_a.astype(jnp.bfloat16), wa_ref[...], preferred_element_type=jnp.float32)
        y_b = jnp.dot(ob_ref[rs, :], wb_ref[...], preferred_element_type=jnp.float32)
        mix = sigmoid(ga_ref[rs, :].astype(jnp.float32)) * y_a + sigmoid(gb_ref[rs, :].astype(jnp.float32)) * y_b
        out = jnp.dot(mix.astype(jnp.bfloat16), wo_ref[...], preferred_element_type=jnp.float32) + bo_ref[...]
        x1 = _layer_norm(DN_ALPHA * x0_ref[rs, :] + out, g1_ref[...], b1_ref[...])
        _store_row_tiles(x1_ref, x1, row0=h * hm)

        r = jnp.dot(x1.astype(jnp.bfloat16), wr_ref[...], preferred_element_type=jnp.float32) + br_ref[...]
        rt = r.T
        lg = jnp.where(grow < N_GROUPS, rt[0:8], neg)
        gmax = jnp.max(lg, axis=0, keepdims=True)
        gi = jnp.min(jnp.where(lg == gmax, grow, 8), axis=0, keepdims=True)
        wg = 1.0 / jnp.sum(jnp.exp(lg - gmax), axis=0, keepdims=True)
        lo = EXPERTS_PER_GROUP * gi
        le = jnp.where((erow >= lo) & (erow < lo + EXPERTS_PER_GROUP), rt[EXPERT_LANE0:EXPERT_LANE0 + N_EXPERTS], neg)
        m1 = jnp.max(le, axis=0, keepdims=True)
        i1 = jnp.min(jnp.where(le == m1, erow, N_EXPERTS), axis=0, keepdims=True)
        le2 = jnp.where(erow == i1, neg, le)
        m2 = jnp.max(le2, axis=0, keepdims=True)
        i2 = jnp.min(jnp.where(le2 == m2, erow, N_EXPERTS), axis=0, keepdims=True)
        t2 = jnp.exp(m2 - m1)
        w1 = wg / (1.0 + t2)
        w2 = wg * t2 / (1.0 + t2)

        hot1 = erow == i1
        hot2 = erow == i2
        hot = (hot1 | hot2).astype(jnp.float32)
        before = jnp.dot(hot.astype(jnp.bfloat16), tri_ref[...], preferred_element_type=jnp.float32) + carry
        rank1 = jnp.sum(jnp.where(hot1, before, 0.0), axis=0, keepdims=True)
        rank2 = jnp.sum(jnp.where(hot2, before, 0.0), axis=0, keepdims=True)
        carry = carry + jnp.sum(hot, axis=1, keepdims=True)

        fields = (i1.astype(jnp.float32), i2.astype(jnp.float32), w1, w2, rank1, rank2)
        route_t = jnp.zeros((ROUTE_LANES, hm), jnp.float32)
        for idx, val in enumerate(fields):
            route_t = jnp.where(frow == idx, val, route_t)
        route_t_ref[:, rs] = route_t[0:route_t_ref.shape[0]]
        route_ref[rs, :] = route_t.T

    carry_ref[...] = jnp.broadcast_to(carry, carry_ref.shape)
    counts_ref[...] = jnp.broadcast_to(carry, counts_ref.shape)


def _post_attention(x0, o_groups, lse_groups, ob, ga, gb, wa, wb, wo, bo, g1, b1, wr, br):
    n, d = x0.shape
    tm = POST_TM
    nt = n // tm
    hm = tm // POST_SPLIT
    tri = jnp.asarray(np.triu(np.ones((hm, hm), np.float32), 1), jnp.bfloat16)

    def tok(width):
        return pl.BlockSpec((tm, width), lambda i: (i, 0))

    def full(a):
        return pl.BlockSpec(a.shape, lambda i: (0,) * a.ndim)

    return pl.pallas_call(
        _post_kernel,
        grid=(nt,),
        in_specs=[
            tok(d),
            *[pl.BlockSpec((1,) + a.shape[1:], lambda i: (i, 0, 0, 0)) for a in (*o_groups, *lse_groups)],
            tok(QB), tok(d), tok(d),
            full(wa), full(wb), full(wo), full(bo), full(g1), full(b1), full(wr), full(br), full(tri),
        ],
        out_specs=[pl.BlockSpec((tm * SUBLANES, LANES), lambda i: (i, 0)), tok(ROUTE_LANES),
                   pl.BlockSpec((8, tm), lambda i: (0, i)), pl.BlockSpec((N_EXPERTS, ROUTE_LANES), lambda i: (0, 0))],
        out_shape=[jax.ShapeDtypeStruct((n * SUBLANES, LANES), jnp.float32),
                   jax.ShapeDtypeStruct((n, ROUTE_LANES), jnp.float32),
                   jax.ShapeDtypeStruct((8, n), jnp.float32),
                   jax.ShapeDtypeStruct((N_EXPERTS, ROUTE_LANES), jnp.float32)],
        scratch_shapes=[pltpu.VMEM((N_EXPERTS, ROUTE_LANES), jnp.float32), pltpu.VMEM((12, tm, LANES), jnp.float32)],
        compiler_params=_cparams(("arbitrary",)),
        name="post_attention_router",
    )(x0, *o_groups, *lse_groups, ob, ga, gb, wa, wb, wo, bo, g1, b1, wr, br, tri)


def _dispatch_kernel(pos0_ref, pos1_ref, x_ref, xs_ref, sem):
    tt = x_ref.shape[0] // SUBLANES

    def issue(j, carry):
        for u in range(2):
            t = 2 * j + u
            src = x_ref.at[pl.ds(pl.multiple_of(t * SUBLANES, SUBLANES), SUBLANES)]
            for p_ref in (pos0_ref, pos1_ref):
                p = p_ref[t]
                dst = xs_ref.at[pl.ds(pl.multiple_of(p * SUBLANES, SUBLANES), SUBLANES)]
                pltpu.make_async_copy(src, dst, sem).start(priority=u)
        return carry

    lax.fori_loop(0, tt // 2, issue, 0, unroll=4)
    for _ in range(2):
        pltpu.make_async_copy(x_ref, xs_ref.at[pl.ds(0, tt * SUBLANES)], sem).wait()


def _dispatch(x1t, pos, n_rows):
    n = x1t.shape[0] // SUBLANES
    tt = DISP_TT
    return pl.pallas_call(
        _dispatch_kernel,
        grid=(n // tt,),
        in_specs=[pl.BlockSpec((tt,), lambda i: (i,), memory_space=pltpu.SMEM),
                  pl.BlockSpec((tt,), lambda i: (i,), memory_space=pltpu.SMEM),
                  pl.BlockSpec((tt * SUBLANES, LANES), lambda i: (i, 0))],
        out_specs=pl.BlockSpec(memory_space=pl.ANY),
        out_shape=jax.ShapeDtypeStruct((n_rows * SUBLANES, LANES), x1t.dtype),
        scratch_shapes=[pltpu.SemaphoreType.DMA(())],
        compiler_params=_cparams(("arbitrary",)),
        name="moe_dispatch",
    )(pos[0], pos[1], x1t)


def _ffn_kernel(item_expert_ref, item_tile_ref, item_lo_ref, item_hi_ref, n_items_ref, item_next_ref, item_slot_ref,
                x_ref, wg_hbm, wu_hbm, wd_hbm, y_ref, wg_bf, wu_bf, wd_bf, y_acc, wg_buf, wu_buf, wd_buf, sems):
    j = pl.program_id(0)

    @pl.when(j == 0)
    def _():
        y_acc[...] = jnp.zeros_like(y_acc)

    def weight_copies(expert, slot):
        return [pltpu.make_async_copy(src.at[expert], dst.at[slot], sems.at[slot])
                for src, dst in ((wg_hbm, wg_buf), (wu_hbm, wu_buf), (wd_hbm, wd_buf))]

    @pl.when(j < n_items_ref[0])
    def _():
        prev = jnp.maximum(j - 1, 0)
        expert = item_expert_ref[j]
        slot = item_slot_ref[j]

        @pl.when(jnp.logical_or(j == 0, expert != item_expert_ref[prev]))
        def _():
            @pl.when(j == 0)
            def _():
                for c in weight_copies(expert, slot):
                    c.start()

            for c in weight_copies(expert, slot):
                c.wait()
            wg_bf[...] = wg_buf[slot].astype(jnp.bfloat16)
            wu_bf[...] = wu_buf[slot].astype(jnp.bfloat16)
            wd_bf[...] = wd_buf[slot].astype(jnp.bfloat16)
            nxt = item_next_ref[j]

            @pl.when(nxt >= 0)
            def _():
                for c in weight_copies(nxt, 1 - slot):
                    c.start()

        tm = x_ref.shape[0] // SUBLANES
        x = _load_row_tiles(x_ref, tm).astype(jnp.bfloat16)
        a = jnp.dot(x, wg_bf[...], preferred_element_type=jnp.float32)
        u = jnp.dot(x, wu_bf[...], preferred_element_type=jnp.float32)
        h = (a * jax.nn.sigmoid(a) * u).astype(jnp.bfloat16)
        y = jnp.dot(h, wd_bf[...], preferred_element_type=jnp.float32)
        rows = lax.broadcasted_iota(jnp.int32, y.shape, 0)
        pltpu.store(y_acc, y, mask=(rows >= item_lo_ref[j]) & (rows < item_hi_ref[j]))
        _store_row_tiles(y_ref, y_acc[...])


def _ffn_items(cnt, n_rows):
    tm = FFN_TM
    n_tiles = n_rows // tm
    max_items = n_tiles + N_EXPERTS - 1
    ends = jnp.cumsum(cnt)
    starts = ends - cnt
    first_tile = starts // tm
    last_tile = jnp.maximum(ends - 1, 0) // tm
    items_per_expert = jnp.where(cnt > 0, last_tile - first_tile + 1, 0)
    item_ends = jnp.cumsum(items_per_expert)
    n_items = item_ends[-1:]
    j = jnp.minimum(jnp.arange(max_items, dtype=jnp.int32), n_items[0] - 1)
    expert = jnp.minimum(jnp.sum(item_ends[None, :] <= j[:, None], axis=1), N_EXPERTS - 1).astype(jnp.int32)
    tile = first_tile[expert] + (j - (item_ends - items_per_expert)[expert])
    lo = jnp.clip(starts[expert] - tile * tm, 0, tm)
    hi = jnp.clip(ends[expert] - tile * tm, 0, tm)
    active = cnt > 0
    ids = jnp.arange(N_EXPERTS, dtype=jnp.int32)
    later = (ids[None, :] > ids[:, None]) & active[None, :]
    next_active = jnp.min(jnp.where(later, ids[None, :], N_EXPERTS), axis=1)
    next_active = jnp.where(next_active < N_EXPERTS, next_active, -1)
    slot = (jnp.cumsum(active.astype(jnp.int32)) - 1) % 2
    i32 = lambda a: a.astype(jnp.int32)
    return (i32(expert), i32(tile), i32(lo), i32(hi), i32(n_items), i32(next_active[expert]), i32(slot[expert]),
            starts)


def _grouped_ffn(xs, items, w_gate, w_up, w_down):
    tm = FFN_TM
    d, de = w_gate.shape[1:]
    max_items = items[0].shape[0]

    def row_map(j, ie, it, *_):
        return (it[j], 0)

    grid_spec = pltpu.PrefetchScalarGridSpec(
        num_scalar_prefetch=len(items),
        grid=(max_items,),
        in_specs=[pl.BlockSpec((tm * SUBLANES, LANES), row_map)] + [pl.BlockSpec(memory_space=pl.ANY)] * 3,
        out_specs=pl.BlockSpec((tm * SUBLANES, LANES), row_map),
        scratch_shapes=[pltpu.VMEM((d, de), jnp.bfloat16), pltpu.VMEM((d, de), jnp.bfloat16),
                        pltpu.VMEM((de, d), jnp.bfloat16), pltpu.VMEM((tm, d), jnp.float32),
                        pltpu.VMEM((2, d, de), jnp.float32), pltpu.VMEM((2, d, de), jnp.float32),
                        pltpu.VMEM((2, de, d), jnp.float32), pltpu.SemaphoreType.DMA((2,))],
    )
    return pl.pallas_call(
        _ffn_kernel,
        grid_spec=grid_spec,
        out_shape=jax.ShapeDtypeStruct(xs.shape, jnp.float32),
        compiler_params=_cparams(("arbitrary",)),
        name="moe_grouped_ffn",
    )(*items, xs, w_gate, w_up, w_down)


def _combine_kernel(pos0_ref, pos1_ref, next0_ref, next1_ref, x1_ref, route_ref, g2_ref, b2_ref, ys_ref, o_ref,
                    buf, sems):
    tt = x1_ref.shape[0] // SUBLANES
    i = pl.program_id(0)
    cur = lax.rem(i, 2)

    def issue_tile(p_refs, b):
        def issue(j, carry):
            for u in range(2):
                t = 2 * j + u
                for slot, p_ref in enumerate(p_refs):
                    p = p_ref[t]
                    src = ys_ref.at[pl.ds(pl.multiple_of(p * SUBLANES, SUBLANES), SUBLANES)]
                    dst = buf.at[b, slot, pl.ds(pl.multiple_of(t * SUBLANES, SUBLANES), SUBLANES)]
                    pltpu.make_async_copy(src, dst, sems.at[b]).start(priority=u)
            return carry

        lax.fori_loop(0, tt // 2, issue, 0, unroll=4)

    @pl.when(i == 0)
    def _():
        issue_tile((pos0_ref, pos1_ref), 0)

    @pl.when(i + 1 < pl.num_programs(0))
    def _():
        issue_tile((next0_ref, next1_ref), 1 - cur)

    for slot in range(2):
        pltpu.make_async_copy(ys_ref.at[pl.ds(0, tt * SUBLANES)], buf.at[cur, slot], sems.at[cur]).wait()
    route = route_ref[...]
    y = route[:, 2:3] * _load_row_tiles(buf.at[cur, 0], tt) + route[:, 3:4] * _load_row_tiles(buf.at[cur, 1], tt)
    o_ref[...] = _layer_norm(DN_ALPHA * _load_row_tiles(x1_ref, tt) + y, g2_ref[...], b2_ref[...])


def _combine(x1t, route, pos, ys, g2, b2, tok0, n_tok):
    d = g2.shape[1]
    tt = COMB_TT
    t0 = tok0 // tt
    nt = n_tok // tt
    return pl.pallas_call(
        _combine_kernel,
        grid=(nt,),
        in_specs=[pl.BlockSpec((tt,), lambda i: (i + t0,), memory_space=pltpu.SMEM),
                  pl.BlockSpec((tt,), lambda i: (i + t0,), memory_space=pltpu.SMEM),
                  pl.BlockSpec((tt,), lambda i: (jnp.minimum(i + 1, nt - 1) + t0,), memory_space=pltpu.SMEM),
                  pl.BlockSpec((tt,), lambda i: (jnp.minimum(i + 1, nt - 1) + t0,), memory_space=pltpu.SMEM),
                  pl.BlockSpec((tt * SUBLANES, LANES), lambda i: (i + t0, 0)),
                  pl.BlockSpec((tt, ROUTE_LANES), lambda i: (i + t0, 0)),
                  pl.BlockSpec((1, d), lambda i: (0, 0)),
                  pl.BlockSpec((1, d), lambda i: (0, 0)),
                  pl.BlockSpec(memory_space=pl.ANY)],
        out_specs=pl.BlockSpec((tt, d), lambda i: (i, 0)),
        out_shape=jax.ShapeDtypeStruct((n_tok, d), jnp.float32),
        scratch_shapes=[pltpu.VMEM((2, 2, tt * SUBLANES, LANES), ys.dtype), pltpu.SemaphoreType.DMA((2,))],
        compiler_params=_cparams(("arbitrary",)),
        name="moe_combine_ln",
    )(pos[0], pos[1], pos[0], pos[1], x1t, route, g2, b2, ys)


def _layer(x_prompt, x_sample, ln_in_g, ln_in_b, w_in, b_in, rel_bias_t5, na_rpb, w_branch_a, w_branch_b,
           w_out, b_out, ln1_g, ln1_b, w_router_group, b_router_group, w_router_expert, b_router_expert,
           w_exp_gate, w_exp_up, w_exp_down, ln2_g, ln2_b):
    bp, tp, d = x_prompt.shape
    bs, ts, _ = x_sample.shape
    seg_tokens = (bp * tp, bs * ts)
    seq_lens = (tp, ts)
    n = sum(seg_tokens)
    xp = x_prompt.reshape(seg_tokens[0], d)
    xs = x_sample.reshape(seg_tokens[1], d)
    row = lambda a: a.reshape(1, -1).astype(jnp.float32)
    bf = lambda a: a.astype(jnp.bfloat16)

    scale = HEAD_DIM ** -0.5
    col_scale = np.ones((w_in.shape[2],), np.float32)
    col_scale[0:QA] = scale
    col_scale[3 * QA:3 * QA + QB] = scale
    w_proj = bf(w_in[0] * col_scale)
    b_proj = row(b_in[0] * col_scale)
    proj = _in_projection(xp, xs, row(ln_in_g), row(ln_in_b), w_proj, b_proj)
    qkv_a, (qb, kb, vb, ga, gb, x0) = proj[:3 * N_DIL_GROUPS], proj[3 * N_DIL_GROUPS:]

    o_groups, lse_groups = [], []
    for g, (window, dil) in enumerate(DIL_PATTERNS):
        assert window // (2 * dil) == HALF_SPAN
        table = rel_bias_t5[:, g * HEADS_PER_DIL_GROUP:(g + 1) * HEADS_PER_DIL_GROUP].astype(jnp.float32).reshape(-1)
        o, lse = _dilated_group(qkv_a[g], qkv_a[N_DIL_GROUPS + g], qkv_a[2 * N_DIL_GROUPS + g], table, dil,
                                seg_tokens, seq_lens)
        o_groups.append(o)
        lse_groups.append(lse)
    ob = _neighborhood(qb, kb, vb, na_rpb[0].astype(jnp.float32).reshape(-1), seg_tokens, seq_lens)

    wr = jnp.zeros((d, ROUTE_LANES), jnp.float32)
    wr = wr.at[:, 0:N_GROUPS].set(w_router_group[0]).at[:, EXPERT_LANE0:EXPERT_LANE0 + N_EXPERTS].set(w_router_expert[0])
    br = jnp.zeros((1, ROUTE_LANES), jnp.float32)
    br = br.at[0, 0:N_GROUPS].set(b_router_group[0]).at[0, EXPERT_LANE0:EXPERT_LANE0 + N_EXPERTS].set(b_router_expert[0])
    x1, route, route_t, counts = _post_attention(
        x0, o_groups, lse_groups, ob, ga, gb, bf(w_branch_a[0]), bf(w_branch_b[0]),
        bf(w_out[0]), row(b_out[0]), row(ln1_g[0]), row(ln1_b[0]), bf(wr), br)

    experts = route_t[0:2].astype(jnp.int32)
    ranks = route_t[4:6].astype(jnp.int32)
    cnt = counts[:, 0].astype(jnp.int32)
    n_rows = 2 * n
    *items, starts = _ffn_items(cnt, n_rows)
    one_hot = (experts[..., None] == jnp.arange(N_EXPERTS, dtype=jnp.int32)).astype(jnp.float32)
    pos = ranks + jnp.dot(one_hot, starts.astype(jnp.float32), precision=lax.Precision.HIGHEST).astype(jnp.int32)

    xsorted = _dispatch(x1, pos, n_rows)
    ys = _grouped_ffn(xsorted, items, w_exp_gate[0], w_exp_up[0], w_exp_down[0])
    y_prompt = _combine(x1, route, pos, ys, row(ln2_g[0]), row(ln2_b[0]), 0, seg_tokens[0])
    y_sample = _combine(x1, route, pos, ys, row(ln2_g[0]), row(ln2_b[0]), seg_tokens[0], seg_tokens[1])
    return y_prompt.reshape(bp, tp, d), y_sample.reshape(bs, ts, d)


def kernel(x_prompt, x_sample, ln_in_g, ln_in_b, w_in, b_in, rel_bias_t5, na_rpb, w_branch_a, w_branch_b, w_out, b_out, ln1_g, ln1_b, w_router_group, b_router_group, w_router_expert, b_router_expert, w_exp_gate, w_exp_up, w_exp_down, ln2_g, ln2_b):
    return _layer(x_prompt, x_sample, ln_in_g, ln_in_b, w_in, b_in, rel_bias_t5, na_rpb, w_branch_a, w_branch_b,
                  w_out, b_out, ln1_g, ln1_b, w_router_group, b_router_group, w_router_expert, b_router_expert,
                  w_exp_gate, w_exp_up, w_exp_down, ln2_g, ln2_b)
```

```python
import functools
import math

import numpy as np
import jax
import jax.numpy as jnp
from jax import lax
from jax.experimental import pallas as pl
from jax.experimental.pallas import tpu as pltpu

HEAD_DIM = 64
DIL_PATTERNS = ((128, 1), (512, 4), (2048, 16))
HEADS_PER_DIL_GROUP = 4
N_DIL_GROUPS = 3
H_A = N_DIL_GROUPS * HEADS_PER_DIL_GROUP
H_B = 8
QA = H_A * HEAD_DIM
QB = H_B * HEAD_DIM
HALF_SPAN = 64
GRID_W = 64
NA_KH = 8
NA_KW = 16
N_BUCKETS = 32
REL_MAX_DIST = 1024
N_GROUPS = 4
EXPERTS_PER_GROUP = 8
N_EXPERTS = N_GROUPS * EXPERTS_PER_GROUP
LN_EPS = 1e-5
NEG_INF = -1e30
DEPTH = 1
DN_ALPHA = (2.0 * DEPTH) ** 0.25

LANES = 128
HEAD_BLOCK = 4 * HEAD_DIM
VMEM_LIMIT_BYTES = 56 * 1024 * 1024

PROJ_TM = 512
DIL_M = 128
DIL_GROUP = 16
NA_ROWS = 32
NA_HALO = NA_KH // 2
POST_TM = 512
POST_SPLIT = 1
FFN_TM = 512
DISP_TT = 4096
COMB_TT = 512
ROUTE_LANES = LANES
EXPERT_LANE0 = 32


def _cparams(sem):
    return pltpu.CompilerParams(dimension_semantics=sem, vmem_limit_bytes=VMEM_LIMIT_BYTES)


SUBLANES = 8


def _store_row_tiles(ref, x, row0=0):
    rows = x.shape[0]
    for c in range(SUBLANES):
        ref[pl.ds(row0 * SUBLANES + c, rows, stride=SUBLANES), :] = x[:, c * LANES:(c + 1) * LANES]


def _load_row_tiles(ref, rows, row0=0):
    return jnp.concatenate([ref[pl.ds(row0 * SUBLANES + c, rows, stride=SUBLANES), :] for c in range(SUBLANES)],
                           axis=1)


def _layer_norm(x, g, b):
    mu = jnp.mean(x, axis=-1, keepdims=True)
    xc = x - mu
    var = jnp.mean(xc * xc, axis=-1, keepdims=True)
    return xc * lax.rsqrt(var + LN_EPS) * g + b


def _strided_rows(scr, slab0, r, count, stride):
    return jnp.concatenate([scr[slab0 + s, pl.ds(r, count, stride=stride), :] for s in range(2)], axis=1)


def _inproj_kernel(xp_ref, xs_ref, g_ref, b_ref, w_ref, bias_ref, *refs, n0_tiles):
    out_refs, scr = refs[:-1], refs[-1]
    i = pl.program_id(0)
    x = jnp.where(i < n0_tiles, xp_ref[...], xs_ref[...])
    x0 = _layer_norm(x, g_ref[...], b_ref[...])
    out_refs[-1][...] = x0
    out_refs = out_refs[:-1]
    xn = x0.astype(jnp.bfloat16)
    tm = xn.shape[0]

    def proj(c0, cw):
        return jnp.dot(xn, w_ref[:, c0:c0 + cw], preferred_element_type=jnp.float32) + bias_ref[:, c0:c0 + cw]

    n_dil = 3 * N_DIL_GROUPS
    for idx in range(n_dil):
        g = idx % N_DIL_GROUPS
        dil = DIL_PATTERNS[g][1]
        o_ref = out_refs[idx]
        y = proj(idx * HEAD_BLOCK, HEAD_BLOCK)
        if dil == 1:
            o_ref[0, 0] = y.astype(o_ref.dtype)
        else:
            slab0 = 2 * (idx % 2)
            scr[slab0] = y[:, :LANES]
            scr[slab0 + 1] = y[:, LANES:]
            for r in range(dil):
                o_ref[0, r] = _strided_rows(scr, slab0, r, tm // dil, dil).astype(o_ref.dtype)
    col = n_dil * HEAD_BLOCK
    for o_ref in out_refs[n_dil:]:
        width = o_ref.shape[1]
        for c in range(0, width, 512):
            cw = min(512, width - c)
            o_ref[:, c:c + cw] = proj(col + c, cw).astype(o_ref.dtype)
        col += width


def _in_projection(xp, xs, ln_g, ln_b, w, b):
    n0, d_model = xp.shape
    n1 = xs.shape[0]
    n = n0 + n1
    tm = PROJ_TM
    n0_tiles, n1_tiles = n0 // tm, n1 // tm
    nt = n0_tiles + n1_tiles
    out_shape, out_specs = [], []
    for _ in range(3):
        for _, dil in DIL_PATTERNS:
            out_shape.append(jax.ShapeDtypeStruct((nt, dil, tm // dil, HEAD_BLOCK), jnp.bfloat16))
            out_specs.append(pl.BlockSpec((1, dil, tm // dil, HEAD_BLOCK), lambda i: (i, 0, 0, 0)))
    for wd, dtype in ((QB, jnp.bfloat16),) * 3 + ((d_model, jnp.bfloat16),) * 2 + ((d_model, jnp.float32),):
        out_shape.append(jax.ShapeDtypeStruct((n, wd), dtype))
        out_specs.append(pl.BlockSpec((tm, wd), lambda i: (i, 0)))
    d_in = w.shape[1]
    kern = functools.partial(_inproj_kernel, n0_tiles=n0_tiles)
    return pl.pallas_call(
        kern,
        grid=(nt,),
        in_specs=[
            pl.BlockSpec((tm, d_model), lambda i: (jnp.minimum(i, n0_tiles - 1), 0)),
            pl.BlockSpec((tm, d_model), lambda i: (jnp.maximum(i - n0_tiles, 0), 0)),
            pl.BlockSpec((1, d_model), lambda i: (0, 0)),
            pl.BlockSpec((1, d_model), lambda i: (0, 0)),
            pl.BlockSpec((d_model, d_in), lambda i: (0, 0)),
            pl.BlockSpec((1, d_in), lambda i: (0, 0)),
        ],
        out_specs=out_specs,
        out_shape=out_shape,
        scratch_shapes=[pltpu.VMEM((4, tm, LANES), jnp.float32)],
        compiler_params=_cparams(("arbitrary",)),
        name="in_projection",
    )(xp, xs, ln_g, ln_b, w, b)


def _t5_bucket_np(rel):
    half = N_BUCKETS // 2
    max_exact = half // 2
    ret = np.where(rel > 0, half, 0)
    n = np.abs(rel)
    nf = np.maximum(n, 1).astype(np.float32)
    large = max_exact + (np.log(nf / np.float32(max_exact)) / np.float32(math.log(REL_MAX_DIST / max_exact))
                         * np.float32(half - max_exact)).astype(np.int32)
    large = np.minimum(large, half - 1)
    return ret + np.where(n < max_exact, n, large)


def _t5_bucket_starts():
    half = N_BUCKETS // 2
    dist = np.arange(0, HALF_SPAN * max(d for _, d in DIL_PATTERNS) + 1)
    buckets = _t5_bucket_np(-dist)
    return [int(np.argmax(buckets >= k)) for k in range(half)]


def _build_dilated_bias(table_ref, bias_scr, dil):
    half = N_BUCKETS // 2
    starts = _t5_bucket_starts()
    kb = DIL_M + 2 * HALF_SPAN
    i = lax.broadcasted_iota(jnp.int32, (DIL_M, kb), 0)
    m = lax.broadcasted_iota(jnp.int32, (DIL_M, kb), 1)
    delta = m - HALF_SPAN - i
    dist = jnp.abs(delta) * dil
    band = jnp.abs(delta) <= HALF_SPAN
    for h in range(HEADS_PER_DIL_GROUP):
        sides = []
        for side in range(2):
            val = jnp.full((DIL_M, kb), table_ref[(side * half + half - 1) * HEADS_PER_DIL_GROUP + h], jnp.float32)
            for k in range(half - 1, 0, -1):
                val = jnp.where(dist < starts[k], table_ref[(side * half + k - 1) * HEADS_PER_DIL_GROUP + h], val)
            sides.append(val)
        base = jnp.where(band, jnp.where(delta > 0, sides[1], sides[0]), NEG_INF)
        rows = slice(h * DIL_M, (h + 1) * DIL_M)
        bias_scr[0, rows, :] = base
        bias_scr[1, rows, :] = jnp.where(m >= HALF_SPAN, base, NEG_INF)
        bias_scr[2, rows, :] = jnp.where(m < kb - HALF_SPAN, base, NEG_INF)


def _stack_heads(q):
    head_of_col = lax.broadcasted_iota(jnp.int32, (1, HEAD_BLOCK), 1) // HEAD_DIM
    zero = jnp.zeros_like(q)
    return jnp.concatenate([jnp.where(head_of_col == h, q, zero) for h in range(4)], axis=0)


def _merge_heads(x, m):
    head_of_col = lax.broadcasted_iota(jnp.int32, (1, HEAD_BLOCK), 1) // HEAD_DIM
    out = jnp.zeros((m, HEAD_BLOCK), x.dtype)
    for h in range(4):
        out = jnp.where(head_of_col == h, x[h * m:(h + 1) * m], out)
    return out


def _softmax_pv(s, v, m_rows):
    mx = jnp.max(s, axis=-1, keepdims=True)
    p = jnp.exp(s - mx)
    l = jnp.sum(p, axis=-1, keepdims=True)
    pv = jnp.dot(p.astype(jnp.bfloat16), v, preferred_element_type=jnp.float32)
    o = _merge_heads(pv * (1.0 / l), m_rows)
    return o, mx + jnp.log(l)


def _dilated_kernel(table_ref, q_ref, kp_ref, kc_ref, kn_ref, vp_ref, vc_ref, vn_ref,
                    o_ref, lse_ref, bias_scr, *, dil, seg0_blocks, blocks_per_seq):
    step = pl.program_id(1)

    @pl.when((pl.program_id(0) == 0) & (step == 0))
    def _():
        _build_dilated_bias(table_ref, bias_scr, dil)

    rows2d = lambda ref: ref[...].reshape(-1, HEAD_BLOCK)
    k = jnp.concatenate([rows2d(r) for r in (kp_ref, kc_ref, kn_ref)], axis=0)
    v = jnp.concatenate([rows2d(r) for r in (vp_ref, vc_ref, vn_ref)], axis=0)
    q = rows2d(q_ref)
    o_parts, lse_parts = [], []
    for g in range(DIL_GROUP):
        jb = step * DIL_GROUP + g
        in0 = jb < seg0_blocks
        jl = jnp.where(in0, jb, jb - seg0_blocks)
        nbs = jnp.where(in0, blocks_per_seq[0], blocks_per_seq[1])
        pos = lax.rem(jl, nbs)
        variant = jnp.where(pos == 0, 1, jnp.where(pos == nbs - 1, 2, 0))
        keys = slice(g * DIL_M, (g + 1) * DIL_M + 2 * HALF_SPAN)
        qs = _stack_heads(q[g * DIL_M:(g + 1) * DIL_M])
        s = lax.dot_general(qs, k[keys], (((1,), (1,)), ((), ())), preferred_element_type=jnp.float32)
        s = s + bias_scr[variant]
        o, lse = _softmax_pv(s, v[keys], DIL_M)
        o_parts.append(o.astype(o_ref.dtype))
        lse_parts.append(_merge_heads(jnp.broadcast_to(lse, (4 * DIL_M, HEAD_BLOCK)), DIL_M))
    o_ref[...] = jnp.concatenate(o_parts, axis=0).reshape(o_ref.shape)
    lse_ref[...] = jnp.concatenate(lse_parts, axis=0).reshape(lse_ref.shape)


def _residue_spec(rows_blk, dil, block_of):
    per_tile = PROJ_TM // dil
    if per_tile >= rows_blk:
        per = per_tile // rows_blk
        return pl.BlockSpec((1, None, rows_blk, HEAD_BLOCK),
                            lambda r, jb: (block_of(jb) // per, r, block_of(jb) % per, 0))
    return pl.BlockSpec((rows_blk // per_tile, None, per_tile, HEAD_BLOCK), lambda r, jb: (block_of(jb), r, 0, 0))


def _dilated_group(q, k, v, table, dil, seg_tokens, seq_lens):
    n = q.shape[0] * PROJ_TM
    rows = n // dil
    step_rows = DIL_GROUP * DIL_M
    assert rows % step_rows == 0 and all(t % (dil * DIL_M) == 0 for t in seq_lens)
    nsteps = rows // step_rows
    nkb = rows // HALF_SPAN
    ratio = step_rows // HALF_SPAN
    seg0_blocks = seg_tokens[0] // dil // DIL_M
    blocks_per_seq = tuple(t // dil // DIL_M for t in seq_lens)
    assert min(blocks_per_seq) >= 2

    qo_spec = _residue_spec(step_rows, dil, lambda jb: jb)
    prev_spec = _residue_spec(HALF_SPAN, dil, lambda jb: jnp.maximum(ratio * jb - 1, 0))
    next_spec = _residue_spec(HALF_SPAN, dil, lambda jb: jnp.minimum(ratio * (jb + 1), nkb - 1))
    kv_specs = [prev_spec, qo_spec, next_spec]
    kern = functools.partial(_dilated_kernel, dil=dil, seg0_blocks=seg0_blocks, blocks_per_seq=blocks_per_seq)
    return pl.pallas_call(
        kern,
        grid=(dil, nsteps),
        in_specs=[pl.BlockSpec(memory_space=pltpu.SMEM), qo_spec] + kv_specs + kv_specs,
        out_specs=[qo_spec, qo_spec],
        out_shape=[jax.ShapeDtypeStruct(q.shape, jnp.bfloat16), jax.ShapeDtypeStruct(q.shape, jnp.float32)],
        scratch_shapes=[pltpu.VMEM((3, HEADS_PER_DIL_GROUP * DIL_M, DIL_M + 2 * HALF_SPAN), jnp.float32)],
        compiler_params=_cparams(("arbitrary", "arbitrary")),
        name=f"dilated_attention_d{dil}",
    )(table, q, k, k, k, v, v, v)


NA_DR = 2 * NA_KH - 1
NA_DC = 2 * NA_KW - 1
NA_PAIRS = NA_DR - 1


def _build_na_bias(rpb_ref, pair_scr, head0):
    qc = lax.broadcasted_iota(jnp.int32, (GRID_W, 2 * GRID_W), 0)
    lane = lax.broadcasted_iota(jnp.int32, (GRID_W, 2 * GRID_W), 1)
    kc = lane % GRID_W
    upper = lane >= GRID_W
    qstart = jnp.clip(qc - NA_KW // 2, 0, GRID_W - NA_KW)
    valid = (kc >= qstart) & (kc < qstart + NA_KW)
    dc = jnp.clip(kc - qc, -(NA_KW - 1), NA_KW - 1) + NA_KW - 1

    def one_pair(idx, carry):
        h = idx // NA_PAIRS
        dr = idx % NA_PAIRS
        base = ((head0 + h) * NA_DR + dr) * NA_DC
        val = jnp.zeros((GRID_W, 2 * GRID_W), jnp.float32)
        for c in range(NA_DC):
            val = jnp.where(dc == c, jnp.where(upper, rpb_ref[base + NA_DC + c], rpb_ref[base + c]), val)
        pair_scr[h, dr] = jnp.where(valid, val, NEG_INF)
        return carry

    lax.fori_loop(0, 4 * NA_PAIRS, one_pair, 0)


def _na_kernel(rpb_ref, q_ref, kp_ref, kc_ref, kn_ref, vp_ref, vc_ref, vn_ref, o_ref, k_scr, v_scr, pair_scr,
               *, seg0_blocks, blocks_per_seq):
    ib = pl.program_id(1)

    @pl.when(ib == 0)
    def _():
        _build_na_bias(rpb_ref, pair_scr, pl.program_id(0) * 4)

    in0 = ib < seg0_blocks
    il = jnp.where(in0, ib, ib - seg0_blocks)
    nbs = jnp.where(in0, blocks_per_seq[0], blocks_per_seq[1])
    pos = lax.rem(il, nbs)
    first = pos == 0
    last = pos == nbs - 1
    blk = NA_ROWS * GRID_W
    halo = NA_HALO * GRID_W
    for scr, (p_ref, c_ref, n_ref) in ((k_scr, (kp_ref, kc_ref, kn_ref)), (v_scr, (vp_ref, vc_ref, vn_ref))):
        scr[0:halo] = p_ref[...]
        scr[halo:halo + blk] = c_ref[...]
        scr[halo + blk:2 * halo + blk] = n_ref[...]
    half = NA_KH // 2
    for rr in range(NA_ROWS):
        inner = NA_HALO + rr - half
        s_first, s_last = max(inner, NA_HALO), min(inner, NA_HALO + NA_ROWS - NA_KH)
        start = jnp.where(first, s_first, jnp.where(last, s_last, inner))
        var = jnp.where(first, NA_HALO + rr - s_first, jnp.where(last, NA_HALO + rr - s_last, half))
        off = pl.multiple_of(start * GRID_W, GRID_W)
        kk = k_scr[pl.ds(off, NA_KH * GRID_W), :]
        vv = v_scr[pl.ds(off, NA_KH * GRID_W), :]
        qs = _stack_heads(q_ref[rr * GRID_W:(rr + 1) * GRID_W, :])
        s = lax.dot_general(qs, kk, (((1,), (1,)), ((), ())), preferred_element_type=jnp.float32)
        bias = jnp.concatenate(
            [jnp.concatenate([pair_scr[h, 2 * p - var + NA_KH - 1] for p in range(NA_KH // 2)], axis=1)
             for h in range(4)], axis=0)
        s = s + bias
        o, _ = _softmax_pv(s, vv, GRID_W)
        o_ref[rr * GRID_W:(rr + 1) * GRID_W, :] = o.astype(o_ref.dtype)


def _neighborhood(qb, kb, vb, rpb_flat, seg_tokens, seq_lens):
    n = qb.shape[0]
    blk = NA_ROWS * GRID_W
    nblk = n // blk
    seg0_blocks = seg_tokens[0] // blk
    blocks_per_seq = tuple(t // blk for t in seq_lens)
    assert min(blocks_per_seq) >= 2
    nset = QB // HEAD_BLOCK

    per = NA_ROWS // NA_HALO
    halo = NA_HALO * GRID_W
    cur_spec = pl.BlockSpec((blk, HEAD_BLOCK), lambda c, ib: (ib, c))
    prev_spec = pl.BlockSpec((halo, HEAD_BLOCK), lambda c, ib: (jnp.maximum(per * ib - 1, 0), c))
    next_spec = pl.BlockSpec((halo, HEAD_BLOCK), lambda c, ib: (jnp.minimum(per * (ib + 1), per * nblk - 1), c))
    kv_specs = [prev_spec, cur_spec, next_spec]

    kern = functools.partial(_na_kernel, seg0_blocks=seg0_blocks, blocks_per_seq=blocks_per_seq)
    return pl.pallas_call(
        kern,
        grid=(nset, nblk),
        in_specs=[pl.BlockSpec(memory_space=pltpu.SMEM), cur_spec] + kv_specs + kv_specs,
        out_specs=cur_spec,
        out_shape=jax.ShapeDtypeStruct((n, QB), jnp.bfloat16),
        scratch_shapes=[pltpu.VMEM((blk + 2 * halo, HEAD_BLOCK), jnp.bfloat16),
                        pltpu.VMEM((blk + 2 * halo, HEAD_BLOCK), jnp.bfloat16),
                        pltpu.VMEM((4, NA_PAIRS, GRID_W, 2 * GRID_W), jnp.float32)],
        compiler_params=_cparams(("arbitrary", "arbitrary")),
        name="neighborhood_attention",
    )(rpb_flat, qb, kb, kb, kb, vb, vb, vb)


def _post_kernel(x0_ref, o0_ref, o1_ref, o2_ref, l0_ref, l1_ref, l2_ref, ob_ref,
                 ga_ref, gb_ref, wa_ref, wb_ref, wo_ref, bo_ref, g1_ref, b1_ref, wr_ref, br_ref, tri_ref,
                 x1_ref, route_ref, route_t_ref, counts_ref, carry_ref, perm_scr):
    i = pl.program_id(0)
    tm = route_ref.shape[0]
    hm = tm // POST_SPLIT

    @pl.when(i == 0)
    def _():
        carry_ref[...] = jnp.zeros_like(carry_ref)

    def token_order(ref, slab0):
        dil, per = ref.shape[1], ref.shape[2]
        if dil == 1:
            return lambda rs: ref[0, 0, rs, :].astype(jnp.float32)
        for r in range(dil):
            val = ref[0, r].astype(jnp.float32)
            for s in range(2):
                perm_scr[slab0 + s, pl.ds(r, per, stride=dil), :] = val[:, s * LANES:(s + 1) * LANES]
        return lambda rs: jnp.concatenate([perm_scr[slab0, rs, :], perm_scr[slab0 + 1, rs, :]], axis=1)

    lse_of = [token_order(ref, 2 * j) for j, ref in enumerate((l0_ref, l1_ref, l2_ref))]
    o_of = [token_order(ref, 6 + 2 * j) for j, ref in enumerate((o0_ref, o1_ref, o2_ref))]
    sigmoid = lambda v: 0.5 * jnp.tanh(0.5 * v) + 0.5
    grow = lax.broadcasted_iota(jnp.int32, (8, hm), 0)
    erow = lax.broadcasted_iota(jnp.int32, (N_EXPERTS, hm), 0)
    frow = lax.broadcasted_iota(jnp.int32, (ROUTE_LANES, hm), 0)
    neg = jnp.float32(-jnp.inf)
    carry = carry_ref[:, 0:1]

    for h in range(POST_SPLIT):
        rs = slice(h * hm, (h + 1) * hm)
        l0, l1, l2 = (f(rs) for f in lse_of)
        lm = jnp.maximum(jnp.maximum(l0, l1), l2)
        e0, e1, e2 = jnp.exp(l0 - lm), jnp.exp(l1 - lm), jnp.exp(l2 - lm)
        inv = 1.0 / (e0 + e1 + e2)
        o_a = (e0 * inv) * o_of[0](rs) + (e1 * inv) * o_of[1](rs) + (e2 * inv) * o_of[2](rs)
        y_a = jnp.dot(o_a.astype(jnp.bfloat16), wa_ref[...], preferred_element_type=jnp.float32)
        y_b = jnp.dot(ob_ref[rs, :], wb_ref[...], preferred_element_type=jnp.float32)
        mix = sigmoid(ga_ref[rs, :].astype(jnp.float32)) * y_a + sigmoid(gb_ref[rs, :].astype(jnp.float32)) * y_b
        out = jnp.dot(mix.astype(jnp.bfloat16), wo_ref[...], preferred_element_type=jnp.float32) + bo_ref[...]
        x1 = _layer_norm(DN_ALPHA * x0_ref[rs, :] + out, g1_ref[...], b1_ref[...])
        _store_row_tiles(x1_ref, x1, row0=h * hm)

        r = jnp.dot(x1.astype(jnp.bfloat16), wr_ref[...], preferred_element_type=jnp.float32) + br_ref[...]
        rt = r.T
        lg = jnp.where(grow < N_GROUPS, rt[0:8], neg)
        gmax = jnp.max(lg, axis=0, keepdims=True)
        gi = jnp.min(jnp.where(lg == gmax, grow, 8), axis=0, keepdims=True)
        wg = 1.0 / jnp.sum(jnp.exp(lg - gmax), axis=0, keepdims=True)
        lo = EXPERTS_PER_GROUP * gi
        le = jnp.where((erow >= lo) & (erow < lo + EXPERTS_PER_GROUP), rt[EXPERT_LANE0:EXPERT_LANE0 + N_EXPERTS], neg)
        m1 = jnp.max(le, axis=0, keepdims=True)
        i1 = jnp.min(jnp.where(le == m1, erow, N_EXPERTS), axis=0, keepdims=True)
        le2 = jnp.where(erow == i1, neg, le)
        m2 = jnp.max(le2, axis=0, keepdims=True)
        i2 = jnp.min(jnp.where(le2 == m2, erow, N_EXPERTS), axis=0, keepdims=True)
        t2 = jnp.exp(m2 - m1)
        w1 = wg / (1.0 + t2)
        w2 = wg * t2 / (1.0 + t2)

        hot1 = erow == i1
        hot2 = erow == i2
        hot = (hot1 | hot2).astype(jnp.float32)
        before = jnp.dot(hot.astype(jnp.bfloat16), tri_ref[...], preferred_element_type=jnp.float32) + carry
        rank1 = jnp.sum(jnp.where(hot1, before, 0.0), axis=0, keepdims=True)
        rank2 = jnp.sum(jnp.where(hot2, before, 0.0), axis=0, keepdims=True)
        carry = carry + jnp.sum(hot, axis=1, keepdims=True)

        fields = (i1.astype(jnp.float32), i2.astype(jnp.float32), w1, w2, rank1, rank2)
        route_t = jnp.zeros((ROUTE_LANES, hm), jnp.float32)
        for idx, val in enumerate(fields):
            route_t = jnp.where(frow == idx, val, route_t)
        route_t_ref[:, rs] = route_t[0:route_t_ref.shape[0]]
        route_ref[rs, :] = route_t.T

    carry_ref[...] = jnp.broadcast_to(carry, carry_ref.shape)
    counts_ref[...] = jnp.broadcast_to(carry, counts_ref.shape)


def _post_attention(x0, o_groups, lse_groups, ob, ga, gb, wa, wb, wo, bo, g1, b1, wr, br):
    n, d = x0.shape
    tm = POST_TM
    nt = n // tm
    hm = tm // POST_SPLIT
    tri = jnp.asarray(np.triu(np.ones((hm, hm), np.float32), 1), jnp.bfloat16)

    def tok(width):
        return pl.BlockSpec((tm, width), lambda i: (i, 0))

    def full(a):
        return pl.BlockSpec(a.shape, lambda i: (0,) * a.ndim)

    return pl.pallas_call(
        _post_kernel,
        grid=(nt,),
        in_specs=[
            tok(d),
            *[pl.BlockSpec((1,) + a.shape[1:], lambda i: (i, 0, 0, 0)) for a in (*o_groups, *lse_groups)],
            tok(QB), tok(d), tok(d),
            full(wa), full(wb), full(wo), full(bo), full(g1), full(b1), full(wr), full(br), full(tri),
        ],
        out_specs=[pl.BlockSpec((tm * SUBLANES, LANES), lambda i: (i, 0)), tok(ROUTE_LANES),
                   pl.BlockSpec((8, tm), lambda i: (0, i)), pl.BlockSpec((N_EXPERTS, ROUTE_LANES), lambda i: (0, 0))],
        out_shape=[jax.ShapeDtypeStruct((n * SUBLANES, LANES), jnp.float32),
                   jax.ShapeDtypeStruct((n, ROUTE_LANES), jnp.float32),
                   jax.ShapeDtypeStruct((8, n), jnp.float32),
                   jax.ShapeDtypeStruct((N_EXPERTS, ROUTE_LANES), jnp.float32)],
        scratch_shapes=[pltpu.VMEM((N_EXPERTS, ROUTE_LANES), jnp.float32), pltpu.VMEM((12, tm, LANES), jnp.float32)],
        compiler_params=_cparams(("arbitrary",)),
        name="post_attention_router",
    )(x0, *o_groups, *lse_groups, ob, ga, gb, wa, wb, wo, bo, g1, b1, wr, br, tri)


def _dispatch_kernel(pos0_ref, pos1_ref, x_ref, x_hbm, xs_ref, sem):
    tt = x_ref.shape[0] // SUBLANES
    base = pl.program_id(0) * tt

    def issue(j, carry):
        for u in range(2):
            t = 2 * j + u
            srcs = (x_ref.at[pl.ds(pl.multiple_of(t * SUBLANES, SUBLANES), SUBLANES)],
                    x_hbm.at[pl.ds(pl.multiple_of((base + t) * SUBLANES, SUBLANES), SUBLANES)])
            for src, p_ref in zip(srcs, (pos0_ref, pos1_ref)):
                p = p_ref[t]
                dst = xs_ref.at[pl.ds(pl.multiple_of(p * SUBLANES, SUBLANES), SUBLANES)]
                pltpu.make_async_copy(src, dst, sem).start(priority=u)
        return carry

    lax.fori_loop(0, tt // 2, issue, 0, unroll=4)
    for _ in range(2):
        pltpu.make_async_copy(x_ref, xs_ref.at[pl.ds(0, tt * SUBLANES)], sem).wait()


def _dispatch(x1t, pos, n_rows):
    n = x1t.shape[0] // SUBLANES
    tt = DISP_TT
    return pl.pallas_call(
        _dispatch_kernel,
        grid=(n // tt,),
        in_specs=[pl.BlockSpec((tt,), lambda i: (i,), memory_space=pltpu.SMEM),
                  pl.BlockSpec((tt,), lambda i: (i,), memory_space=pltpu.SMEM),
                  pl.BlockSpec((tt * SUBLANES, LANES), lambda i: (i, 0)),
                  pl.BlockSpec(memory_space=pl.ANY)],
        out_specs=pl.BlockSpec(memory_space=pl.ANY),
        out_shape=jax.ShapeDtypeStruct((n_rows * SUBLANES, LANES), x1t.dtype),
        scratch_shapes=[pltpu.SemaphoreType.DMA(())],
        compiler_params=_cparams(("arbitrary",)),
        name="moe_dispatch",
    )(pos[0], pos[1], x1t, x1t)


def _ffn_kernel(item_expert_ref, item_tile_ref, item_lo_ref, item_hi_ref, n_items_ref, item_next_ref, item_slot_ref,
                x_ref, wg_hbm, wu_hbm, wd_hbm, y_ref, wg_bf, wu_bf, wd_bf, y_acc, wg_buf, wu_buf, wd_buf, sems):
    j = pl.program_id(0)

    @pl.when(j == 0)
    def _():
        y_acc[...] = jnp.zeros_like(y_acc)

    def weight_copies(expert, slot):
        return [pltpu.make_async_copy(src.at[expert], dst.at[slot], sems.at[slot])
                for src, dst in ((wg_hbm, wg_buf), (wu_hbm, wu_buf), (wd_hbm, wd_buf))]

    @pl.when(j < n_items_ref[0])
    def _():
        prev = jnp.maximum(j - 1, 0)
        expert = item_expert_ref[j]
        slot = item_slot_ref[j]

        @pl.when(jnp.logical_or(j == 0, expert != item_expert_ref[prev]))
        def _():
            @pl.when(j == 0)
            def _():
                for c in weight_copies(expert, slot):
                    c.start()

            for c in weight_copies(expert, slot):
                c.wait()
            wg_bf[...] = wg_buf[slot].astype(jnp.bfloat16)
            wu_bf[...] = wu_buf[slot].astype(jnp.bfloat16)
            wd_bf[...] = wd_buf[slot].astype(jnp.bfloat16)
            nxt = item_next_ref[j]

            @pl.when(nxt >= 0)
            def _():
                for c in weight_copies(nxt, 1 - slot):
                    c.start()

        tm = x_ref.shape[0] // SUBLANES
        x = _load_row_tiles(x_ref, tm).astype(jnp.bfloat16)
        a = jnp.dot(x, wg_bf[...], preferred_element_type=jnp.float32)
        u = jnp.dot(x, wu_bf[...], preferred_element_type=jnp.float32)
        h = (a * jax.nn.sigmoid(a) * u).astype(jnp.bfloat16)
        y = jnp.dot(h, wd_bf[...], preferred_element_type=jnp.float32)
        rows = lax.broadcasted_iota(jnp.int32, y.shape, 0)
        pltpu.store(y_acc, y, mask=(rows >= item_lo_ref[j]) & (rows < item_hi_ref[j]))
        _store_row_tiles(y_ref, y_acc[...])


def _ffn_items(cnt, n_rows):
    tm = FFN_TM
    n_tiles = n_rows // tm
    max_items = n_tiles + N_EXPERTS - 1
    ends = jnp.cumsum(cnt)
    starts = ends - cnt
    first_tile = starts // tm
    last_tile = jnp.maximum(ends - 1, 0) // tm
    items_per_expert = jnp.where(cnt > 0, last_tile - first_tile + 1, 0)
    item_ends = jnp.cumsum(items_per_expert)
    n_items = item_ends[-1:]
    j = jnp.minimum(jnp.arange(max_items, dtype=jnp.int32), n_items[0] - 1)
    expert = jnp.minimum(jnp.sum(item_ends[None, :] <= j[:, None], axis=1), N_EXPERTS - 1).astype(jnp.int32)
    tile = first_tile[expert] + (j - (item_ends - items_per_expert)[expert])
    lo = jnp.clip(starts[expert] - tile * tm, 0, tm)
    hi = jnp.clip(ends[expert] - tile * tm, 0, tm)
    active = cnt > 0
    ids = jnp.arange(N_EXPERTS, dtype=jnp.int32)
    later = (ids[None, :] > ids[:, None]) & active[None, :]
    next_active = jnp.min(jnp.where(later, ids[None, :], N_EXPERTS), axis=1)
    next_active = jnp.where(next_active < N_EXPERTS, next_active, -1)
    slot = (jnp.cumsum(active.astype(jnp.int32)) - 1) % 2
    i32 = lambda a: a.astype(jnp.int32)
    return (i32(expert), i32(tile), i32(lo), i32(hi), i32(n_items), i32(next_active[expert]), i32(slot[expert]),
            starts)


def _grouped_ffn(xs, items, w_gate, w_up, w_down):
    tm = FFN_TM
    d, de = w_gate.shape[1:]
    max_items = items[0].shape[0]

    def row_map(j, ie, it, *_):
        return (it[j], 0)

    grid_spec = pltpu.PrefetchScalarGridSpec(
        num_scalar_prefetch=len(items),
        grid=(max_items,),
        in_specs=[pl.BlockSpec((tm * SUBLANES, LANES), row_map)] + [pl.BlockSpec(memory_space=pl.ANY)] * 3,
        out_specs=pl.BlockSpec((tm * SUBLANES, LANES), row_map),
        scratch_shapes=[pltpu.VMEM((d, de), jnp.bfloat16), pltpu.VMEM((d, de), jnp.bfloat16),
                        pltpu.VMEM((de, d), jnp.bfloat16), pltpu.VMEM((tm, d), jnp.float32),
                        pltpu.VMEM((2, d, de), jnp.float32), pltpu.VMEM((2, d, de), jnp.float32),
                        pltpu.VMEM((2, de, d), jnp.float32), pltpu.SemaphoreType.DMA((2,))],
    )
    return pl.pallas_call(
        _ffn_kernel,
        grid_spec=grid_spec,
        out_shape=jax.ShapeDtypeStruct(xs.shape, jnp.float32),
        compiler_params=_cparams(("arbitrary",)),
        name="moe_grouped_ffn",
    )(*items, xs, w_gate, w_up, w_down)


def _combine_kernel(pos0_ref, pos1_ref, next0_ref, next1_ref, x1_ref, route_ref, g2_ref, b2_ref, ys_ref, o_ref,
                    buf, sems):
    tt = x1_ref.shape[0] // SUBLANES
    i = pl.program_id(0)
    cur = lax.rem(i, 2)

    def issue_tile(p_refs, b):
        def issue(j, carry):
            for u in range(2):
                t = 2 * j + u
                for slot, p_ref in enumerate(p_refs):
                    p = p_ref[t]
                    src = ys_ref.at[pl.ds(pl.multiple_of(p * SUBLANES, SUBLANES), SUBLANES)]
                    dst = buf.at[b, slot, pl.ds(pl.multiple_of(t * SUBLANES, SUBLANES), SUBLANES)]
                    pltpu.make_async_copy(src, dst, sems.at[b]).start(priority=u)
            return carry

        lax.fori_loop(0, tt // 2, issue, 0, unroll=4)

    @pl.when(i == 0)
    def _():
        issue_tile((pos0_ref, pos1_ref), 0)

    @pl.when(i + 1 < pl.num_programs(0))
    def _():
        issue_tile((next0_ref, next1_ref), 1 - cur)

    for slot in range(2):
        pltpu.make_async_copy(ys_ref.at[pl.ds(0, tt * SUBLANES)], buf.at[cur, slot], sems.at[cur]).wait()
    route = route_ref[...]
    y = route[:, 2:3] * _load_row_tiles(buf.at[cur, 0], tt) + route[:, 3:4] * _load_row_tiles(buf.at[cur, 1], tt)
    o_ref[...] = _layer_norm(DN_ALPHA * _load_row_tiles(x1_ref, tt) + y, g2_ref[...], b2_ref[...])


def _combine(x1t, route, pos, ys, g2, b2, tok0, n_tok):
    d = g2.shape[1]
    tt = COMB_TT
    t0 = tok0 // tt
    nt = n_tok // tt
    return pl.pallas_call(
        _combine_kernel,
        grid=(nt,),
        in_specs=[pl.BlockSpec((tt,), lambda i: (i + t0,), memory_space=pltpu.SMEM),
                  pl.BlockSpec((tt,), lambda i: (i + t0,), memory_space=pltpu.SMEM),
                  pl.BlockSpec((tt,), lambda i: (jnp.minimum(i + 1, nt - 1) + t0,), memory_space=pltpu.SMEM),
                  pl.BlockSpec((tt,), lambda i: (jnp.minimum(i + 1, nt - 1) + t0,), memory_space=pltpu.SMEM),
                  pl.BlockSpec((tt * SUBLANES, LANES), lambda i: (i + t0, 0)),
                  pl.BlockSpec((tt, ROUTE_LANES), lambda i: (i + t0, 0)),
                  pl.BlockSpec((1, d), lambda i: (0, 0)),
                  pl.BlockSpec((1, d), lambda i: (0, 0)),
                  pl.BlockSpec(memory_space=pl.ANY)],
        out_specs=pl.BlockSpec((tt, d), lambda i: (i, 0)),
        out_shape=jax.ShapeDtypeStruct((n_tok, d), jnp.float32),
        scratch_shapes=[pltpu.VMEM((2, 2, tt * SUBLANES, LANES), ys.dtype), pltpu.SemaphoreType.DMA((2,))],
        compiler_params=_cparams(("arbitrary",)),
        name="moe_combine_ln",
    )(pos[0], pos[1], pos[0], pos[1], x1t, route, g2, b2, ys)


def _layer(x_prompt, x_sample, ln_in_g, ln_in_b, w_in, b_in, rel_bias_t5, na_rpb, w_branch_a, w_branch_b,
           w_out, b_out, ln1_g, ln1_b, w_router_group, b_router_group, w_router_expert, b_router_expert,
           w_exp_gate, w_exp_up, w_exp_down, ln2_g, ln2_b):
    bp, tp, d = x_prompt.shape
    bs, ts, _ = x_sample.shape
    seg_tokens = (bp * tp, bs * ts)
    seq_lens = (tp, ts)
    n = sum(seg_tokens)
    xp = x_prompt.reshape(seg_tokens[0], d)
    xs = x_sample.reshape(seg_tokens[1], d)
    row = lambda a: a.reshape(1, -1).astype(jnp.float32)
    bf = lambda a: a.astype(jnp.bfloat16)

    scale = HEAD_DIM ** -0.5
    col_scale = np.ones((w_in.shape[2],), np.float32)
    col_scale[0:QA] = scale
    col_scale[3 * QA:3 * QA + QB] = scale
    w_proj = bf(w_in[0] * col_scale)
    b_proj = row(b_in[0] * col_scale)
    proj = _in_projection(xp, xs, row(ln_in_g), row(ln_in_b), w_proj, b_proj)
    qkv_a, (qb, kb, vb, ga, gb, x0) = proj[:3 * N_DIL_GROUPS], proj[3 * N_DIL_GROUPS:]

    o_groups, lse_groups = [], []
    for g, (window, dil) in enumerate(DIL_PATTERNS):
        assert window // (2 * dil) == HALF_SPAN
        table = rel_bias_t5[:, g * HEADS_PER_DIL_GROUP:(g + 1) * HEADS_PER_DIL_GROUP].astype(jnp.float32).reshape(-1)
        o, lse = _dilated_group(qkv_a[g], qkv_a[N_DIL_GROUPS + g], qkv_a[2 * N_DIL_GROUPS + g], table, dil,
                                seg_tokens, seq_lens)
        o_groups.append(o)
        lse_groups.append(lse)
    ob = _neighborhood(qb, kb, vb, na_rpb[0].astype(jnp.float32).reshape(-1), seg_tokens, seq_lens)

    wr = jnp.zeros((d, ROUTE_LANES), jnp.float32)
    wr = wr.at[:, 0:N_GROUPS].set(w_router_group[0]).at[:, EXPERT_LANE0:EXPERT_LANE0 + N_EXPERTS].set(w_router_expert[0])
    br = jnp.zeros((1, ROUTE_LANES), jnp.float32)
    br = br.at[0, 0:N_GROUPS].set(b_router_group[0]).at[0, EXPERT_LANE0:EXPERT_LANE0 + N_EXPERTS].set(b_router_expert[0])
    x1, route, route_t, counts = _post_attention(
        x0, o_groups, lse_groups, ob, ga, gb, bf(w_branch_a[0]), bf(w_branch_b[0]),
        bf(w_out[0]), row(b_out[0]), row(ln1_g[0]), row(ln1_b[0]), bf(wr), br)

    experts = route_t[0:2].astype(jnp.int32)
    ranks = route_t[4:6].astype(jnp.int32)
    cnt = counts[:, 0].astype(jnp.int32)
    n_rows = 2 * n
    *items, starts = _ffn_items(cnt, n_rows)
    one_hot = (experts[..., None] == jnp.arange(N_EXPERTS, dtype=jnp.int32)).astype(jnp.float32)
    pos = ranks + jnp.dot(one_hot, starts.astype(jnp.float32), precision=lax.Precision.HIGHEST).astype(jnp.int32)

    xsorted = _dispatch(x1, pos, n_rows)
    ys = _grouped_ffn(xsorted, items, w_exp_gate[0], w_exp_up[0], w_exp_down[0])
    y_prompt = _combine(x1, route, pos, ys, row(ln2_g[0]), row(ln2_b[0]), 0, seg_tokens[0])
    y_sample = _combine(x1, route, pos, ys, row(ln2_g[0]), row(ln2_b[0]), seg_tokens[0], seg_tokens[1])
    return y_prompt.reshape(bp, tp, d), y_sample.reshape(bs, ts, d)


def kernel(x_prompt, x_sample, ln_in_g, ln_in_b, w_in, b_in, rel_bias_t5, na_rpb, w_branch_a, w_branch_b, w_out, b_out, ln1_g, ln1_b, w_router_group, b_router_group, w_router_expert, b_router_expert, w_exp_gate, w_exp_up, w_exp_down, ln2_g, ln2_b):
    return _layer(x_prompt, x_sample, ln_in_g, ln_in_b, w_in, b_in, rel_bias_t5, na_rpb, w_branch_a, w_branch_b,
                  w_out, b_out, ln1_g, ln1_b, w_router_group, b_router_group, w_router_expert, b_router_expert,
                  w_exp_gate, w_exp_up, w_exp_down, ln2_g, ln2_b)
```

```python
import functools
import math

import numpy as np
import jax
import jax.numpy as jnp
from jax import lax
from jax.experimental import pallas as pl
from jax.experimental.pallas import tpu as pltpu

HEAD_DIM = 64
DIL_PATTERNS = ((128, 1), (512, 4), (2048, 16))
HEADS_PER_DIL_GROUP = 4
N_DIL_GROUPS = 3
H_A = N_DIL_GROUPS * HEADS_PER_DIL_GROUP
H_B = 8
QA = H_A * HEAD_DIM
QB = H_B * HEAD_DIM
HALF_SPAN = 64
GRID_W = 64
NA_KH = 8
NA_KW = 16
N_BUCKETS = 32
REL_MAX_DIST = 1024
N_GROUPS = 4
EXPERTS_PER_GROUP = 8
N_EXPERTS = N_GROUPS * EXPERTS_PER_GROUP
LN_EPS = 1e-5
NEG_INF = -1e30
DEPTH = 1
DN_ALPHA = (2.0 * DEPTH) ** 0.25

LANES = 128
HEAD_BLOCK = 4 * HEAD_DIM
VMEM_LIMIT_BYTES = 56 * 1024 * 1024

PROJ_TM = 512
DIL_M = 128
DIL_GROUP = 16
NA_ROWS = 32
NA_HALO = NA_KH // 2
POST_TM = 512
POST_SPLIT = 1
ROUTE_TT = 2048
ROUTE_SUB = 512
FFN_TM = 512
DISP_TT = 4096
COMB_TT = 512
ROUTE_LANES = LANES
EXPERT_LANE0 = 32


def _cparams(sem):
    return pltpu.CompilerParams(dimension_semantics=sem, vmem_limit_bytes=VMEM_LIMIT_BYTES)


SUBLANES = 8


def _store_row_tiles(ref, x, row0=0):
    rows = x.shape[0]
    for c in range(SUBLANES):
        ref[pl.ds(row0 * SUBLANES + c, rows, stride=SUBLANES), :] = x[:, c * LANES:(c + 1) * LANES]


def _load_row_tiles(ref, rows, row0=0):
    return jnp.concatenate([ref[pl.ds(row0 * SUBLANES + c, rows, stride=SUBLANES), :] for c in range(SUBLANES)],
                           axis=1)


def _layer_norm(x, g, b):
    mu = jnp.mean(x, axis=-1, keepdims=True)
    xc = x - mu
    var = jnp.mean(xc * xc, axis=-1, keepdims=True)
    return xc * lax.rsqrt(var + LN_EPS) * g + b


def _strided_rows(scr, slab0, r, count, stride):
    return jnp.concatenate([scr[slab0 + s, pl.ds(r, count, stride=stride), :] for s in range(2)], axis=1)


def _inproj_kernel(xp_ref, xs_ref, g_ref, b_ref, w_ref, bias_ref, *refs, n0_tiles):
    out_refs, scr = refs[:-1], refs[-1]
    i = pl.program_id(0)
    x = jnp.where(i < n0_tiles, xp_ref[...], xs_ref[...])
    x0 = _layer_norm(x, g_ref[...], b_ref[...])
    out_refs[-1][...] = x0
    out_refs = out_refs[:-1]
    xn = x0.astype(jnp.bfloat16)
    tm = xn.shape[0]

    def proj(c0, cw):
        return jnp.dot(xn, w_ref[:, c0:c0 + cw], preferred_element_type=jnp.float32) + bias_ref[:, c0:c0 + cw]

    n_dil = 3 * N_DIL_GROUPS
    for idx in range(n_dil):
        g = idx % N_DIL_GROUPS
        dil = DIL_PATTERNS[g][1]
        o_ref = out_refs[idx]
        y = proj(idx * HEAD_BLOCK, HEAD_BLOCK)
        if dil == 1:
            o_ref[0, 0] = y.astype(o_ref.dtype)
        else:
            slab0 = 2 * (idx % 2)
            scr[slab0] = y[:, :LANES]
            scr[slab0 + 1] = y[:, LANES:]
            for r in range(dil):
                o_ref[0, r] = _strided_rows(scr, slab0, r, tm // dil, dil).astype(o_ref.dtype)
    col = n_dil * HEAD_BLOCK
    for o_ref in out_refs[n_dil:]:
        width = o_ref.shape[1]
        for c in range(0, width, 512):
            cw = min(512, width - c)
            o_ref[:, c:c + cw] = proj(col + c, cw).astype(o_ref.dtype)
        col += width


def _in_projection(xp, xs, ln_g, ln_b, w, b):
    n0, d_model = xp.shape
    n1 = xs.shape[0]
    n = n0 + n1
    tm = PROJ_TM
    n0_tiles, n1_tiles = n0 // tm, n1 // tm
    nt = n0_tiles + n1_tiles
    out_shape, out_specs = [], []
    for _ in range(3):
        for _, dil in DIL_PATTERNS:
            out_shape.append(jax.ShapeDtypeStruct((nt, dil, tm // dil, HEAD_BLOCK), jnp.bfloat16))
            out_specs.append(pl.BlockSpec((1, dil, tm // dil, HEAD_BLOCK), lambda i: (i, 0, 0, 0)))
    for wd, dtype in ((QB, jnp.bfloat16),) * 3 + ((d_model, jnp.bfloat16),) * 2 + ((d_model, jnp.float32),):
        out_shape.append(jax.ShapeDtypeStruct((n, wd), dtype))
        out_specs.append(pl.BlockSpec((tm, wd), lambda i: (i, 0)))
    d_in = w.shape[1]
    kern = functools.partial(_inproj_kernel, n0_tiles=n0_tiles)
    return pl.pallas_call(
        kern,
        grid=(nt,),
        in_specs=[
            pl.BlockSpec((tm, d_model), lambda i: (jnp.minimum(i, n0_tiles - 1), 0)),
            pl.BlockSpec((tm, d_model), lambda i: (jnp.maximum(i - n0_tiles, 0), 0)),
            pl.BlockSpec((1, d_model), lambda i: (0, 0)),
            pl.BlockSpec((1, d_model), lambda i: (0, 0)),
            pl.BlockSpec((d_model, d_in), lambda i: (0, 0)),
            pl.BlockSpec((1, d_in), lambda i: (0, 0)),
        ],
        out_specs=out_specs,
        out_shape=out_shape,
        scratch_shapes=[pltpu.VMEM((4, tm, LANES), jnp.float32)],
        compiler_params=_cparams(("arbitrary",)),
        name="in_projection",
    )(xp, xs, ln_g, ln_b, w, b)


def _t5_bucket_np(rel):
    half = N_BUCKETS // 2
    max_exact = half // 2
    ret = np.where(rel > 0, half, 0)
    n = np.abs(rel)
    nf = np.maximum(n, 1).astype(np.float32)
    large = max_exact + (np.log(nf / np.float32(max_exact)) / np.float32(math.log(REL_MAX_DIST / max_exact))
                         * np.float32(half - max_exact)).astype(np.int32)
    large = np.minimum(large, half - 1)
    return ret + np.where(n < max_exact, n, large)


def _t5_bucket_starts():
    half = N_BUCKETS // 2
    dist = np.arange(0, HALF_SPAN * max(d for _, d in DIL_PATTERNS) + 1)
    buckets = _t5_bucket_np(-dist)
    return [int(np.argmax(buckets >= k)) for k in range(half)]


def _build_dilated_bias(table_ref, bias_scr, dil):
    half = N_BUCKETS // 2
    starts = _t5_bucket_starts()
    kb = DIL_M + 2 * HALF_SPAN
    i = lax.broadcasted_iota(jnp.int32, (DIL_M, kb), 0)
    m = lax.broadcasted_iota(jnp.int32, (DIL_M, kb), 1)
    delta = m - HALF_SPAN - i
    dist = jnp.abs(delta) * dil
    band = jnp.abs(delta) <= HALF_SPAN
    for h in range(HEADS_PER_DIL_GROUP):
        sides = []
        for side in range(2):
            val = jnp.full((DIL_M, kb), table_ref[(side * half + half - 1) * HEADS_PER_DIL_GROUP + h], jnp.float32)
            for k in range(half - 1, 0, -1):
                val = jnp.where(dist < starts[k], table_ref[(side * half + k - 1) * HEADS_PER_DIL_GROUP + h], val)
            sides.append(val)
        base = jnp.where(band, jnp.where(delta > 0, sides[1], sides[0]), NEG_INF)
        rows = slice(h * DIL_M, (h + 1) * DIL_M)
        bias_scr[0, rows, :] = base
        bias_scr[1, rows, :] = jnp.where(m >= HALF_SPAN, base, NEG_INF)
        bias_scr[2, rows, :] = jnp.where(m < kb - HALF_SPAN, base, NEG_INF)


def _stack_heads(q):
    head_of_col = lax.broadcasted_iota(jnp.int32, (1, HEAD_BLOCK), 1) // HEAD_DIM
    zero = jnp.zeros_like(q)
    return jnp.concatenate([jnp.where(head_of_col == h, q, zero) for h in range(4)], axis=0)


def _merge_heads(x, m):
    head_of_col = lax.broadcasted_iota(jnp.int32, (1, HEAD_BLOCK), 1) // HEAD_DIM
    out = jnp.zeros((m, HEAD_BLOCK), x.dtype)
    for h in range(4):
        out = jnp.where(head_of_col == h, x[h * m:(h + 1) * m], out)
    return out


def _softmax_pv(s, v, m_rows):
    mx = jnp.max(s, axis=-1, keepdims=True)
    p = jnp.exp(s - mx)
    l = jnp.sum(p, axis=-1, keepdims=True)
    pv = jnp.dot(p.astype(jnp.bfloat16), v, preferred_element_type=jnp.float32)
    o = _merge_heads(pv * (1.0 / l), m_rows)
    return o, mx + jnp.log(l)


def _dilated_kernel(table_ref, q_ref, kp_ref, kc_ref, kn_ref, vp_ref, vc_ref, vn_ref,
                    o_ref, lse_ref, bias_scr, *, dil, seg0_blocks, blocks_per_seq):
    step = pl.program_id(1)

    @pl.when((pl.program_id(0) == 0) & (step == 0))
    def _():
        _build_dilated_bias(table_ref, bias_scr, dil)

    rows2d = lambda ref: ref[...].reshape(-1, HEAD_BLOCK)
    k = jnp.concatenate([rows2d(r) for r in (kp_ref, kc_ref, kn_ref)], axis=0)
    v = jnp.concatenate([rows2d(r) for r in (vp_ref, vc_ref, vn_ref)], axis=0)
    q = rows2d(q_ref)
    o_parts, lse_parts = [], []
    for g in range(DIL_GROUP):
        jb = step * DIL_GROUP + g
        in0 = jb < seg0_blocks
        jl = jnp.where(in0, jb, jb - seg0_blocks)
        nbs = jnp.where(in0, blocks_per_seq[0], blocks_per_seq[1])
        pos = lax.rem(jl, nbs)
        variant = jnp.where(pos == 0, 1, jnp.where(pos == nbs - 1, 2, 0))
        keys = slice(g * DIL_M, (g + 1) * DIL_M + 2 * HALF_SPAN)
        qs = _stack_heads(q[g * DIL_M:(g + 1) * DIL_M])
        s = lax.dot_general(qs, k[keys], (((1,), (1,)), ((), ())), preferred_element_type=jnp.float32)
        s = s + bias_scr[variant]
        o, lse = _softmax_pv(s, v[keys], DIL_M)
        o_parts.append(o.astype(o_ref.dtype))
        lse_parts.append(_merge_heads(jnp.broadcast_to(lse, (4 * DIL_M, HEAD_BLOCK)), DIL_M))
    o_ref[...] = jnp.concatenate(o_parts, axis=0).reshape(o_ref.shape)
    lse_ref[...] = jnp.concatenate(lse_parts, axis=0).reshape(lse_ref.shape)


def _residue_spec(rows_blk, dil, block_of):
    per_tile = PROJ_TM // dil
    if per_tile >= rows_blk:
        per = per_tile // rows_blk
        return pl.BlockSpec((1, None, rows_blk, HEAD_BLOCK),
                            lambda r, jb: (block_of(jb) // per, r, block_of(jb) % per, 0))
    return pl.BlockSpec((rows_blk // per_tile, None, per_tile, HEAD_BLOCK), lambda r, jb: (block_of(jb), r, 0, 0))


def _dilated_group(q, k, v, table, dil, seg_tokens, seq_lens):
    n = q.shape[0] * PROJ_TM
    rows = n // dil
    step_rows = DIL_GROUP * DIL_M
    assert rows % step_rows == 0 and all(t % (dil * DIL_M) == 0 for t in seq_lens)
    nsteps = rows // step_rows
    nkb = rows // HALF_SPAN
    ratio = step_rows // HALF_SPAN
    seg0_blocks = seg_tokens[0] // dil // DIL_M
    blocks_per_seq = tuple(t // dil // DIL_M for t in seq_lens)
    assert min(blocks_per_seq) >= 2

    qo_spec = _residue_spec(step_rows, dil, lambda jb: jb)
    prev_spec = _residue_spec(HALF_SPAN, dil, lambda jb: jnp.maximum(ratio * jb - 1, 0))
    next_spec = _residue_spec(HALF_SPAN, dil, lambda jb: jnp.minimum(ratio * (jb + 1), nkb - 1))
    kv_specs = [prev_spec, qo_spec, next_spec]
    kern = functools.partial(_dilated_kernel, dil=dil, seg0_blocks=seg0_blocks, blocks_per_seq=blocks_per_seq)
    return pl.pallas_call(
        kern,
        grid=(dil, nsteps),
        in_specs=[pl.BlockSpec(memory_space=pltpu.SMEM), qo_spec] + kv_specs + kv_specs,
        out_specs=[qo_spec, qo_spec],
        out_shape=[jax.ShapeDtypeStruct(q.shape, jnp.bfloat16), jax.ShapeDtypeStruct(q.shape, jnp.float32)],
        scratch_shapes=[pltpu.VMEM((3, HEADS_PER_DIL_GROUP * DIL_M, DIL_M + 2 * HALF_SPAN), jnp.float32)],
        compiler_params=_cparams(("arbitrary", "arbitrary")),
        name=f"dilated_attention_d{dil}",
    )(table, q, k, k, k, v, v, v)


NA_DR = 2 * NA_KH - 1
NA_DC = 2 * NA_KW - 1
NA_PAIRS = NA_DR - 1


def _build_na_bias(rpb_ref, pair_scr, head0):
    qc = lax.broadcasted_iota(jnp.int32, (GRID_W, 2 * GRID_W), 0)
    lane = lax.broadcasted_iota(jnp.int32, (GRID_W, 2 * GRID_W), 1)
    kc = lane % GRID_W
    upper = lane >= GRID_W
    qstart = jnp.clip(qc - NA_KW // 2, 0, GRID_W - NA_KW)
    valid = (kc >= qstart) & (kc < qstart + NA_KW)
    dc = jnp.clip(kc - qc, -(NA_KW - 1), NA_KW - 1) + NA_KW - 1

    def one_pair(idx, carry):
        h = idx // NA_PAIRS
        dr = idx % NA_PAIRS
        base = ((head0 + h) * NA_DR + dr) * NA_DC
        val = jnp.zeros((GRID_W, 2 * GRID_W), jnp.float32)
        for c in range(NA_DC):
            val = jnp.where(dc == c, jnp.where(upper, rpb_ref[base + NA_DC + c], rpb_ref[base + c]), val)
        pair_scr[h, dr] = jnp.where(valid, val, NEG_INF)
        return carry

    lax.fori_loop(0, 4 * NA_PAIRS, one_pair, 0)


def _na_kernel(rpb_ref, q_ref, kp_ref, kc_ref, kn_ref, vp_ref, vc_ref, vn_ref, o_ref, k_scr, v_scr, pair_scr,
               *, seg0_blocks, blocks_per_seq):
    ib = pl.program_id(1)

    @pl.when(ib == 0)
    def _():
        _build_na_bias(rpb_ref, pair_scr, pl.program_id(0) * 4)

    in0 = ib < seg0_blocks
    il = jnp.where(in0, ib, ib - seg0_blocks)
    nbs = jnp.where(in0, blocks_per_seq[0], blocks_per_seq[1])
    pos = lax.rem(il, nbs)
    first = pos == 0
    last = pos == nbs - 1
    blk = NA_ROWS * GRID_W
    halo = NA_HALO * GRID_W
    for scr, (p_ref, c_ref, n_ref) in ((k_scr, (kp_ref, kc_ref, kn_ref)), (v_scr, (vp_ref, vc_ref, vn_ref))):
        scr[0:halo] = p_ref[...]
        scr[halo:halo + blk] = c_ref[...]
        scr[halo + blk:2 * halo + blk] = n_ref[...]
    half = NA_KH // 2
    for rr in range(NA_ROWS):
        inner = NA_HALO + rr - half
        s_first, s_last = max(inner, NA_HALO), min(inner, NA_HALO + NA_ROWS - NA_KH)
        start = jnp.where(first, s_first, jnp.where(last, s_last, inner))
        var = jnp.where(first, NA_HALO + rr - s_first, jnp.where(last, NA_HALO + rr - s_last, half))
        off = pl.multiple_of(start * GRID_W, GRID_W)
        kk = k_scr[pl.ds(off, NA_KH * GRID_W), :]
        vv = v_scr[pl.ds(off, NA_KH * GRID_W), :]
        qs = _stack_heads(q_ref[rr * GRID_W:(rr + 1) * GRID_W, :])
        s = lax.dot_general(qs, kk, (((1,), (1,)), ((), ())), preferred_element_type=jnp.float32)
        bias = jnp.concatenate(
            [jnp.concatenate([pair_scr[h, 2 * p - var + NA_KH - 1] for p in range(NA_KH // 2)], axis=1)
             for h in range(4)], axis=0)
        s = s + bias
        o, _ = _softmax_pv(s, vv, GRID_W)
        o_ref[rr * GRID_W:(rr + 1) * GRID_W, :] = o.astype(o_ref.dtype)


def _neighborhood(qb, kb, vb, rpb_flat, seg_tokens, seq_lens):
    n = qb.shape[0]
    blk = NA_ROWS * GRID_W
    nblk = n // blk
    seg0_blocks = seg_tokens[0] // blk
    blocks_per_seq = tuple(t // blk for t in seq_lens)
    assert min(blocks_per_seq) >= 2
    nset = QB // HEAD_BLOCK

    per = NA_ROWS // NA_HALO
    halo = NA_HALO * GRID_W
    cur_spec = pl.BlockSpec((blk, HEAD_BLOCK), lambda c, ib: (ib, c))
    prev_spec = pl.BlockSpec((halo, HEAD_BLOCK), lambda c, ib: (jnp.maximum(per * ib - 1, 0), c))
    next_spec = pl.BlockSpec((halo, HEAD_BLOCK), lambda c, ib: (jnp.minimum(per * (ib + 1), per * nblk - 1), c))
    kv_specs = [prev_spec, cur_spec, next_spec]

    kern = functools.partial(_na_kernel, seg0_blocks=seg0_blocks, blocks_per_seq=blocks_per_seq)
    return pl.pallas_call(
        kern,
        grid=(nset, nblk),
        in_specs=[pl.BlockSpec(memory_space=pltpu.SMEM), cur_spec] + kv_specs + kv_specs,
        out_specs=cur_spec,
        out_shape=jax.ShapeDtypeStruct((n, QB), jnp.bfloat16),
        scratch_shapes=[pltpu.VMEM((blk + 2 * halo, HEAD_BLOCK), jnp.bfloat16),
                        pltpu.VMEM((blk + 2 * halo, HEAD_BLOCK), jnp.bfloat16),
                        pltpu.VMEM((4, NA_PAIRS, GRID_W, 2 * GRID_W), jnp.float32)],
        compiler_params=_cparams(("arbitrary", "arbitrary")),
        name="neighborhood_attention",
    )(rpb_flat, qb, kb, kb, kb, vb, vb, vb)


def _post_kernel(x0_ref, o0_ref, o1_ref, o2_ref, l0_ref, l1_ref, l2_ref, ob_ref,
                 ga_ref, gb_ref, wa_ref, wb_ref, wo_ref, bo_ref, g1_ref, b1_ref, wr_ref, br_ref,
                 x1_ref, logits_ref, perm_scr):
    tm = logits_ref.shape[0]
    hm = tm // POST_SPLIT

    def token_order(ref, slab0):
        dil, per = ref.shape[1], ref.shape[2]
        if dil == 1:
            return lambda rs: ref[0, 0, rs, :].astype(jnp.float32)
        for r in range(dil):
            val = ref[0, r].astype(jnp.float32)
            for s in range(2):
                perm_scr[slab0 + s, pl.ds(r, per, stride=dil), :] = val[:, s * LANES:(s + 1) * LANES]
        return lambda rs: jnp.concatenate([perm_scr[slab0, rs, :], perm_scr[slab0 + 1, rs, :]], axis=1)

    lse_of = [token_order(ref, 2 * j) for j, ref in enumerate((l0_ref, l1_ref, l2_ref))]
    o_of = [token_order(ref, 6 + 2 * j) for j, ref in enumerate((o0_ref, o1_ref, o2_ref))]
    sigmoid = lambda v: 0.5 * jnp.tanh(0.5 * v) + 0.5

    for h in range(POST_SPLIT):
        rs = slice(h * hm, (h + 1) * hm)
        l0, l1, l2 = (f(rs) for f in lse_of)
        lm = jnp.maximum(jnp.maximum(l0, l1), l2)
        e0, e1, e2 = jnp.exp(l0 - lm), jnp.exp(l1 - lm), jnp.exp(l2 - lm)
        inv = 1.0 / (e0 + e1 + e2)
        o_a = (e0 * inv) * o_of[0](rs) + (e1 * inv) * o_of[1](rs) + (e2 * inv) * o_of[2](rs)
        y_a = jnp.dot(o_a.astype(jnp.bfloat16), wa_ref[...], preferred_element_type=jnp.float32)
        y_b = jnp.dot(ob_ref[rs, :], wb_ref[...], preferred_element_type=jnp.float32)
        mix = sigmoid(ga_ref[rs, :].astype(jnp.float32)) * y_a + sigmoid(gb_ref[rs, :].astype(jnp.float32)) * y_b
        out = jnp.dot(mix.astype(jnp.bfloat16), wo_ref[...], preferred_element_type=jnp.float32) + bo_ref[...]
        x1 = _layer_norm(DN_ALPHA * x0_ref[rs, :] + out, g1_ref[...], b1_ref[...])
        _store_row_tiles(x1_ref, x1, row0=h * hm)

        logits_ref[rs, :] = (jnp.dot(x1.astype(jnp.bfloat16), wr_ref[...], preferred_element_type=jnp.float32)
                             + br_ref[...])


def _route_kernel(logits_ref, tri_ref, route_ref, route_t_ref, counts_ref, carry_ref):
    hm = tri_ref.shape[0]

    @pl.when(pl.program_id(0) == 0)
    def _():
        carry_ref[...] = jnp.zeros_like(carry_ref)

    grow = lax.broadcasted_iota(jnp.int32, (8, hm), 0)
    erow = lax.broadcasted_iota(jnp.int32, (N_EXPERTS, hm), 0)
    frow = lax.broadcasted_iota(jnp.int32, (ROUTE_LANES, hm), 0)
    neg = jnp.float32(-jnp.inf)
    carry = carry_ref[:, 0:1]

    for h in range(logits_ref.shape[0] // hm):
        rs = slice(h * hm, (h + 1) * hm)
        rt = logits_ref[rs, :].T
        lg = jnp.where(grow < N_GROUPS, rt[0:8], neg)
        gmax = jnp.max(lg, axis=0, keepdims=True)
        gi = jnp.min(jnp.where(lg == gmax, grow, 8), axis=0, keepdims=True)
        wg = 1.0 / jnp.sum(jnp.exp(lg - gmax), axis=0, keepdims=True)
        lo = EXPERTS_PER_GROUP * gi
        le = jnp.where((erow >= lo) & (erow < lo + EXPERTS_PER_GROUP), rt[EXPERT_LANE0:EXPERT_LANE0 + N_EXPERTS], neg)
        m1 = jnp.max(le, axis=0, keepdims=True)
        i1 = jnp.min(jnp.where(le == m1, erow, N_EXPERTS), axis=0, keepdims=True)
        le2 = jnp.where(erow == i1, neg, le)
        m2 = jnp.max(le2, axis=0, keepdims=True)
        i2 = jnp.min(jnp.where(le2 == m2, erow, N_EXPERTS), axis=0, keepdims=True)
        t2 = jnp.exp(m2 - m1)
        w1 = wg / (1.0 + t2)
        w2 = wg * t2 / (1.0 + t2)

        hot1 = erow == i1
        hot2 = erow == i2
        hot = (hot1 | hot2).astype(jnp.float32)
        before = jnp.dot(hot.astype(jnp.bfloat16), tri_ref[...], preferred_element_type=jnp.float32) + carry
        rank1 = jnp.sum(jnp.where(hot1, before, 0.0), axis=0, keepdims=True)
        rank2 = jnp.sum(jnp.where(hot2, before, 0.0), axis=0, keepdims=True)
        carry = carry + jnp.sum(hot, axis=1, keepdims=True)

        fields = (i1.astype(jnp.float32), i2.astype(jnp.float32), w1, w2, rank1, rank2)
        route_t = jnp.zeros((ROUTE_LANES, hm), jnp.float32)
        for idx, val in enumerate(fields):
            route_t = jnp.where(frow == idx, val, route_t)
        route_t_ref[:, rs] = route_t[0:route_t_ref.shape[0]]
        route_ref[rs, :] = route_t.T

    carry_ref[...] = jnp.broadcast_to(carry, carry_ref.shape)
    counts_ref[...] = jnp.broadcast_to(carry, counts_ref.shape)


def _post_attention(x0, o_groups, lse_groups, ob, ga, gb, wa, wb, wo, bo, g1, b1, wr, br):
    n, d = x0.shape
    tm = POST_TM
    nt = n // tm

    def tok(width):
        return pl.BlockSpec((tm, width), lambda i: (i, 0))

    def full(a):
        return pl.BlockSpec(a.shape, lambda i: (0,) * a.ndim)

    x1t, logits = pl.pallas_call(
        _post_kernel,
        grid=(nt,),
        in_specs=[
            tok(d),
            *[pl.BlockSpec((1,) + a.shape[1:], lambda i: (i, 0, 0, 0)) for a in (*o_groups, *lse_groups)],
            tok(QB), tok(d), tok(d),
            full(wa), full(wb), full(wo), full(bo), full(g1), full(b1), full(wr), full(br),
        ],
        out_specs=[pl.BlockSpec((tm * SUBLANES, LANES), lambda i: (i, 0)), tok(ROUTE_LANES)],
        out_shape=[jax.ShapeDtypeStruct((n * SUBLANES, LANES), jnp.float32),
                   jax.ShapeDtypeStruct((n, ROUTE_LANES), jnp.float32)],
        scratch_shapes=[pltpu.VMEM((12, tm, LANES), jnp.float32)],
        compiler_params=_cparams(("arbitrary",)),
        name="post_attention",
    )(x0, *o_groups, *lse_groups, ob, ga, gb, wa, wb, wo, bo, g1, b1, wr, br)

    rt, hm = ROUTE_TT, ROUTE_SUB
    tri = jnp.asarray(np.triu(np.ones((hm, hm), np.float32), 1), jnp.bfloat16)
    route, route_t, counts = pl.pallas_call(
        _route_kernel,
        grid=(n // rt,),
        in_specs=[pl.BlockSpec((rt, ROUTE_LANES), lambda i: (i, 0)), pl.BlockSpec((hm, hm), lambda i: (0, 0))],
        out_specs=[pl.BlockSpec((rt, ROUTE_LANES), lambda i: (i, 0)), pl.BlockSpec((8, rt), lambda i: (0, i)),
                   pl.BlockSpec((N_EXPERTS, ROUTE_LANES), lambda i: (0, 0))],
        out_shape=[jax.ShapeDtypeStruct((n, ROUTE_LANES), jnp.float32),
                   jax.ShapeDtypeStruct((8, n), jnp.float32),
                   jax.ShapeDtypeStruct((N_EXPERTS, ROUTE_LANES), jnp.float32)],
        scratch_shapes=[pltpu.VMEM((N_EXPERTS, ROUTE_LANES), jnp.float32)],
        compiler_params=_cparams(("arbitrary",)),
        name="moe_router",
    )(logits, tri)
    return x1t, route, route_t, counts


def _dispatch_kernel(pos0_ref, pos1_ref, x_ref, xs_ref, sem):
    tt = x_ref.shape[0] // SUBLANES

    def issue(j, carry):
        for u in range(2):
            t = 2 * j + u
            src = x_ref.at[pl.ds(pl.multiple_of(t * SUBLANES, SUBLANES), SUBLANES)]
            for p_ref in (pos0_ref, pos1_ref):
                p = p_ref[t]
                dst = xs_ref.at[pl.ds(pl.multiple_of(p * SUBLANES, SUBLANES), SUBLANES)]
                pltpu.make_async_copy(src, dst, sem).start(priority=u)
        return carry

    lax.fori_loop(0, tt // 2, issue, 0, unroll=4)
    for _ in range(2):
        pltpu.make_async_copy(x_ref, xs_ref.at[pl.ds(0, tt * SUBLANES)], sem).wait()


def _dispatch(x1t, pos, n_rows):
    n = x1t.shape[0] // SUBLANES
    tt = DISP_TT
    return pl.pallas_call(
        _dispatch_kernel,
        grid=(n // tt,),
        in_specs=[pl.BlockSpec((tt,), lambda i: (i,), memory_space=pltpu.SMEM),
                  pl.BlockSpec((tt,), lambda i: (i,), memory_space=pltpu.SMEM),
                  pl.BlockSpec((tt * SUBLANES, LANES), lambda i: (i, 0))],
        out_specs=pl.BlockSpec(memory_space=pl.ANY),
        out_shape=jax.ShapeDtypeStruct((n_rows * SUBLANES, LANES), x1t.dtype),
        scratch_shapes=[pltpu.SemaphoreType.DMA(())],
        compiler_params=_cparams(("arbitrary",)),
        name="moe_dispatch",
    )(pos[0], pos[1], x1t)


def _ffn_kernel(item_expert_ref, item_tile_ref, item_lo_ref, item_hi_ref, n_items_ref, item_next_ref, item_slot_ref,
                x_ref, wg_hbm, wu_hbm, wd_hbm, y_ref, wg_bf, wu_bf, wd_bf, y_acc, wg_buf, wu_buf, wd_buf, sems):
    j = pl.program_id(0)

    @pl.when(j == 0)
    def _():
        y_acc[...] = jnp.zeros_like(y_acc)

    def weight_copies(expert, slot):
        return [pltpu.make_async_copy(src.at[expert], dst.at[slot], sems.at[slot])
                for src, dst in ((wg_hbm, wg_buf), (wu_hbm, wu_buf), (wd_hbm, wd_buf))]

    @pl.when(j < n_items_ref[0])
    def _():
        prev = jnp.maximum(j - 1, 0)
        expert = item_expert_ref[j]
        slot = item_slot_ref[j]

        @pl.when(jnp.logical_or(j == 0, expert != item_expert_ref[prev]))
        def _():
            @pl.when(j == 0)
            def _():
                for c in weight_copies(expert, slot):
                    c.start()

            for c in weight_copies(expert, slot):
                c.wait()
            wg_bf[...] = wg_buf[slot].astype(jnp.bfloat16)
            wu_bf[...] = wu_buf[slot].astype(jnp.bfloat16)
            wd_bf[...] = wd_buf[slot].astype(jnp.bfloat16)
            nxt = item_next_ref[j]

            @pl.when(nxt >= 0)
            def _():
                for c in weight_copies(nxt, 1 - slot):
                    c.start()

        tm = x_ref.shape[0] // SUBLANES
        x = _load_row_tiles(x_ref, tm).astype(jnp.bfloat16)
        a = jnp.dot(x, wg_bf[...], preferred_element_type=jnp.float32)
        u = jnp.dot(x, wu_bf[...], preferred_element_type=jnp.float32)
        h = (a * jax.nn.sigmoid(a) * u).astype(jnp.bfloat16)
        y = jnp.dot(h, wd_bf[...], preferred_element_type=jnp.float32)
        rows = lax.broadcasted_iota(jnp.int32, y.shape, 0)
        pltpu.store(y_acc, y, mask=(rows >= item_lo_ref[j]) & (rows < item_hi_ref[j]))
        _store_row_tiles(y_ref, y_acc[...])


def _ffn_items(cnt, n_rows):
    tm = FFN_TM
    n_tiles = n_rows // tm
    max_items = n_tiles + N_EXPERTS - 1
    ends = jnp.cumsum(cnt)
    starts = ends - cnt
    first_tile = starts // tm
    last_tile = jnp.maximum(ends - 1, 0) // tm
    items_per_expert = jnp.where(cnt > 0, last_tile - first_tile + 1, 0)
    item_ends = jnp.cumsum(items_per_expert)
    n_items = item_ends[-1:]
    j = jnp.minimum(jnp.arange(max_items, dtype=jnp.int32), n_items[0] - 1)
    expert = jnp.minimum(jnp.sum(item_ends[None, :] <= j[:, None], axis=1), N_EXPERTS - 1).astype(jnp.int32)
    tile = first_tile[expert] + (j - (item_ends - items_per_expert)[expert])
    lo = jnp.clip(starts[expert] - tile * tm, 0, tm)
    hi = jnp.clip(ends[expert] - tile * tm, 0, tm)
    active = cnt > 0
    ids = jnp.arange(N_EXPERTS, dtype=jnp.int32)
    later = (ids[None, :] > ids[:, None]) & active[None, :]
    next_active = jnp.min(jnp.where(later, ids[None, :], N_EXPERTS), axis=1)
    next_active = jnp.where(next_active < N_EXPERTS, next_active, -1)
    slot = (jnp.cumsum(active.astype(jnp.int32)) - 1) % 2
    i32 = lambda a: a.astype(jnp.int32)
    return (i32(expert), i32(tile), i32(lo), i32(hi), i32(n_items), i32(next_active[expert]), i32(slot[expert]),
            starts)


def _grouped_ffn(xs, items, w_gate, w_up, w_down):
    tm = FFN_TM
    d, de = w_gate.shape[1:]
    max_items = items[0].shape[0]

    def row_map(j, ie, it, *_):
        return (it[j], 0)

    grid_spec = pltpu.PrefetchScalarGridSpec(
        num_scalar_prefetch=len(items),
        grid=(max_items,),
        in_specs=[pl.BlockSpec((tm * SUBLANES, LANES), row_map)] + [pl.BlockSpec(memory_space=pl.ANY)] * 3,
        out_specs=pl.BlockSpec((tm * SUBLANES, LANES), row_map),
        scratch_shapes=[pltpu.VMEM((d, de), jnp.bfloat16), pltpu.VMEM((d, de), jnp.bfloat16),
                        pltpu.VMEM((de, d), jnp.bfloat16), pltpu.VMEM((tm, d), jnp.float32),
                        pltpu.VMEM((2, d, de), jnp.float32), pltpu.VMEM((2, d, de), jnp.float32),
                        pltpu.VMEM((2, de, d), jnp.float32), pltpu.SemaphoreType.DMA((2,))],
    )
    return pl.pallas_call(
        _ffn_kernel,
        grid_spec=grid_spec,
        out_shape=jax.ShapeDtypeStruct(xs.shape, jnp.float32),
        compiler_params=_cparams(("arbitrary",)),
        name="moe_grouped_ffn",
    )(*items, xs, w_gate, w_up, w_down)


def _combine_kernel(pos0_ref, pos1_ref, next0_ref, next1_ref, x1_ref, route_ref, g2_ref, b2_ref, ys_ref, o_ref,
                    buf, sems):
    tt = x1_ref.shape[0] // SUBLANES
    i = pl.program_id(0)
    cur = lax.rem(i, 2)

    def issue_tile(p_refs, b):
        def issue(j, carry):
            for u in range(2):
                t = 2 * j + u
                for slot, p_ref in enumerate(p_refs):
                    p = p_ref[t]
                    src = ys_ref.at[pl.ds(pl.multiple_of(p * SUBLANES, SUBLANES), SUBLANES)]
                    dst = buf.at[b, slot, pl.ds(pl.multiple_of(t * SUBLANES, SUBLANES), SUBLANES)]
                    pltpu.make_async_copy(src, dst, sems.at[b]).start(priority=u)
            return carry

        lax.fori_loop(0, tt // 2, issue, 0, unroll=4)

    @pl.when(i == 0)
    def _():
        issue_tile((pos0_ref, pos1_ref), 0)

    @pl.when(i + 1 < pl.num_programs(0))
    def _():
        issue_tile((next0_ref, next1_ref), 1 - cur)

    for slot in range(2):
        pltpu.make_async_copy(ys_ref.at[pl.ds(0, tt * SUBLANES)], buf.at[cur, slot], sems.at[cur]).wait()
    route = route_ref[...]
    y = route[:, 2:3] * _load_row_tiles(buf.at[cur, 0], tt) + route[:, 3:4] * _load_row_tiles(buf.at[cur, 1], tt)
    o_ref[...] = _layer_norm(DN_ALPHA * _load_row_tiles(x1_ref, tt) + y, g2_ref[...], b2_ref[...])


def _combine(x1t, route, pos, ys, g2, b2, tok0, n_tok):
    d = g2.shape[1]
    tt = COMB_TT
    t0 = tok0 // tt
    nt = n_tok // tt
    return pl.pallas_call(
        _combine_kernel,
        grid=(nt,),
        in_specs=[pl.BlockSpec((tt,), lambda i: (i + t0,), memory_space=pltpu.SMEM),
                  pl.BlockSpec((tt,), lambda i: (i + t0,), memory_space=pltpu.SMEM),
                  pl.BlockSpec((tt,), lambda i: (jnp.minimum(i + 1, nt - 1) + t0,), memory_space=pltpu.SMEM),
                  pl.BlockSpec((tt,), lambda i: (jnp.minimum(i + 1, nt - 1) + t0,), memory_space=pltpu.SMEM),
                  pl.BlockSpec((tt * SUBLANES, LANES), lambda i: (i + t0, 0)),
                  pl.BlockSpec((tt, ROUTE_LANES), lambda i: (i + t0, 0)),
                  pl.BlockSpec((1, d), lambda i: (0, 0)),
                  pl.BlockSpec((1, d), lambda i: (0, 0)),
                  pl.BlockSpec(memory_space=pl.ANY)],
        out_specs=pl.BlockSpec((tt, d), lambda i: (i, 0)),
        out_shape=jax.ShapeDtypeStruct((n_tok, d), jnp.float32),
        scratch_shapes=[pltpu.VMEM((2, 2, tt * SUBLANES, LANES), ys.dtype), pltpu.SemaphoreType.DMA((2,))],
        compiler_params=_cparams(("arbitrary",)),
        name="moe_combine_ln",
    )(pos[0], pos[1], pos[0], pos[1], x1t, route, g2, b2, ys)


def _layer(x_prompt, x_sample, ln_in_g, ln_in_b, w_in, b_in, rel_bias_t5, na_rpb, w_branch_a, w_branch_b,
           w_out, b_out, ln1_g, ln1_b, w_router_group, b_router_group, w_router_expert, b_router_expert,
           w_exp_gate, w_exp_up, w_exp_down, ln2_g, ln2_b):
    bp, tp, d = x_prompt.shape
    bs, ts, _ = x_sample.shape
    seg_tokens = (bp * tp, bs * ts)
    seq_lens = (tp, ts)
    n = sum(seg_tokens)
    xp = x_prompt.reshape(seg_tokens[0], d)
    xs = x_sample.reshape(seg_tokens[1], d)
    row = lambda a: a.reshape(1, -1).astype(jnp.float32)
    bf = lambda a: a.astype(jnp.bfloat16)

    scale = HEAD_DIM ** -0.5
    col_scale = np.ones((w_in.shape[2],), np.float32)
    col_scale[0:QA] = scale
    col_scale[3 * QA:3 * QA + QB] = scale
    w_proj = bf(w_in[0] * col_scale)
    b_proj = row(b_in[0] * col_scale)
    proj = _in_projection(xp, xs, row(ln_in_g), row(ln_in_b), w_proj, b_proj)
    qkv_a, (qb, kb, vb, ga, gb, x0) = proj[:3 * N_DIL_GROUPS], proj[3 * N_DIL_GROUPS:]

    o_groups, lse_groups = [], []
    for g, (window, dil) in enumerate(DIL_PATTERNS):
        assert window // (2 * dil) == HALF_SPAN
        table = rel_bias_t5[:, g * HEADS_PER_DIL_GROUP:(g + 1) * HEADS_PER_DIL_GROUP].astype(jnp.float32).reshape(-1)
        o, lse = _dilated_group(qkv_a[g], qkv_a[N_DIL_GROUPS + g], qkv_a[2 * N_DIL_GROUPS + g], table, dil,
                                seg_tokens, seq_lens)
        o_groups.append(o)
        lse_groups.append(lse)
    ob = _neighborhood(qb, kb, vb, na_rpb[0].astype(jnp.float32).reshape(-1), seg_tokens, seq_lens)

    wr = jnp.zeros((d, ROUTE_LANES), jnp.float32)
    wr = wr.at[:, 0:N_GROUPS].set(w_router_group[0]).at[:, EXPERT_LANE0:EXPERT_LANE0 + N_EXPERTS].set(w_router_expert[0])
    br = jnp.zeros((1, ROUTE_LANES), jnp.float32)
    br = br.at[0, 0:N_GROUPS].set(b_router_group[0]).at[0, EXPERT_LANE0:EXPERT_LANE0 + N_EXPERTS].set(b_router_expert[0])
    x1, route, route_t, counts = _post_attention(
        x0, o_groups, lse_groups, ob, ga, gb, bf(w_branch_a[0]), bf(w_branch_b[0]),
        bf(w_out[0]), row(b_out[0]), row(ln1_g[0]), row(ln1_b[0]), bf(wr), br)

    experts = route_t[0:2].astype(jnp.int32)
    ranks = route_t[4:6].astype(jnp.int32)
    cnt = counts[:, 0].astype(jnp.int32)
    n_rows = 2 * n
    *items, starts = _ffn_items(cnt, n_rows)
    one_hot = (experts[..., None] == jnp.arange(N_EXPERTS, dtype=jnp.int32)).astype(jnp.float32)
    pos = ranks + jnp.dot(one_hot, starts.astype(jnp.float32), precision=lax.Precision.HIGHEST).astype(jnp.int32)

    xsorted = _dispatch(x1, pos, n_rows)
    ys = _grouped_ffn(xsorted, items, w_exp_gate[0], w_exp_up[0], w_exp_down[0])
    y_prompt = _combine(x1, route, pos, ys, row(ln2_g[0]), row(ln2_b[0]), 0, seg_tokens[0])
    y_sample = _combine(x1, route, pos, ys, row(ln2_g[0]), row(ln2_b[0]), seg_tokens[0], seg_tokens[1])
    return y_prompt.reshape(bp, tp, d), y_sample.reshape(bs, ts, d)


def kernel(x_prompt, x_sample, ln_in_g, ln_in_b, w_in, b_in, rel_bias_t5, na_rpb, w_branch_a, w_branch_b, w_out, b_out, ln1_g, ln1_b, w_router_group, b_router_group, w_router_expert, b_router_expert, w_exp_gate, w_exp_up, w_exp_down, ln2_g, ln2_b):
    return _layer(x_prompt, x_sample, ln_in_g, ln_in_b, w_in, b_in, rel_bias_t5, na_rpb, w_branch_a, w_branch_b,
                  w_out, b_out, ln1_g, ln1_b, w_router_group, b_router_group, w_router_expert, b_router_expert,
                  w_exp_gate, w_exp_up, w_exp_down, ln2_g, ln2_b)
```
